```python
import jax, jax.numpy as jnp
from jax import lax
import numpy as np

D_MODEL = 1024
BATCH = 8
SEQ = 2048
DEPTH = 2
DEC_BATCH = 32
DEC_SEQ = 2048
PAST_LEN = 128

N_META = 16
RMS_EPS = 1e-6
BLK = 128
GQA_HEADS = 16
GQA_KV_HEADS = 4
GQA_GROUP = GQA_HEADS // GQA_KV_HEADS
GQA_HEAD_DIM = D_MODEL // GQA_HEADS
GQA_ROT = GQA_HEAD_DIM // 4
ROPE_THETA = 500000.0
WINDOW = 128
GQA_QKV = (GQA_HEADS + 2 * GQA_KV_HEADS) * GQA_HEAD_DIM
MLA_HEADS = 16
MLA_NOPE = 64
MLA_ROPE = 32
MLA_V = 64
MLA_Q_RANK = 384
MLA_KV_RANK = 256
MLA_ROPE_THETA = 10000.0
MLA_QK = MLA_NOPE + MLA_ROPE
FFN_DIM = 2816
N_EXPERTS = 8
TOP_K = 2
EXPERT_DIM = 3584
N_A = (DEPTH + 1) // 2
N_B = DEPTH // 2

kernel_name = "hybrid_swa_sink_mla_moe_encoder"


def rmsnorm(x, g):
    xf = x.astype(jnp.float32)
    y = xf * lax.rsqrt(jnp.mean(xf * xf, axis=-1, keepdims=True) + RMS_EPS)
    return (y * g.astype(jnp.float32)).astype(x.dtype)


def rope_tables(length, dim, theta):
    inv = jnp.power(jnp.float32(theta), -(jnp.arange(0, dim, 2, dtype=jnp.float32) / dim))
    ang = jnp.arange(length, dtype=jnp.float32)[:, None] * inv[None, :]
    return jnp.cos(ang), jnp.sin(ang)


def apply_rope(x, cos, sin):
    half = cos.shape[-1]
    shape = (cos.shape[0],) + (1,) * (x.ndim - 3) + (half,)
    c = cos.reshape(shape)
    s = sin.reshape(shape)
    x1 = x[..., :half].astype(jnp.float32)
    x2 = x[..., half:].astype(jnp.float32)
    return jnp.concatenate([x1 * c - x2 * s, x2 * c + x1 * s], axis=-1).astype(x.dtype)


def partial_rope(x, cos, sin):
    rot = 2 * cos.shape[-1]
    return jnp.concatenate([apply_rope(x[..., :rot], cos, sin), x[..., rot:]], axis=-1)


def _sink_softmax_av(s, valid, sink, vals):
    s = jnp.where(valid, s, -jnp.inf)
    sk = sink.astype(jnp.float32)[None, :, :, None, None]
    m = jnp.maximum(jnp.max(s, axis=-1, keepdims=True), sk)
    p = jnp.exp(s - m)
    p = p / (jnp.sum(p, axis=-1, keepdims=True) + jnp.exp(sk - m))
    return jnp.einsum('bkgqj,bjkd->bqkgd', p.astype(vals.dtype), vals)


def window_gqa(h, w_qkv, w_o, sink, cos, sin):
    B, L, _ = h.shape
    S = L - N_META
    NB = S // BLK
    HD = GQA_HEAD_DIM
    KV, G = GQA_KV_HEADS, GQA_GROUP
    qkv = h @ w_qkv
    q = qkv[..., :GQA_HEADS * HD].reshape(B, L, KV, G, HD)
    k = qkv[..., GQA_HEADS * HD:(GQA_HEADS + KV) * HD].reshape(B, L, KV, HD)
    v = qkv[..., (GQA_HEADS + KV) * HD:].reshape(B, L, KV, HD)
    q = partial_rope(q, cos, sin) * (HD ** -0.5)
    k = partial_rope(k, cos, sin)
    sink = sink.reshape(KV, G)
    k_meta, v_meta = k[:, :N_META], v[:, :N_META]

    keys = jnp.concatenate([k_meta, k[:, N_META:N_META + BLK]], axis=1)
    vals = jnp.concatenate([v_meta, v[:, N_META:N_META + BLK]], axis=1)
    s = jnp.einsum('bqkgd,bjkd->bkgqj', q[:, :N_META], keys, preferred_element_type=jnp.float32)
    qp = jnp.arange(N_META)[:, None]
    kp = jnp.arange(N_META + BLK)[None, :]
    valid = (kp < N_META) | (kp - qp <= WINDOW)
    o_meta = _sink_softmax_av(s, valid, sink, vals)

    k_pad = jnp.pad(k[:, N_META:], ((0, 0), (BLK, BLK), (0, 0), (0, 0)))
    v_pad = jnp.pad(v[:, N_META:], ((0, 0), (BLK, BLK), (0, 0), (0, 0)))
    q_blocks = jnp.moveaxis(q[:, N_META:].reshape(B, NB, BLK, KV, G, HD), 1, 0)
    qi = jnp.arange(BLK)[:, None]
    kj = jnp.arange(3 * BLK)[None, :]
    meta_visible = jnp.ones((BLK, N_META), dtype=bool)

    def one_block(args):
        qb, b = args
        kb = lax.dynamic_slice_in_dim(k_pad, b * BLK, 3 * BLK, axis=1)
        vb = lax.dynamic_slice_in_dim(v_pad, b * BLK, 3 * BLK, axis=1)
        kk = jnp.concatenate([k_meta, kb], axis=1)
        vv = jnp.concatenate([v_meta, vb], axis=1)
        sc = jnp.einsum('bqkgd,bjkd->bkgqj', qb, kk, preferred_element_type=jnp.float32)
        rk = b * BLK + kj - BLK
        local = (jnp.abs(qi - kj + BLK) <= WINDOW) & (rk >= 0) & (rk < S)
        ok = jnp.concatenate([meta_visible, local], axis=1)
        return _sink_softmax_av(sc, ok, sink, vv)

    o_real = lax.map(one_block, (q_blocks, jnp.arange(NB)))
    o_real = jnp.moveaxis(o_real, 0, 1).reshape(B, S, KV, G, HD)
    o = jnp.concatenate([o_meta, o_real], axis=1).reshape(B, L, GQA_HEADS * HD)
    return o @ w_o


def mla(h, w_dkv, q_norm, kv_norm, w_uq, w_ukv, w_o, cos, sin):
    B, L, _ = h.shape
    S = L - N_META
    NB = S // BLK
    lat = h @ w_dkv
    cq = rmsnorm(lat[..., :MLA_Q_RANK], q_norm)
    ckv = rmsnorm(lat[..., MLA_Q_RANK:MLA_Q_RANK + MLA_KV_RANK], kv_norm)
    k_rope = apply_rope(lat[..., MLA_Q_RANK + MLA_KV_RANK:][:, :, None, :], cos, sin)
    q = (cq @ w_uq).reshape(B, L, MLA_HEADS, MLA_QK)
    q = jnp.concatenate([q[..., :MLA_NOPE], apply_rope(q[..., MLA_NOPE:], cos, sin)], axis=-1)
    q = q * (MLA_QK ** -0.5)
    kv = (ckv @ w_ukv).reshape(B, L, MLA_HEADS, MLA_NOPE + MLA_V)
    k = jnp.concatenate([kv[..., :MLA_NOPE],
                         jnp.broadcast_to(k_rope, (B, L, MLA_HEADS, MLA_ROPE))], axis=-1)
    v = kv[..., MLA_NOPE:]

    def attend(qb):
        sc = jnp.einsum('bqhd,bkhd->bhqk', qb, k, preferred_element_type=jnp.float32)
        p = jax.nn.softmax(sc, axis=-1)
        return jnp.einsum('bhqk,bkhd->bqhd', p.astype(v.dtype), v)

    o_meta = attend(q[:, :N_META])
    q_blocks = jnp.moveaxis(q[:, N_META:].reshape(B, NB, BLK, MLA_HEADS, MLA_QK), 1, 0)
    o_real = jnp.moveaxis(lax.map(attend, q_blocks), 0, 1).reshape(B, S, MLA_HEADS, MLA_V)
    o = jnp.concatenate([o_meta, o_real], axis=1).reshape(B, L, MLA_HEADS * MLA_V)
    return o @ w_o


def swiglu(h, wg, wu, wd):
    return (jax.nn.silu(h @ wg) * (h @ wu)) @ wd


def moe_swiglu(h, router, wg, wu, wd):
    B, L, D = h.shape
    N = B * L
    x2 = h.reshape(N, D)
    logits = x2.astype(jnp.float32) @ router.astype(jnp.float32)
    top_val, top_idx = lax.top_k(logits, TOP_K)
    gates = jax.nn.softmax(top_val, axis=-1)
    A = N * TOP_K
    e_flat = top_idx.reshape(A).astype(jnp.int32)
    g_flat = gates.reshape(A)
    t_flat = (jnp.arange(A, dtype=jnp.int32) // TOP_K)
    order = jnp.argsort(e_flat)
    e_sorted, t_sorted, g_sorted = e_flat[order], t_flat[order], g_flat[order]
    counts = jnp.bincount(e_flat, length=N_EXPERTS).astype(jnp.int32)
    start = jnp.cumsum(counts) - counts
    padded = ((counts + BLK - 1) // BLK) * BLK
    pend = jnp.cumsum(padded)
    pstart = pend - padded
    dest = pstart[e_sorted] + (jnp.arange(A, dtype=jnp.int32) - start[e_sorted])
    n_blocks = -(-A // BLK) + N_EXPERTS
    R = n_blocks * BLK
    row_tok = jnp.full((R,), N, dtype=jnp.int32).at[dest].set(t_sorted)
    row_gate = jnp.zeros((R,), jnp.float32).at[dest].set(g_sorted)
    xpad = jnp.concatenate([x2, jnp.zeros((1, D), x2.dtype)], axis=0)
    xs = xpad[row_tok].reshape(n_blocks, BLK, D)
    block_e = jnp.clip(jnp.searchsorted(pend, jnp.arange(n_blocks, dtype=jnp.int32) * BLK,
                                        side='right'), 0, N_EXPERTS - 1)

    def expert_block(args):
        xb, e = args
        return (jax.nn.silu(xb @ wg[e]) * (xb @ wu[e])) @ wd[e]

    ys = lax.map(expert_block, (xs, block_e)).reshape(R, D)
    out = jnp.zeros((N + 1, D), jnp.float32).at[row_tok].add(ys.astype(jnp.float32) * row_gate[:, None])
    return out[:N].astype(h.dtype).reshape(B, L, D)


def trunk(x, meta, norm_mix, norm_ffn, norm_final,
          a_wqkv, a_wo, a_sink,
          b_wdkv, b_qnorm, b_kvnorm, b_wuq, b_wukv, b_wo,
          f_wg, f_wu, f_wd,
          m_router, m_wg, m_wu, m_wd):
    B, S, D = x.shape
    L = S + N_META
    h = jnp.concatenate([jnp.broadcast_to(meta.astype(x.dtype)[None], (B, N_META, D)), x], axis=1)
    cos_a, sin_a = rope_tables(L, GQA_ROT, ROPE_THETA)
    cos_b, sin_b = rope_tables(L, MLA_ROPE, MLA_ROPE_THETA)
    for i in range(DEPTH):
        j = i // 2
        hn = rmsnorm(h, norm_mix[i])
        if i % 2 == 0:
            h = h + window_gqa(hn, a_wqkv[j], a_wo[j], a_sink[j], cos_a, sin_a)
            h = h + swiglu(rmsnorm(h, norm_ffn[i]), f_wg[j], f_wu[j], f_wd[j])
        else:
            h = h + mla(hn, b_wdkv[j], b_qnorm[j], b_kvnorm[j], b_wuq[j], b_wukv[j], b_wo[j],
                        cos_b, sin_b)
            h = h + moe_swiglu(rmsnorm(h, norm_ffn[i]), m_router[j], m_wg[j], m_wu[j], m_wd[j])
    return rmsnorm(h, norm_final)[:, N_META:]


def setup_inputs(seed: int = 0) -> dict:
    key = jax.random.key(seed)
    ks = jax.random.split(key, 24)
    f32 = jnp.float32

    def w(k, shape, fan_in):
        return jax.random.normal(k, shape, f32) * (fan_in ** -0.5)

    def gain(k, shape):
        return 1.0 + 0.01 * jax.random.normal(k, shape, f32)

    D = D_MODEL
    return {
        "x_prompt": jax.random.normal(ks[0], (BATCH, SEQ, D), f32),
        "x_sample": jax.random.normal(ks[1], (DEC_BATCH, DEC_SEQ, D), f32),
        "meta": jax.random.normal(ks[2], (N_META, D), f32),
        "norm_mix": gain(ks[3], (DEPTH, D)),
        "norm_ffn": gain(ks[4], (DEPTH, D)),
        "norm_final": gain(ks[5], (D,)),
        "a_wqkv": w(ks[6], (N_A, D, GQA_QKV), D),
        "a_wo": w(ks[7], (N_A, GQA_HEADS * GQA_HEAD_DIM, D), GQA_HEADS * GQA_HEAD_DIM),
        "a_sink": 0.1 * jax.random.normal(ks[8], (N_A, GQA_HEADS), f32),
        "b_wdkv": w(ks[9], (N_B, D, MLA_Q_RANK + MLA_KV_RANK + MLA_ROPE), D),
        "b_qnorm": gain(ks[10], (N_B, MLA_Q_RANK)),
        "b_kvnorm": gain(ks[11], (N_B, MLA_KV_RANK)),
        "b_wuq": w(ks[12], (N_B, MLA_Q_RANK, MLA_HEADS * MLA_QK), MLA_Q_RANK),
        "b_wukv": w(ks[13], (N_B, MLA_KV_RANK, MLA_HEADS * (MLA_NOPE + MLA_V)), MLA_KV_RANK),
        "b_wo": w(ks[14], (N_B, MLA_HEADS * MLA_V, D), MLA_HEADS * MLA_V),
        "f_wg": w(ks[15], (N_A, D, FFN_DIM), D),
        "f_wu": w(ks[16], (N_A, D, FFN_DIM), D),
        "f_wd": w(ks[17], (N_A, FFN_DIM, D), FFN_DIM),
        "m_router": w(ks[18], (N_B, D, N_EXPERTS), D),
        "m_wg": w(ks[19], (N_B, N_EXPERTS, D, EXPERT_DIM), D),
        "m_wu": w(ks[20], (N_B, N_EXPERTS, D, EXPERT_DIM), D),
        "m_wd": w(ks[21], (N_B, N_EXPERTS, EXPERT_DIM, D), EXPERT_DIM),
    }


def reference(x_prompt, x_sample, meta, norm_mix, norm_ffn, norm_final,
              a_wqkv, a_wo, a_sink,
              b_wdkv, b_qnorm, b_kvnorm, b_wuq, b_wukv, b_wo,
              f_wg, f_wu, f_wd,
              m_router, m_wg, m_wu, m_wd):
    y_prompt = trunk(x_prompt, meta, norm_mix, norm_ffn, norm_final,
                     a_wqkv, a_wo, a_sink,
                     b_wdkv, b_qnorm, b_kvnorm, b_wuq, b_wukv, b_wo,
                     f_wg, f_wu, f_wd,
                     m_router, m_wg, m_wu, m_wd)
    y_sample = trunk(x_sample, meta, norm_mix, norm_ffn, norm_final,
                     a_wqkv, a_wo, a_sink,
                     b_wdkv, b_qnorm, b_kvnorm, b_wuq, b_wukv, b_wo,
                     f_wg, f_wu, f_wd,
                     m_router, m_wg, m_wu, m_wd)
    return (y_prompt, y_sample)
```

```python
import functools
import math

import jax
import jax.numpy as jnp
from jax import lax
from jax.experimental import pallas as pl
from jax.experimental.pallas import tpu as pltpu

F32 = jnp.float32
BF16 = jnp.bfloat16
U32 = jnp.uint32
I32 = jnp.int32

D_MODEL = 1024
N_META = 16
RMS_EPS = 1e-6
BLK = 128
GQA_HEADS = 16
GQA_KV = 4
GQA_GROUP = 4
GQA_HD = 64
GQA_ROT = 16
GQA_THETA = 500000.0
MLA_HEADS = 16
MLA_NOPE = 64
MLA_ROPE = 32
MLA_V = 64
MLA_Q_RANK = 384
MLA_KV_RANK = 256
MLA_THETA = 10000.0
MLA_QK = MLA_NOPE + MLA_ROPE
N_EXPERTS = 8
LOG2E = 1.4426950408889634

LANES = 128
VMEM_LIMIT = 56 * 1024 * 1024


def _params(sem, vmem=VMEM_LIMIT):
    return pltpu.CompilerParams(dimension_semantics=sem, vmem_limit_bytes=vmem)


def _rms(x, g):
    ms = jnp.mean(x * x, axis=-1, keepdims=True)
    return x * lax.rsqrt(ms + RMS_EPS) * g


def _dot(a, b):
    return jnp.dot(a, b, preferred_element_type=F32)


def _dot_nt(a, b):
    return lax.dot_general(a, b, (((1,), (1,)), ((), ())), preferred_element_type=F32)


def _pick_tile(n, candidates):
    for c in candidates:
        if n % c == 0:
            return c
    raise ValueError(f"no tile in {candidates} divides {n}")


def _rope_lane_tables(pos, theta, rot, period, offset):
    half = rot // 2
    inv = jnp.power(jnp.float32(theta), -(jnp.arange(0, rot, 2, dtype=F32) / rot))
    ang = pos[:, None] * inv[None, :]
    cos, sin = jnp.cos(ang), jnp.sin(ang)
    lane = jnp.arange(LANES)
    r = (lane % period) - offset
    is_x1 = (r >= 0) & (r < half)
    is_x2 = (r >= half) & (r < rot)
    f = jnp.clip(jnp.where(is_x2, r - half, r), 0, half - 1)
    cos_l = cos[:, f]
    sin_l = sin[:, f]
    c = jnp.where((is_x1 | is_x2)[None, :], cos_l, 1.0)
    s1 = jnp.where(is_x2[None, :], sin_l, 0.0)
    s2 = jnp.where(is_x1[None, :], -sin_l, 0.0)
    return c.astype(F32), s1.astype(F32), s2.astype(F32)


def _rope(x, c, s1, s2, half):
    return x * c + pltpu.roll(x, half, 1) * s1 + pltpu.roll(x, LANES - half, 1) * s2


def _qkv_kernel(h_ref, g_ref, w_ref, c_ref, s1_ref, s2_ref, q_ref, k_ref, v_ref):
    xn = _rms(h_ref[...], g_ref[...]).astype(BF16)
    qkv = _dot(xn, w_ref[...])
    c, s1, s2 = c_ref[...], s1_ref[...], s2_ref[...]
    nq = GQA_HEADS * GQA_HD // LANES
    nk = GQA_KV * GQA_HD // LANES
    for i in range(nq):
        q_ref[:, i * LANES:(i + 1) * LANES] = _rope(
            qkv[:, i * LANES:(i + 1) * LANES], c, s1, s2, GQA_ROT // 2).astype(BF16)
    for i in range(nk):
        lo = (nq + i) * LANES
        k_ref[:, i * LANES:(i + 1) * LANES] = _rope(
            qkv[:, lo:lo + LANES], c, s1, s2, GQA_ROT // 2).astype(BF16)
    v_ref[...] = qkv[:, (nq + nk) * LANES:].astype(BF16)


def _table_index(i, n_real_tiles, tiles_per_seq):
    return jnp.where(i < n_real_tiles, i % tiles_per_seq, tiles_per_seq + i - n_real_tiles)


def _qkv_call(h, g, w, tabs, *, tm, n_real, seq):
    ntot = h.shape[0]
    nrt, tps = n_real // tm, seq // tm
    tab_spec = pl.BlockSpec((tm, LANES), lambda i: (_table_index(i, nrt, tps), 0))
    return pl.pallas_call(
        _qkv_kernel,
        grid=(ntot // tm,),
        in_specs=[
            pl.BlockSpec((tm, D_MODEL), lambda i: (i, 0)),
            pl.BlockSpec((1, D_MODEL), lambda i: (0, 0)),
            pl.BlockSpec(w.shape, lambda i: (0, 0)),
            tab_spec, tab_spec, tab_spec,
        ],
        out_specs=[
            pl.BlockSpec((tm, 1024), lambda i: (i, 0)),
            pl.BlockSpec((tm, 256), lambda i: (i, 0)),
            pl.BlockSpec((tm, 256), lambda i: (i, 0)),
        ],
        out_shape=[
            jax.ShapeDtypeStruct((ntot, 1024), BF16),
            jax.ShapeDtypeStruct((ntot, 256), BF16),
            jax.ShapeDtypeStruct((ntot, 256), BF16),
        ],
        compiler_params=_params(("parallel",)),
        name="qkv_rope",
    )(h, g, w, *tabs)


def _sink_attend(q_lo_hi, pieces, sinks):
    outs = []
    for qh, sink in zip(q_lo_hi, sinks):
        ss = []
        for k, _, mask in pieces:
            s = _dot_nt(qh, k)
            if mask is not None:
                s = jnp.where(mask, s, -jnp.inf)
            ss.append(s)
        m = sink
        for s in ss:
            m = jnp.maximum(m, jnp.max(s, axis=-1, keepdims=True))
        den = jnp.exp(sink - m)
        acc = None
        for s, (_, v, _) in zip(ss, pieces):
            p = jnp.exp(s - m)
            den = den + jnp.sum(p, axis=-1, keepdims=True)
            pv = _dot(p.astype(BF16), v)
            acc = pv if acc is None else acc + pv
        outs.append(acc / den)
    lane = lax.broadcasted_iota(I32, outs[0].shape, 1)
    return jnp.where(lane < GQA_HD, outs[0], outs[1])


def _split_halves(qg):
    lane = lax.broadcasted_iota(I32, qg.shape, 1)
    zero = jnp.zeros_like(qg)
    return jnp.where(lane < GQA_HD, qg, zero), jnp.where(lane >= GQA_HD, qg, zero)


def _win_attn_kernel(sink_ref, q_ref, kp_ref, kc_ref, kn_ref, vp_ref, vc_ref, vn_ref,
                     km_ref, vm_ref, o_ref, *, nb):
    j = pl.program_id(1)
    qi = lax.broadcasted_iota(I32, (BLK, BLK), 0)
    kj = lax.broadcasted_iota(I32, (BLK, BLK), 1)
    ok_p = (kj >= qi) & (j > 0)
    ok_n = (kj <= qi) & (j < nb - 1)
    for p in range(2):
        sl = slice(p * LANES, (p + 1) * LANES)
        pieces = [
            (km_ref[:, sl], vm_ref[:, sl], None),
            (kp_ref[:, sl], vp_ref[:, sl], ok_p),
            (kc_ref[:, sl], vc_ref[:, sl], None),
            (kn_ref[:, sl], vn_ref[:, sl], ok_n),
        ]
        for g in range(GQA_GROUP):
            grp = 4 * p + g
            gs = slice(grp * LANES, (grp + 1) * LANES)
            sinks = (sink_ref[(2 * p) * GQA_GROUP + g], sink_ref[(2 * p + 1) * GQA_GROUP + g])
            o = _sink_attend(_split_halves(q_ref[:, gs]), pieces, sinks)
            o_ref[:, gs] = o.astype(BF16)


def _win_attn_call(sink, q, k, v, *, batch, seq, n_real):
    ntot = q.shape[0]
    nb = seq // BLK
    mrow = n_real // N_META

    def qmap(b, j, s):
        return (b * nb + j, 0)

    def pmap(b, j, s):
        return (b * nb + jnp.maximum(j - 1, 0), 0)

    def nmap(b, j, s):
        return (b * nb + jnp.minimum(j + 1, nb - 1), 0)

    def mmap(b, j, s):
        return (mrow + b, 0)

    kv_spec = lambda f: pl.BlockSpec((BLK, 256), f)
    grid_spec = pltpu.PrefetchScalarGridSpec(
        num_scalar_prefetch=1,
        grid=(batch, nb),
        in_specs=[
            pl.BlockSpec((BLK, 1024), qmap),
            kv_spec(pmap), kv_spec(qmap), kv_spec(nmap),
            kv_spec(pmap), kv_spec(qmap), kv_spec(nmap),
            pl.BlockSpec((N_META, 256), mmap),
            pl.BlockSpec((N_META, 256), mmap),
        ],
        out_specs=pl.BlockSpec((BLK, 1024), qmap),
    )
    return pl.pallas_call(
        functools.partial(_win_attn_kernel, nb=nb),
        grid_spec=grid_spec,
        out_shape=jax.ShapeDtypeStruct((n_real, 1024), BF16),
        compiler_params=_params(("parallel", "parallel")),
        name="win_attn",
    )(sink, q, k, k, k, v, v, v, k, v)


def _win_meta_kernel(sink_ref, q_ref, km_ref, kc_ref, vm_ref, vc_ref, o_ref, *, batch):
    i = pl.program_id(0)

    @pl.when(i < batch)
    def _():
        qp = lax.broadcasted_iota(I32, (N_META, BLK), 0)
        kj = lax.broadcasted_iota(I32, (N_META, BLK), 1)
        ok = (N_META + kj - qp) <= BLK
        for p in range(2):
            sl = slice(p * LANES, (p + 1) * LANES)
            pieces = [(km_ref[:, sl], vm_ref[:, sl], None), (kc_ref[:, sl], vc_ref[:, sl], ok)]
            for g in range(GQA_GROUP):
                grp = 4 * p + g
                gs = slice(grp * LANES, (grp + 1) * LANES)
                sinks = (sink_ref[(2 * p) * GQA_GROUP + g], sink_ref[(2 * p + 1) * GQA_GROUP + g])
                o = _sink_attend(_split_halves(q_ref[:, gs]), pieces, sinks)
                o_ref[:, gs] = o.astype(BF16)

    @pl.when(i >= batch)
    def _():
        o_ref[...] = jnp.zeros_like(o_ref)


def _win_meta_call(sink, q, k, v, *, batch, seq, n_real):
    ntot = q.shape[0]
    nb = seq // BLK
    mrow = n_real // N_META
    ntail_blocks = (ntot - n_real) // N_META

    def tmap(i, s):
        return (mrow + i, 0)

    def cmap(i, s):
        return (jnp.minimum(i, batch - 1) * nb, 0)

    grid_spec = pltpu.PrefetchScalarGridSpec(
        num_scalar_prefetch=1,
        grid=(ntail_blocks,),
        in_specs=[
            pl.BlockSpec((N_META, 1024), tmap),
            pl.BlockSpec((N_META, 256), tmap),
            pl.BlockSpec((BLK, 256), cmap),
            pl.BlockSpec((N_META, 256), tmap),
            pl.BlockSpec((BLK, 256), cmap),
        ],
        out_specs=pl.BlockSpec((N_META, 1024), lambda i, s: (i, 0)),
    )
    return pl.pallas_call(
        functools.partial(_win_meta_kernel, batch=batch),
        grid_spec=grid_spec,
        out_shape=jax.ShapeDtypeStruct((ntot - n_real, 1024), BF16),
        compiler_params=_params(("parallel",)),
        name="win_attn_meta",
    )(sink, q, k, k, v, v)


def _attn_out_tile(o_real_ref, o_tail_ref, n_real_tiles):
    is_real = pl.program_id(0) < n_real_tiles
    return jnp.where(is_real, o_real_ref[...], o_tail_ref[...])


def _attn_out_specs(tm, n_real_tiles):
    return [
        pl.BlockSpec((tm, 1024), lambda i: (jnp.minimum(i, n_real_tiles - 1), 0)),
        pl.BlockSpec((tm, 1024), lambda i: (jnp.maximum(i - n_real_tiles, 0), 0)),
    ]


def _wo_kernel(o_real_ref, o_tail_ref, w_ref, h_ref, g_ref, hout_ref, xn_ref, *, n_real_tiles):
    o = _attn_out_tile(o_real_ref, o_tail_ref, n_real_tiles)
    hn = _dot(o, w_ref[...]) + h_ref[...]
    hout_ref[...] = hn
    xn_ref[...] = _rms(hn, g_ref[...]).astype(BF16)


def _wo_call(o_real, o_tail, w, h, g, *, tm):
    ntot = h.shape[0]
    nrt = o_real.shape[0] // tm
    row = lambda i: (i, 0)
    fix = lambda i: (0, 0)
    return pl.pallas_call(
        functools.partial(_wo_kernel, n_real_tiles=nrt),
        grid=(ntot // tm,),
        in_specs=_attn_out_specs(tm, nrt) + [
            pl.BlockSpec((1024, D_MODEL), fix),
            pl.BlockSpec((tm, D_MODEL), row),
            pl.BlockSpec((1, D_MODEL), fix),
        ],
        out_specs=[pl.BlockSpec((tm, D_MODEL), row), pl.BlockSpec((tm, D_MODEL), row)],
        out_shape=[jax.ShapeDtypeStruct((ntot, D_MODEL), F32),
                   jax.ShapeDtypeStruct((ntot, D_MODEL), BF16)],
        compiler_params=_params(("parallel",)),
        name="wo_res_norm",
    )(o_real, o_tail, w, h, g)


def _pack_bf16_pair(a, b):
    ra = pltpu.bitcast(a.astype(BF16).astype(F32), U32)
    rb = pltpu.bitcast(b.astype(BF16).astype(F32), U32)
    return ra | (rb >> 16)


def _unpack_bf16_pair(p):
    a = pltpu.bitcast(p & jnp.uint32(0xFFFF0000), F32).astype(BF16)
    b = pltpu.bitcast(p << 16, F32).astype(BF16)
    return a, b


def _wo_router_kernel(o_real_ref, o_tail_ref, w_ref, h_ref, g_ref, r_ref,
                      hout_ref, xpk_ref, route_ref, cnt_ref, carry_ref, *, n_real_tiles):
    i = pl.program_id(0)
    tm = h_ref.shape[0]

    @pl.when(i == 0)
    def _():
        carry_ref[...] = jnp.zeros_like(carry_ref)

    o = _attn_out_tile(o_real_ref, o_tail_ref, n_real_tiles)
    hn = _dot(o, w_ref[...]) + h_ref[...]
    hout_ref[...] = hn
    y = _rms(hn, g_ref[...])
    half = D_MODEL // 2
    xpk_ref[...] = _pack_bf16_pair(y[:, :half], y[:, half:])

    yhi = y.astype(BF16)
    ylo = (y - yhi.astype(F32)).astype(BF16)
    r2 = r_ref[...]
    t = _dot(yhi, r2)
    logits = t[:, :LANES] + t[:, LANES:] + _dot(ylo, r2[:, :LANES])

    lane = lax.broadcasted_iota(I32, (tm, LANES), 1)
    lm = jnp.where(lane < N_EXPERTS, logits, -jnp.inf)
    m1 = jnp.max(lm, axis=-1, keepdims=True)
    i1 = jnp.min(jnp.where(lm == m1, lane, LANES), axis=-1, keepdims=True)
    lm2 = jnp.where(lane == i1, -jnp.inf, lm)
    m2 = jnp.max(lm2, axis=-1, keepdims=True)
    i2 = jnp.min(jnp.where(lm2 == m2, lane, LANES), axis=-1, keepdims=True)
    e21 = jnp.exp(m2 - m1)
    g0 = 1.0 / (1.0 + e21)
    g1 = e21 / (1.0 + e21)

    onehot = ((lane == i1) | (lane == i2)).astype(F32)
    rr = lax.broadcasted_iota(I32, (tm, tm), 0)
    cc = lax.broadcasted_iota(I32, (tm, tm), 1)
    lower = jnp.where(cc < rr, 1.0, 0.0).astype(BF16)
    before = _dot(lower, onehot.astype(BF16)) + carry_ref[0:1, :]
    rank1 = jnp.sum(jnp.where(lane == i1, before, 0.0), axis=-1, keepdims=True)
    rank2 = jnp.sum(jnp.where(lane == i2, before, 0.0), axis=-1, keepdims=True)
    carry = carry_ref[0:1, :] + jnp.sum(onehot, axis=0, keepdims=True)
    carry_ref[...] = jnp.broadcast_to(carry, carry_ref.shape)
    cnt_ref[...] = jnp.broadcast_to(carry, cnt_ref.shape)

    out = jnp.where(lane == 0, i1.astype(F32), 0.0)
    out = jnp.where(lane == 1, i2.astype(F32), out)
    out = jnp.where(lane == 2, rank1, out)
    out = jnp.where(lane == 3, rank2, out)
    out = jnp.where(lane == 4, g0, out)
    out = jnp.where(lane == 5, g1, out)
    route_ref[...] = out


def _wo_router_call(o_real, o_tail, w, h, g, r2, *, tm):
    ntot = h.shape[0]
    nrt = o_real.shape[0] // tm
    row = lambda i: (i, 0)
    fix = lambda i: (0, 0)
    return pl.pallas_call(
        functools.partial(_wo_router_kernel, n_real_tiles=nrt),
        grid=(ntot // tm,),
        in_specs=_attn_out_specs(tm, nrt) + [
            pl.BlockSpec((1024, D_MODEL), fix),
            pl.BlockSpec((tm, D_MODEL), row),
            pl.BlockSpec((1, D_MODEL), fix),
            pl.BlockSpec((D_MODEL, 2 * LANES), fix),
        ],
        out_specs=[
            pl.BlockSpec((tm, D_MODEL), row),
            pl.BlockSpec((tm, D_MODEL // 2), row),
            pl.BlockSpec((tm, LANES), row),
            pl.BlockSpec((8, LANES), fix),
        ],
        out_shape=[
            jax.ShapeDtypeStruct((ntot, D_MODEL), F32),
            jax.ShapeDtypeStruct((ntot, D_MODEL // 2), U32),
            jax.ShapeDtypeStruct((ntot, LANES), F32),
            jax.ShapeDtypeStruct((8, LANES), F32),
        ],
        scratch_shapes=[pltpu.VMEM((8, LANES), F32)],
        compiler_params=_params(("arbitrary",)),
        name="wo_res_norm_router",
    )(o_real, o_tail, w, h, g, r2)


def _swiglu_chunk(x, wg, wu, wd):
    g = _dot(x, wg)
    u = _dot(x, wu)
    a = (g * (1.0 / (1.0 + jnp.exp(-g)))) * u
    return _dot(a.astype(BF16), wd)


def _ffn_dense_kernel(x_ref, wg_ref, wu_ref, wd_ref, h_ref, g_ref, hout_ref, xn_ref, acc_ref):
    c = pl.program_id(1)

    @pl.when(c == 0)
    def _():
        acc_ref[...] = jnp.zeros_like(acc_ref)

    acc_ref[...] += _swiglu_chunk(x_ref[...], wg_ref[...], wu_ref[...], wd_ref[...])

    @pl.when(c == pl.num_programs(1) - 1)
    def _():
        hn = h_ref[...] + acc_ref[...]
        hout_ref[...] = hn
        xn_ref[...] = _rms(hn, g_ref[...]).astype(BF16)


def _ffn_dense_call(x, wg, wu, wd, h, g, *, tm, tf):
    ntot = x.shape[0]
    fdim = wg.shape[1]
    row = lambda i, c: (i, 0)
    return pl.pallas_call(
        _ffn_dense_kernel,
        grid=(ntot // tm, fdim // tf),
        in_specs=[
            pl.BlockSpec((tm, D_MODEL), row),
            pl.BlockSpec((D_MODEL, tf), lambda i, c: (0, c)),
            pl.BlockSpec((D_MODEL, tf), lambda i, c: (0, c)),
            pl.BlockSpec((tf, D_MODEL), lambda i, c: (c, 0)),
            pl.BlockSpec((tm, D_MODEL), row),
            pl.BlockSpec((1, D_MODEL), lambda i, c: (0, 0)),
        ],
        out_specs=[pl.BlockSpec((tm, D_MODEL), row), pl.BlockSpec((tm, D_MODEL), row)],
        out_shape=[jax.ShapeDtypeStruct((ntot, D_MODEL), F32),
                   jax.ShapeDtypeStruct((ntot, D_MODEL), BF16)],
        scratch_shapes=[pltpu.VMEM((tm, D_MODEL), F32)],
        compiler_params=_params(("parallel", "arbitrary")),
        name="ffn_dense",
    )(x, wg, wu, wd, h, g)


def _ffn_expert_kernel(be_ref, nu_ref, xs_ref, wg_ref, wu_ref, wd_ref, ys_ref, acc_ref, xb_ref):
    del be_ref
    i = pl.program_id(0)
    c = pl.program_id(1)
    last = pl.num_programs(1) - 1
    used = i < nu_ref[0]

    @pl.when(used & (c == 0))
    def _():
        acc_ref[...] = jnp.zeros_like(acc_ref)
        a, b = _unpack_bf16_pair(xs_ref[...])
        half = D_MODEL // 2
        xb_ref[:, :half] = a
        xb_ref[:, half:] = b

    @pl.when(used)
    def _():
        acc_ref[...] += _swiglu_chunk(xb_ref[...], wg_ref[...], wu_ref[...], wd_ref[...])

    @pl.when(used & (c == last))
    def _():
        ys_ref[...] = acc_ref[...]

    @pl.when(jnp.logical_not(used) & (c == last))
    def _():
        ys_ref[...] = jnp.zeros_like(ys_ref)


def _ffn_expert_call(block_e, n_used, xs, wg, wu, wd, *, tm, tf):
    rows = xs.shape[0]
    fdim = wg.shape[2]
    nch = fdim // tf

    def cidx(i, c, nu):
        return jnp.where(i < nu[0], c, nch - 1)

    grid_spec = pltpu.PrefetchScalarGridSpec(
        num_scalar_prefetch=2,
        grid=(rows // tm, nch),
        in_specs=[
            pl.BlockSpec((tm, D_MODEL // 2), lambda i, c, be, nu: (i, 0)),
            pl.BlockSpec((None, D_MODEL, tf), lambda i, c, be, nu: (be[i], 0, cidx(i, c, nu))),
            pl.BlockSpec((None, D_MODEL, tf), lambda i, c, be, nu: (be[i], 0, cidx(i, c, nu))),
            pl.BlockSpec((None, tf, D_MODEL), lambda i, c, be, nu: (be[i], cidx(i, c, nu), 0)),
        ],
        out_specs=pl.BlockSpec((tm, D_MODEL), lambda i, c, be, nu: (i, 0)),
        scratch_shapes=[pltpu.VMEM((tm, D_MODEL), F32), pltpu.VMEM((tm, D_MODEL), BF16)],
    )
    return pl.pallas_call(
        _ffn_expert_kernel,
        grid_spec=grid_spec,
        out_shape=jax.ShapeDtypeStruct((rows, D_MODEL), F32),
        compiler_params=_params(("parallel", "arbitrary")),
        name="ffn_expert",
    )(block_e, n_used, xs, wg, wu, wd)


def _row_copy(src, dst, sem):
    return pltpu.make_async_copy(src, dst, sem)


def _dispatch_kernel(dest_ref, x_ref, xs_in_ref, xs_ref, sem):
    del xs_in_ref
    td = x_ref.shape[0]

    def issue(t, carry):
        for s in range(2):
            d = dest_ref[0, s, t]
            _row_copy(x_ref.at[pl.ds(t, 1)], xs_ref.at[pl.ds(d, 1)], sem).start()
        return carry

    lax.fori_loop(0, td, issue, 0)

    def drain(t, carry):
        for s in range(2):
            _row_copy(x_ref.at[pl.ds(0, 1)], xs_ref.at[pl.ds(0, 1)], sem).wait()
        return carry

    lax.fori_loop(0, td, drain, 0)


def _dispatch_call(dest, xpk, xs_init, *, td):
    ntot = xpk.shape[0]
    return pl.pallas_call(
        _dispatch_kernel,
        grid=(ntot // td,),
        in_specs=[
            pl.BlockSpec((1, 2, td), lambda i: (i, 0, 0), memory_space=pltpu.SMEM),
            pl.BlockSpec((td, D_MODEL // 2), lambda i: (i, 0)),
            pl.BlockSpec(memory_space=pl.ANY),
        ],
        out_specs=pl.BlockSpec(memory_space=pl.ANY),
        out_shape=jax.ShapeDtypeStruct(xs_init.shape, U32),
        input_output_aliases={2: 0},
        scratch_shapes=[pltpu.SemaphoreType.DMA(())],
        compiler_params=_params(("arbitrary",)),
        name="moe_dispatch",
    )(dest, xpk, xs_init)


def _combine_kernel(dest_ref, ys_ref, h_ref, route_ref, g_ref, out_ref, y0_ref, y1_ref, sem):
    tc = h_ref.shape[0]
    bufs = (y0_ref, y1_ref)

    def issue(t, carry):
        for s in range(2):
            d = dest_ref[0, s, t]
            _row_copy(ys_ref.at[pl.ds(d, 1)], bufs[s].at[pl.ds(t, 1)], sem).start()
        return carry

    lax.fori_loop(0, tc, issue, 0)

    def drain(t, carry):
        for s in range(2):
            _row_copy(ys_ref.at[pl.ds(0, 1)], bufs[s].at[pl.ds(0, 1)], sem).wait()
        return carry

    lax.fori_loop(0, tc, drain, 0)

    r = route_ref[...]
    lane = lax.broadcasted_iota(I32, r.shape, 1)
    g0 = jnp.sum(jnp.where(lane == 4, r, 0.0), axis=-1, keepdims=True)
    g1 = jnp.sum(jnp.where(lane == 5, r, 0.0), axis=-1, keepdims=True)
    hn = h_ref[...] + (y0_ref[...] * g0 + y1_ref[...] * g1)
    out_ref[...] = _rms(hn, g_ref[...])


def _combine_call(dest, ys, h, route, g, *, tc, tile0, ntiles):
    return pl.pallas_call(
        _combine_kernel,
        grid=(ntiles,),
        in_specs=[
            pl.BlockSpec((1, 2, tc), lambda i: (tile0 + i, 0, 0), memory_space=pltpu.SMEM),
            pl.BlockSpec(memory_space=pl.ANY),
            pl.BlockSpec((tc, D_MODEL), lambda i: (tile0 + i, 0)),
            pl.BlockSpec((tc, LANES), lambda i: (tile0 + i, 0)),
            pl.BlockSpec((1, D_MODEL), lambda i: (0, 0)),
        ],
        out_specs=pl.BlockSpec((tc, D_MODEL), lambda i: (i, 0)),
        out_shape=jax.ShapeDtypeStruct((ntiles * tc, D_MODEL), F32),
        scratch_shapes=[pltpu.VMEM((tc, D_MODEL), F32), pltpu.VMEM((tc, D_MODEL), F32),
                        pltpu.SemaphoreType.DMA(())],
        compiler_params=_params(("arbitrary",)),
        name="moe_combine_norm",
    )(dest, ys, h, route, g)


def _mla_proj_kernel(x_ref, wd_ref, qn_ref, kvn_ref, wuq_ref, wuk_ref, wuv_ref,
                     c_ref, s1_ref, s2_ref, q_ref, k_ref, v_ref):
    lat = _dot(x_ref[...], wd_ref[...])
    cq = _rms(lat[:, :MLA_Q_RANK], qn_ref[...]).astype(BF16)
    ckv = _rms(lat[:, MLA_Q_RANK:MLA_Q_RANK + MLA_KV_RANK], kvn_ref[...]).astype(BF16)
    c, s1, s2 = c_ref[...], s1_ref[...], s2_ref[...]
    half = MLA_ROPE // 2
    kr = _rope(lat[:, MLA_Q_RANK + MLA_KV_RANK:], c, s1, s2, half)
    q = _dot(cq, wuq_ref[...])
    k = _dot(ckv, wuk_ref[...])
    scale = (MLA_QK ** -0.5) * LOG2E
    for h in range(MLA_HEADS):
        sl = slice(h * LANES, (h + 1) * LANES)
        q_ref[:, sl] = (_rope(q[:, sl], c, s1, s2, half) * scale).astype(BF16)
        k_ref[:, sl] = (k[:, sl] + kr).astype(BF16)
    v_ref[...] = _dot(ckv, wuv_ref[...]).astype(BF16)


def _mla_proj_call(x, wd, qn, kvn, wuq, wuk, wuv, tabs, *, tm, n_real, seq):
    ntot = x.shape[0]
    nrt, tps = n_real // tm, seq // tm
    row = lambda i: (i, 0)
    fix = lambda i: (0, 0)
    tab_spec = pl.BlockSpec((tm, LANES), lambda i: (_table_index(i, nrt, tps), 0))
    return pl.pallas_call(
        _mla_proj_kernel,
        grid=(ntot // tm,),
        in_specs=[
            pl.BlockSpec((tm, D_MODEL), row),
            pl.BlockSpec(wd.shape, fix),
            pl.BlockSpec((1, MLA_Q_RANK), fix),
            pl.BlockSpec((1, MLA_KV_RANK), fix),
            pl.BlockSpec(wuq.shape, fix),
            pl.BlockSpec(wuk.shape, fix),
            pl.BlockSpec(wuv.shape, fix),
            tab_spec, tab_spec, tab_spec,
        ],
        out_specs=[
            pl.BlockSpec((tm, 2048), row),
            pl.BlockSpec((tm, 2048), row),
            pl.BlockSpec((tm, 1024), row),
        ],
        out_shape=[
            jax.ShapeDtypeStruct((ntot, 2048), BF16),
            jax.ShapeDtypeStruct((ntot, 2048), BF16),
            jax.ShapeDtypeStruct((ntot, 1024), BF16),
        ],
        compiler_params=_params(("parallel",)),
        name="mla_proj",
    )(x, wd, qn, kvn, wuq, wuk, wuv, *tabs)


def _mla_attend(q_ref, kr_ref, km_ref, vr_ref, vm_ref, o_ref):
    for pair in range(MLA_HEADS // 2):
        vs = slice(pair * LANES, (pair + 1) * LANES)
        v_r, v_m = vr_ref[:, vs], vm_ref[:, vs]
        outs = []
        for hh in range(2):
            h = 2 * pair + hh
            hs = slice(h * LANES, (h + 1) * LANES)
            qh = q_ref[:, hs]
            s_r = _dot_nt(qh, kr_ref[:, hs])
            s_m = _dot_nt(qh, km_ref[:, hs])
            m = jnp.maximum(jnp.max(s_r, axis=-1, keepdims=True), jnp.max(s_m, axis=-1, keepdims=True))
            p_r = jnp.exp2(s_r - m)
            p_m = jnp.exp2(s_m - m)
            den = jnp.sum(p_r, axis=-1, keepdims=True) + jnp.sum(p_m, axis=-1, keepdims=True)
            acc = _dot(p_r.astype(BF16), v_r) + _dot(p_m.astype(BF16), v_m)
            outs.append(acc / den)
        lane = lax.broadcasted_iota(I32, outs[0].shape, 1)
        o_ref[:, vs] = jnp.where(lane < MLA_V, outs[0], outs[1]).astype(BF16)


def _mla_attn_kernel(q_ref, kr_ref, km_ref, vr_ref, vm_ref, o_ref):
    _mla_attend(q_ref, kr_ref, km_ref, vr_ref, vm_ref, o_ref)


def _mla_attn_call(q, k, v, *, batch, seq, n_real, tq):
    ntot = q.shape[0]
    nq = seq // tq
    mrow = n_real // N_META
    qmap = lambda b, j: (b * nq + j, 0)
    rmap = lambda b, j: (b, 0)
    mmap = lambda b, j: (mrow + b, 0)
    return pl.pallas_call(
        _mla_attn_kernel,
        grid=(batch, nq),
        in_specs=[
            pl.BlockSpec((tq, 2048), qmap),
            pl.BlockSpec((seq, 2048), rmap),
            pl.BlockSpec((N_META, 2048), mmap),
            pl.BlockSpec((seq, 1024), rmap),
            pl.BlockSpec((N_META, 1024), mmap),
        ],
        out_specs=pl.BlockSpec((tq, 1024), qmap),
        out_shape=jax.ShapeDtypeStruct((n_real, 1024), BF16),
        compiler_params=_params(("parallel", "arbitrary")),
        name="mla_attn",
    )(q, k, k, v, v)


def _mla_meta_kernel(q_ref, kr_ref, km_ref, vr_ref, vm_ref, o_ref, *, batch):
    i = pl.program_id(0)

    @pl.when(i < batch)
    def _():
        _mla_attend(q_ref, kr_ref, km_ref, vr_ref, vm_ref, o_ref)

    @pl.when(i >= batch)
    def _():
        o_ref[...] = jnp.zeros_like(o_ref)


def _mla_meta_call(q, k, v, *, batch, seq, n_real):
    ntot = q.shape[0]
    mrow = n_real // N_META
    ntail_blocks = (ntot - n_real) // N_META
    tmap = lambda i: (mrow + i, 0)
    rmap = lambda i: (jnp.minimum(i, batch - 1), 0)
    return pl.pallas_call(
        functools.partial(_mla_meta_kernel, batch=batch),
        grid=(ntail_blocks,),
        in_specs=[
            pl.BlockSpec((N_META, 2048), tmap),
            pl.BlockSpec((seq, 2048), rmap),
            pl.BlockSpec((N_META, 2048), tmap),
            pl.BlockSpec((seq, 1024), rmap),
            pl.BlockSpec((N_META, 1024), tmap),
        ],
        out_specs=pl.BlockSpec((N_META, 1024), lambda i: (i, 0)),
        out_shape=jax.ShapeDtypeStruct((ntot - n_real, 1024), BF16),
        compiler_params=_params(("arbitrary",)),
        name="mla_attn_meta",
    )(q, k, k, v, v)


def _gqa_q_perm():
    cols = []
    for p in range(2):
        for g in range(GQA_GROUP):
            for half in range(2):
                head = (2 * p + half) * GQA_GROUP + g
                cols.extend(range(head * GQA_HD, (head + 1) * GQA_HD))
    return jnp.asarray(cols, dtype=I32)


def _prep_gqa(a_wqkv, a_wo):
    perm = _gqa_q_perm()
    nq = GQA_HEADS * GQA_HD
    wq = a_wqkv[:, :nq] * (GQA_HD ** -0.5)
    w = jnp.concatenate([wq[:, perm], a_wqkv[:, nq:]], axis=1).astype(BF16)
    return w, a_wo[perm, :].astype(BF16)


def _prep_mla(b_wdkv, b_wuq, b_wukv):
    d = b_wdkv.shape[0]
    lat = MLA_Q_RANK + MLA_KV_RANK
    z = lambda n: jnp.zeros((d, n), F32)
    wd = jnp.concatenate([b_wdkv[:, :lat], z(MLA_NOPE), b_wdkv[:, lat:], z(LANES - MLA_QK)], axis=1)
    wuq = b_wuq.reshape(MLA_Q_RANK, MLA_HEADS, MLA_QK)
    wuq = jnp.pad(wuq, ((0, 0), (0, 0), (0, LANES - MLA_QK))).reshape(MLA_Q_RANK, MLA_HEADS * LANES)
    wukv = b_wukv.reshape(MLA_KV_RANK, MLA_HEADS, MLA_NOPE + MLA_V)
    wuk = jnp.pad(wukv[:, :, :MLA_NOPE], ((0, 0), (0, 0), (0, LANES - MLA_NOPE)))
    wuk = wuk.reshape(MLA_KV_RANK, MLA_HEADS * LANES)
    wuv = wukv[:, :, MLA_NOPE:].reshape(MLA_KV_RANK, MLA_HEADS * MLA_V)
    return wd.astype(BF16), wuq.astype(BF16), wuk.astype(BF16), wuv.astype(BF16)


def _prep_router(m_router):
    hi = m_router.astype(BF16)
    lo = (m_router - hi.astype(F32)).astype(BF16)
    pad = lambda r: jnp.pad(r, ((0, 0), (0, LANES - N_EXPERTS)))
    return jnp.concatenate([pad(hi), pad(lo)], axis=1)


def _positions(seq, n_tail, n_meta_rows):
    real = N_META + jnp.arange(seq, dtype=F32)
    t = jnp.arange(n_tail)
    tail = jnp.where(t < n_meta_rows, t % N_META, 0).astype(F32)
    return jnp.concatenate([real, tail])


def _trunk_flat(x_groups, meta, norm_mix, norm_ffn, norm_final,
                a_wqkv, a_wo, a_sink,
                b_wdkv, b_qnorm, b_kvnorm, b_wuq, b_wukv, b_wo,
                f_wg, f_wu, f_wd,
                m_router, m_wg, m_wu, m_wd):
    seq = x_groups[0].shape[1]
    batches = [x.shape[0] for x in x_groups]
    batch = sum(batches)
    n_real = batch * seq
    n_meta_rows = batch * N_META
    big = _pick_tile(seq, (1024, 512, 256, 128))
    tp = min(big, 512)
    ntot = -(-(n_real + n_meta_rows) // big) * big
    n_tail = ntot - n_real

    tail = jnp.concatenate([
        jnp.broadcast_to(meta.astype(F32)[None], (batch, N_META, D_MODEL)).reshape(n_meta_rows, D_MODEL),
        jnp.zeros((n_tail - n_meta_rows, D_MODEL), F32)], axis=0)
    h = jnp.concatenate([x.reshape(-1, D_MODEL) for x in x_groups] + [tail], axis=0)

    pos = _positions(seq, n_tail, n_meta_rows)
    tabs_a = _rope_lane_tables(pos, GQA_THETA, GQA_ROT, GQA_HD, 0)
    tabs_b = _rope_lane_tables(pos, MLA_THETA, MLA_ROPE, LANES, MLA_NOPE)
    row = lambda v: v.reshape(1, -1).astype(F32)

    wqkv, wo_a = _prep_gqa(a_wqkv[0], a_wo[0])
    q, k, v = _qkv_call(h, row(norm_mix[0]), wqkv, tabs_a, tm=tp, n_real=n_real, seq=seq)
    sink = a_sink[0].astype(F32)
    o = _win_attn_call(sink, q, k, v, batch=batch, seq=seq, n_real=n_real)
    o_tail = _win_meta_call(sink, q, k, v, batch=batch, seq=seq, n_real=n_real)
    h, xn = _wo_call(o, o_tail, wo_a, h, row(norm_ffn[0]), tm=big)
    fdim = f_wg.shape[2]
    tf = _pick_tile(fdim, (1408, 512, 256, 128))
    h, xn = _ffn_dense_call(xn, f_wg[0].astype(BF16), f_wu[0].astype(BF16), f_wd[0].astype(BF16),
                            h, row(norm_mix[1]), tm=tp, tf=tf)

    wd, wuq, wuk, wuv = _prep_mla(b_wdkv[0], b_wuq[0], b_wukv[0])
    q, k, v = _mla_proj_call(xn, wd, row(b_qnorm[0]), row(b_kvnorm[0]), wuq, wuk, wuv, tabs_b,
                             tm=tp, n_real=n_real, seq=seq)
    tq = _pick_tile(seq, (256, 128))
    o = _mla_attn_call(q, k, v, batch=batch, seq=seq, n_real=n_real, tq=tq)
    o_tail = _mla_meta_call(q, k, v, batch=batch, seq=seq, n_real=n_real)
    tr = tp
    h, xpk, route, counts = _wo_router_call(o, o_tail, b_wo[0].astype(BF16), h, row(norm_ffn[1]),
                                            _prep_router(m_router[0]), tm=tr)

    tmx = big
    counts = counts[0, :N_EXPERTS].astype(I32)
    padded = ((counts + tmx - 1) // tmx) * tmx
    pend = jnp.cumsum(padded)
    pstart = pend - padded
    e_idx = route[:, 0:2].astype(I32)
    rank = route[:, 2:4].astype(I32)
    dest = pstart[e_idx] + rank
    n_assign = 2 * ntot
    n_blocks = -(-n_assign // tmx) + N_EXPERTS
    rows = n_blocks * tmx
    block_e = jnp.clip(jnp.searchsorted(pend, jnp.arange(n_blocks, dtype=I32) * tmx, side="right"),
                       0, N_EXPERTS - 1).astype(I32)
    n_used = (pend[-1] // tmx).astype(I32).reshape(1)
    block_e = jnp.where(jnp.arange(n_blocks) < n_used[0], block_e, block_e[jnp.maximum(n_used[0] - 1, 0)])

    td = tp
    dest_t = dest.reshape(ntot // td, td, 2).transpose(0, 2, 1)
    xs = _dispatch_call(dest_t, xpk, jnp.zeros((rows, D_MODEL // 2), U32), td=td)
    edim = m_wg.shape[3]
    tfx = _pick_tile(edim, (896, 512, 256, 128))
    ys = _ffn_expert_call(block_e, n_used, xs, m_wg[0].astype(BF16), m_wu[0].astype(BF16),
                          m_wd[0].astype(BF16), tm=tmx, tf=tfx)

    outs = []
    tile0 = 0
    for b, x in zip(batches, x_groups):
        ntiles = b * seq // td
        y = _combine_call(dest_t, ys, h, route, row(norm_final), tc=td, tile0=tile0, ntiles=ntiles)
        outs.append(y.reshape(b, seq, D_MODEL))
        tile0 += ntiles
    return tuple(outs)


def kernel(x_prompt, x_sample, meta, norm_mix, norm_ffn, norm_final, a_wqkv, a_wo, a_sink, b_wdkv, b_qnorm, b_kvnorm, b_wuq, b_wukv, b_wo, f_wg, f_wu, f_wd, m_router, m_wg, m_wu, m_wd):
    assert x_prompt.shape[1] == x_sample.shape[1] and x_prompt.shape[1] % BLK == 0
    assert norm_mix.shape[0] == 2, "two layers: windowed GQA + dense FFN, then MLA + MoE"
    return _trunk_flat((x_prompt, x_sample), meta, norm_mix, norm_ffn, norm_final,
                       a_wqkv, a_wo, a_sink,
                       b_wdkv, b_qnorm, b_kvnorm, b_wuq, b_wukv, b_wo,
                       f_wg, f_wu, f_wd,
                       m_router, m_wg, m_wu, m_wd)
```

```python
import functools
import math

import jax
import jax.numpy as jnp
from jax import lax
from jax.experimental import pallas as pl
from jax.experimental.pallas import tpu as pltpu

F32 = jnp.float32
BF16 = jnp.bfloat16
U32 = jnp.uint32
I32 = jnp.int32

D_MODEL = 1024
N_META = 16
RMS_EPS = 1e-6
BLK = 128
GQA_HEADS = 16
GQA_KV = 4
GQA_GROUP = 4
GQA_HD = 64
GQA_ROT = 16
GQA_THETA = 500000.0
MLA_HEADS = 16
MLA_NOPE = 64
MLA_ROPE = 32
MLA_V = 64
MLA_Q_RANK = 384
MLA_KV_RANK = 256
MLA_THETA = 10000.0
MLA_QK = MLA_NOPE + MLA_ROPE
N_EXPERTS = 8
LOG2E = 1.4426950408889634

LANES = 128
VMEM_LIMIT = 56 * 1024 * 1024


def _params(sem, vmem=VMEM_LIMIT):
    return pltpu.CompilerParams(dimension_semantics=sem, vmem_limit_bytes=vmem)


def _rms(x, g):
    ms = jnp.mean(x * x, axis=-1, keepdims=True)
    return x * lax.rsqrt(ms + RMS_EPS) * g


def _dot(a, b):
    return jnp.dot(a, b, preferred_element_type=F32)


def _dot_nt(a, b):
    return lax.dot_general(a, b, (((1,), (1,)), ((), ())), preferred_element_type=F32)


def _pick_tile(n, candidates):
    for c in candidates:
        if n % c == 0:
            return c
    raise ValueError(f"no tile in {candidates} divides {n}")


def _rope_lane_tables(pos, theta, rot, period, offset):
    half = rot // 2
    inv = jnp.power(jnp.float32(theta), -(jnp.arange(0, rot, 2, dtype=F32) / rot))
    ang = pos[:, None] * inv[None, :]
    cos, sin = jnp.cos(ang), jnp.sin(ang)
    lane = jnp.arange(LANES)
    r = (lane % period) - offset
    is_x1 = (r >= 0) & (r < half)
    is_x2 = (r >= half) & (r < rot)
    f = jnp.clip(jnp.where(is_x2, r - half, r), 0, half - 1)
    cos_l = cos[:, f]
    sin_l = sin[:, f]
    c = jnp.where((is_x1 | is_x2)[None, :], cos_l, 1.0)
    s1 = jnp.where(is_x2[None, :], sin_l, 0.0)
    s2 = jnp.where(is_x1[None, :], -sin_l, 0.0)
    return c.astype(F32), s1.astype(F32), s2.astype(F32)


def _rope(x, c, s1, s2, half):
    return x * c + pltpu.roll(x, half, 1) * s1 + pltpu.roll(x, LANES - half, 1) * s2


def _qkv_kernel(h_ref, g_ref, w_ref, c_ref, s1_ref, s2_ref, q_ref, k_ref, v_ref):
    xn = _rms(h_ref[...], g_ref[...]).astype(BF16)
    qkv = _dot(xn, w_ref[...])
    c, s1, s2 = c_ref[...], s1_ref[...], s2_ref[...]
    nq = GQA_HEADS * GQA_HD // LANES
    nk = GQA_KV
    for i in range(nq):
        q_ref[:, i * LANES:(i + 1) * LANES] = _rope(
            qkv[:, i * LANES:(i + 1) * LANES], c, s1, s2, GQA_ROT // 2).astype(BF16)
    for i in range(nk):
        lo = (nq + i) * LANES
        k_ref[:, i * LANES:(i + 1) * LANES] = _rope(
            qkv[:, lo:lo + LANES], c, s1, s2, GQA_ROT // 2).astype(BF16)
    v_ref[...] = qkv[:, (nq + nk) * LANES:].astype(BF16)


def _table_index(i, n_real_tiles, tiles_per_seq):
    return jnp.where(i < n_real_tiles, i % tiles_per_seq, tiles_per_seq + i - n_real_tiles)


def _qkv_call(h, g, w, tabs, *, tm, n_real, seq):
    ntot = h.shape[0]
    nrt, tps = n_real // tm, seq // tm
    tab_spec = pl.BlockSpec((tm, LANES), lambda i: (_table_index(i, nrt, tps), 0))
    return pl.pallas_call(
        _qkv_kernel,
        grid=(ntot // tm,),
        in_specs=[
            pl.BlockSpec((tm, D_MODEL), lambda i: (i, 0)),
            pl.BlockSpec((1, D_MODEL), lambda i: (0, 0)),
            pl.BlockSpec(w.shape, lambda i: (0, 0)),
            tab_spec, tab_spec, tab_spec,
        ],
        out_specs=[
            pl.BlockSpec((tm, 1024), lambda i: (i, 0)),
            pl.BlockSpec((tm, GQA_KV * LANES), lambda i: (i, 0)),
            pl.BlockSpec((tm, GQA_KV * LANES), lambda i: (i, 0)),
        ],
        out_shape=[
            jax.ShapeDtypeStruct((ntot, 1024), BF16),
            jax.ShapeDtypeStruct((ntot, GQA_KV * LANES), BF16),
            jax.ShapeDtypeStruct((ntot, GQA_KV * LANES), BF16),
        ],
        compiler_params=_params(("parallel",)),
        name="qkv_rope",
    )(h, g, w, *tabs)


def _gqa_attend(sink_ref, q_ref, o_ref, k_pieces, v_pieces, bias):
    nq = q_ref.shape[0]
    nkeys = bias.shape[1]
    have = sum(p.shape[0] for p in k_pieces)
    row = lax.broadcasted_iota(I32, (2 * nq, 1), 0)
    lane_o = lax.broadcasted_iota(I32, (2 * nq, LANES), 1)
    lane_k = lax.broadcasted_iota(I32, (nkeys, LANES), 1)
    pad = [jnp.zeros((nkeys - have, LANES), BF16)] if nkeys > have else []
    for kv in range(GQA_KV):
        sl = slice(kv * LANES, (kv + 1) * LANES)
        kcat = jnp.concatenate([p[:, sl] for p in k_pieces] + pad, axis=0)
        vcat = jnp.concatenate([p[:, sl] for p in v_pieces] + pad, axis=0)
        zero = jnp.zeros_like(kcat)
        k_par = (jnp.where(lane_k < GQA_HD, kcat, zero), jnp.where(lane_k >= GQA_HD, kcat, zero))
        lo = slice(2 * kv * LANES, (2 * kv + 1) * LANES)
        hi = slice((2 * kv + 1) * LANES, (2 * kv + 2) * LANES)
        qp = jnp.concatenate([q_ref[:, lo], q_ref[:, hi]], axis=0)
        outs = []
        for par in range(2):
            head = kv * GQA_GROUP + par
            sink = jnp.where(row < nq, sink_ref[head], sink_ref[head + 2])
            s = _dot_nt(qp, k_par[par]) + bias
            m = jnp.maximum(jnp.max(s, axis=-1, keepdims=True), sink)
            p = jnp.exp(s - m)
            den = jnp.sum(p, axis=-1, keepdims=True) + jnp.exp(sink - m)
            outs.append(_dot(p.astype(BF16), vcat) / den)
        o = jnp.where(lane_o < GQA_HD, outs[0], outs[1]).astype(BF16)
        o_ref[:, lo] = o[:nq]
        o_ref[:, hi] = o[nq:]


def _win_attn_kernel(sink_ref, q_ref, kp_ref, kc_ref, kn_ref, vp_ref, vc_ref, vn_ref,
                     km_ref, vm_ref, o_ref, *, nb):
    j = pl.program_id(1)
    nkeys = 4 * BLK
    qi = lax.broadcasted_iota(I32, (2 * BLK, nkeys), 0) & (BLK - 1)
    c = lax.broadcasted_iota(I32, (2 * BLK, nkeys), 1)
    neg = jnp.float32(-jnp.inf)
    b_p = jnp.where((c >= qi) & (j > 0), 0.0, neg)
    b_n = jnp.where(((c - 2 * BLK) <= qi) & (j < nb - 1), 0.0, neg)
    b_m = jnp.where(c < 3 * BLK + N_META, 0.0, neg)
    bias = jnp.where(c < BLK, b_p, jnp.where(c < 2 * BLK, 0.0, jnp.where(c < 3 * BLK, b_n, b_m)))
    _gqa_attend(sink_ref, q_ref, o_ref, (kp_ref, kc_ref, kn_ref, km_ref),
                (vp_ref, vc_ref, vn_ref, vm_ref), bias)


def _win_attn_call(sink, q, k, v, *, batch, seq, n_real):
    ntot = q.shape[0]
    nb = seq // BLK
    mrow = n_real // N_META

    def qmap(b, j, s):
        return (b * nb + j, 0)

    def pmap(b, j, s):
        return (b * nb + jnp.maximum(j - 1, 0), 0)

    def nmap(b, j, s):
        return (b * nb + jnp.minimum(j + 1, nb - 1), 0)

    def mmap(b, j, s):
        return (mrow + b, 0)

    kvw = GQA_KV * LANES
    kv_spec = lambda f: pl.BlockSpec((BLK, kvw), f)
    grid_spec = pltpu.PrefetchScalarGridSpec(
        num_scalar_prefetch=1,
        grid=(batch, nb),
        in_specs=[
            pl.BlockSpec((BLK, 1024), qmap),
            kv_spec(pmap), kv_spec(qmap), kv_spec(nmap),
            kv_spec(pmap), kv_spec(qmap), kv_spec(nmap),
            pl.BlockSpec((N_META, kvw), mmap),
            pl.BlockSpec((N_META, kvw), mmap),
        ],
        out_specs=pl.BlockSpec((BLK, 1024), qmap),
    )
    return pl.pallas_call(
        functools.partial(_win_attn_kernel, nb=nb),
        grid_spec=grid_spec,
        out_shape=jax.ShapeDtypeStruct((n_real, 1024), BF16),
        compiler_params=_params(("parallel", "parallel")),
        name="win_attn",
    )(sink, q, k, k, k, v, v, v, k, v)


def _win_meta_kernel(sink_ref, q_ref, km_ref, kc_ref, vm_ref, vc_ref, o_ref, *, batch):
    i = pl.program_id(0)

    @pl.when(i < batch)
    def _():
        nkeys = 2 * BLK
        qp = lax.broadcasted_iota(I32, (2 * N_META, nkeys), 0) & (N_META - 1)
        c = lax.broadcasted_iota(I32, (2 * N_META, nkeys), 1)
        neg = jnp.float32(-jnp.inf)
        b_c = jnp.where((N_META + c - qp) <= BLK, 0.0, neg)
        bias = jnp.where(c < BLK, b_c, jnp.where(c < BLK + N_META, 0.0, neg))
        _gqa_attend(sink_ref, q_ref, o_ref, (kc_ref, km_ref), (vc_ref, vm_ref), bias)

    @pl.when(i >= batch)
    def _():
        o_ref[...] = jnp.zeros_like(o_ref)


def _win_meta_call(sink, q, k, v, *, batch, seq, n_real):
    ntot = q.shape[0]
    nb = seq // BLK
    mrow = n_real // N_META
    ntail_blocks = (ntot - n_real) // N_META

    def tmap(i, s):
        return (mrow + i, 0)

    def cmap(i, s):
        return (jnp.minimum(i, batch - 1) * nb, 0)

    grid_spec = pltpu.PrefetchScalarGridSpec(
        num_scalar_prefetch=1,
        grid=(ntail_blocks,),
        in_specs=[
            pl.BlockSpec((N_META, 1024), tmap),
            pl.BlockSpec((N_META, GQA_KV * LANES), tmap),
            pl.BlockSpec((BLK, GQA_KV * LANES), cmap),
            pl.BlockSpec((N_META, GQA_KV * LANES), tmap),
            pl.BlockSpec((BLK, GQA_KV * LANES), cmap),
        ],
        out_specs=pl.BlockSpec((N_META, 1024), lambda i, s: (i, 0)),
    )
    return pl.pallas_call(
        functools.partial(_win_meta_kernel, batch=batch),
        grid_spec=grid_spec,
        out_shape=jax.ShapeDtypeStruct((ntot - n_real, 1024), BF16),
        compiler_params=_params(("parallel",)),
        name="win_attn_meta",
    )(sink, q, k, k, v, v)


def _attn_out_tile(o_real_ref, o_tail_ref, n_real_tiles):
    is_real = pl.program_id(0) < n_real_tiles
    return jnp.where(is_real, o_real_ref[...], o_tail_ref[...])


def _attn_out_specs(tm, n_real_tiles):
    return [
        pl.BlockSpec((tm, 1024), lambda i: (jnp.minimum(i, n_real_tiles - 1), 0)),
        pl.BlockSpec((tm, 1024), lambda i: (jnp.maximum(i - n_real_tiles, 0), 0)),
    ]


def _wo_kernel(o_real_ref, o_tail_ref, w_ref, h_ref, g_ref, hout_ref, xn_ref, *, n_real_tiles):
    o = _attn_out_tile(o_real_ref, o_tail_ref, n_real_tiles)
    hn = _dot(o, w_ref[...]) + h_ref[...]
    hout_ref[...] = hn
    xn_ref[...] = _rms(hn, g_ref[...]).astype(BF16)


def _wo_call(o_real, o_tail, w, h, g, *, tm):
    ntot = h.shape[0]
    nrt = o_real.shape[0] // tm
    row = lambda i: (i, 0)
    fix = lambda i: (0, 0)
    return pl.pallas_call(
        functools.partial(_wo_kernel, n_real_tiles=nrt),
        grid=(ntot // tm,),
        in_specs=_attn_out_specs(tm, nrt) + [
            pl.BlockSpec((1024, D_MODEL), fix),
            pl.BlockSpec((tm, D_MODEL), row),
            pl.BlockSpec((1, D_MODEL), fix),
        ],
        out_specs=[pl.BlockSpec((tm, D_MODEL), row), pl.BlockSpec((tm, D_MODEL), row)],
        out_shape=[jax.ShapeDtypeStruct((ntot, D_MODEL), F32),
                   jax.ShapeDtypeStruct((ntot, D_MODEL), BF16)],
        compiler_params=_params(("parallel",)),
        name="wo_res_norm",
    )(o_real, o_tail, w, h, g)


def _pack_bf16_pair(a, b):
    ra = pltpu.bitcast(a.astype(BF16).astype(F32), U32)
    rb = pltpu.bitcast(b.astype(BF16).astype(F32), U32)
    return ra | (rb >> 16)


def _unpack_bf16_pair(p):
    a = pltpu.bitcast(p & jnp.uint32(0xFFFF0000), F32).astype(BF16)
    b = pltpu.bitcast(p << 16, F32).astype(BF16)
    return a, b


def _wo_router_kernel(o_real_ref, o_tail_ref, w_ref, h_ref, g_ref, r_ref,
                      hout_ref, xpk_ref, route_ref, cnt_ref, carry_ref, *, n_real_tiles):
    i = pl.program_id(0)
    tm = h_ref.shape[0]

    @pl.when(i == 0)
    def _():
        carry_ref[...] = jnp.zeros_like(carry_ref)

    o = _attn_out_tile(o_real_ref, o_tail_ref, n_real_tiles)
    hn = _dot(o, w_ref[...]) + h_ref[...]
    hout_ref[...] = hn
    y = _rms(hn, g_ref[...])
    half = D_MODEL // 2
    xpk_ref[...] = _pack_bf16_pair(y[:, :half], y[:, half:])

    yhi = y.astype(BF16)
    ylo = (y - yhi.astype(F32)).astype(BF16)
    r2 = r_ref[...]
    t = _dot(yhi, r2)
    logits = t[:, :LANES] + t[:, LANES:] + _dot(ylo, r2[:, :LANES])

    lane = lax.broadcasted_iota(I32, (tm, LANES), 1)
    lm = jnp.where(lane < N_EXPERTS, logits, -jnp.inf)
    m1 = jnp.max(lm, axis=-1, keepdims=True)
    i1 = jnp.min(jnp.where(lm == m1, lane, LANES), axis=-1, keepdims=True)
    lm2 = jnp.where(lane == i1, -jnp.inf, lm)
    m2 = jnp.max(lm2, axis=-1, keepdims=True)
    i2 = jnp.min(jnp.where(lm2 == m2, lane, LANES), axis=-1, keepdims=True)
    e21 = jnp.exp(m2 - m1)
    g0 = 1.0 / (1.0 + e21)
    g1 = e21 / (1.0 + e21)

    onehot = ((lane == i1) | (lane == i2)).astype(F32)
    rr = lax.broadcasted_iota(I32, (tm, tm), 0)
    cc = lax.broadcasted_iota(I32, (tm, tm), 1)
    lower = jnp.where(cc < rr, 1.0, 0.0).astype(BF16)
    before = _dot(lower, onehot.astype(BF16)) + carry_ref[0:1, :]
    rank1 = jnp.sum(jnp.where(lane == i1, before, 0.0), axis=-1, keepdims=True)
    rank2 = jnp.sum(jnp.where(lane == i2, before, 0.0), axis=-1, keepdims=True)
    carry = carry_ref[0:1, :] + jnp.sum(onehot, axis=0, keepdims=True)
    carry_ref[...] = jnp.broadcast_to(carry, carry_ref.shape)
    cnt_ref[...] = jnp.broadcast_to(carry, cnt_ref.shape)

    out = jnp.where(lane == 0, i1.astype(F32), 0.0)
    out = jnp.where(lane == 1, i2.astype(F32), out)
    out = jnp.where(lane == 2, rank1, out)
    out = jnp.where(lane == 3, rank2, out)
    out = jnp.where(lane == 4, g0, out)
    out = jnp.where(lane == 5, g1, out)
    route_ref[...] = out


def _wo_router_call(o_real, o_tail, w, h, g, r2, *, tm):
    ntot = h.shape[0]
    nrt = o_real.shape[0] // tm
    row = lambda i: (i, 0)
    fix = lambda i: (0, 0)
    return pl.pallas_call(
        functools.partial(_wo_router_kernel, n_real_tiles=nrt),
        grid=(ntot // tm,),
        in_specs=_attn_out_specs(tm, nrt) + [
            pl.BlockSpec((1024, D_MODEL), fix),
            pl.BlockSpec((tm, D_MODEL), row),
            pl.BlockSpec((1, D_MODEL), fix),
            pl.BlockSpec((D_MODEL, 2 * LANES), fix),
        ],
        out_specs=[
            pl.BlockSpec((tm, D_MODEL), row),
            pl.BlockSpec((tm, D_MODEL // 2), row),
            pl.BlockSpec((tm, LANES), row),
            pl.BlockSpec((8, LANES), fix),
        ],
        out_shape=[
            jax.ShapeDtypeStruct((ntot, D_MODEL), F32),
            jax.ShapeDtypeStruct((ntot, D_MODEL // 2), U32),
            jax.ShapeDtypeStruct((ntot, LANES), F32),
            jax.ShapeDtypeStruct((8, LANES), F32),
        ],
        scratch_shapes=[pltpu.VMEM((8, LANES), F32)],
        compiler_params=_params(("arbitrary",)),
        name="wo_res_norm_router",
    )(o_real, o_tail, w, h, g, r2)


def _swiglu_chunk(x, wg, wu, wd):
    g = _dot(x, wg)
    u = _dot(x, wu)
    a = (g * (1.0 / (1.0 + jnp.exp(-g)))) * u
    return _dot(a.astype(BF16), wd)


def _ffn_dense_kernel(x_ref, wg_ref, wu_ref, wd_ref, h_ref, g_ref, hout_ref, xn_ref, *, tf):
    x = x_ref[...]
    hn = h_ref[...]
    for c in range(wg_ref.shape[1] // tf):
        cs = slice(c * tf, (c + 1) * tf)
        hn = hn + _swiglu_chunk(x, wg_ref[:, cs], wu_ref[:, cs], wd_ref[cs, :])
    hout_ref[...] = hn
    xn_ref[...] = _rms(hn, g_ref[...]).astype(BF16)


def _ffn_dense_call(x, wg, wu, wd, h, g, *, tm, tf):
    ntot = x.shape[0]
    fdim = wg.shape[1]
    row = lambda i: (i, 0)
    fix = lambda i: (0, 0)
    once = pl.Buffered(1)
    return pl.pallas_call(
        functools.partial(_ffn_dense_kernel, tf=tf),
        grid=(ntot // tm,),
        in_specs=[
            pl.BlockSpec((tm, D_MODEL), row),
            pl.BlockSpec((D_MODEL, fdim), fix, pipeline_mode=once),
            pl.BlockSpec((D_MODEL, fdim), fix, pipeline_mode=once),
            pl.BlockSpec((fdim, D_MODEL), fix, pipeline_mode=once),
            pl.BlockSpec((tm, D_MODEL), row),
            pl.BlockSpec((1, D_MODEL), fix),
        ],
        out_specs=[pl.BlockSpec((tm, D_MODEL), row), pl.BlockSpec((tm, D_MODEL), row)],
        out_shape=[jax.ShapeDtypeStruct((ntot, D_MODEL), F32),
                   jax.ShapeDtypeStruct((ntot, D_MODEL), BF16)],
        compiler_params=_params(("parallel",)),
        name="ffn_dense",
    )(x, wg, wu, wd, h, g)


def _ffn_expert_kernel(be_ref, nu_ref, xs_ref, wg_ref, wu_ref, wd_ref, ys_ref, acc_ref, xb_ref):
    del be_ref
    i = pl.program_id(0)
    c = pl.program_id(1)
    last = pl.num_programs(1) - 1
    used = i < nu_ref[0]

    @pl.when(used & (c == 0))
    def _():
        acc_ref[...] = jnp.zeros_like(acc_ref)
        a, b = _unpack_bf16_pair(xs_ref[...])
        half = D_MODEL // 2
        xb_ref[:, :half] = a
        xb_ref[:, half:] = b

    @pl.when(used)
    def _():
        acc_ref[...] += _swiglu_chunk(xb_ref[...], wg_ref[...], wu_ref[...], wd_ref[...])

    @pl.when(used & (c == last))
    def _():
        ys_ref[...] = acc_ref[...]

    @pl.when(jnp.logical_not(used) & (c == last))
    def _():
        ys_ref[...] = jnp.zeros_like(ys_ref)


def _ffn_expert_call(block_e, n_used, xs, wg, wu, wd, *, tm, tf):
    rows = xs.shape[0]
    fdim = wg.shape[2]
    nch = fdim // tf

    def cidx(i, c, nu):
        return jnp.where(i < nu[0], c, nch - 1)

    grid_spec = pltpu.PrefetchScalarGridSpec(
        num_scalar_prefetch=2,
        grid=(rows // tm, nch),
        in_specs=[
            pl.BlockSpec((tm, D_MODEL // 2), lambda i, c, be, nu: (i, 0)),
            pl.BlockSpec((None, D_MODEL, tf), lambda i, c, be, nu: (be[i], 0, cidx(i, c, nu))),
            pl.BlockSpec((None, D_MODEL, tf), lambda i, c, be, nu: (be[i], 0, cidx(i, c, nu))),
            pl.BlockSpec((None, tf, D_MODEL), lambda i, c, be, nu: (be[i], cidx(i, c, nu), 0)),
        ],
        out_specs=pl.BlockSpec((tm, D_MODEL), lambda i, c, be, nu: (i, 0)),
        scratch_shapes=[pltpu.VMEM((tm, D_MODEL), F32), pltpu.VMEM((tm, D_MODEL), BF16)],
    )
    return pl.pallas_call(
        _ffn_expert_kernel,
        grid_spec=grid_spec,
        out_shape=jax.ShapeDtypeStruct((rows, D_MODEL), F32),
        compiler_params=_params(("parallel", "arbitrary")),
        name="ffn_expert",
    )(block_e, n_used, xs, wg, wu, wd)


DMA_UNROLL = 8


def _row_copy(src, dst, sem):
    return pltpu.make_async_copy(src, dst, sem)


def _dispatch_kernel(dest_ref, x_ref, xs_in_ref, xs_ref, sem):
    del xs_in_ref
    td = x_ref.shape[0]

    def issue(t, carry):
        for s in range(2):
            d = dest_ref[0, s, t]
            _row_copy(x_ref.at[pl.ds(t, 1)], xs_ref.at[pl.ds(d, 1)], sem).start(priority=s)
        return carry

    lax.fori_loop(0, td, issue, 0, unroll=DMA_UNROLL)

    def drain(t, carry):
        for s in range(2):
            _row_copy(x_ref.at[pl.ds(0, 1)], xs_ref.at[pl.ds(0, 1)], sem).wait()
        return carry

    lax.fori_loop(0, td, drain, 0, unroll=DMA_UNROLL)


def _dispatch_call(dest, xpk, xs_init, *, td):
    ntot = xpk.shape[0]
    return pl.pallas_call(
        _dispatch_kernel,
        grid=(ntot // td,),
        in_specs=[
            pl.BlockSpec((1, 2, td), lambda i: (i, 0, 0), memory_space=pltpu.SMEM),
            pl.BlockSpec((td, D_MODEL // 2), lambda i: (i, 0)),
            pl.BlockSpec(memory_space=pl.ANY),
        ],
        out_specs=pl.BlockSpec(memory_space=pl.ANY),
        out_shape=jax.ShapeDtypeStruct(xs_init.shape, U32),
        input_output_aliases={2: 0},
        scratch_shapes=[pltpu.SemaphoreType.DMA(())],
        compiler_params=_params(("arbitrary",)),
        name="moe_dispatch",
    )(dest, xpk, xs_init)


def _combine_kernel(dest_ref, ys_ref, h_ref, route_ref, g_ref, out_ref, y0_ref, y1_ref, sem):
    tc = h_ref.shape[0]
    bufs = (y0_ref, y1_ref)

    def issue(t, carry):
        for s in range(2):
            d = dest_ref[0, s, t]
            _row_copy(ys_ref.at[pl.ds(d, 1)], bufs[s].at[pl.ds(t, 1)], sem).start(priority=s)
        return carry

    lax.fori_loop(0, tc, issue, 0, unroll=DMA_UNROLL)

    def drain(t, carry):
        for s in range(2):
            _row_copy(ys_ref.at[pl.ds(0, 1)], bufs[s].at[pl.ds(0, 1)], sem).wait()
        return carry

    lax.fori_loop(0, tc, drain, 0, unroll=DMA_UNROLL)

    r = route_ref[...]
    lane = lax.broadcasted_iota(I32, r.shape, 1)
    g0 = jnp.sum(jnp.where(lane == 4, r, 0.0), axis=-1, keepdims=True)
    g1 = jnp.sum(jnp.where(lane == 5, r, 0.0), axis=-1, keepdims=True)
    hn = h_ref[...] + (y0_ref[...] * g0 + y1_ref[...] * g1)
    out_ref[...] = _rms(hn, g_ref[...])


def _combine_call(dest, ys, h, route, g, *, tc, tile0, ntiles):
    return pl.pallas_call(
        _combine_kernel,
        grid=(ntiles,),
        in_specs=[
            pl.BlockSpec((1, 2, tc), lambda i: (tile0 + i, 0, 0), memory_space=pltpu.SMEM),
            pl.BlockSpec(memory_space=pl.ANY),
            pl.BlockSpec((tc, D_MODEL), lambda i: (tile0 + i, 0)),
            pl.BlockSpec((tc, LANES), lambda i: (tile0 + i, 0)),
            pl.BlockSpec((1, D_MODEL), lambda i: (0, 0)),
        ],
        out_specs=pl.BlockSpec((tc, D_MODEL), lambda i: (i, 0)),
        out_shape=jax.ShapeDtypeStruct((ntiles * tc, D_MODEL), F32),
        scratch_shapes=[pltpu.VMEM((tc, D_MODEL), F32), pltpu.VMEM((tc, D_MODEL), F32),
                        pltpu.SemaphoreType.DMA(())],
        compiler_params=_params(("arbitrary",)),
        name="moe_combine_norm",
    )(dest, ys, h, route, g)


def _mla_proj_kernel(x_ref, wd_ref, qn_ref, kvn_ref, wuq_ref, wuk_ref, wuv_ref,
                     c_ref, s1_ref, s2_ref, q_ref, k_ref, v_ref):
    lat = _dot(x_ref[...], wd_ref[...])
    cq = _rms(lat[:, :MLA_Q_RANK], qn_ref[...]).astype(BF16)
    ckv = _rms(lat[:, MLA_Q_RANK:MLA_Q_RANK + MLA_KV_RANK], kvn_ref[...]).astype(BF16)
    c, s1, s2 = c_ref[...], s1_ref[...], s2_ref[...]
    half = MLA_ROPE // 2
    kr = _rope(lat[:, MLA_Q_RANK + MLA_KV_RANK:], c, s1, s2, half)
    q = _dot(cq, wuq_ref[...])
    k = _dot(ckv, wuk_ref[...])
    scale = (MLA_QK ** -0.5) * LOG2E
    for h in range(MLA_HEADS):
        sl = slice(h * LANES, (h + 1) * LANES)
        q_ref[:, sl] = (_rope(q[:, sl], c, s1, s2, half) * scale).astype(BF16)
        k_ref[:, sl] = (k[:, sl] + kr).astype(BF16)
    v_ref[...] = _dot(ckv, wuv_ref[...]).astype(BF16)


def _mla_proj_call(x, wd, qn, kvn, wuq, wuk, wuv, tabs, *, tm, n_real, seq):
    ntot = x.shape[0]
    nrt, tps = n_real // tm, seq // tm
    row = lambda i: (i, 0)
    fix = lambda i: (0, 0)
    tab_spec = pl.BlockSpec((tm, LANES), lambda i: (_table_index(i, nrt, tps), 0))
    return pl.pallas_call(
        _mla_proj_kernel,
        grid=(ntot // tm,),
        in_specs=[
            pl.BlockSpec((tm, D_MODEL), row),
            pl.BlockSpec(wd.shape, fix),
            pl.BlockSpec((1, MLA_Q_RANK), fix),
            pl.BlockSpec((1, MLA_KV_RANK), fix),
            pl.BlockSpec(wuq.shape, fix),
            pl.BlockSpec(wuk.shape, fix),
            pl.BlockSpec(wuv.shape, fix),
            tab_spec, tab_spec, tab_spec,
        ],
        out_specs=[
            pl.BlockSpec((tm, 2048), row),
            pl.BlockSpec((tm, 2048), row),
            pl.BlockSpec((tm, 1024), row),
        ],
        out_shape=[
            jax.ShapeDtypeStruct((ntot, 2048), BF16),
            jax.ShapeDtypeStruct((ntot, 2048), BF16),
            jax.ShapeDtypeStruct((ntot, 1024), BF16),
        ],
        compiler_params=_params(("parallel",)),
        name="mla_proj",
    )(x, wd, qn, kvn, wuq, wuk, wuv, *tabs)


def _mla_attend(q_ref, kr_ref, km_ref, vr_ref, vm_ref, o_ref):
    for pair in range(MLA_HEADS // 2):
        vs = slice(pair * LANES, (pair + 1) * LANES)
        v_r, v_m = vr_ref[:, vs], vm_ref[:, vs]
        outs = []
        for hh in range(2):
            h = 2 * pair + hh
            hs = slice(h * LANES, (h + 1) * LANES)
            qh = q_ref[:, hs]
            s_r = _dot_nt(qh, kr_ref[:, hs])
            s_m = _dot_nt(qh, km_ref[:, hs])
            m = jnp.maximum(jnp.max(s_r, axis=-1, keepdims=True), jnp.max(s_m, axis=-1, keepdims=True))
            p_r = jnp.exp2(s_r - m)
            p_m = jnp.exp2(s_m - m)
            den = jnp.sum(p_r, axis=-1, keepdims=True) + jnp.sum(p_m, axis=-1, keepdims=True)
            acc = _dot(p_r.astype(BF16), v_r) + _dot(p_m.astype(BF16), v_m)
            outs.append(acc / den)
        lane = lax.broadcasted_iota(I32, outs[0].shape, 1)
        o_ref[:, vs] = jnp.where(lane < MLA_V, outs[0], outs[1]).astype(BF16)


def _mla_attn_kernel(q_ref, kr_ref, km_ref, vr_ref, vm_ref, o_ref):
    _mla_attend(q_ref, kr_ref, km_ref, vr_ref, vm_ref, o_ref)


def _mla_attn_call(q, k, v, *, batch, seq, n_real, tq):
    ntot = q.shape[0]
    nq = seq // tq
    mrow = n_real // N_META
    qmap = lambda b, j: (b * nq + j, 0)
    rmap = lambda b, j: (b, 0)
    mmap = lambda b, j: (mrow + b, 0)
    return pl.pallas_call(
        _mla_attn_kernel,
        grid=(batch, nq),
        in_specs=[
            pl.BlockSpec((tq, 2048), qmap),
            pl.BlockSpec((seq, 2048), rmap),
            pl.BlockSpec((N_META, 2048), mmap),
            pl.BlockSpec((seq, 1024), rmap),
            pl.BlockSpec((N_META, 1024), mmap),
        ],
        out_specs=pl.BlockSpec((tq, 1024), qmap),
        out_shape=jax.ShapeDtypeStruct((n_real, 1024), BF16),
        compiler_params=_params(("parallel", "arbitrary")),
        name="mla_attn",
    )(q, k, k, v, v)


def _mla_meta_kernel(q_ref, kr_ref, km_ref, vr_ref, vm_ref, o_ref, *, batch):
    i = pl.program_id(0)

    @pl.when(i < batch)
    def _():
        _mla_attend(q_ref, kr_ref, km_ref, vr_ref, vm_ref, o_ref)

    @pl.when(i >= batch)
    def _():
        o_ref[...] = jnp.zeros_like(o_ref)


def _mla_meta_call(q, k, v, *, batch, seq, n_real):
    ntot = q.shape[0]
    mrow = n_real // N_META
    ntail_blocks = (ntot - n_real) // N_META
    tmap = lambda i: (mrow + i, 0)
    rmap = lambda i: (jnp.minimum(i, batch - 1), 0)
    return pl.pallas_call(
        functools.partial(_mla_meta_kernel, batch=batch),
        grid=(ntail_blocks,),
        in_specs=[
            pl.BlockSpec((N_META, 2048), tmap),
            pl.BlockSpec((seq, 2048), rmap),
            pl.BlockSpec((N_META, 2048), tmap),
            pl.BlockSpec((seq, 1024), rmap),
            pl.BlockSpec((N_META, 1024), tmap),
        ],
        out_specs=pl.BlockSpec((N_META, 1024), lambda i: (i, 0)),
        out_shape=jax.ShapeDtypeStruct((ntot - n_real, 1024), BF16),
        compiler_params=_params(("arbitrary",)),
        name="mla_attn_meta",
    )(q, k, k, v, v)


def _prep_gqa(a_wqkv, a_wo):
    d = a_wqkv.shape[0]
    nq = GQA_HEADS * GQA_HD
    nkv = GQA_KV * GQA_HD
    wq = a_wqkv[:, :nq] * (GQA_HD ** -0.5)

    def dup(w):
        w = w.reshape(d, GQA_KV, 1, GQA_HD)
        return jnp.broadcast_to(w, (d, GQA_KV, 2, GQA_HD)).reshape(d, 2 * nkv)

    w = jnp.concatenate([wq, dup(a_wqkv[:, nq:nq + nkv]), dup(a_wqkv[:, nq + nkv:])], axis=1)
    return w.astype(BF16), a_wo.astype(BF16)


def _prep_mla(b_wdkv, b_wuq, b_wukv):
    d = b_wdkv.shape[0]
    lat = MLA_Q_RANK + MLA_KV_RANK
    z = lambda n: jnp.zeros((d, n), F32)
    wd = jnp.concatenate([b_wdkv[:, :lat], z(MLA_NOPE), b_wdkv[:, lat:], z(LANES - MLA_QK)], axis=1)
    wuq = b_wuq.reshape(MLA_Q_RANK, MLA_HEADS, MLA_QK)
    wuq = jnp.pad(wuq, ((0, 0), (0, 0), (0, LANES - MLA_QK))).reshape(MLA_Q_RANK, MLA_HEADS * LANES)
    wukv = b_wukv.reshape(MLA_KV_RANK, MLA_HEADS, MLA_NOPE + MLA_V)
    wuk = jnp.pad(wukv[:, :, :MLA_NOPE], ((0, 0), (0, 0), (0, LANES - MLA_NOPE)))
    wuk = wuk.reshape(MLA_KV_RANK, MLA_HEADS * LANES)
    wuv = wukv[:, :, MLA_NOPE:].reshape(MLA_KV_RANK, MLA_HEADS * MLA_V)
    return wd.astype(BF16), wuq.astype(BF16), wuk.astype(BF16), wuv.astype(BF16)


def _prep_router(m_router):
    hi = m_router.astype(BF16)
    lo = (m_router - hi.astype(F32)).astype(BF16)
    pad = lambda r: jnp.pad(r, ((0, 0), (0, LANES - N_EXPERTS)))
    return jnp.concatenate([pad(hi), pad(lo)], axis=1)


def _positions(seq, n_tail, n_meta_rows):
    real = N_META + jnp.arange(seq, dtype=F32)
    t = jnp.arange(n_tail)
    tail = jnp.where(t < n_meta_rows, t % N_META, 0).astype(F32)
    return jnp.concatenate([real, tail])


def _trunk_flat(x_groups, meta, norm_mix, norm_ffn, norm_final,
                a_wqkv, a_wo, a_sink,
                b_wdkv, b_qnorm, b_kvnorm, b_wuq, b_wukv, b_wo,
                f_wg, f_wu, f_wd,
                m_router, m_wg, m_wu, m_wd):
    seq = x_groups[0].shape[1]
    batches = [x.shape[0] for x in x_groups]
    batch = sum(batches)
    n_real = batch * seq
    n_meta_rows = batch * N_META
    big = _pick_tile(seq, (1024, 512, 256, 128))
    tp = min(big, 512)
    ntot = -(-(n_real + n_meta_rows) // big) * big
    n_tail = ntot - n_real

    tail = jnp.concatenate([
        jnp.broadcast_to(meta.astype(F32)[None], (batch, N_META, D_MODEL)).reshape(n_meta_rows, D_MODEL),
        jnp.zeros((n_tail - n_meta_rows, D_MODEL), F32)], axis=0)
    h = jnp.concatenate([x.reshape(-1, D_MODEL) for x in x_groups] + [tail], axis=0)

    pos = _positions(seq, n_tail, n_meta_rows)
    tabs_a = _rope_lane_tables(pos, GQA_THETA, GQA_ROT, GQA_HD, 0)
    tabs_b = _rope_lane_tables(pos, MLA_THETA, MLA_ROPE, LANES, MLA_NOPE)
    row = lambda v: v.reshape(1, -1).astype(F32)

    wqkv, wo_a = _prep_gqa(a_wqkv[0], a_wo[0])
    q, k, v = _qkv_call(h, row(norm_mix[0]), wqkv, tabs_a, tm=tp, n_real=n_real, seq=seq)
    sink = a_sink[0].astype(F32)
    o = _win_attn_call(sink, q, k, v, batch=batch, seq=seq, n_real=n_real)
    o_tail = _win_meta_call(sink, q, k, v, batch=batch, seq=seq, n_real=n_real)
    h, xn = _wo_call(o, o_tail, wo_a, h, row(norm_ffn[0]), tm=big)
    fdim = f_wg.shape[2]
    tf = _pick_tile(fdim, (512, 256, 128))
    h, xn = _ffn_dense_call(xn, f_wg[0].astype(BF16), f_wu[0].astype(BF16), f_wd[0].astype(BF16),
                            h, row(norm_mix[1]), tm=tp, tf=tf)

    wd, wuq, wuk, wuv = _prep_mla(b_wdkv[0], b_wuq[0], b_wukv[0])
    q, k, v = _mla_proj_call(xn, wd, row(b_qnorm[0]), row(b_kvnorm[0]), wuq, wuk, wuv, tabs_b,
                             tm=tp, n_real=n_real, seq=seq)
    tq = _pick_tile(seq, (256, 128))
    o = _mla_attn_call(q, k, v, batch=batch, seq=seq, n_real=n_real, tq=tq)
    o_tail = _mla_meta_call(q, k, v, batch=batch, seq=seq, n_real=n_real)
    tr = tp
    h, xpk, route, counts = _wo_router_call(o, o_tail, b_wo[0].astype(BF16), h, row(norm_ffn[1]),
                                            _prep_router(m_router[0]), tm=tr)

    tmx = big
    counts = counts[0, :N_EXPERTS].astype(I32)
    padded = ((counts + tmx - 1) // tmx) * tmx
    pend = jnp.cumsum(padded)
    pstart = pend - padded
    e_idx = route[:, 0:2].astype(I32)
    rank = route[:, 2:4].astype(I32)
    dest = pstart[e_idx] + rank
    n_assign = 2 * ntot
    n_blocks = -(-n_assign // tmx) + N_EXPERTS
    rows = n_blocks * tmx
    block_e = jnp.clip(jnp.searchsorted(pend, jnp.arange(n_blocks, dtype=I32) * tmx, side="right"),
                       0, N_EXPERTS - 1).astype(I32)
    n_used = (pend[-1] // tmx).astype(I32).reshape(1)
    block_e = jnp.where(jnp.arange(n_blocks) < n_used[0], block_e, block_e[jnp.maximum(n_used[0] - 1, 0)])

    td = tp
    dest_t = dest.reshape(ntot // td, td, 2).transpose(0, 2, 1)
    xs = _dispatch_call(dest_t, xpk, jnp.zeros((rows, D_MODEL // 2), U32), td=td)
    edim = m_wg.shape[3]
    tfx = _pick_tile(edim, (512, 256, 128))
    ys = _ffn_expert_call(block_e, n_used, xs, m_wg[0].astype(BF16), m_wu[0].astype(BF16),
                          m_wd[0].astype(BF16), tm=tmx, tf=tfx)

    outs = []
    tile0 = 0
    for b, x in zip(batches, x_groups):
        ntiles = b * seq // td
        y = _combine_call(dest_t, ys, h, route, row(norm_final), tc=td, tile0=tile0, ntiles=ntiles)
        outs.append(y.reshape(b, seq, D_MODEL))
        tile0 += ntiles
    return tuple(outs)


def kernel(x_prompt, x_sample, meta, norm_mix, norm_ffn, norm_final, a_wqkv, a_wo, a_sink, b_wdkv, b_qnorm, b_kvnorm, b_wuq, b_wukv, b_wo, f_wg, f_wu, f_wd, m_router, m_wg, m_wu, m_wd):
    assert x_prompt.shape[1] == x_sample.shape[1] and x_prompt.shape[1] % BLK == 0
    assert norm_mix.shape[0] == 2, "two layers: windowed GQA + dense FFN, then MLA + MoE"
    return _trunk_flat((x_prompt, x_sample), meta, norm_mix, norm_ffn, norm_final,
                       a_wqkv, a_wo, a_sink,
                       b_wdkv, b_qnorm, b_kvnorm, b_wuq, b_wukv, b_wo,
                       f_wg, f_wu, f_wd,
                       m_router, m_wg, m_wu, m_wd)
```

```python
import functools
import math

import jax
import jax.numpy as jnp
from jax import lax
from jax.experimental import pallas as pl
from jax.experimental.pallas import tpu as pltpu

F32 = jnp.float32
BF16 = jnp.bfloat16
U32 = jnp.uint32
I32 = jnp.int32

D_MODEL = 1024
N_META = 16
RMS_EPS = 1e-6
BLK = 128
GQA_HEADS = 16
GQA_KV = 4
GQA_GROUP = 4
GQA_HD = 64
GQA_ROT = 16
GQA_THETA = 500000.0
MLA_HEADS = 16
MLA_NOPE = 64
MLA_ROPE = 32
MLA_V = 64
MLA_Q_RANK = 384
MLA_KV_RANK = 256
MLA_THETA = 10000.0
MLA_QK = MLA_NOPE + MLA_ROPE
N_EXPERTS = 8
LOG2E = 1.4426950408889634

LANES = 128
VMEM_LIMIT = 56 * 1024 * 1024


def _params(sem, vmem=VMEM_LIMIT):
    return pltpu.CompilerParams(dimension_semantics=sem, vmem_limit_bytes=vmem)


def _rms(x, g):
    ms = jnp.mean(x * x, axis=-1, keepdims=True)
    return x * lax.rsqrt(ms + RMS_EPS) * g


def _dot(a, b):
    return jnp.dot(a, b, preferred_element_type=F32)


def _dot_nt(a, b):
    return lax.dot_general(a, b, (((1,), (1,)), ((), ())), preferred_element_type=F32)


def _pick_tile(n, candidates):
    for c in candidates:
        if n % c == 0:
            return c
    raise ValueError(f"no tile in {candidates} divides {n}")


def _rope_lane_tables(pos, theta, rot, period, offset):
    half = rot // 2
    inv = jnp.power(jnp.float32(theta), -(jnp.arange(0, rot, 2, dtype=F32) / rot))
    ang = pos[:, None] * inv[None, :]
    cos, sin = jnp.cos(ang), jnp.sin(ang)
    lane = jnp.arange(LANES)
    r = (lane % period) - offset
    is_x1 = (r >= 0) & (r < half)
    is_x2 = (r >= half) & (r < rot)
    f = jnp.clip(jnp.where(is_x2, r - half, r), 0, half - 1)
    cos_l = cos[:, f]
    sin_l = sin[:, f]
    c = jnp.where((is_x1 | is_x2)[None, :], cos_l, 1.0)
    s1 = jnp.where(is_x2[None, :], sin_l, 0.0)
    s2 = jnp.where(is_x1[None, :], -sin_l, 0.0)
    return c.astype(F32), s1.astype(F32), s2.astype(F32)


def _rope(x, c, s1, s2, half):
    return x * c + pltpu.roll(x, half, 1) * s1 + pltpu.roll(x, LANES - half, 1) * s2


def _qkv_kernel(h_ref, g_ref, w_ref, c_ref, s1_ref, s2_ref, q_ref, k_ref, v_ref):
    xn = _rms(h_ref[...], g_ref[...]).astype(BF16)
    qkv = _dot(xn, w_ref[...])
    c, s1, s2 = c_ref[...], s1_ref[...], s2_ref[...]
    nq = GQA_HEADS * GQA_HD // LANES
    nk = GQA_KV
    for i in range(nq):
        q_ref[:, i * LANES:(i + 1) * LANES] = _rope(
            qkv[:, i * LANES:(i + 1) * LANES], c, s1, s2, GQA_ROT // 2).astype(BF16)
    for i in range(nk):
        lo = (nq + i) * LANES
        k_ref[:, i * LANES:(i + 1) * LANES] = _rope(
            qkv[:, lo:lo + LANES], c, s1, s2, GQA_ROT // 2).astype(BF16)
    v_ref[...] = qkv[:, (nq + nk) * LANES:].astype(BF16)


def _table_index(i, n_real_tiles, tiles_per_seq):
    return jnp.where(i < n_real_tiles, i % tiles_per_seq, tiles_per_seq + i - n_real_tiles)


def _qkv_call(h, g, w, tabs, *, tm, n_real, seq):
    ntot = h.shape[0]
    nrt, tps = n_real // tm, seq // tm
    tab_spec = pl.BlockSpec((tm, LANES), lambda i: (_table_index(i, nrt, tps), 0))
    return pl.pallas_call(
        _qkv_kernel,
        grid=(ntot // tm,),
        in_specs=[
            pl.BlockSpec((tm, D_MODEL), lambda i: (i, 0)),
            pl.BlockSpec((1, D_MODEL), lambda i: (0, 0)),
            pl.BlockSpec(w.shape, lambda i: (0, 0)),
            tab_spec, tab_spec, tab_spec,
        ],
        out_specs=[
            pl.BlockSpec((tm, 1024), lambda i: (i, 0)),
            pl.BlockSpec((tm, GQA_KV * LANES), lambda i: (i, 0)),
            pl.BlockSpec((tm, GQA_KV * LANES), lambda i: (i, 0)),
        ],
        out_shape=[
            jax.ShapeDtypeStruct((ntot, 1024), BF16),
            jax.ShapeDtypeStruct((ntot, GQA_KV * LANES), BF16),
            jax.ShapeDtypeStruct((ntot, GQA_KV * LANES), BF16),
        ],
        compiler_params=_params(("parallel",)),
        name="qkv_rope",
    )(h, g, w, *tabs)


def _gqa_attend(sink_ref, q_ref, o_ref, k_pieces, v_pieces, bias):
    nq = q_ref.shape[0]
    nkeys = bias.shape[1]
    have = sum(p.shape[0] for p in k_pieces)
    row = lax.broadcasted_iota(I32, (2 * nq, 1), 0)
    lane_o = lax.broadcasted_iota(I32, (2 * nq, LANES), 1)
    lane_k = lax.broadcasted_iota(I32, (nkeys, LANES), 1)
    pad = [jnp.zeros((nkeys - have, LANES), BF16)] if nkeys > have else []

    def scores(kv):
        sl = slice(kv * LANES, (kv + 1) * LANES)
        kcat = jnp.concatenate([p[:, sl] for p in k_pieces] + pad, axis=0)
        zero = jnp.zeros_like(kcat)
        lo = slice(2 * kv * LANES, (2 * kv + 1) * LANES)
        hi = slice((2 * kv + 1) * LANES, (2 * kv + 2) * LANES)
        qp = jnp.concatenate([q_ref[:, lo], q_ref[:, hi]], axis=0)
        return (_dot_nt(qp, jnp.where(lane_k < GQA_HD, kcat, zero)) + bias,
                _dot_nt(qp, jnp.where(lane_k >= GQA_HD, kcat, zero)) + bias)

    def softmax(kv, ss):
        out = []
        for par, s in enumerate(ss):
            head = kv * GQA_GROUP + par
            sink = jnp.where(row < nq, sink_ref[head], sink_ref[head + 2])
            m = jnp.maximum(jnp.max(s, axis=-1, keepdims=True), sink)
            p = jnp.exp(s - m)
            den = jnp.sum(p, axis=-1, keepdims=True) + jnp.exp(sink - m)
            out.append((p.astype(BF16), den))
        return out

    def weighted_values(kv, pd):
        sl = slice(kv * LANES, (kv + 1) * LANES)
        vcat = jnp.concatenate([p[:, sl] for p in v_pieces] + pad, axis=0)
        outs = [_dot(p, vcat) / den for p, den in pd]
        o = jnp.where(lane_o < GQA_HD, outs[0], outs[1]).astype(BF16)
        o_ref[:, 2 * kv * LANES:(2 * kv + 1) * LANES] = o[:nq]
        o_ref[:, (2 * kv + 1) * LANES:(2 * kv + 2) * LANES] = o[nq:]

    nxt = scores(0)
    pending = None
    for kv in range(GQA_KV):
        ss = nxt
        if kv + 1 < GQA_KV:
            nxt = scores(kv + 1)
        pd = softmax(kv, ss)
        if pending is not None:
            weighted_values(*pending)
        pending = (kv, pd)
    weighted_values(*pending)


def _win_attn_kernel(sink_ref, q_ref, kp_ref, kc_ref, kn_ref, vp_ref, vc_ref, vn_ref,
                     km_ref, vm_ref, o_ref, *, nb):
    j = pl.program_id(1)
    nkeys = 4 * BLK
    qi = lax.broadcasted_iota(I32, (2 * BLK, nkeys), 0) & (BLK - 1)
    c = lax.broadcasted_iota(I32, (2 * BLK, nkeys), 1)
    neg = jnp.float32(-jnp.inf)
    b_p = jnp.where((c >= qi) & (j > 0), 0.0, neg)
    b_n = jnp.where(((c - 2 * BLK) <= qi) & (j < nb - 1), 0.0, neg)
    b_m = jnp.where(c < 3 * BLK + N_META, 0.0, neg)
    bias = jnp.where(c < BLK, b_p, jnp.where(c < 2 * BLK, 0.0, jnp.where(c < 3 * BLK, b_n, b_m)))
    _gqa_attend(sink_ref, q_ref, o_ref, (kp_ref, kc_ref, kn_ref, km_ref),
                (vp_ref, vc_ref, vn_ref, vm_ref), bias)


def _win_attn_call(sink, q, k, v, *, batch, seq, n_real):
    ntot = q.shape[0]
    nb = seq // BLK
    mrow = n_real // N_META

    def qmap(b, j, s):
        return (b * nb + j, 0)

    def pmap(b, j, s):
        return (b * nb + jnp.maximum(j - 1, 0), 0)

    def nmap(b, j, s):
        return (b * nb + jnp.minimum(j + 1, nb - 1), 0)

    def mmap(b, j, s):
        return (mrow + b, 0)

    kvw = GQA_KV * LANES
    kv_spec = lambda f: pl.BlockSpec((BLK, kvw), f)
    grid_spec = pltpu.PrefetchScalarGridSpec(
        num_scalar_prefetch=1,
        grid=(batch, nb),
        in_specs=[
            pl.BlockSpec((BLK, 1024), qmap),
            kv_spec(pmap), kv_spec(qmap), kv_spec(nmap),
            kv_spec(pmap), kv_spec(qmap), kv_spec(nmap),
            pl.BlockSpec((N_META, kvw), mmap),
            pl.BlockSpec((N_META, kvw), mmap),
        ],
        out_specs=pl.BlockSpec((BLK, 1024), qmap),
    )
    return pl.pallas_call(
        functools.partial(_win_attn_kernel, nb=nb),
        grid_spec=grid_spec,
        out_shape=jax.ShapeDtypeStruct((n_real, 1024), BF16),
        compiler_params=_params(("parallel", "parallel")),
        name="win_attn",
    )(sink, q, k, k, k, v, v, v, k, v)


def _win_meta_kernel(sink_ref, q_ref, km_ref, kc_ref, vm_ref, vc_ref, o_ref, *, batch):
    i = pl.program_id(0)

    @pl.when(i < batch)
    def _():
        nkeys = 2 * BLK
        qp = lax.broadcasted_iota(I32, (2 * N_META, nkeys), 0) & (N_META - 1)
        c = lax.broadcasted_iota(I32, (2 * N_META, nkeys), 1)
        neg = jnp.float32(-jnp.inf)
        b_c = jnp.where((N_META + c - qp) <= BLK, 0.0, neg)
        bias = jnp.where(c < BLK, b_c, jnp.where(c < BLK + N_META, 0.0, neg))
        _gqa_attend(sink_ref, q_ref, o_ref, (kc_ref, km_ref), (vc_ref, vm_ref), bias)

    @pl.when(i >= batch)
    def _():
        o_ref[...] = jnp.zeros_like(o_ref)


def _win_meta_call(sink, q, k, v, *, batch, seq, n_real):
    ntot = q.shape[0]
    nb = seq // BLK
    mrow = n_real // N_META
    ntail_blocks = (ntot - n_real) // N_META

    def tmap(i, s):
        return (mrow + i, 0)

    def cmap(i, s):
        return (jnp.minimum(i, batch - 1) * nb, 0)

    grid_spec = pltpu.PrefetchScalarGridSpec(
        num_scalar_prefetch=1,
        grid=(ntail_blocks,),
        in_specs=[
            pl.BlockSpec((N_META, 1024), tmap),
            pl.BlockSpec((N_META, GQA_KV * LANES), tmap),
            pl.BlockSpec((BLK, GQA_KV * LANES), cmap),
            pl.BlockSpec((N_META, GQA_KV * LANES), tmap),
            pl.BlockSpec((BLK, GQA_KV * LANES), cmap),
        ],
        out_specs=pl.BlockSpec((N_META, 1024), lambda i, s: (i, 0)),
    )
    return pl.pallas_call(
        functools.partial(_win_meta_kernel, batch=batch),
        grid_spec=grid_spec,
        out_shape=jax.ShapeDtypeStruct((ntot - n_real, 1024), BF16),
        compiler_params=_params(("parallel",)),
        name="win_attn_meta",
    )(sink, q, k, k, v, v)


def _attn_out_tile(o_real_ref, o_tail_ref, n_real_tiles):
    is_real = pl.program_id(0) < n_real_tiles
    return jnp.where(is_real, o_real_ref[...], o_tail_ref[...])


def _attn_out_specs(tm, n_real_tiles):
    return [
        pl.BlockSpec((tm, 1024), lambda i: (jnp.minimum(i, n_real_tiles - 1), 0)),
        pl.BlockSpec((tm, 1024), lambda i: (jnp.maximum(i - n_real_tiles, 0), 0)),
    ]


def _wo_kernel(o_real_ref, o_tail_ref, w_ref, h_ref, g_ref, hout_ref, xn_ref, *, n_real_tiles):
    o = _attn_out_tile(o_real_ref, o_tail_ref, n_real_tiles)
    hn = _dot(o, w_ref[...]) + h_ref[...]
    hout_ref[...] = hn
    xn_ref[...] = _rms(hn, g_ref[...]).astype(BF16)


def _wo_call(o_real, o_tail, w, h, g, *, tm):
    ntot = h.shape[0]
    nrt = o_real.shape[0] // tm
    row = lambda i: (i, 0)
    fix = lambda i: (0, 0)
    return pl.pallas_call(
        functools.partial(_wo_kernel, n_real_tiles=nrt),
        grid=(ntot // tm,),
        in_specs=_attn_out_specs(tm, nrt) + [
            pl.BlockSpec((1024, D_MODEL), fix),
            pl.BlockSpec((tm, D_MODEL), row),
            pl.BlockSpec((1, D_MODEL), fix),
        ],
        out_specs=[pl.BlockSpec((tm, D_MODEL), row), pl.BlockSpec((tm, D_MODEL), row)],
        out_shape=[jax.ShapeDtypeStruct((ntot, D_MODEL), F32),
                   jax.ShapeDtypeStruct((ntot, D_MODEL), BF16)],
        compiler_params=_params(("parallel",)),
        name="wo_res_norm",
    )(o_real, o_tail, w, h, g)


def _pack_bf16_pair(a, b):
    ra = pltpu.bitcast(a.astype(BF16).astype(F32), U32)
    rb = pltpu.bitcast(b.astype(BF16).astype(F32), U32)
    return ra | (rb >> 16)


def _unpack_bf16_pair(p):
    a = pltpu.bitcast(p & jnp.uint32(0xFFFF0000), F32).astype(BF16)
    b = pltpu.bitcast(p << 16, F32).astype(BF16)
    return a, b


def _wo_router_kernel(o_real_ref, o_tail_ref, w_ref, h_ref, g_ref, r_ref,
                      hout_ref, xpk_ref, route_ref, cnt_ref, carry_ref, *, n_real_tiles):
    i = pl.program_id(0)
    tm = h_ref.shape[0]

    @pl.when(i == 0)
    def _():
        carry_ref[...] = jnp.zeros_like(carry_ref)

    o = _attn_out_tile(o_real_ref, o_tail_ref, n_real_tiles)
    hn = _dot(o, w_ref[...]) + h_ref[...]
    hout_ref[...] = hn
    y = _rms(hn, g_ref[...])
    half = D_MODEL // 2
    xpk_ref[...] = _pack_bf16_pair(y[:, :half], y[:, half:])

    yhi = y.astype(BF16)
    ylo = (y - yhi.astype(F32)).astype(BF16)
    r2 = r_ref[...]
    t = _dot(yhi, r2)
    logits = t[:, :LANES] + t[:, LANES:] + _dot(ylo, r2[:, :LANES])

    lane = lax.broadcasted_iota(I32, (tm, LANES), 1)
    lm = jnp.where(lane < N_EXPERTS, logits, -jnp.inf)
    m1 = jnp.max(lm, axis=-1, keepdims=True)
    i1 = jnp.min(jnp.where(lm == m1, lane, LANES), axis=-1, keepdims=True)
    lm2 = jnp.where(lane == i1, -jnp.inf, lm)
    m2 = jnp.max(lm2, axis=-1, keepdims=True)
    i2 = jnp.min(jnp.where(lm2 == m2, lane, LANES), axis=-1, keepdims=True)
    e21 = jnp.exp(m2 - m1)
    g0 = 1.0 / (1.0 + e21)
    g1 = e21 / (1.0 + e21)

    onehot = ((lane == i1) | (lane == i2)).astype(F32)
    rr = lax.broadcasted_iota(I32, (tm, tm), 0)
    cc = lax.broadcasted_iota(I32, (tm, tm), 1)
    lower = jnp.where(cc < rr, 1.0, 0.0).astype(BF16)
    before = _dot(lower, onehot.astype(BF16)) + carry_ref[0:1, :]
    rank1 = jnp.sum(jnp.where(lane == i1, before, 0.0), axis=-1, keepdims=True)
    rank2 = jnp.sum(jnp.where(lane == i2, before, 0.0), axis=-1, keepdims=True)
    carry = carry_ref[0:1, :] + jnp.sum(onehot, axis=0, keepdims=True)
    carry_ref[...] = jnp.broadcast_to(carry, carry_ref.shape)
    cnt_ref[...] = jnp.broadcast_to(carry, cnt_ref.shape)

    out = jnp.where(lane == 0, i1.astype(F32), 0.0)
    out = jnp.where(lane == 1, i2.astype(F32), out)
    out = jnp.where(lane == 2, rank1, out)
    out = jnp.where(lane == 3, rank2, out)
    out = jnp.where(lane == 4, g0, out)
    out = jnp.where(lane == 5, g1, out)
    route_ref[...] = out


def _wo_router_call(o_real, o_tail, w, h, g, r2, *, tm):
    ntot = h.shape[0]
    nrt = o_real.shape[0] // tm
    row = lambda i: (i, 0)
    fix = lambda i: (0, 0)
    return pl.pallas_call(
        functools.partial(_wo_router_kernel, n_real_tiles=nrt),
        grid=(ntot // tm,),
        in_specs=_attn_out_specs(tm, nrt) + [
            pl.BlockSpec((1024, D_MODEL), fix),
            pl.BlockSpec((tm, D_MODEL), row),
            pl.BlockSpec((1, D_MODEL), fix),
            pl.BlockSpec((D_MODEL, 2 * LANES), fix),
        ],
        out_specs=[
            pl.BlockSpec((tm, D_MODEL), row),
            pl.BlockSpec((tm, D_MODEL // 2), row),
            pl.BlockSpec((tm, LANES), row),
            pl.BlockSpec((8, LANES), fix),
        ],
        out_shape=[
            jax.ShapeDtypeStruct((ntot, D_MODEL), F32),
            jax.ShapeDtypeStruct((ntot, D_MODEL // 2), U32),
            jax.ShapeDtypeStruct((ntot, LANES), F32),
            jax.ShapeDtypeStruct((8, LANES), F32),
        ],
        scratch_shapes=[pltpu.VMEM((8, LANES), F32)],
        compiler_params=_params(("arbitrary",)),
        name="wo_res_norm_router",
    )(o_real, o_tail, w, h, g, r2)


def _swiglu_chunk(x, wg, wu, wd):
    g = _dot(x, wg)
    u = _dot(x, wu)
    a = (g * (1.0 / (1.0 + jnp.exp(-g)))) * u
    return _dot(a.astype(BF16), wd)


def _ffn_dense_kernel(x_ref, wg_ref, wu_ref, wd_ref, h_ref, g_ref, hout_ref, xn_ref, *, tf):
    x = x_ref[...]
    hn = h_ref[...]
    for c in range(wg_ref.shape[1] // tf):
        cs = slice(c * tf, (c + 1) * tf)
        hn = hn + _swiglu_chunk(x, wg_ref[:, cs], wu_ref[:, cs], wd_ref[cs, :])
    hout_ref[...] = hn
    xn_ref[...] = _rms(hn, g_ref[...]).astype(BF16)


def _ffn_dense_call(x, wg, wu, wd, h, g, *, tm, tf):
    ntot = x.shape[0]
    fdim = wg.shape[1]
    row = lambda i: (i, 0)
    fix = lambda i: (0, 0)
    once = pl.Buffered(1)
    return pl.pallas_call(
        functools.partial(_ffn_dense_kernel, tf=tf),
        grid=(ntot // tm,),
        in_specs=[
            pl.BlockSpec((tm, D_MODEL), row),
            pl.BlockSpec((D_MODEL, fdim), fix, pipeline_mode=once),
            pl.BlockSpec((D_MODEL, fdim), fix, pipeline_mode=once),
            pl.BlockSpec((fdim, D_MODEL), fix, pipeline_mode=once),
            pl.BlockSpec((tm, D_MODEL), row),
            pl.BlockSpec((1, D_MODEL), fix),
        ],
        out_specs=[pl.BlockSpec((tm, D_MODEL), row), pl.BlockSpec((tm, D_MODEL), row)],
        out_shape=[jax.ShapeDtypeStruct((ntot, D_MODEL), F32),
                   jax.ShapeDtypeStruct((ntot, D_MODEL), BF16)],
        compiler_params=_params(("parallel",)),
        name="ffn_dense",
    )(x, wg, wu, wd, h, g)


def _ffn_expert_kernel(be_ref, nu_ref, xs_ref, wg_ref, wu_ref, wd_ref, ys_ref):
    del be_ref
    used = pl.program_id(0) < nu_ref[0]

    @pl.when(used)
    def _():
        a, b = _unpack_bf16_pair(xs_ref[...])
        x = jnp.concatenate([a, b], axis=1)

        def chunk(c, acc):
            return acc + _swiglu_chunk(x, wg_ref[c], wu_ref[c], wd_ref[c])

        ys_ref[...] = lax.fori_loop(0, wg_ref.shape[0], chunk, jnp.zeros(ys_ref.shape, F32))

    @pl.when(jnp.logical_not(used))
    def _():
        ys_ref[...] = jnp.zeros_like(ys_ref)


def _ffn_expert_call(block_e, n_used, xs, wg, wu, wd, *, tm):
    rows = xs.shape[0]
    once = pl.Buffered(1)
    wmap = lambda i, be, nu: (be[i], 0, 0, 0)
    grid_spec = pltpu.PrefetchScalarGridSpec(
        num_scalar_prefetch=2,
        grid=(rows // tm,),
        in_specs=[
            pl.BlockSpec((tm, D_MODEL // 2), lambda i, be, nu: (i, 0)),
            pl.BlockSpec((None,) + wg.shape[1:], wmap, pipeline_mode=once),
            pl.BlockSpec((None,) + wu.shape[1:], wmap, pipeline_mode=once),
            pl.BlockSpec((None,) + wd.shape[1:], wmap, pipeline_mode=once),
        ],
        out_specs=pl.BlockSpec((tm, D_MODEL), lambda i, be, nu: (i, 0)),
    )
    return pl.pallas_call(
        _ffn_expert_kernel,
        grid_spec=grid_spec,
        out_shape=jax.ShapeDtypeStruct((rows, D_MODEL), F32),
        compiler_params=_params(("arbitrary",)),
        name="ffn_expert",
    )(block_e, n_used, xs, wg, wu, wd)


DMA_UNROLL = 8


def _row_copy(src, dst, sem):
    return pltpu.make_async_copy(src, dst, sem)


def _dispatch_kernel(dest_ref, x_ref, xs_in_ref, xs_ref, sem):
    del xs_in_ref
    td = x_ref.shape[0]

    def issue(t, carry):
        for s in range(2):
            d = dest_ref[0, s, t]
            _row_copy(x_ref.at[pl.ds(t, 1)], xs_ref.at[pl.ds(d, 1)], sem).start(priority=s)
        return carry

    lax.fori_loop(0, td, issue, 0, unroll=DMA_UNROLL)

    def drain(t, carry):
        for s in range(2):
            _row_copy(x_ref.at[pl.ds(0, 1)], xs_ref.at[pl.ds(0, 1)], sem).wait()
        return carry

    lax.fori_loop(0, td, drain, 0, unroll=DMA_UNROLL)


def _dispatch_call(dest, xpk, xs_init, *, td):
    ntot = xpk.shape[0]
    return pl.pallas_call(
        _dispatch_kernel,
        grid=(ntot // td,),
        in_specs=[
            pl.BlockSpec((1, 2, td), lambda i: (i, 0, 0), memory_space=pltpu.SMEM),
            pl.BlockSpec((td, D_MODEL // 2), lambda i: (i, 0)),
            pl.BlockSpec(memory_space=pl.ANY),
        ],
        out_specs=pl.BlockSpec(memory_space=pl.ANY),
        out_shape=jax.ShapeDtypeStruct(xs_init.shape, U32),
        input_output_aliases={2: 0},
        scratch_shapes=[pltpu.SemaphoreType.DMA(())],
        compiler_params=_params(("arbitrary",)),
        name="moe_dispatch",
    )(dest, xpk, xs_init)


def _combine_kernel(dest_ref, ys_ref, h_ref, route_ref, g_ref, out_ref, y0_ref, y1_ref, sem):
    tc = h_ref.shape[0]
    bufs = (y0_ref, y1_ref)

    def issue(t, carry):
        for s in range(2):
            d = dest_ref[0, s, t]
            _row_copy(ys_ref.at[pl.ds(d, 1)], bufs[s].at[pl.ds(t, 1)], sem).start(priority=s)
        return carry

    lax.fori_loop(0, tc, issue, 0, unroll=DMA_UNROLL)

    def drain(t, carry):
        for s in range(2):
            _row_copy(ys_ref.at[pl.ds(0, 1)], bufs[s].at[pl.ds(0, 1)], sem).wait()
        return carry

    lax.fori_loop(0, tc, drain, 0, unroll=DMA_UNROLL)

    r = route_ref[...]
    lane = lax.broadcasted_iota(I32, r.shape, 1)
    g0 = jnp.sum(jnp.where(lane == 4, r, 0.0), axis=-1, keepdims=True)
    g1 = jnp.sum(jnp.where(lane == 5, r, 0.0), axis=-1, keepdims=True)
    hn = h_ref[...] + (y0_ref[...] * g0 + y1_ref[...] * g1)
    out_ref[...] = _rms(hn, g_ref[...])


def _combine_call(dest, ys, h, route, g, *, tc, tile0, ntiles):
    return pl.pallas_call(
        _combine_kernel,
        grid=(ntiles,),
        in_specs=[
            pl.BlockSpec((1, 2, tc), lambda i: (tile0 + i, 0, 0), memory_space=pltpu.SMEM),
            pl.BlockSpec(memory_space=pl.ANY),
            pl.BlockSpec((tc, D_MODEL), lambda i: (tile0 + i, 0)),
            pl.BlockSpec((tc, LANES), lambda i: (tile0 + i, 0)),
            pl.BlockSpec((1, D_MODEL), lambda i: (0, 0)),
        ],
        out_specs=pl.BlockSpec((tc, D_MODEL), lambda i: (i, 0)),
        out_shape=jax.ShapeDtypeStruct((ntiles * tc, D_MODEL), F32),
        scratch_shapes=[pltpu.VMEM((tc, D_MODEL), F32), pltpu.VMEM((tc, D_MODEL), F32),
                        pltpu.SemaphoreType.DMA(())],
        compiler_params=_params(("arbitrary",)),
        name="moe_combine_norm",
    )(dest, ys, h, route, g)


def _mla_proj_kernel(x_ref, wd_ref, qn_ref, kvn_ref, wuq_ref, wuk_ref, wuv_ref, wuvt_ref,
                     c_ref, s1_ref, s2_ref, q_ref, k_ref, v_ref, vt_ref):
    lat = _dot(x_ref[...], wd_ref[...])
    cq = _rms(lat[:, :MLA_Q_RANK], qn_ref[...]).astype(BF16)
    ckv = _rms(lat[:, MLA_Q_RANK:MLA_Q_RANK + MLA_KV_RANK], kvn_ref[...]).astype(BF16)
    c, s1, s2 = c_ref[...], s1_ref[...], s2_ref[...]
    half = MLA_ROPE // 2
    kr = _rope(lat[:, MLA_Q_RANK + MLA_KV_RANK:], c, s1, s2, half)
    q = _dot(cq, wuq_ref[...])
    k = _dot(ckv, wuk_ref[...])
    scale = (MLA_QK ** -0.5) * LOG2E
    for h in range(MLA_HEADS):
        sl = slice(h * LANES, (h + 1) * LANES)
        q_ref[:, sl] = (_rope(q[:, sl], c, s1, s2, half) * scale).astype(BF16)
        k_ref[:, sl] = (k[:, sl] + kr).astype(BF16)
    v_ref[...] = _dot(ckv, wuv_ref[...]).astype(BF16)
    vt_ref[...] = _dot_nt(wuvt_ref[...], ckv).astype(BF16)


def _mla_proj_call(x, wd, qn, kvn, wuq, wuk, wuv, tabs, *, tm, n_real, seq):
    ntot = x.shape[0]
    nrt, tps = n_real // tm, seq // tm
    row = lambda i: (i, 0)
    fix = lambda i: (0, 0)
    tab_spec = pl.BlockSpec((tm, LANES), lambda i: (_table_index(i, nrt, tps), 0))
    return pl.pallas_call(
        _mla_proj_kernel,
        grid=(ntot // tm,),
        in_specs=[
            pl.BlockSpec((tm, D_MODEL), row),
            pl.BlockSpec(wd.shape, fix),
            pl.BlockSpec((1, MLA_Q_RANK), fix),
            pl.BlockSpec((1, MLA_KV_RANK), fix),
            pl.BlockSpec(wuq.shape, fix),
            pl.BlockSpec(wuk.shape, fix),
            pl.BlockSpec(wuv.shape, fix),
            pl.BlockSpec(wuv.shape[::-1], fix),
            tab_spec, tab_spec, tab_spec,
        ],
        out_specs=[
            pl.BlockSpec((tm, 2048), row),
            pl.BlockSpec((tm, 2048), row),
            pl.BlockSpec((tm, 1024), row),
            pl.BlockSpec((1024, tm), lambda i: (0, i)),
        ],
        out_shape=[
            jax.ShapeDtypeStruct((ntot, 2048), BF16),
            jax.ShapeDtypeStruct((ntot, 2048), BF16),
            jax.ShapeDtypeStruct((ntot, 1024), BF16),
            jax.ShapeDtypeStruct((1024, ntot), BF16),
        ],
        compiler_params=_params(("parallel",)),
        name="mla_proj",
    )(x, wd, qn, kvn, wuq, wuk, wuv, wuv.T, *tabs)


def _mla_attend(q_ref, kr_ref, km_ref, vr_ref, vm_ref, o_ref):
    for pair in range(MLA_HEADS // 2):
        vs = slice(pair * LANES, (pair + 1) * LANES)
        v_r, v_m = vr_ref[:, vs], vm_ref[:, vs]
        outs = []
        for hh in range(2):
            h = 2 * pair + hh
            hs = slice(h * LANES, (h + 1) * LANES)
            qh = q_ref[:, hs]
            s_r = _dot_nt(qh, kr_ref[:, hs])
            s_m = _dot_nt(qh, km_ref[:, hs])
            m = jnp.maximum(jnp.max(s_r, axis=-1, keepdims=True), jnp.max(s_m, axis=-1, keepdims=True))
            p_r = jnp.exp2(s_r - m)
            p_m = jnp.exp2(s_m - m)
            den = jnp.sum(p_r, axis=-1, keepdims=True) + jnp.sum(p_m, axis=-1, keepdims=True)
            acc = _dot(p_r.astype(BF16), v_r) + _dot(p_m.astype(BF16), v_m)
            outs.append(acc / den)
        lane = lax.broadcasted_iota(I32, outs[0].shape, 1)
        o_ref[:, vs] = jnp.where(lane < MLA_V, outs[0], outs[1]).astype(BF16)


META_PER_BLOCK = LANES // N_META
REDUCE_ROWS = 64


def _col_reduce(x, op):
    rows, n = x.shape
    if rows > REDUCE_ROWS and rows % REDUCE_ROWS == 0:
        x = op(x.reshape(rows // REDUCE_ROWS, REDUCE_ROWS, n), axis=0)
    return op(x, axis=0, keepdims=True)


def _mla_attn_kernel(q_ref, kr_ref, kt_ref, vtr_ref, vtt_ref, o_ref, ot_ref):
    b = pl.program_id(0)
    r = lax.broadcasted_iota(I32, (LANES, 1), 0)
    mine = jnp.right_shift(r, int(math.log2(N_META))) == (b % META_PER_BLOCK)
    mbias = jnp.where(mine, 0.0, -jnp.inf).astype(F32)
    def scores(h):
        hs = slice(h * LANES, (h + 1) * LANES)
        qh = q_ref[:, hs]
        return _dot_nt(kr_ref[:, hs], qh), _dot_nt(kt_ref[:, hs], qh) + mbias

    def weighted_values(h, p_r, p_m, den):
        vs = slice(h * MLA_V, (h + 1) * MLA_V)
        acc = _dot(vtr_ref[vs, :], p_r) + _dot(vtt_ref[vs, :], p_m)
        ot_ref[vs, :] = acc / den

    nxt = scores(0)
    pending = None
    for h in range(MLA_HEADS):
        s_r, s_m = nxt
        if h + 1 < MLA_HEADS:
            nxt = scores(h + 1)
        m = jnp.maximum(_col_reduce(s_r, jnp.max), _col_reduce(s_m, jnp.max))
        p_r = jnp.exp2(s_r - m)
        p_m = jnp.exp2(s_m - m)
        den = _col_reduce(p_r, jnp.sum) + _col_reduce(p_m, jnp.sum)
        if pending is not None:
            weighted_values(*pending)
        pending = (h, p_r.astype(BF16), p_m.astype(BF16), den)
    weighted_values(*pending)
    o_ref[...] = ot_ref[...].T.astype(BF16)


def _mla_attn_call(q, k, vt, *, batch, seq, n_real, tq):
    nq = seq // tq
    tail0 = n_real // LANES
    qmap = lambda b, j: (b * nq + j, 0)
    return pl.pallas_call(
        _mla_attn_kernel,
        grid=(batch, nq),
        in_specs=[
            pl.BlockSpec((tq, 2048), qmap),
            pl.BlockSpec((seq, 2048), lambda b, j: (b, 0)),
            pl.BlockSpec((LANES, 2048), lambda b, j: (tail0 + b // META_PER_BLOCK, 0)),
            pl.BlockSpec((1024, seq), lambda b, j: (0, b)),
            pl.BlockSpec((1024, LANES), lambda b, j: (0, tail0 + b // META_PER_BLOCK)),
        ],
        out_specs=pl.BlockSpec((tq, 1024), qmap),
        out_shape=jax.ShapeDtypeStruct((n_real, 1024), BF16),
        scratch_shapes=[pltpu.VMEM((1024, tq), F32)],
        compiler_params=_params(("parallel", "arbitrary")),
        name="mla_attn",
    )(q, k, k, vt, vt)


def _mla_meta_kernel(q_ref, kr_ref, km_ref, vr_ref, vm_ref, o_ref, *, batch):
    i = pl.program_id(0)

    @pl.when(i < batch)
    def _():
        _mla_attend(q_ref, kr_ref, km_ref, vr_ref, vm_ref, o_ref)

    @pl.when(i >= batch)
    def _():
        o_ref[...] = jnp.zeros_like(o_ref)


def _mla_meta_call(q, k, v, *, batch, seq, n_real):
    ntot = q.shape[0]
    mrow = n_real // N_META
    ntail_blocks = (ntot - n_real) // N_META
    tmap = lambda i: (mrow + i, 0)
    rmap = lambda i: (jnp.minimum(i, batch - 1), 0)
    return pl.pallas_call(
        functools.partial(_mla_meta_kernel, batch=batch),
        grid=(ntail_blocks,),
        in_specs=[
            pl.BlockSpec((N_META, 2048), tmap),
            pl.BlockSpec((seq, 2048), rmap),
            pl.BlockSpec((N_META, 2048), tmap),
            pl.BlockSpec((seq, 1024), rmap),
            pl.BlockSpec((N_META, 1024), tmap),
        ],
        out_specs=pl.BlockSpec((N_META, 1024), lambda i: (i, 0)),
        out_shape=jax.ShapeDtypeStruct((ntot - n_real, 1024), BF16),
        compiler_params=_params(("arbitrary",)),
        name="mla_attn_meta",
    )(q, k, k, v, v)


def _prep_gqa(a_wqkv, a_wo):
    d = a_wqkv.shape[0]
    nq = GQA_HEADS * GQA_HD
    nkv = GQA_KV * GQA_HD
    wq = a_wqkv[:, :nq] * (GQA_HD ** -0.5)

    def dup(w):
        w = w.reshape(d, GQA_KV, 1, GQA_HD)
        return jnp.broadcast_to(w, (d, GQA_KV, 2, GQA_HD)).reshape(d, 2 * nkv)

    w = jnp.concatenate([wq, dup(a_wqkv[:, nq:nq + nkv]), dup(a_wqkv[:, nq + nkv:])], axis=1)
    return w.astype(BF16), a_wo.astype(BF16)


def _prep_mla(b_wdkv, b_wuq, b_wukv):
    d = b_wdkv.shape[0]
    lat = MLA_Q_RANK + MLA_KV_RANK
    z = lambda n: jnp.zeros((d, n), F32)
    wd = jnp.concatenate([b_wdkv[:, :lat], z(MLA_NOPE), b_wdkv[:, lat:], z(LANES - MLA_QK)], axis=1)
    wuq = b_wuq.reshape(MLA_Q_RANK, MLA_HEADS, MLA_QK)
    wuq = jnp.pad(wuq, ((0, 0), (0, 0), (0, LANES - MLA_QK))).reshape(MLA_Q_RANK, MLA_HEADS * LANES)
    wukv = b_wukv.reshape(MLA_KV_RANK, MLA_HEADS, MLA_NOPE + MLA_V)
    wuk = jnp.pad(wukv[:, :, :MLA_NOPE], ((0, 0), (0, 0), (0, LANES - MLA_NOPE)))
    wuk = wuk.reshape(MLA_KV_RANK, MLA_HEADS * LANES)
    wuv = wukv[:, :, MLA_NOPE:].reshape(MLA_KV_RANK, MLA_HEADS * MLA_V)
    return wd.astype(BF16), wuq.astype(BF16), wuk.astype(BF16), wuv.astype(BF16)


def _prep_router(m_router):
    hi = m_router.astype(BF16)
    lo = (m_router - hi.astype(F32)).astype(BF16)
    pad = lambda r: jnp.pad(r, ((0, 0), (0, LANES - N_EXPERTS)))
    return jnp.concatenate([pad(hi), pad(lo)], axis=1)


def _positions(seq, n_tail, n_meta_rows):
    real = N_META + jnp.arange(seq, dtype=F32)
    t = jnp.arange(n_tail)
    tail = jnp.where(t < n_meta_rows, t % N_META, 0).astype(F32)
    return jnp.concatenate([real, tail])


def _trunk_flat(x_groups, meta, norm_mix, norm_ffn, norm_final,
                a_wqkv, a_wo, a_sink,
                b_wdkv, b_qnorm, b_kvnorm, b_wuq, b_wukv, b_wo,
                f_wg, f_wu, f_wd,
                m_router, m_wg, m_wu, m_wd):
    seq = x_groups[0].shape[1]
    batches = [x.shape[0] for x in x_groups]
    batch = sum(batches)
    n_real = batch * seq
    n_meta_rows = batch * N_META
    big = _pick_tile(seq, (1024, 512, 256, 128))
    tp = min(big, 512)
    ntot = -(-(n_real + n_meta_rows) // big) * big
    n_tail = ntot - n_real

    tail = jnp.concatenate([
        jnp.broadcast_to(meta.astype(F32)[None], (batch, N_META, D_MODEL)).reshape(n_meta_rows, D_MODEL),
        jnp.zeros((n_tail - n_meta_rows, D_MODEL), F32)], axis=0)
    h = jnp.concatenate([x.reshape(-1, D_MODEL) for x in x_groups] + [tail], axis=0)

    pos = _positions(seq, n_tail, n_meta_rows)
    tabs_a = _rope_lane_tables(pos, GQA_THETA, GQA_ROT, GQA_HD, 0)
    tabs_b = _rope_lane_tables(pos, MLA_THETA, MLA_ROPE, LANES, MLA_NOPE)
    row = lambda v: v.reshape(1, -1).astype(F32)

    wqkv, wo_a = _prep_gqa(a_wqkv[0], a_wo[0])
    q, k, v = _qkv_call(h, row(norm_mix[0]), wqkv, tabs_a, tm=tp, n_real=n_real, seq=seq)
    sink = a_sink[0].astype(F32)
    o = _win_attn_call(sink, q, k, v, batch=batch, seq=seq, n_real=n_real)
    o_tail = _win_meta_call(sink, q, k, v, batch=batch, seq=seq, n_real=n_real)
    h, xn = _wo_call(o, o_tail, wo_a, h, row(norm_ffn[0]), tm=big)
    fdim = f_wg.shape[2]
    tf = _pick_tile(fdim, (512, 256, 128))
    h, xn = _ffn_dense_call(xn, f_wg[0].astype(BF16), f_wu[0].astype(BF16), f_wd[0].astype(BF16),
                            h, row(norm_mix[1]), tm=tp, tf=tf)

    wd, wuq, wuk, wuv = _prep_mla(b_wdkv[0], b_wuq[0], b_wukv[0])
    q, k, v, vt = _mla_proj_call(xn, wd, row(b_qnorm[0]), row(b_kvnorm[0]), wuq, wuk, wuv, tabs_b,
                                 tm=tp, n_real=n_real, seq=seq)
    tq = _pick_tile(seq, (256, 128))
    o = _mla_attn_call(q, k, vt, batch=batch, seq=seq, n_real=n_real, tq=tq)
    o_tail = _mla_meta_call(q, k, v, batch=batch, seq=seq, n_real=n_real)
    tr = tp
    h, xpk, route, counts = _wo_router_call(o, o_tail, b_wo[0].astype(BF16), h, row(norm_ffn[1]),
                                            _prep_router(m_router[0]), tm=tr)

    tmx = big
    counts = counts[0, :N_EXPERTS].astype(I32)
    padded = ((counts + tmx - 1) // tmx) * tmx
    pend = jnp.cumsum(padded)
    pstart = pend - padded
    e_idx = route[:, 0:2].astype(I32)
    rank = route[:, 2:4].astype(I32)
    dest = pstart[e_idx] + rank
    n_assign = 2 * ntot
    n_blocks = -(-n_assign // tmx) + N_EXPERTS
    rows = n_blocks * tmx
    block_e = jnp.clip(jnp.searchsorted(pend, jnp.arange(n_blocks, dtype=I32) * tmx, side="right"),
                       0, N_EXPERTS - 1).astype(I32)
    n_used = (pend[-1] // tmx).astype(I32).reshape(1)
    block_e = jnp.where(jnp.arange(n_blocks) < n_used[0], block_e, block_e[jnp.maximum(n_used[0] - 1, 0)])

    td = tp
    dest_t = dest.reshape(ntot // td, td, 2).transpose(0, 2, 1)
    xs = _dispatch_call(dest_t, xpk, jnp.zeros((rows, D_MODEL // 2), U32), td=td)
    edim = m_wg.shape[3]
    tfx = _pick_tile(edim, (512, 256, 128))
    nchx = edim // tfx

    def up_chunks(w):
        return w.astype(BF16).reshape(N_EXPERTS, D_MODEL, nchx, tfx).transpose(0, 2, 1, 3)

    wdx = m_wd[0].astype(BF16).reshape(N_EXPERTS, nchx, tfx, D_MODEL)
    ys = _ffn_expert_call(block_e, n_used, xs, up_chunks(m_wg[0]), up_chunks(m_wu[0]), wdx, tm=tmx)

    outs = []
    tile0 = 0
    for b, x in zip(batches, x_groups):
        ntiles = b * seq // td
        y = _combine_call(dest_t, ys, h, route, row(norm_final), tc=td, tile0=tile0, ntiles=ntiles)
        outs.append(y.reshape(b, seq, D_MODEL))
        tile0 += ntiles
    return tuple(outs)


def kernel(x_prompt, x_sample, meta, norm_mix, norm_ffn, norm_final, a_wqkv, a_wo, a_sink, b_wdkv, b_qnorm, b_kvnorm, b_wuq, b_wukv, b_wo, f_wg, f_wu, f_wd, m_router, m_wg, m_wu, m_wd):
    assert x_prompt.shape[1] == x_sample.shape[1] and x_prompt.shape[1] % BLK == 0
    assert norm_mix.shape[0] == 2, "two layers: windowed GQA + dense FFN, then MLA + MoE"
    return _trunk_flat((x_prompt, x_sample), meta, norm_mix, norm_ffn, norm_final,
                       a_wqkv, a_wo, a_sink,
                       b_wdkv, b_qnorm, b_kvnorm, b_wuq, b_wukv, b_wo,
                       f_wg, f_wu, f_wd,
                       m_router, m_wg, m_wu, m_wd)
```

```python
import functools
import math

import jax
import jax.numpy as jnp
from jax import lax
from jax.experimental import pallas as pl
from jax.experimental.pallas import tpu as pltpu

F32 = jnp.float32
BF16 = jnp.bfloat16
U32 = jnp.uint32
I32 = jnp.int32

D_MODEL = 1024
N_META = 16
RMS_EPS = 1e-6
BLK = 128
GQA_HEADS = 16
GQA_KV = 4
GQA_GROUP = 4
GQA_HD = 64
GQA_ROT = 16
GQA_THETA = 500000.0
MLA_HEADS = 16
MLA_NOPE = 64
MLA_ROPE = 32
MLA_V = 64
MLA_Q_RANK = 384
MLA_KV_RANK = 256
MLA_THETA = 10000.0
MLA_QK = MLA_NOPE + MLA_ROPE
N_EXPERTS = 8
LOG2E = 1.4426950408889634

LANES = 128
VMEM_LIMIT = 56 * 1024 * 1024


def _params(sem, vmem=VMEM_LIMIT):
    return pltpu.CompilerParams(dimension_semantics=sem, vmem_limit_bytes=vmem)


def _rms(x, g):
    ms = jnp.mean(x * x, axis=-1, keepdims=True)
    return x * lax.rsqrt(ms + RMS_EPS) * g


def _dot(a, b):
    return jnp.dot(a, b, preferred_element_type=F32)


def _dot_nt(a, b):
    return lax.dot_general(a, b, (((1,), (1,)), ((), ())), preferred_element_type=F32)


def _pick_tile(n, candidates):
    for c in candidates:
        if n % c == 0:
            return c
    raise ValueError(f"no tile in {candidates} divides {n}")


def _rope_lane_tables(pos, theta, rot, period, offset):
    half = rot // 2
    inv = jnp.power(jnp.float32(theta), -(jnp.arange(0, rot, 2, dtype=F32) / rot))
    ang = pos[:, None] * inv[None, :]
    cos, sin = jnp.cos(ang), jnp.sin(ang)
    lane = jnp.arange(LANES)
    r = (lane % period) - offset
    is_x1 = (r >= 0) & (r < half)
    is_x2 = (r >= half) & (r < rot)
    f = jnp.clip(jnp.where(is_x2, r - half, r), 0, half - 1)
    cos_l = cos[:, f]
    sin_l = sin[:, f]
    c = jnp.where((is_x1 | is_x2)[None, :], cos_l, 1.0)
    s1 = jnp.where(is_x2[None, :], sin_l, 0.0)
    s2 = jnp.where(is_x1[None, :], -sin_l, 0.0)
    return c.astype(F32), s1.astype(F32), s2.astype(F32)


def _rope(x, c, s1, s2, half):
    return x * c + pltpu.roll(x, half, 1) * s1 + pltpu.roll(x, LANES - half, 1) * s2


def _qkv_kernel(h_ref, g_ref, w_ref, c_ref, s1_ref, s2_ref, q_ref, k_ref, v_ref):
    xn = _rms(h_ref[...], g_ref[...]).astype(BF16)
    qkv = _dot(xn, w_ref[...])
    c, s1, s2 = c_ref[...], s1_ref[...], s2_ref[...]
    nq = GQA_HEADS * GQA_HD // LANES
    nk = GQA_KV
    for i in range(nq):
        q_ref[:, i * LANES:(i + 1) * LANES] = _rope(
            qkv[:, i * LANES:(i + 1) * LANES], c, s1, s2, GQA_ROT // 2).astype(BF16)
    for i in range(nk):
        lo = (nq + i) * LANES
        k_ref[:, i * LANES:(i + 1) * LANES] = _rope(
            qkv[:, lo:lo + LANES], c, s1, s2, GQA_ROT // 2).astype(BF16)
    v_ref[...] = qkv[:, (nq + nk) * LANES:].astype(BF16)


def _table_index(i, n_real_tiles, tiles_per_seq):
    return jnp.where(i < n_real_tiles, i % tiles_per_seq, tiles_per_seq + i - n_real_tiles)


def _qkv_call(h, g, w, tabs, *, tm, n_real, seq):
    ntot = h.shape[0]
    nrt, tps = n_real // tm, seq // tm
    tab_spec = pl.BlockSpec((tm, LANES), lambda i: (_table_index(i, nrt, tps), 0))
    return pl.pallas_call(
        _qkv_kernel,
        grid=(ntot // tm,),
        in_specs=[
            pl.BlockSpec((tm, D_MODEL), lambda i: (i, 0)),
            pl.BlockSpec((1, D_MODEL), lambda i: (0, 0)),
            pl.BlockSpec(w.shape, lambda i: (0, 0)),
            tab_spec, tab_spec, tab_spec,
        ],
        out_specs=[
            pl.BlockSpec((tm, 1024), lambda i: (i, 0)),
            pl.BlockSpec((tm, GQA_KV * LANES), lambda i: (i, 0)),
            pl.BlockSpec((tm, GQA_KV * LANES), lambda i: (i, 0)),
        ],
        out_shape=[
            jax.ShapeDtypeStruct((ntot, 1024), BF16),
            jax.ShapeDtypeStruct((ntot, GQA_KV * LANES), BF16),
            jax.ShapeDtypeStruct((ntot, GQA_KV * LANES), BF16),
        ],
        compiler_params=_params(("parallel",)),
        name="qkv_rope",
    )(h, g, w, *tabs)


def _gqa_attend(sink_ref, q_ref, o_ref, k_pieces, v_pieces, bias):
    nq = q_ref.shape[0]
    nkeys = bias.shape[1]
    have = sum(p.shape[0] for p in k_pieces)
    row = lax.broadcasted_iota(I32, (2 * nq, 1), 0)
    lane_o = lax.broadcasted_iota(I32, (2 * nq, LANES), 1)
    lane_k = lax.broadcasted_iota(I32, (nkeys, LANES), 1)
    pad = [jnp.zeros((nkeys - have, LANES), BF16)] if nkeys > have else []

    def scores(kv):
        sl = slice(kv * LANES, (kv + 1) * LANES)
        kcat = jnp.concatenate([p[:, sl] for p in k_pieces] + pad, axis=0)
        zero = jnp.zeros_like(kcat)
        lo = slice(2 * kv * LANES, (2 * kv + 1) * LANES)
        hi = slice((2 * kv + 1) * LANES, (2 * kv + 2) * LANES)
        qp = jnp.concatenate([q_ref[:, lo], q_ref[:, hi]], axis=0)
        return (_dot_nt(qp, jnp.where(lane_k < GQA_HD, kcat, zero)) + bias,
                _dot_nt(qp, jnp.where(lane_k >= GQA_HD, kcat, zero)) + bias)

    def softmax(kv, ss):
        out = []
        for par, s in enumerate(ss):
            head = kv * GQA_GROUP + par
            sink = jnp.where(row < nq, sink_ref[head], sink_ref[head + 2])
            m = jnp.maximum(jnp.max(s, axis=-1, keepdims=True), sink)
            p = jnp.exp(s - m)
            den = jnp.sum(p, axis=-1, keepdims=True) + jnp.exp(sink - m)
            out.append((p.astype(BF16), den))
        return out

    def weighted_values(kv, pd):
        sl = slice(kv * LANES, (kv + 1) * LANES)
        vcat = jnp.concatenate([p[:, sl] for p in v_pieces] + pad, axis=0)
        outs = [_dot(p, vcat) / den for p, den in pd]
        o = jnp.where(lane_o < GQA_HD, outs[0], outs[1]).astype(BF16)
        o_ref[:, 2 * kv * LANES:(2 * kv + 1) * LANES] = o[:nq]
        o_ref[:, (2 * kv + 1) * LANES:(2 * kv + 2) * LANES] = o[nq:]

    nxt = scores(0)
    pending = None
    for kv in range(GQA_KV):
        ss = nxt
        if kv + 1 < GQA_KV:
            nxt = scores(kv + 1)
        pd = softmax(kv, ss)
        if pending is not None:
            weighted_values(*pending)
        pending = (kv, pd)
    weighted_values(*pending)


def _win_attn_kernel(sink_ref, q_ref, kp_ref, kc_ref, kn_ref, vp_ref, vc_ref, vn_ref,
                     km_ref, vm_ref, o_ref, *, nb):
    j = pl.program_id(1)
    nkeys = 4 * BLK
    qi = lax.broadcasted_iota(I32, (2 * BLK, nkeys), 0) & (BLK - 1)
    c = lax.broadcasted_iota(I32, (2 * BLK, nkeys), 1)
    neg = jnp.float32(-jnp.inf)
    b_p = jnp.where((c >= qi) & (j > 0), 0.0, neg)
    b_n = jnp.where(((c - 2 * BLK) <= qi) & (j < nb - 1), 0.0, neg)
    b_m = jnp.where(c < 3 * BLK + N_META, 0.0, neg)
    bias = jnp.where(c < BLK, b_p, jnp.where(c < 2 * BLK, 0.0, jnp.where(c < 3 * BLK, b_n, b_m)))
    _gqa_attend(sink_ref, q_ref, o_ref, (kp_ref, kc_ref, kn_ref, km_ref),
                (vp_ref, vc_ref, vn_ref, vm_ref), bias)


def _win_attn_call(sink, q, k, v, *, batch, seq, n_real):
    ntot = q.shape[0]
    nb = seq // BLK
    mrow = n_real // N_META

    def qmap(b, j, s):
        return (b * nb + j, 0)

    def pmap(b, j, s):
        return (b * nb + jnp.maximum(j - 1, 0), 0)

    def nmap(b, j, s):
        return (b * nb + jnp.minimum(j + 1, nb - 1), 0)

    def mmap(b, j, s):
        return (mrow + b, 0)

    kvw = GQA_KV * LANES
    kv_spec = lambda f: pl.BlockSpec((BLK, kvw), f)
    grid_spec = pltpu.PrefetchScalarGridSpec(
        num_scalar_prefetch=1,
        grid=(batch, nb),
        in_specs=[
            pl.BlockSpec((BLK, 1024), qmap),
            kv_spec(pmap), kv_spec(qmap), kv_spec(nmap),
            kv_spec(pmap), kv_spec(qmap), kv_spec(nmap),
            pl.BlockSpec((N_META, kvw), mmap),
            pl.BlockSpec((N_META, kvw), mmap),
        ],
        out_specs=pl.BlockSpec((BLK, 1024), qmap),
    )
    return pl.pallas_call(
        functools.partial(_win_attn_kernel, nb=nb),
        grid_spec=grid_spec,
        out_shape=jax.ShapeDtypeStruct((n_real, 1024), BF16),
        compiler_params=_params(("parallel", "parallel")),
        name="win_attn",
    )(sink, q, k, k, k, v, v, v, k, v)


def _win_meta_kernel(sink_ref, q_ref, km_ref, kc_ref, vm_ref, vc_ref, o_ref, *, batch):
    i = pl.program_id(0)

    @pl.when(i < batch)
    def _():
        nkeys = 2 * BLK
        qp = lax.broadcasted_iota(I32, (2 * N_META, nkeys), 0) & (N_META - 1)
        c = lax.broadcasted_iota(I32, (2 * N_META, nkeys), 1)
        neg = jnp.float32(-jnp.inf)
        b_c = jnp.where((N_META + c - qp) <= BLK, 0.0, neg)
        bias = jnp.where(c < BLK, b_c, jnp.where(c < BLK + N_META, 0.0, neg))
        _gqa_attend(sink_ref, q_ref, o_ref, (kc_ref, km_ref), (vc_ref, vm_ref), bias)

    @pl.when(i >= batch)
    def _():
        o_ref[...] = jnp.zeros_like(o_ref)


def _win_meta_call(sink, q, k, v, *, batch, seq, n_real):
    ntot = q.shape[0]
    nb = seq // BLK
    mrow = n_real // N_META
    ntail_blocks = (ntot - n_real) // N_META

    def tmap(i, s):
        return (mrow + i, 0)

    def cmap(i, s):
        return (jnp.minimum(i, batch - 1) * nb, 0)

    grid_spec = pltpu.PrefetchScalarGridSpec(
        num_scalar_prefetch=1,
        grid=(ntail_blocks,),
        in_specs=[
            pl.BlockSpec((N_META, 1024), tmap),
            pl.BlockSpec((N_META, GQA_KV * LANES), tmap),
            pl.BlockSpec((BLK, GQA_KV * LANES), cmap),
            pl.BlockSpec((N_META, GQA_KV * LANES), tmap),
            pl.BlockSpec((BLK, GQA_KV * LANES), cmap),
        ],
        out_specs=pl.BlockSpec((N_META, 1024), lambda i, s: (i, 0)),
    )
    return pl.pallas_call(
        functools.partial(_win_meta_kernel, batch=batch),
        grid_spec=grid_spec,
        out_shape=jax.ShapeDtypeStruct((ntot - n_real, 1024), BF16),
        compiler_params=_params(("parallel",)),
        name="win_attn_meta",
    )(sink, q, k, k, v, v)


def _attn_out_tile(o_real_ref, o_tail_ref, n_real_tiles):
    is_real = pl.program_id(0) < n_real_tiles
    return jnp.where(is_real, o_real_ref[...], o_tail_ref[...])


def _attn_out_specs(tm, n_real_tiles):
    return [
        pl.BlockSpec((tm, 1024), lambda i: (jnp.minimum(i, n_real_tiles - 1), 0)),
        pl.BlockSpec((tm, 1024), lambda i: (jnp.maximum(i - n_real_tiles, 0), 0)),
    ]


def _wo_kernel(o_real_ref, o_tail_ref, w_ref, h_ref, g_ref, hout_ref, xn_ref, *, n_real_tiles):
    o = _attn_out_tile(o_real_ref, o_tail_ref, n_real_tiles)
    hn = _dot(o, w_ref[...]) + h_ref[...]
    hout_ref[...] = hn
    xn_ref[...] = _rms(hn, g_ref[...]).astype(BF16)


def _wo_call(o_real, o_tail, w, h, g, *, tm):
    ntot = h.shape[0]
    nrt = o_real.shape[0] // tm
    row = lambda i: (i, 0)
    fix = lambda i: (0, 0)
    return pl.pallas_call(
        functools.partial(_wo_kernel, n_real_tiles=nrt),
        grid=(ntot // tm,),
        in_specs=_attn_out_specs(tm, nrt) + [
            pl.BlockSpec((1024, D_MODEL), fix),
            pl.BlockSpec((tm, D_MODEL), row),
            pl.BlockSpec((1, D_MODEL), fix),
        ],
        out_specs=[pl.BlockSpec((tm, D_MODEL), row), pl.BlockSpec((tm, D_MODEL), row)],
        out_shape=[jax.ShapeDtypeStruct((ntot, D_MODEL), F32),
                   jax.ShapeDtypeStruct((ntot, D_MODEL), BF16)],
        compiler_params=_params(("parallel",)),
        name="wo_res_norm",
    )(o_real, o_tail, w, h, g)


def _pack_bf16_pair(a, b):
    ra = pltpu.bitcast(a.astype(BF16).astype(F32), U32)
    rb = pltpu.bitcast(b.astype(BF16).astype(F32), U32)
    return ra | (rb >> 16)


def _unpack_bf16_pair(p):
    a = pltpu.bitcast(p & jnp.uint32(0xFFFF0000), F32).astype(BF16)
    b = pltpu.bitcast(p << 16, F32).astype(BF16)
    return a, b


def _wo_router_kernel(o_real_ref, o_tail_ref, w_ref, h_ref, g_ref, r_ref,
                      hout_ref, xpk_ref, route_ref, cnt_ref, carry_ref, *, n_real_tiles):
    i = pl.program_id(0)
    tm = h_ref.shape[0]

    @pl.when(i == 0)
    def _():
        carry_ref[...] = jnp.zeros_like(carry_ref)

    o = _attn_out_tile(o_real_ref, o_tail_ref, n_real_tiles)
    hn = _dot(o, w_ref[...]) + h_ref[...]
    hout_ref[...] = hn
    y = _rms(hn, g_ref[...])
    half = D_MODEL // 2
    xpk_ref[...] = _pack_bf16_pair(y[:, :half], y[:, half:])

    yhi = y.astype(BF16)
    ylo = (y - yhi.astype(F32)).astype(BF16)
    r2 = r_ref[...]
    t = _dot(yhi, r2)
    logits = t[:, :LANES] + t[:, LANES:] + _dot(ylo, r2[:, :LANES])

    lane = lax.broadcasted_iota(I32, (tm, LANES), 1)
    lm = jnp.where(lane < N_EXPERTS, logits, -jnp.inf)
    m1 = jnp.max(lm, axis=-1, keepdims=True)
    i1 = jnp.min(jnp.where(lm == m1, lane, LANES), axis=-1, keepdims=True)
    lm2 = jnp.where(lane == i1, -jnp.inf, lm)
    m2 = jnp.max(lm2, axis=-1, keepdims=True)
    i2 = jnp.min(jnp.where(lm2 == m2, lane, LANES), axis=-1, keepdims=True)
    e21 = jnp.exp(m2 - m1)
    g0 = 1.0 / (1.0 + e21)
    g1 = e21 / (1.0 + e21)

    onehot = ((lane == i1) | (lane == i2)).astype(F32)
    rr = lax.broadcasted_iota(I32, (tm, tm), 0)
    cc = lax.broadcasted_iota(I32, (tm, tm), 1)
    lower = jnp.where(cc < rr, 1.0, 0.0).astype(BF16)
    before = _dot(lower, onehot.astype(BF16)) + carry_ref[0:1, :]
    rank1 = jnp.sum(jnp.where(lane == i1, before, 0.0), axis=-1, keepdims=True)
    rank2 = jnp.sum(jnp.where(lane == i2, before, 0.0), axis=-1, keepdims=True)
    carry = carry_ref[0:1, :] + jnp.sum(onehot, axis=0, keepdims=True)
    carry_ref[...] = jnp.broadcast_to(carry, carry_ref.shape)
    cnt_ref[...] = jnp.broadcast_to(carry, cnt_ref.shape)

    out = jnp.where(lane == 0, i1.astype(F32), 0.0)
    out = jnp.where(lane == 1, i2.astype(F32), out)
    out = jnp.where(lane == 2, rank1, out)
    out = jnp.where(lane == 3, rank2, out)
    out = jnp.where(lane == 4, g0, out)
    out = jnp.where(lane == 5, g1, out)
    route_ref[...] = out


def _wo_router_call(o_real, o_tail, w, h, g, r2, *, tm):
    ntot = h.shape[0]
    nrt = o_real.shape[0] // tm
    row = lambda i: (i, 0)
    fix = lambda i: (0, 0)
    return pl.pallas_call(
        functools.partial(_wo_router_kernel, n_real_tiles=nrt),
        grid=(ntot // tm,),
        in_specs=_attn_out_specs(tm, nrt) + [
            pl.BlockSpec((1024, D_MODEL), fix),
            pl.BlockSpec((tm, D_MODEL), row),
            pl.BlockSpec((1, D_MODEL), fix),
            pl.BlockSpec((D_MODEL, 2 * LANES), fix),
        ],
        out_specs=[
            pl.BlockSpec((tm, D_MODEL), row),
            pl.BlockSpec((tm, D_MODEL // 2), row),
            pl.BlockSpec((tm, LANES), row),
            pl.BlockSpec((8, LANES), fix),
        ],
        out_shape=[
            jax.ShapeDtypeStruct((ntot, D_MODEL), F32),
            jax.ShapeDtypeStruct((ntot, D_MODEL // 2), U32),
            jax.ShapeDtypeStruct((ntot, LANES), F32),
            jax.ShapeDtypeStruct((8, LANES), F32),
        ],
        scratch_shapes=[pltpu.VMEM((8, LANES), F32)],
        compiler_params=_params(("arbitrary",)),
        name="wo_res_norm_router",
    )(o_real, o_tail, w, h, g, r2)


def _swiglu_chunk(x, wg, wu, wd):
    g = _dot(x, wg)
    u = _dot(x, wu)
    a = (g * (1.0 / (1.0 + jnp.exp(-g)))) * u
    return _dot(a.astype(BF16), wd)


def _ffn_dense_kernel(x_ref, wg_ref, wu_ref, wd_ref, h_ref, g_ref, hout_ref, xn_ref, *, tf):
    x = x_ref[...]
    hn = h_ref[...]
    for c in range(wg_ref.shape[1] // tf):
        cs = slice(c * tf, (c + 1) * tf)
        hn = hn + _swiglu_chunk(x, wg_ref[:, cs], wu_ref[:, cs], wd_ref[cs, :])
    hout_ref[...] = hn
    xn_ref[...] = _rms(hn, g_ref[...]).astype(BF16)


def _ffn_dense_call(x, wg, wu, wd, h, g, *, tm, tf):
    ntot = x.shape[0]
    fdim = wg.shape[1]
    row = lambda i: (i, 0)
    fix = lambda i: (0, 0)
    once = pl.Buffered(1)
    return pl.pallas_call(
        functools.partial(_ffn_dense_kernel, tf=tf),
        grid=(ntot // tm,),
        in_specs=[
            pl.BlockSpec((tm, D_MODEL), row),
            pl.BlockSpec((D_MODEL, fdim), fix, pipeline_mode=once),
            pl.BlockSpec((D_MODEL, fdim), fix, pipeline_mode=once),
            pl.BlockSpec((fdim, D_MODEL), fix, pipeline_mode=once),
            pl.BlockSpec((tm, D_MODEL), row),
            pl.BlockSpec((1, D_MODEL), fix),
        ],
        out_specs=[pl.BlockSpec((tm, D_MODEL), row), pl.BlockSpec((tm, D_MODEL), row)],
        out_shape=[jax.ShapeDtypeStruct((ntot, D_MODEL), F32),
                   jax.ShapeDtypeStruct((ntot, D_MODEL), BF16)],
        compiler_params=_params(("parallel",)),
        name="ffn_dense",
    )(x, wg, wu, wd, h, g)


def _ffn_expert_kernel(be_ref, nu_ref, xs_ref, wg_ref, wu_ref, wd_ref, ys_ref, *, tf):
    del be_ref
    used = pl.program_id(0) < nu_ref[0]

    @pl.when(used)
    def _():
        a, b = _unpack_bf16_pair(xs_ref[...])
        x = jnp.concatenate([a, b], axis=1)
        acc = None
        for c in range(wg_ref.shape[1] // tf):
            cs = slice(c * tf, (c + 1) * tf)
            y = _swiglu_chunk(x, wg_ref[:, cs], wu_ref[:, cs], wd_ref[cs, :])
            acc = y if acc is None else acc + y
        ys_ref[...] = acc

    @pl.when(jnp.logical_not(used))
    def _():
        ys_ref[...] = jnp.zeros_like(ys_ref)


def _ffn_expert_call(block_e, n_used, xs, wg, wu, wd, *, tm, tf):
    rows = xs.shape[0]
    once = pl.Buffered(1)
    wmap = lambda i, be, nu: (be[i], 0, 0)
    grid_spec = pltpu.PrefetchScalarGridSpec(
        num_scalar_prefetch=2,
        grid=(rows // tm,),
        in_specs=[
            pl.BlockSpec((tm, D_MODEL // 2), lambda i, be, nu: (i, 0)),
            pl.BlockSpec((None,) + wg.shape[1:], wmap, pipeline_mode=once),
            pl.BlockSpec((None,) + wu.shape[1:], wmap, pipeline_mode=once),
            pl.BlockSpec((None,) + wd.shape[1:], wmap, pipeline_mode=once),
        ],
        out_specs=pl.BlockSpec((tm, D_MODEL), lambda i, be, nu: (i, 0)),
    )
    return pl.pallas_call(
        functools.partial(_ffn_expert_kernel, tf=tf),
        grid_spec=grid_spec,
        out_shape=jax.ShapeDtypeStruct((rows, D_MODEL), F32),
        compiler_params=_params(("arbitrary",)),
        name="ffn_expert",
    )(block_e, n_used, xs, wg, wu, wd)


DMA_UNROLL = 8


def _row_copy(src, dst, sem):
    return pltpu.make_async_copy(src, dst, sem)


def _dispatch_kernel(dest_ref, x_ref, xs_in_ref, xs_ref, sem):
    del xs_in_ref
    td = x_ref.shape[0]

    def issue(t, carry):
        for s in range(2):
            d = dest_ref[0, s, t]
            _row_copy(x_ref.at[pl.ds(t, 1)], xs_ref.at[pl.ds(d, 1)], sem).start(priority=s)
        return carry

    lax.fori_loop(0, td, issue, 0, unroll=DMA_UNROLL)

    def drain(t, carry):
        for s in range(2):
            _row_copy(x_ref.at[pl.ds(0, 1)], xs_ref.at[pl.ds(0, 1)], sem).wait()
        return carry

    lax.fori_loop(0, td, drain, 0, unroll=DMA_UNROLL)


def _dispatch_call(dest, xpk, xs_init, *, td):
    ntot = xpk.shape[0]
    return pl.pallas_call(
        _dispatch_kernel,
        grid=(ntot // td,),
        in_specs=[
            pl.BlockSpec((1, 2, td), lambda i: (i, 0, 0), memory_space=pltpu.SMEM),
            pl.BlockSpec((td, D_MODEL // 2), lambda i: (i, 0)),
            pl.BlockSpec(memory_space=pl.ANY),
        ],
        out_specs=pl.BlockSpec(memory_space=pl.ANY),
        out_shape=jax.ShapeDtypeStruct(xs_init.shape, U32),
        input_output_aliases={2: 0},
        scratch_shapes=[pltpu.SemaphoreType.DMA(())],
        compiler_params=_params(("arbitrary",)),
        name="moe_dispatch",
    )(dest, xpk, xs_init)


def _combine_kernel(dest_ref, ys_ref, h_ref, route_ref, g_ref, out_ref, y0_ref, y1_ref, sem):
    tc = h_ref.shape[0]
    bufs = (y0_ref, y1_ref)

    def issue(t, carry):
        for s in range(2):
            d = dest_ref[0, s, t]
            _row_copy(ys_ref.at[pl.ds(d, 1)], bufs[s].at[pl.ds(t, 1)], sem).start(priority=s)
        return carry

    lax.fori_loop(0, tc, issue, 0, unroll=DMA_UNROLL)

    def drain(t, carry):
        for s in range(2):
            _row_copy(ys_ref.at[pl.ds(0, 1)], bufs[s].at[pl.ds(0, 1)], sem).wait()
        return carry

    lax.fori_loop(0, tc, drain, 0, unroll=DMA_UNROLL)

    r = route_ref[...]
    lane = lax.broadcasted_iota(I32, r.shape, 1)
    g0 = jnp.sum(jnp.where(lane == 4, r, 0.0), axis=-1, keepdims=True)
    g1 = jnp.sum(jnp.where(lane == 5, r, 0.0), axis=-1, keepdims=True)
    hn = h_ref[...] + (y0_ref[...] * g0 + y1_ref[...] * g1)
    out_ref[...] = _rms(hn, g_ref[...])


def _combine_call(dest, ys, h, route, g, *, tc, tile0, ntiles):
    return pl.pallas_call(
        _combine_kernel,
        grid=(ntiles,),
        in_specs=[
            pl.BlockSpec((1, 2, tc), lambda i: (tile0 + i, 0, 0), memory_space=pltpu.SMEM),
            pl.BlockSpec(memory_space=pl.ANY),
            pl.BlockSpec((tc, D_MODEL), lambda i: (tile0 + i, 0)),
            pl.BlockSpec((tc, LANES), lambda i: (tile0 + i, 0)),
            pl.BlockSpec((1, D_MODEL), lambda i: (0, 0)),
        ],
        out_specs=pl.BlockSpec((tc, D_MODEL), lambda i: (i, 0)),
        out_shape=jax.ShapeDtypeStruct((ntiles * tc, D_MODEL), F32),
        scratch_shapes=[pltpu.VMEM((tc, D_MODEL), F32), pltpu.VMEM((tc, D_MODEL), F32),
                        pltpu.SemaphoreType.DMA(())],
        compiler_params=_params(("arbitrary",)),
        name="moe_combine_norm",
    )(dest, ys, h, route, g)


def _mla_proj_kernel(x_ref, wd_ref, qn_ref, kvn_ref, wuq_ref, wuk_ref, wuv_ref, wuvt_ref,
                     c_ref, s1_ref, s2_ref, q_ref, k_ref, v_ref, vt_ref):
    lat = _dot(x_ref[...], wd_ref[...])
    cq = _rms(lat[:, :MLA_Q_RANK], qn_ref[...]).astype(BF16)
    ckv = _rms(lat[:, MLA_Q_RANK:MLA_Q_RANK + MLA_KV_RANK], kvn_ref[...]).astype(BF16)
    c, s1, s2 = c_ref[...], s1_ref[...], s2_ref[...]
    half = MLA_ROPE // 2
    kr = _rope(lat[:, MLA_Q_RANK + MLA_KV_RANK:], c, s1, s2, half)
    q = _dot(cq, wuq_ref[...])
    k = _dot(ckv, wuk_ref[...])
    scale = (MLA_QK ** -0.5) * LOG2E
    for h in range(MLA_HEADS):
        sl = slice(h * LANES, (h + 1) * LANES)
        q_ref[:, sl] = (_rope(q[:, sl], c, s1, s2, half) * scale).astype(BF16)
        k_ref[:, sl] = (k[:, sl] + kr).astype(BF16)
    v_ref[...] = _dot(ckv, wuv_ref[...]).astype(BF16)
    vt_ref[...] = _dot_nt(wuvt_ref[...], ckv).astype(BF16)


def _mla_proj_call(x, wd, qn, kvn, wuq, wuk, wuv, tabs, *, tm, n_real, seq):
    ntot = x.shape[0]
    nrt, tps = n_real // tm, seq // tm
    row = lambda i: (i, 0)
    fix = lambda i: (0, 0)
    tab_spec = pl.BlockSpec((tm, LANES), lambda i: (_table_index(i, nrt, tps), 0))
    return pl.pallas_call(
        _mla_proj_kernel,
        grid=(ntot // tm,),
        in_specs=[
            pl.BlockSpec((tm, D_MODEL), row),
            pl.BlockSpec(wd.shape, fix),
            pl.BlockSpec((1, MLA_Q_RANK), fix),
            pl.BlockSpec((1, MLA_KV_RANK), fix),
            pl.BlockSpec(wuq.shape, fix),
            pl.BlockSpec(wuk.shape, fix),
            pl.BlockSpec(wuv.shape, fix),
            pl.BlockSpec(wuv.shape[::-1], fix),
            tab_spec, tab_spec, tab_spec,
        ],
        out_specs=[
            pl.BlockSpec((tm, 2048), row),
            pl.BlockSpec((tm, 2048), row),
            pl.BlockSpec((tm, 1024), row),
            pl.BlockSpec((1024, tm), lambda i: (0, i)),
        ],
        out_shape=[
            jax.ShapeDtypeStruct((ntot, 2048), BF16),
            jax.ShapeDtypeStruct((ntot, 2048), BF16),
            jax.ShapeDtypeStruct((ntot, 1024), BF16),
            jax.ShapeDtypeStruct((1024, ntot), BF16),
        ],
        compiler_params=_params(("parallel",)),
        name="mla_proj",
    )(x, wd, qn, kvn, wuq, wuk, wuv, wuv.T, *tabs)


def _mla_attend(q_ref, kr_ref, km_ref, vr_ref, vm_ref, o_ref):
    for pair in range(MLA_HEADS // 2):
        vs = slice(pair * LANES, (pair + 1) * LANES)
        v_r, v_m = vr_ref[:, vs], vm_ref[:, vs]
        outs = []
        for hh in range(2):
            h = 2 * pair + hh
            hs = slice(h * LANES, (h + 1) * LANES)
            qh = q_ref[:, hs]
            s_r = _dot_nt(qh, kr_ref[:, hs])
            s_m = _dot_nt(qh, km_ref[:, hs])
            m = jnp.maximum(jnp.max(s_r, axis=-1, keepdims=True), jnp.max(s_m, axis=-1, keepdims=True))
            p_r = jnp.exp2(s_r - m)
            p_m = jnp.exp2(s_m - m)
            den = jnp.sum(p_r, axis=-1, keepdims=True) + jnp.sum(p_m, axis=-1, keepdims=True)
            acc = _dot(p_r.astype(BF16), v_r) + _dot(p_m.astype(BF16), v_m)
            outs.append(acc / den)
        lane = lax.broadcasted_iota(I32, outs[0].shape, 1)
        o_ref[:, vs] = jnp.where(lane < MLA_V, outs[0], outs[1]).astype(BF16)


META_PER_BLOCK = LANES // N_META
REDUCE_ROWS = 64


def _col_reduce(x, op):
    rows, n = x.shape
    if rows > REDUCE_ROWS and rows % REDUCE_ROWS == 0:
        x = op(x.reshape(rows // REDUCE_ROWS, REDUCE_ROWS, n), axis=0)
    return op(x, axis=0, keepdims=True)


def _mla_attn_kernel(q_ref, kr_ref, kt_ref, vtr_ref, vtt_ref, o_ref, ot_ref):
    b = pl.program_id(0)
    r = lax.broadcasted_iota(I32, (LANES, 1), 0)
    mine = jnp.right_shift(r, int(math.log2(N_META))) == (b % META_PER_BLOCK)
    mbias = jnp.where(mine, 0.0, -jnp.inf).astype(F32)
    def scores(h):
        hs = slice(h * LANES, (h + 1) * LANES)
        qh = q_ref[:, hs]
        return _dot_nt(kr_ref[:, hs], qh), _dot_nt(kt_ref[:, hs], qh) + mbias

    def weighted_values(h, p_r, p_m, den):
        vs = slice(h * MLA_V, (h + 1) * MLA_V)
        acc = _dot(vtr_ref[vs, :], p_r) + _dot(vtt_ref[vs, :], p_m)
        ot_ref[vs, :] = acc / den

    nxt = scores(0)
    pending = None
    for h in range(MLA_HEADS):
        s_r, s_m = nxt
        if h + 1 < MLA_HEADS:
            nxt = scores(h + 1)
        m = jnp.maximum(_col_reduce(s_r, jnp.max), _col_reduce(s_m, jnp.max))
        p_r = jnp.exp2(s_r - m)
        p_m = jnp.exp2(s_m - m)
        den = _col_reduce(p_r, jnp.sum) + _col_reduce(p_m, jnp.sum)
        if pending is not None:
            weighted_values(*pending)
        pending = (h, p_r.astype(BF16), p_m.astype(BF16), den)
    weighted_values(*pending)
    o_ref[...] = ot_ref[...].T.astype(BF16)


def _mla_attn_call(q, k, vt, *, batch, seq, n_real, tq):
    nq = seq // tq
    tail0 = n_real // LANES
    qmap = lambda b, j: (b * nq + j, 0)
    return pl.pallas_call(
        _mla_attn_kernel,
        grid=(batch, nq),
        in_specs=[
            pl.BlockSpec((tq, 2048), qmap),
            pl.BlockSpec((seq, 2048), lambda b, j: (b, 0)),
            pl.BlockSpec((LANES, 2048), lambda b, j: (tail0 + b // META_PER_BLOCK, 0)),
            pl.BlockSpec((1024, seq), lambda b, j: (0, b)),
            pl.BlockSpec((1024, LANES), lambda b, j: (0, tail0 + b // META_PER_BLOCK)),
        ],
        out_specs=pl.BlockSpec((tq, 1024), qmap),
        out_shape=jax.ShapeDtypeStruct((n_real, 1024), BF16),
        scratch_shapes=[pltpu.VMEM((1024, tq), F32)],
        compiler_params=_params(("parallel", "arbitrary")),
        name="mla_attn",
    )(q, k, k, vt, vt)


def _mla_meta_kernel(q_ref, kr_ref, km_ref, vr_ref, vm_ref, o_ref, *, batch):
    i = pl.program_id(0)

    @pl.when(i < batch)
    def _():
        _mla_attend(q_ref, kr_ref, km_ref, vr_ref, vm_ref, o_ref)

    @pl.when(i >= batch)
    def _():
        o_ref[...] = jnp.zeros_like(o_ref)


def _mla_meta_call(q, k, v, *, batch, seq, n_real):
    ntot = q.shape[0]
    mrow = n_real // N_META
    ntail_blocks = (ntot - n_real) // N_META
    tmap = lambda i: (mrow + i, 0)
    rmap = lambda i: (jnp.minimum(i, batch - 1), 0)
    return pl.pallas_call(
        functools.partial(_mla_meta_kernel, batch=batch),
        grid=(ntail_blocks,),
        in_specs=[
            pl.BlockSpec((N_META, 2048), tmap),
            pl.BlockSpec((seq, 2048), rmap),
            pl.BlockSpec((N_META, 2048), tmap),
            pl.BlockSpec((seq, 1024), rmap),
            pl.BlockSpec((N_META, 1024), tmap),
        ],
        out_specs=pl.BlockSpec((N_META, 1024), lambda i: (i, 0)),
        out_shape=jax.ShapeDtypeStruct((ntot - n_real, 1024), BF16),
        compiler_params=_params(("arbitrary",)),
        name="mla_attn_meta",
    )(q, k, k, v, v)


def _prep_gqa(a_wqkv, a_wo):
    d = a_wqkv.shape[0]
    nq = GQA_HEADS * GQA_HD
    nkv = GQA_KV * GQA_HD
    wq = a_wqkv[:, :nq] * (GQA_HD ** -0.5)

    def dup(w):
        w = w.reshape(d, GQA_KV, 1, GQA_HD)
        return jnp.broadcast_to(w, (d, GQA_KV, 2, GQA_HD)).reshape(d, 2 * nkv)

    w = jnp.concatenate([wq, dup(a_wqkv[:, nq:nq + nkv]), dup(a_wqkv[:, nq + nkv:])], axis=1)
    return w.astype(BF16), a_wo.astype(BF16)


def _prep_mla(b_wdkv, b_wuq, b_wukv):
    d = b_wdkv.shape[0]
    lat = MLA_Q_RANK + MLA_KV_RANK
    z = lambda n: jnp.zeros((d, n), F32)
    wd = jnp.concatenate([b_wdkv[:, :lat], z(MLA_NOPE), b_wdkv[:, lat:], z(LANES - MLA_QK)], axis=1)
    wuq = b_wuq.reshape(MLA_Q_RANK, MLA_HEADS, MLA_QK)
    wuq = jnp.pad(wuq, ((0, 0), (0, 0), (0, LANES - MLA_QK))).reshape(MLA_Q_RANK, MLA_HEADS * LANES)
    wukv = b_wukv.reshape(MLA_KV_RANK, MLA_HEADS, MLA_NOPE + MLA_V)
    wuk = jnp.pad(wukv[:, :, :MLA_NOPE], ((0, 0), (0, 0), (0, LANES - MLA_NOPE)))
    wuk = wuk.reshape(MLA_KV_RANK, MLA_HEADS * LANES)
    wuv = wukv[:, :, MLA_NOPE:].reshape(MLA_KV_RANK, MLA_HEADS * MLA_V)
    return wd.astype(BF16), wuq.astype(BF16), wuk.astype(BF16), wuv.astype(BF16)


def _prep_router(m_router):
    hi = m_router.astype(BF16)
    lo = (m_router - hi.astype(F32)).astype(BF16)
    pad = lambda r: jnp.pad(r, ((0, 0), (0, LANES - N_EXPERTS)))
    return jnp.concatenate([pad(hi), pad(lo)], axis=1)


def _positions(seq, n_tail, n_meta_rows):
    real = N_META + jnp.arange(seq, dtype=F32)
    t = jnp.arange(n_tail)
    tail = jnp.where(t < n_meta_rows, t % N_META, 0).astype(F32)
    return jnp.concatenate([real, tail])


def _trunk_flat(x_groups, meta, norm_mix, norm_ffn, norm_final,
                a_wqkv, a_wo, a_sink,
                b_wdkv, b_qnorm, b_kvnorm, b_wuq, b_wukv, b_wo,
                f_wg, f_wu, f_wd,
                m_router, m_wg, m_wu, m_wd):
    seq = x_groups[0].shape[1]
    batches = [x.shape[0] for x in x_groups]
    batch = sum(batches)
    n_real = batch * seq
    n_meta_rows = batch * N_META
    big = _pick_tile(seq, (1024, 512, 256, 128))
    tp = min(big, 512)
    ntot = -(-(n_real + n_meta_rows) // big) * big
    n_tail = ntot - n_real

    tail = jnp.concatenate([
        jnp.broadcast_to(meta.astype(F32)[None], (batch, N_META, D_MODEL)).reshape(n_meta_rows, D_MODEL),
        jnp.zeros((n_tail - n_meta_rows, D_MODEL), F32)], axis=0)
    h = jnp.concatenate([x.reshape(-1, D_MODEL) for x in x_groups] + [tail], axis=0)

    pos = _positions(seq, n_tail, n_meta_rows)
    tabs_a = _rope_lane_tables(pos, GQA_THETA, GQA_ROT, GQA_HD, 0)
    tabs_b = _rope_lane_tables(pos, MLA_THETA, MLA_ROPE, LANES, MLA_NOPE)
    row = lambda v: v.reshape(1, -1).astype(F32)

    wqkv, wo_a = _prep_gqa(a_wqkv[0], a_wo[0])
    q, k, v = _qkv_call(h, row(norm_mix[0]), wqkv, tabs_a, tm=tp, n_real=n_real, seq=seq)
    sink = a_sink[0].astype(F32)
    o = _win_attn_call(sink, q, k, v, batch=batch, seq=seq, n_real=n_real)
    o_tail = _win_meta_call(sink, q, k, v, batch=batch, seq=seq, n_real=n_real)
    h, xn = _wo_call(o, o_tail, wo_a, h, row(norm_ffn[0]), tm=big)
    fdim = f_wg.shape[2]
    tf = _pick_tile(fdim, (512, 256, 128))
    h, xn = _ffn_dense_call(xn, f_wg[0].astype(BF16), f_wu[0].astype(BF16), f_wd[0].astype(BF16),
                            h, row(norm_mix[1]), tm=tp, tf=tf)

    wd, wuq, wuk, wuv = _prep_mla(b_wdkv[0], b_wuq[0], b_wukv[0])
    q, k, v, vt = _mla_proj_call(xn, wd, row(b_qnorm[0]), row(b_kvnorm[0]), wuq, wuk, wuv, tabs_b,
                                 tm=tp, n_real=n_real, seq=seq)
    tq = _pick_tile(seq, (256, 128))
    o = _mla_attn_call(q, k, vt, batch=batch, seq=seq, n_real=n_real, tq=tq)
    o_tail = _mla_meta_call(q, k, v, batch=batch, seq=seq, n_real=n_real)
    tr = tp
    h, xpk, route, counts = _wo_router_call(o, o_tail, b_wo[0].astype(BF16), h, row(norm_ffn[1]),
                                            _prep_router(m_router[0]), tm=tr)

    tmx = big
    counts = counts[0, :N_EXPERTS].astype(I32)
    padded = ((counts + tmx - 1) // tmx) * tmx
    pend = jnp.cumsum(padded)
    pstart = pend - padded
    e_idx = route[:, 0:2].astype(I32)
    rank = route[:, 2:4].astype(I32)
    dest = pstart[e_idx] + rank
    n_assign = 2 * ntot
    n_blocks = -(-n_assign // tmx) + N_EXPERTS
    rows = n_blocks * tmx
    block_e = jnp.clip(jnp.searchsorted(pend, jnp.arange(n_blocks, dtype=I32) * tmx, side="right"),
                       0, N_EXPERTS - 1).astype(I32)
    n_used = (pend[-1] // tmx).astype(I32).reshape(1)
    block_e = jnp.where(jnp.arange(n_blocks) < n_used[0], block_e, block_e[jnp.maximum(n_used[0] - 1, 0)])

    td = tp
    dest_t = dest.reshape(ntot // td, td, 2).transpose(0, 2, 1)
    xs = _dispatch_call(dest_t, xpk, jnp.zeros((rows, D_MODEL // 2), U32), td=td)
    edim = m_wg.shape[3]
    tfx = _pick_tile(edim, (512, 256, 128))
    ys = _ffn_expert_call(block_e, n_used, xs, m_wg[0].astype(BF16), m_wu[0].astype(BF16),
                          m_wd[0].astype(BF16), tm=tmx, tf=tfx)

    outs = []
    tile0 = 0
    for b, x in zip(batches, x_groups):
        ntiles = b * seq // td
        y = _combine_call(dest_t, ys, h, route, row(norm_final), tc=td, tile0=tile0, ntiles=ntiles)
        outs.append(y.reshape(b, seq, D_MODEL))
        tile0 += ntiles
    return tuple(outs)


def kernel(x_prompt, x_sample, meta, norm_mix, norm_ffn, norm_final, a_wqkv, a_wo, a_sink, b_wdkv, b_qnorm, b_kvnorm, b_wuq, b_wukv, b_wo, f_wg, f_wu, f_wd, m_router, m_wg, m_wu, m_wd):
    assert x_prompt.shape[1] == x_sample.shape[1] and x_prompt.shape[1] % BLK == 0
    assert norm_mix.shape[0] == 2, "two layers: windowed GQA + dense FFN, then MLA + MoE"
    return _trunk_flat((x_prompt, x_sample), meta, norm_mix, norm_ffn, norm_final,
                       a_wqkv, a_wo, a_sink,
                       b_wdkv, b_qnorm, b_kvnorm, b_wuq, b_wukv, b_wo,
                       f_wg, f_wu, f_wd,
                       m_router, m_wg, m_wu, m_wd)
```

```python
import functools
import math

import jax
import jax.numpy as jnp
from jax import lax
from jax.experimental import pallas as pl
from jax.experimental.pallas import tpu as pltpu

F32 = jnp.float32
BF16 = jnp.bfloat16
U32 = jnp.uint32
I32 = jnp.int32

D_MODEL = 1024
N_META = 16
RMS_EPS = 1e-6
BLK = 128
GQA_HEADS = 16
GQA_KV = 4
GQA_GROUP = 4
GQA_HD = 64
GQA_ROT = 16
GQA_THETA = 500000.0
MLA_HEADS = 16
MLA_NOPE = 64
MLA_ROPE = 32
MLA_V = 64
MLA_Q_RANK = 384
MLA_KV_RANK = 256
MLA_THETA = 10000.0
MLA_QK = MLA_NOPE + MLA_ROPE
N_EXPERTS = 8
LOG2E = 1.4426950408889634

LANES = 128
VMEM_LIMIT = 56 * 1024 * 1024


def _params(sem, vmem=VMEM_LIMIT):
    return pltpu.CompilerParams(dimension_semantics=sem, vmem_limit_bytes=vmem)


def _rms(x, g):
    ms = jnp.mean(x * x, axis=-1, keepdims=True)
    return x * lax.rsqrt(ms + RMS_EPS) * g


def _dot(a, b):
    return jnp.dot(a, b, preferred_element_type=F32)


def _dot_nt(a, b):
    return lax.dot_general(a, b, (((1,), (1,)), ((), ())), preferred_element_type=F32)


def _pick_tile(n, candidates):
    for c in candidates:
        if n % c == 0:
            return c
    raise ValueError(f"no tile in {candidates} divides {n}")


def _rope_lane_tables(pos, theta, rot, period, offset):
    half = rot // 2
    inv = jnp.power(jnp.float32(theta), -(jnp.arange(0, rot, 2, dtype=F32) / rot))
    ang = pos[:, None] * inv[None, :]
    cos, sin = jnp.cos(ang), jnp.sin(ang)
    lane = jnp.arange(LANES)
    r = (lane % period) - offset
    is_x1 = (r >= 0) & (r < half)
    is_x2 = (r >= half) & (r < rot)
    f = jnp.clip(jnp.where(is_x2, r - half, r), 0, half - 1)
    cos_l = cos[:, f]
    sin_l = sin[:, f]
    c = jnp.where((is_x1 | is_x2)[None, :], cos_l, 1.0)
    s1 = jnp.where(is_x2[None, :], sin_l, 0.0)
    s2 = jnp.where(is_x1[None, :], -sin_l, 0.0)
    return c.astype(F32), s1.astype(F32), s2.astype(F32)


def _rope(x, c, s1, s2, half):
    return x * c + pltpu.roll(x, half, 1) * s1 + pltpu.roll(x, LANES - half, 1) * s2


def _qkv_kernel(h_ref, g_ref, w_ref, c_ref, s1_ref, s2_ref, q_ref, k_ref, v_ref):
    xn = _rms(h_ref[...], g_ref[...]).astype(BF16)
    qkv = _dot(xn, w_ref[...])
    c, s1, s2 = c_ref[...], s1_ref[...], s2_ref[...]
    nq = GQA_HEADS * GQA_HD // LANES
    nk = GQA_KV
    for i in range(nq):
        q_ref[:, i * LANES:(i + 1) * LANES] = _rope(
            qkv[:, i * LANES:(i + 1) * LANES], c, s1, s2, GQA_ROT // 2).astype(BF16)
    for i in range(nk):
        lo = (nq + i) * LANES
        k_ref[:, i * LANES:(i + 1) * LANES] = _rope(
            qkv[:, lo:lo + LANES], c, s1, s2, GQA_ROT // 2).astype(BF16)
    v_ref[...] = qkv[:, (nq + nk) * LANES:].astype(BF16)


def _table_index(i, n_real_tiles, tiles_per_seq):
    return jnp.where(i < n_real_tiles, i % tiles_per_seq, tiles_per_seq + i - n_real_tiles)


def _qkv_call(h, g, w, tabs, *, tm, n_real, seq):
    ntot = h.shape[0]
    nrt, tps = n_real // tm, seq // tm
    tab_spec = pl.BlockSpec((tm, LANES), lambda i: (_table_index(i, nrt, tps), 0))
    return pl.pallas_call(
        _qkv_kernel,
        grid=(ntot // tm,),
        in_specs=[
            pl.BlockSpec((tm, D_MODEL), lambda i: (i, 0)),
            pl.BlockSpec((1, D_MODEL), lambda i: (0, 0)),
            pl.BlockSpec(w.shape, lambda i: (0, 0)),
            tab_spec, tab_spec, tab_spec,
        ],
        out_specs=[
            pl.BlockSpec((tm, 1024), lambda i: (i, 0)),
            pl.BlockSpec((tm, GQA_KV * LANES), lambda i: (i, 0)),
            pl.BlockSpec((tm, GQA_KV * LANES), lambda i: (i, 0)),
        ],
        out_shape=[
            jax.ShapeDtypeStruct((ntot, 1024), BF16),
            jax.ShapeDtypeStruct((ntot, GQA_KV * LANES), BF16),
            jax.ShapeDtypeStruct((ntot, GQA_KV * LANES), BF16),
        ],
        compiler_params=_params(("parallel",)),
        name="qkv_rope",
    )(h, g, w, *tabs)


def _gqa_attend(sink_ref, q_ref, o_ref, k_pieces, v_pieces, bias):
    nq = q_ref.shape[0]
    nkeys = bias.shape[1]
    have = sum(p.shape[0] for p in k_pieces)
    row = lax.broadcasted_iota(I32, (2 * nq, 1), 0)
    lane_o = lax.broadcasted_iota(I32, (2 * nq, LANES), 1)
    lane_k = lax.broadcasted_iota(I32, (nkeys, LANES), 1)
    pad = [jnp.zeros((nkeys - have, LANES), BF16)] if nkeys > have else []

    def scores(kv):
        sl = slice(kv * LANES, (kv + 1) * LANES)
        kcat = jnp.concatenate([p[:, sl] for p in k_pieces] + pad, axis=0)
        zero = jnp.zeros_like(kcat)
        lo = slice(2 * kv * LANES, (2 * kv + 1) * LANES)
        hi = slice((2 * kv + 1) * LANES, (2 * kv + 2) * LANES)
        qp = jnp.concatenate([q_ref[:, lo], q_ref[:, hi]], axis=0)
        return (_dot_nt(qp, jnp.where(lane_k < GQA_HD, kcat, zero)) + bias,
                _dot_nt(qp, jnp.where(lane_k >= GQA_HD, kcat, zero)) + bias)

    def softmax(kv, ss):
        out = []
        for par, s in enumerate(ss):
            head = kv * GQA_GROUP + par
            sink = jnp.where(row < nq, sink_ref[head], sink_ref[head + 2])
            m = jnp.maximum(jnp.max(s, axis=-1, keepdims=True), sink)
            p = jnp.exp(s - m)
            den = jnp.sum(p, axis=-1, keepdims=True) + jnp.exp(sink - m)
            out.append((p.astype(BF16), den))
        return out

    def weighted_values(kv, pd):
        sl = slice(kv * LANES, (kv + 1) * LANES)
        vcat = jnp.concatenate([p[:, sl] for p in v_pieces] + pad, axis=0)
        outs = [_dot(p, vcat) / den for p, den in pd]
        o = jnp.where(lane_o < GQA_HD, outs[0], outs[1]).astype(BF16)
        o_ref[:, 2 * kv * LANES:(2 * kv + 1) * LANES] = o[:nq]
        o_ref[:, (2 * kv + 1) * LANES:(2 * kv + 2) * LANES] = o[nq:]

    nxt = scores(0)
    pending = None
    for kv in range(GQA_KV):
        ss = nxt
        if kv + 1 < GQA_KV:
            nxt = scores(kv + 1)
        pd = softmax(kv, ss)
        if pending is not None:
            weighted_values(*pending)
        pending = (kv, pd)
    weighted_values(*pending)


def _win_attn_kernel(sink_ref, q_ref, kp_ref, kc_ref, kn_ref, vp_ref, vc_ref, vn_ref,
                     km_ref, vm_ref, o_ref, *, nb):
    j = pl.program_id(1)
    nkeys = 4 * BLK
    qi = lax.broadcasted_iota(I32, (2 * BLK, nkeys), 0) & (BLK - 1)
    c = lax.broadcasted_iota(I32, (2 * BLK, nkeys), 1)
    neg = jnp.float32(-jnp.inf)
    b_p = jnp.where((c >= qi) & (j > 0), 0.0, neg)
    b_n = jnp.where(((c - 2 * BLK) <= qi) & (j < nb - 1), 0.0, neg)
    b_m = jnp.where(c < 3 * BLK + N_META, 0.0, neg)
    bias = jnp.where(c < BLK, b_p, jnp.where(c < 2 * BLK, 0.0, jnp.where(c < 3 * BLK, b_n, b_m)))
    _gqa_attend(sink_ref, q_ref, o_ref, (kp_ref, kc_ref, kn_ref, km_ref),
                (vp_ref, vc_ref, vn_ref, vm_ref), bias)


def _win_attn_call(sink, q, k, v, *, batch, seq, n_real):
    ntot = q.shape[0]
    nb = seq // BLK
    mrow = n_real // N_META

    def qmap(b, j, s):
        return (b * nb + j, 0)

    def pmap(b, j, s):
        return (b * nb + jnp.maximum(j - 1, 0), 0)

    def nmap(b, j, s):
        return (b * nb + jnp.minimum(j + 1, nb - 1), 0)

    def mmap(b, j, s):
        return (mrow + b, 0)

    kvw = GQA_KV * LANES
    kv_spec = lambda f: pl.BlockSpec((BLK, kvw), f)
    grid_spec = pltpu.PrefetchScalarGridSpec(
        num_scalar_prefetch=1,
        grid=(batch, nb),
        in_specs=[
            pl.BlockSpec((BLK, 1024), qmap),
            kv_spec(pmap), kv_spec(qmap), kv_spec(nmap),
            kv_spec(pmap), kv_spec(qmap), kv_spec(nmap),
            pl.BlockSpec((N_META, kvw), mmap),
            pl.BlockSpec((N_META, kvw), mmap),
        ],
        out_specs=pl.BlockSpec((BLK, 1024), qmap),
    )
    return pl.pallas_call(
        functools.partial(_win_attn_kernel, nb=nb),
        grid_spec=grid_spec,
        out_shape=jax.ShapeDtypeStruct((n_real, 1024), BF16),
        compiler_params=_params(("parallel", "parallel")),
        name="win_attn",
    )(sink, q, k, k, k, v, v, v, k, v)


def _win_meta_kernel(sink_ref, q_ref, km_ref, kc_ref, vm_ref, vc_ref, o_ref, *, batch):
    i = pl.program_id(0)

    @pl.when(i < batch)
    def _():
        nkeys = 2 * BLK
        qp = lax.broadcasted_iota(I32, (2 * N_META, nkeys), 0) & (N_META - 1)
        c = lax.broadcasted_iota(I32, (2 * N_META, nkeys), 1)
        neg = jnp.float32(-jnp.inf)
        b_c = jnp.where((N_META + c - qp) <= BLK, 0.0, neg)
        bias = jnp.where(c < BLK, b_c, jnp.where(c < BLK + N_META, 0.0, neg))
        _gqa_attend(sink_ref, q_ref, o_ref, (kc_ref, km_ref), (vc_ref, vm_ref), bias)

    @pl.when(i >= batch)
    def _():
        o_ref[...] = jnp.zeros_like(o_ref)


def _win_meta_call(sink, q, k, v, *, batch, seq, n_real):
    ntot = q.shape[0]
    nb = seq // BLK
    mrow = n_real // N_META
    ntail_blocks = (ntot - n_real) // N_META

    def tmap(i, s):
        return (mrow + i, 0)

    def cmap(i, s):
        return (jnp.minimum(i, batch - 1) * nb, 0)

    grid_spec = pltpu.PrefetchScalarGridSpec(
        num_scalar_prefetch=1,
        grid=(ntail_blocks,),
        in_specs=[
            pl.BlockSpec((N_META, 1024), tmap),
            pl.BlockSpec((N_META, GQA_KV * LANES), tmap),
            pl.BlockSpec((BLK, GQA_KV * LANES), cmap),
            pl.BlockSpec((N_META, GQA_KV * LANES), tmap),
            pl.BlockSpec((BLK, GQA_KV * LANES), cmap),
        ],
        out_specs=pl.BlockSpec((N_META, 1024), lambda i, s: (i, 0)),
    )
    return pl.pallas_call(
        functools.partial(_win_meta_kernel, batch=batch),
        grid_spec=grid_spec,
        out_shape=jax.ShapeDtypeStruct((ntot - n_real, 1024), BF16),
        compiler_params=_params(("parallel",)),
        name="win_attn_meta",
    )(sink, q, k, k, v, v)


def _attn_out_tile(o_real_ref, o_tail_ref, n_real_tiles):
    is_real = pl.program_id(0) < n_real_tiles
    return jnp.where(is_real, o_real_ref[...], o_tail_ref[...])


def _attn_out_specs(tm, n_real_tiles):
    return [
        pl.BlockSpec((tm, 1024), lambda i: (jnp.minimum(i, n_real_tiles - 1), 0)),
        pl.BlockSpec((tm, 1024), lambda i: (jnp.maximum(i - n_real_tiles, 0), 0)),
    ]


def _wo_kernel(o_real_ref, o_tail_ref, w_ref, h_ref, g_ref, hout_ref, xn_ref, *, n_real_tiles):
    o = _attn_out_tile(o_real_ref, o_tail_ref, n_real_tiles)
    hn = _dot(o, w_ref[...]) + h_ref[...]
    hout_ref[...] = hn
    xn_ref[...] = _rms(hn, g_ref[...]).astype(BF16)


def _wo_call(o_real, o_tail, w, h, g, *, tm):
    ntot = h.shape[0]
    nrt = o_real.shape[0] // tm
    row = lambda i: (i, 0)
    fix = lambda i: (0, 0)
    return pl.pallas_call(
        functools.partial(_wo_kernel, n_real_tiles=nrt),
        grid=(ntot // tm,),
        in_specs=_attn_out_specs(tm, nrt) + [
            pl.BlockSpec((1024, D_MODEL), fix),
            pl.BlockSpec((tm, D_MODEL), row),
            pl.BlockSpec((1, D_MODEL), fix),
        ],
        out_specs=[pl.BlockSpec((tm, D_MODEL), row), pl.BlockSpec((tm, D_MODEL), row)],
        out_shape=[jax.ShapeDtypeStruct((ntot, D_MODEL), F32),
                   jax.ShapeDtypeStruct((ntot, D_MODEL), BF16)],
        compiler_params=_params(("parallel",)),
        name="wo_res_norm",
    )(o_real, o_tail, w, h, g)


XPK_ROWS = D_MODEL // 2 // LANES
YS_ROWS = D_MODEL // LANES


def _store_token_major(ref, x):
    t, n = x.shape
    k = n // LANES
    for s in range(k):
        ref[pl.ds(s, t, stride=k), :] = x[:, s * LANES:(s + 1) * LANES]


def _load_token_major(ref, k):
    t = ref.shape[0] // k
    return [ref[pl.ds(s, t, stride=k), :] for s in range(k)]


def _pack_bf16_pair(a, b):
    ra = pltpu.bitcast(a.astype(BF16).astype(F32), U32)
    rb = pltpu.bitcast(b.astype(BF16).astype(F32), U32)
    return ra | (rb >> 16)


def _unpack_bf16_pair(p):
    a = pltpu.bitcast(p & jnp.uint32(0xFFFF0000), F32).astype(BF16)
    b = pltpu.bitcast(p << 16, F32).astype(BF16)
    return a, b


def _wo_router_kernel(o_real_ref, o_tail_ref, w_ref, h_ref, g_ref, r_ref,
                      hout_ref, xpk_ref, route_ref, cnt_ref, carry_ref, *, n_real_tiles):
    i = pl.program_id(0)
    tm = h_ref.shape[0]

    @pl.when(i == 0)
    def _():
        carry_ref[...] = jnp.zeros_like(carry_ref)

    o = _attn_out_tile(o_real_ref, o_tail_ref, n_real_tiles)
    hn = _dot(o, w_ref[...]) + h_ref[...]
    hout_ref[...] = hn
    y = _rms(hn, g_ref[...])
    half = D_MODEL // 2
    _store_token_major(xpk_ref, _pack_bf16_pair(y[:, :half], y[:, half:]))

    yhi = y.astype(BF16)
    ylo = (y - yhi.astype(F32)).astype(BF16)
    r2 = r_ref[...]
    t = _dot(yhi, r2)
    logits = t[:, :LANES] + t[:, LANES:] + _dot(ylo, r2[:, :LANES])

    lane = lax.broadcasted_iota(I32, (tm, LANES), 1)
    lm = jnp.where(lane < N_EXPERTS, logits, -jnp.inf)
    m1 = jnp.max(lm, axis=-1, keepdims=True)
    i1 = jnp.min(jnp.where(lm == m1, lane, LANES), axis=-1, keepdims=True)
    lm2 = jnp.where(lane == i1, -jnp.inf, lm)
    m2 = jnp.max(lm2, axis=-1, keepdims=True)
    i2 = jnp.min(jnp.where(lm2 == m2, lane, LANES), axis=-1, keepdims=True)
    e21 = jnp.exp(m2 - m1)
    g0 = 1.0 / (1.0 + e21)
    g1 = e21 / (1.0 + e21)

    onehot = ((lane == i1) | (lane == i2)).astype(F32)
    rr = lax.broadcasted_iota(I32, (tm, tm), 0)
    cc = lax.broadcasted_iota(I32, (tm, tm), 1)
    lower = jnp.where(cc < rr, 1.0, 0.0).astype(BF16)
    before = _dot(lower, onehot.astype(BF16)) + carry_ref[0:1, :]
    rank1 = jnp.sum(jnp.where(lane == i1, before, 0.0), axis=-1, keepdims=True)
    rank2 = jnp.sum(jnp.where(lane == i2, before, 0.0), axis=-1, keepdims=True)
    carry = carry_ref[0:1, :] + jnp.sum(onehot, axis=0, keepdims=True)
    carry_ref[...] = jnp.broadcast_to(carry, carry_ref.shape)
    cnt_ref[...] = jnp.broadcast_to(carry, cnt_ref.shape)

    out = jnp.where(lane == 0, i1.astype(F32), 0.0)
    out = jnp.where(lane == 1, i2.astype(F32), out)
    out = jnp.where(lane == 2, rank1, out)
    out = jnp.where(lane == 3, rank2, out)
    out = jnp.where(lane == 4, g0, out)
    out = jnp.where(lane == 5, g1, out)
    route_ref[...] = out


def _wo_router_call(o_real, o_tail, w, h, g, r2, *, tm):
    ntot = h.shape[0]
    nrt = o_real.shape[0] // tm
    row = lambda i: (i, 0)
    fix = lambda i: (0, 0)
    return pl.pallas_call(
        functools.partial(_wo_router_kernel, n_real_tiles=nrt),
        grid=(ntot // tm,),
        in_specs=_attn_out_specs(tm, nrt) + [
            pl.BlockSpec((1024, D_MODEL), fix),
            pl.BlockSpec((tm, D_MODEL), row),
            pl.BlockSpec((1, D_MODEL), fix),
            pl.BlockSpec((D_MODEL, 2 * LANES), fix),
        ],
        out_specs=[
            pl.BlockSpec((tm, D_MODEL), row),
            pl.BlockSpec((tm * XPK_ROWS, LANES), row),
            pl.BlockSpec((tm, LANES), row),
            pl.BlockSpec((8, LANES), fix),
        ],
        out_shape=[
            jax.ShapeDtypeStruct((ntot, D_MODEL), F32),
            jax.ShapeDtypeStruct((ntot * XPK_ROWS, LANES), U32),
            jax.ShapeDtypeStruct((ntot, LANES), F32),
            jax.ShapeDtypeStruct((8, LANES), F32),
        ],
        scratch_shapes=[pltpu.VMEM((8, LANES), F32)],
        compiler_params=_params(("arbitrary",)),
        name="wo_res_norm_router",
    )(o_real, o_tail, w, h, g, r2)


def _swiglu_chunk(x, wg, wu, wd):
    g = _dot(x, wg)
    u = _dot(x, wu)
    a = (g * (1.0 / (1.0 + jnp.exp(-g)))) * u
    return _dot(a.astype(BF16), wd)


def _ffn_dense_kernel(x_ref, wg_ref, wu_ref, wd_ref, h_ref, g_ref, hout_ref, xn_ref, *, tf):
    x = x_ref[...]
    hn = h_ref[...]
    for c in range(wg_ref.shape[1] // tf):
        cs = slice(c * tf, (c + 1) * tf)
        hn = hn + _swiglu_chunk(x, wg_ref[:, cs], wu_ref[:, cs], wd_ref[cs, :])
    hout_ref[...] = hn
    xn_ref[...] = _rms(hn, g_ref[...]).astype(BF16)


def _ffn_dense_call(x, wg, wu, wd, h, g, *, tm, tf):
    ntot = x.shape[0]
    fdim = wg.shape[1]
    row = lambda i: (i, 0)
    fix = lambda i: (0, 0)
    once = pl.Buffered(1)
    return pl.pallas_call(
        functools.partial(_ffn_dense_kernel, tf=tf),
        grid=(ntot // tm,),
        in_specs=[
            pl.BlockSpec((tm, D_MODEL), row),
            pl.BlockSpec((D_MODEL, fdim), fix, pipeline_mode=once),
            pl.BlockSpec((D_MODEL, fdim), fix, pipeline_mode=once),
            pl.BlockSpec((fdim, D_MODEL), fix, pipeline_mode=once),
            pl.BlockSpec((tm, D_MODEL), row),
            pl.BlockSpec((1, D_MODEL), fix),
        ],
        out_specs=[pl.BlockSpec((tm, D_MODEL), row), pl.BlockSpec((tm, D_MODEL), row)],
        out_shape=[jax.ShapeDtypeStruct((ntot, D_MODEL), F32),
                   jax.ShapeDtypeStruct((ntot, D_MODEL), BF16)],
        compiler_params=_params(("parallel",)),
        name="ffn_dense",
    )(x, wg, wu, wd, h, g)


def _ffn_expert_kernel(be_ref, nu_ref, xs_ref, wg_ref, wu_ref, wd_ref, ys_ref, *, tf):
    del be_ref
    used = pl.program_id(0) < nu_ref[0]

    @pl.when(used)
    def _():
        halves = [_unpack_bf16_pair(blk) for blk in _load_token_major(xs_ref, XPK_ROWS)]
        x = jnp.concatenate([a for a, _ in halves] + [b for _, b in halves], axis=1)
        acc = None
        for c in range(wg_ref.shape[1] // tf):
            cs = slice(c * tf, (c + 1) * tf)
            y = _swiglu_chunk(x, wg_ref[:, cs], wu_ref[:, cs], wd_ref[cs, :])
            acc = y if acc is None else acc + y
        _store_token_major(ys_ref, acc)

    @pl.when(jnp.logical_not(used))
    def _():
        ys_ref[...] = jnp.zeros_like(ys_ref)


def _ffn_expert_call(block_e, n_used, xs, wg, wu, wd, *, tm, tf):
    rows = xs.shape[0] // XPK_ROWS
    once = pl.Buffered(1)
    wmap = lambda i, be, nu: (be[i], 0, 0)
    grid_spec = pltpu.PrefetchScalarGridSpec(
        num_scalar_prefetch=2,
        grid=(rows // tm,),
        in_specs=[
            pl.BlockSpec((tm * XPK_ROWS, LANES), lambda i, be, nu: (i, 0)),
            pl.BlockSpec((None,) + wg.shape[1:], wmap, pipeline_mode=once),
            pl.BlockSpec((None,) + wu.shape[1:], wmap, pipeline_mode=once),
            pl.BlockSpec((None,) + wd.shape[1:], wmap, pipeline_mode=once),
        ],
        out_specs=pl.BlockSpec((tm * YS_ROWS, LANES), lambda i, be, nu: (i, 0)),
    )
    return pl.pallas_call(
        functools.partial(_ffn_expert_kernel, tf=tf),
        grid_spec=grid_spec,
        out_shape=jax.ShapeDtypeStruct((rows * YS_ROWS, LANES), F32),
        compiler_params=_params(("arbitrary",)),
        name="ffn_expert",
    )(block_e, n_used, xs, wg, wu, wd)


DMA_UNROLL = 8


def _token_copy(src, src_tok, dst, dst_tok, rows, sem):
    return pltpu.make_async_copy(src.at[pl.ds(pl.multiple_of(src_tok * rows, rows), rows)],
                                 dst.at[pl.ds(pl.multiple_of(dst_tok * rows, rows), rows)], sem)


def _dispatch_kernel(dest_ref, x_ref, xs_in_ref, xs_ref, sem):
    del xs_in_ref
    td = x_ref.shape[0] // XPK_ROWS

    def issue(t, carry):
        for s in range(2):
            _token_copy(x_ref, t, xs_ref, dest_ref[0, s, t], XPK_ROWS, sem).start(priority=s)
        return carry

    lax.fori_loop(0, td, issue, 0, unroll=DMA_UNROLL)

    def drain(t, carry):
        for s in range(2):
            _token_copy(x_ref, 0, xs_ref, 0, XPK_ROWS, sem).wait()
        return carry

    lax.fori_loop(0, td, drain, 0, unroll=DMA_UNROLL)


def _dispatch_call(dest, xpk, xs_init, *, td):
    ntot = xpk.shape[0] // XPK_ROWS
    return pl.pallas_call(
        _dispatch_kernel,
        grid=(ntot // td,),
        in_specs=[
            pl.BlockSpec((1, 2, td), lambda i: (i, 0, 0), memory_space=pltpu.SMEM),
            pl.BlockSpec((td * XPK_ROWS, LANES), lambda i: (i, 0)),
            pl.BlockSpec(memory_space=pl.ANY),
        ],
        out_specs=pl.BlockSpec(memory_space=pl.ANY),
        out_shape=jax.ShapeDtypeStruct(xs_init.shape, U32),
        input_output_aliases={2: 0},
        scratch_shapes=[pltpu.SemaphoreType.DMA(())],
        compiler_params=_params(("arbitrary",)),
        name="moe_dispatch",
    )(dest, xpk, xs_init)


def _combine_kernel(dest_ref, ys_ref, h_ref, route_ref, g_ref, out_ref, y0_ref, y1_ref, sem):
    tc = h_ref.shape[0]
    bufs = (y0_ref, y1_ref)

    def issue(t, carry):
        for s in range(2):
            _token_copy(ys_ref, dest_ref[0, s, t], bufs[s], t, YS_ROWS, sem).start(priority=s)
        return carry

    lax.fori_loop(0, tc, issue, 0, unroll=DMA_UNROLL)

    def drain(t, carry):
        for s in range(2):
            _token_copy(ys_ref, 0, bufs[s], 0, YS_ROWS, sem).wait()
        return carry

    lax.fori_loop(0, tc, drain, 0, unroll=DMA_UNROLL)

    r = route_ref[...]
    lane = lax.broadcasted_iota(I32, r.shape, 1)
    g0 = jnp.sum(jnp.where(lane == 4, r, 0.0), axis=-1, keepdims=True)
    g1 = jnp.sum(jnp.where(lane == 5, r, 0.0), axis=-1, keepdims=True)
    y0 = _load_token_major(y0_ref, YS_ROWS)
    y1 = _load_token_major(y1_ref, YS_ROWS)
    moe = jnp.concatenate([a * g0 + b * g1 for a, b in zip(y0, y1)], axis=1)
    out_ref[...] = _rms(h_ref[...] + moe, g_ref[...])


def _combine_call(dest, ys, h, route, g, *, tc, tile0, ntiles):
    return pl.pallas_call(
        _combine_kernel,
        grid=(ntiles,),
        in_specs=[
            pl.BlockSpec((1, 2, tc), lambda i: (tile0 + i, 0, 0), memory_space=pltpu.SMEM),
            pl.BlockSpec(memory_space=pl.ANY),
            pl.BlockSpec((tc, D_MODEL), lambda i: (tile0 + i, 0)),
            pl.BlockSpec((tc, LANES), lambda i: (tile0 + i, 0)),
            pl.BlockSpec((1, D_MODEL), lambda i: (0, 0)),
        ],
        out_specs=pl.BlockSpec((tc, D_MODEL), lambda i: (i, 0)),
        out_shape=jax.ShapeDtypeStruct((ntiles * tc, D_MODEL), F32),
        scratch_shapes=[pltpu.VMEM((tc * YS_ROWS, LANES), F32), pltpu.VMEM((tc * YS_ROWS, LANES), F32),
                        pltpu.SemaphoreType.DMA(())],
        compiler_params=_params(("arbitrary",)),
        name="moe_combine_norm",
    )(dest, ys, h, route, g)


def _mla_proj_kernel(x_ref, wd_ref, qn_ref, kvn_ref, wuq_ref, wuk_ref, wuv_ref, wuvt_ref,
                     c_ref, s1_ref, s2_ref, q_ref, k_ref, v_ref, vt_ref):
    lat = _dot(x_ref[...], wd_ref[...])
    cq = _rms(lat[:, :MLA_Q_RANK], qn_ref[...]).astype(BF16)
    ckv = _rms(lat[:, MLA_Q_RANK:MLA_Q_RANK + MLA_KV_RANK], kvn_ref[...]).astype(BF16)
    c, s1, s2 = c_ref[...], s1_ref[...], s2_ref[...]
    half = MLA_ROPE // 2
    kr = _rope(lat[:, MLA_Q_RANK + MLA_KV_RANK:], c, s1, s2, half)
    q = _dot(cq, wuq_ref[...])
    k = _dot(ckv, wuk_ref[...])
    scale = (MLA_QK ** -0.5) * LOG2E
    for h in range(MLA_HEADS):
        sl = slice(h * LANES, (h + 1) * LANES)
        q_ref[:, sl] = (_rope(q[:, sl], c, s1, s2, half) * scale).astype(BF16)
        k_ref[:, sl] = (k[:, sl] + kr).astype(BF16)
    v_ref[...] = _dot(ckv, wuv_ref[...]).astype(BF16)
    vt_ref[...] = _dot_nt(wuvt_ref[...], ckv).astype(BF16)


def _mla_proj_call(x, wd, qn, kvn, wuq, wuk, wuv, tabs, *, tm, n_real, seq):
    ntot = x.shape[0]
    nrt, tps = n_real // tm, seq // tm
    row = lambda i: (i, 0)
    fix = lambda i: (0, 0)
    tab_spec = pl.BlockSpec((tm, LANES), lambda i: (_table_index(i, nrt, tps), 0))
    return pl.pallas_call(
        _mla_proj_kernel,
        grid=(ntot // tm,),
        in_specs=[
            pl.BlockSpec((tm, D_MODEL), row),
            pl.BlockSpec(wd.shape, fix),
            pl.BlockSpec((1, MLA_Q_RANK), fix),
            pl.BlockSpec((1, MLA_KV_RANK), fix),
            pl.BlockSpec(wuq.shape, fix),
            pl.BlockSpec(wuk.shape, fix),
            pl.BlockSpec(wuv.shape, fix),
            pl.BlockSpec(wuv.shape[::-1], fix),
            tab_spec, tab_spec, tab_spec,
        ],
        out_specs=[
            pl.BlockSpec((tm, 2048), row),
            pl.BlockSpec((tm, 2048), row),
            pl.BlockSpec((tm, 1024), row),
            pl.BlockSpec((1024, tm), lambda i: (0, i)),
        ],
        out_shape=[
            jax.ShapeDtypeStruct((ntot, 2048), BF16),
            jax.ShapeDtypeStruct((ntot, 2048), BF16),
            jax.ShapeDtypeStruct((ntot, 1024), BF16),
            jax.ShapeDtypeStruct((1024, ntot), BF16),
        ],
        compiler_params=_params(("parallel",)),
        name="mla_proj",
    )(x, wd, qn, kvn, wuq, wuk, wuv, wuv.T, *tabs)


def _mla_attend(q_ref, kr_ref, km_ref, vr_ref, vm_ref, o_ref):
    for pair in range(MLA_HEADS // 2):
        vs = slice(pair * LANES, (pair + 1) * LANES)
        v_r, v_m = vr_ref[:, vs], vm_ref[:, vs]
        outs = []
        for hh in range(2):
            h = 2 * pair + hh
            hs = slice(h * LANES, (h + 1) * LANES)
            qh = q_ref[:, hs]
            s_r = _dot_nt(qh, kr_ref[:, hs])
            s_m = _dot_nt(qh, km_ref[:, hs])
            m = jnp.maximum(jnp.max(s_r, axis=-1, keepdims=True), jnp.max(s_m, axis=-1, keepdims=True))
            p_r = jnp.exp2(s_r - m)
            p_m = jnp.exp2(s_m - m)
            den = jnp.sum(p_r, axis=-1, keepdims=True) + jnp.sum(p_m, axis=-1, keepdims=True)
            acc = _dot(p_r.astype(BF16), v_r) + _dot(p_m.astype(BF16), v_m)
            outs.append(acc / den)
        lane = lax.broadcasted_iota(I32, outs[0].shape, 1)
        o_ref[:, vs] = jnp.where(lane < MLA_V, outs[0], outs[1]).astype(BF16)


META_PER_BLOCK = LANES // N_META
REDUCE_ROWS = 64


def _col_reduce(x, op):
    rows, n = x.shape
    if rows > REDUCE_ROWS and rows % REDUCE_ROWS == 0:
        x = op(x.reshape(rows // REDUCE_ROWS, REDUCE_ROWS, n), axis=0)
    return op(x, axis=0, keepdims=True)


def _mla_attn_kernel(q_ref, kr_ref, kt_ref, vtr_ref, vtt_ref, o_ref, ot_ref):
    b = pl.program_id(0)
    r = lax.broadcasted_iota(I32, (LANES, 1), 0)
    mine = jnp.right_shift(r, int(math.log2(N_META))) == (b % META_PER_BLOCK)
    mbias = jnp.where(mine, 0.0, -jnp.inf).astype(F32)
    def scores(h):
        hs = slice(h * LANES, (h + 1) * LANES)
        qh = q_ref[:, hs]
        return _dot_nt(kr_ref[:, hs], qh), _dot_nt(kt_ref[:, hs], qh) + mbias

    def weighted_values(h, p_r, p_m, den):
        vs = slice(h * MLA_V, (h + 1) * MLA_V)
        acc = _dot(vtr_ref[vs, :], p_r) + _dot(vtt_ref[vs, :], p_m)
        ot_ref[vs, :] = acc / den

    nxt = scores(0)
    pending = None
    for h in range(MLA_HEADS):
        s_r, s_m = nxt
        if h + 1 < MLA_HEADS:
            nxt = scores(h + 1)
        m = jnp.maximum(_col_reduce(s_r, jnp.max), _col_reduce(s_m, jnp.max))
        p_r = jnp.exp2(s_r - m)
        p_m = jnp.exp2(s_m - m)
        den = _col_reduce(p_r, jnp.sum) + _col_reduce(p_m, jnp.sum)
        if pending is not None:
            weighted_values(*pending)
        pending = (h, p_r.astype(BF16), p_m.astype(BF16), den)
    weighted_values(*pending)
    o_ref[...] = ot_ref[...].T.astype(BF16)


def _mla_attn_call(q, k, vt, *, batch, seq, n_real, tq):
    nq = seq // tq
    tail0 = n_real // LANES
    qmap = lambda b, j: (b * nq + j, 0)
    return pl.pallas_call(
        _mla_attn_kernel,
        grid=(batch, nq),
        in_specs=[
            pl.BlockSpec((tq, 2048), qmap),
            pl.BlockSpec((seq, 2048), lambda b, j: (b, 0)),
            pl.BlockSpec((LANES, 2048), lambda b, j: (tail0 + b // META_PER_BLOCK, 0)),
            pl.BlockSpec((1024, seq), lambda b, j: (0, b)),
            pl.BlockSpec((1024, LANES), lambda b, j: (0, tail0 + b // META_PER_BLOCK)),
        ],
        out_specs=pl.BlockSpec((tq, 1024), qmap),
        out_shape=jax.ShapeDtypeStruct((n_real, 1024), BF16),
        scratch_shapes=[pltpu.VMEM((1024, tq), F32)],
        compiler_params=_params(("parallel", "arbitrary")),
        name="mla_attn",
    )(q, k, k, vt, vt)


def _mla_meta_kernel(q_ref, kr_ref, km_ref, vr_ref, vm_ref, o_ref, *, batch):
    i = pl.program_id(0)

    @pl.when(i < batch)
    def _():
        _mla_attend(q_ref, kr_ref, km_ref, vr_ref, vm_ref, o_ref)

    @pl.when(i >= batch)
    def _():
        o_ref[...] = jnp.zeros_like(o_ref)


def _mla_meta_call(q, k, v, *, batch, seq, n_real):
    ntot = q.shape[0]
    mrow = n_real // N_META
    ntail_blocks = (ntot - n_real) // N_META
    tmap = lambda i: (mrow + i, 0)
    rmap = lambda i: (jnp.minimum(i, batch - 1), 0)
    return pl.pallas_call(
        functools.partial(_mla_meta_kernel, batch=batch),
        grid=(ntail_blocks,),
        in_specs=[
            pl.BlockSpec((N_META, 2048), tmap),
            pl.BlockSpec((seq, 2048), rmap),
            pl.BlockSpec((N_META, 2048), tmap),
            pl.BlockSpec((seq, 1024), rmap),
            pl.BlockSpec((N_META, 1024), tmap),
        ],
        out_specs=pl.BlockSpec((N_META, 1024), lambda i: (i, 0)),
        out_shape=jax.ShapeDtypeStruct((ntot - n_real, 1024), BF16),
        compiler_params=_params(("arbitrary",)),
        name="mla_attn_meta",
    )(q, k, k, v, v)


def _prep_gqa(a_wqkv, a_wo):
    d = a_wqkv.shape[0]
    nq = GQA_HEADS * GQA_HD
    nkv = GQA_KV * GQA_HD
    wq = a_wqkv[:, :nq] * (GQA_HD ** -0.5)

    def dup(w):
        w = w.reshape(d, GQA_KV, 1, GQA_HD)
        return jnp.broadcast_to(w, (d, GQA_KV, 2, GQA_HD)).reshape(d, 2 * nkv)

    w = jnp.concatenate([wq, dup(a_wqkv[:, nq:nq + nkv]), dup(a_wqkv[:, nq + nkv:])], axis=1)
    return w.astype(BF16), a_wo.astype(BF16)


def _prep_mla(b_wdkv, b_wuq, b_wukv):
    d = b_wdkv.shape[0]
    lat = MLA_Q_RANK + MLA_KV_RANK
    z = lambda n: jnp.zeros((d, n), F32)
    wd = jnp.concatenate([b_wdkv[:, :lat], z(MLA_NOPE), b_wdkv[:, lat:], z(LANES - MLA_QK)], axis=1)
    wuq = b_wuq.reshape(MLA_Q_RANK, MLA_HEADS, MLA_QK)
    wuq = jnp.pad(wuq, ((0, 0), (0, 0), (0, LANES - MLA_QK))).reshape(MLA_Q_RANK, MLA_HEADS * LANES)
    wukv = b_wukv.reshape(MLA_KV_RANK, MLA_HEADS, MLA_NOPE + MLA_V)
    wuk = jnp.pad(wukv[:, :, :MLA_NOPE], ((0, 0), (0, 0), (0, LANES - MLA_NOPE)))
    wuk = wuk.reshape(MLA_KV_RANK, MLA_HEADS * LANES)
    wuv = wukv[:, :, MLA_NOPE:].reshape(MLA_KV_RANK, MLA_HEADS * MLA_V)
    return wd.astype(BF16), wuq.astype(BF16), wuk.astype(BF16), wuv.astype(BF16)


def _prep_router(m_router):
    hi = m_router.astype(BF16)
    lo = (m_router - hi.astype(F32)).astype(BF16)
    pad = lambda r: jnp.pad(r, ((0, 0), (0, LANES - N_EXPERTS)))
    return jnp.concatenate([pad(hi), pad(lo)], axis=1)


def _positions(seq, n_tail, n_meta_rows):
    real = N_META + jnp.arange(seq, dtype=F32)
    t = jnp.arange(n_tail)
    tail = jnp.where(t < n_meta_rows, t % N_META, 0).astype(F32)
    return jnp.concatenate([real, tail])


def _trunk_flat(x_groups, meta, norm_mix, norm_ffn, norm_final,
                a_wqkv, a_wo, a_sink,
                b_wdkv, b_qnorm, b_kvnorm, b_wuq, b_wukv, b_wo,
                f_wg, f_wu, f_wd,
                m_router, m_wg, m_wu, m_wd):
    seq = x_groups[0].shape[1]
    batches = [x.shape[0] for x in x_groups]
    batch = sum(batches)
    n_real = batch * seq
    n_meta_rows = batch * N_META
    big = _pick_tile(seq, (1024, 512, 256, 128))
    tp = min(big, 512)
    ntot = -(-(n_real + n_meta_rows) // big) * big
    n_tail = ntot - n_real

    tail = jnp.concatenate([
        jnp.broadcast_to(meta.astype(F32)[None], (batch, N_META, D_MODEL)).reshape(n_meta_rows, D_MODEL),
        jnp.zeros((n_tail - n_meta_rows, D_MODEL), F32)], axis=0)
    h = jnp.concatenate([x.reshape(-1, D_MODEL) for x in x_groups] + [tail], axis=0)

    pos = _positions(seq, n_tail, n_meta_rows)
    tabs_a = _rope_lane_tables(pos, GQA_THETA, GQA_ROT, GQA_HD, 0)
    tabs_b = _rope_lane_tables(pos, MLA_THETA, MLA_ROPE, LANES, MLA_NOPE)
    row = lambda v: v.reshape(1, -1).astype(F32)

    wqkv, wo_a = _prep_gqa(a_wqkv[0], a_wo[0])
    q, k, v = _qkv_call(h, row(norm_mix[0]), wqkv, tabs_a, tm=tp, n_real=n_real, seq=seq)
    sink = a_sink[0].astype(F32)
    o = _win_attn_call(sink, q, k, v, batch=batch, seq=seq, n_real=n_real)
    o_tail = _win_meta_call(sink, q, k, v, batch=batch, seq=seq, n_real=n_real)
    h, xn = _wo_call(o, o_tail, wo_a, h, row(norm_ffn[0]), tm=big)
    fdim = f_wg.shape[2]
    tf = _pick_tile(fdim, (512, 256, 128))
    h, xn = _ffn_dense_call(xn, f_wg[0].astype(BF16), f_wu[0].astype(BF16), f_wd[0].astype(BF16),
                            h, row(norm_mix[1]), tm=tp, tf=tf)

    wd, wuq, wuk, wuv = _prep_mla(b_wdkv[0], b_wuq[0], b_wukv[0])
    q, k, v, vt = _mla_proj_call(xn, wd, row(b_qnorm[0]), row(b_kvnorm[0]), wuq, wuk, wuv, tabs_b,
                                 tm=tp, n_real=n_real, seq=seq)
    tq = _pick_tile(seq, (256, 128))
    o = _mla_attn_call(q, k, vt, batch=batch, seq=seq, n_real=n_real, tq=tq)
    o_tail = _mla_meta_call(q, k, v, batch=batch, seq=seq, n_real=n_real)
    tr = tp
    h, xpk, route, counts = _wo_router_call(o, o_tail, b_wo[0].astype(BF16), h, row(norm_ffn[1]),
                                            _prep_router(m_router[0]), tm=tr)

    tmx = big
    counts = counts[0, :N_EXPERTS].astype(I32)
    padded = ((counts + tmx - 1) // tmx) * tmx
    pend = jnp.cumsum(padded)
    pstart = pend - padded
    e_idx = route[:, 0:2].astype(I32)
    rank = route[:, 2:4].astype(I32)
    dest = pstart[e_idx] + rank
    n_assign = 2 * ntot
    n_blocks = -(-n_assign // tmx) + N_EXPERTS
    rows = n_blocks * tmx
    block_e = jnp.clip(jnp.searchsorted(pend, jnp.arange(n_blocks, dtype=I32) * tmx, side="right"),
                       0, N_EXPERTS - 1).astype(I32)
    n_used = (pend[-1] // tmx).astype(I32).reshape(1)
    block_e = jnp.where(jnp.arange(n_blocks) < n_used[0], block_e, block_e[jnp.maximum(n_used[0] - 1, 0)])

    td = tp
    dest_t = dest.reshape(ntot // td, td, 2).transpose(0, 2, 1)
    xs = _dispatch_call(dest_t, xpk, jnp.zeros((rows * XPK_ROWS, LANES), U32), td=td)
    edim = m_wg.shape[3]
    tfx = _pick_tile(edim, (512, 256, 128))
    ys = _ffn_expert_call(block_e, n_used, xs, m_wg[0].astype(BF16), m_wu[0].astype(BF16),
                          m_wd[0].astype(BF16), tm=tmx, tf=tfx)

    outs = []
    tile0 = 0
    for b, x in zip(batches, x_groups):
        ntiles = b * seq // td
        y = _combine_call(dest_t, ys, h, route, row(norm_final), tc=td, tile0=tile0, ntiles=ntiles)
        outs.append(y.reshape(b, seq, D_MODEL))
        tile0 += ntiles
    return tuple(outs)


def kernel(x_prompt, x_sample, meta, norm_mix, norm_ffn, norm_final, a_wqkv, a_wo, a_sink, b_wdkv, b_qnorm, b_kvnorm, b_wuq, b_wukv, b_wo, f_wg, f_wu, f_wd, m_router, m_wg, m_wu, m_wd):
    assert x_prompt.shape[1] == x_sample.shape[1] and x_prompt.shape[1] % BLK == 0
    assert norm_mix.shape[0] == 2, "two layers: windowed GQA + dense FFN, then MLA + MoE"
    return _trunk_flat((x_prompt, x_sample), meta, norm_mix, norm_ffn, norm_final,
                       a_wqkv, a_wo, a_sink,
                       b_wdkv, b_qnorm, b_kvnorm, b_wuq, b_wukv, b_wo,
                       f_wg, f_wu, f_wd,
                       m_router, m_wg, m_wu, m_wd)
```

```python
import functools
import math

import jax
import jax.numpy as jnp
from jax import lax
from jax.experimental import pallas as pl
from jax.experimental.pallas import tpu as pltpu

F32 = jnp.float32
BF16 = jnp.bfloat16
U32 = jnp.uint32
I32 = jnp.int32

D_MODEL = 1024
N_META = 16
RMS_EPS = 1e-6
BLK = 128
GQA_HEADS = 16
GQA_KV = 4
GQA_GROUP = 4
GQA_HD = 64
GQA_ROT = 16
GQA_THETA = 500000.0
MLA_HEADS = 16
MLA_NOPE = 64
MLA_ROPE = 32
MLA_V = 64
MLA_Q_RANK = 384
MLA_KV_RANK = 256
MLA_THETA = 10000.0
MLA_QK = MLA_NOPE + MLA_ROPE
N_EXPERTS = 8
LOG2E = 1.4426950408889634

LANES = 128
VMEM_LIMIT = 56 * 1024 * 1024


def _params(sem, vmem=VMEM_LIMIT):
    return pltpu.CompilerParams(dimension_semantics=sem, vmem_limit_bytes=vmem)


def _rms(x, g):
    ms = jnp.mean(x * x, axis=-1, keepdims=True)
    return x * lax.rsqrt(ms + RMS_EPS) * g


def _dot(a, b):
    return jnp.dot(a, b, preferred_element_type=F32)


def _dot_nt(a, b):
    return lax.dot_general(a, b, (((1,), (1,)), ((), ())), preferred_element_type=F32)


def _pick_tile(n, candidates):
    for c in candidates:
        if n % c == 0:
            return c
    raise ValueError(f"no tile in {candidates} divides {n}")


def _rope_lane_tables(pos, theta, rot, period, offset):
    half = rot // 2
    inv = jnp.power(jnp.float32(theta), -(jnp.arange(0, rot, 2, dtype=F32) / rot))
    ang = pos[:, None] * inv[None, :]
    cos, sin = jnp.cos(ang), jnp.sin(ang)
    lane = jnp.arange(LANES)
    r = (lane % period) - offset
    is_x1 = (r >= 0) & (r < half)
    is_x2 = (r >= half) & (r < rot)
    f = jnp.clip(jnp.where(is_x2, r - half, r), 0, half - 1)
    cos_l = cos[:, f]
    sin_l = sin[:, f]
    c = jnp.where((is_x1 | is_x2)[None, :], cos_l, 1.0)
    s1 = jnp.where(is_x2[None, :], sin_l, 0.0)
    s2 = jnp.where(is_x1[None, :], -sin_l, 0.0)
    return c.astype(F32), s1.astype(F32), s2.astype(F32)


def _rope(x, c, s1, s2, half):
    return x * c + pltpu.roll(x, half, 1) * s1 + pltpu.roll(x, LANES - half, 1) * s2


def _source_specs(sources, tm):
    specs, start = [], 0
    for src in sources:
        n = src.shape[0] // tm
        specs.append(pl.BlockSpec((tm, src.shape[1]),
                                  lambda i, start=start, n=n: (jnp.clip(i - start, 0, n - 1), 0)))
        start += n
    return specs


def _select_source(refs, src_rows, tm):
    i = pl.program_id(0)
    ends, acc = [], 0
    for n in src_rows:
        acc += n // tm
        ends.append(acc)
    x = refs[-1][...]
    for r, end in reversed(list(zip(refs[:-1], ends[:-1]))):
        x = jnp.where(i < end, r[...], x)
    return x


def _qkv_kernel(*refs, n_src, src_rows, tm):
    h_refs = refs[:n_src]
    g_ref, w_ref, c_ref, s1_ref, s2_ref, q_ref, k_ref, v_ref = refs[n_src:]
    xn = _rms(_select_source(h_refs, src_rows, tm), g_ref[...]).astype(BF16)
    qkv = _dot(xn, w_ref[...])
    c, s1, s2 = c_ref[...], s1_ref[...], s2_ref[...]
    nq = GQA_HEADS * GQA_HD // LANES
    nk = GQA_KV
    for i in range(nq):
        q_ref[:, i * LANES:(i + 1) * LANES] = (_rope(
            qkv[:, i * LANES:(i + 1) * LANES], c, s1, s2, GQA_ROT // 2) * LOG2E).astype(BF16)
    for i in range(nk):
        lo = (nq + i) * LANES
        k_ref[:, i * LANES:(i + 1) * LANES] = _rope(
            qkv[:, lo:lo + LANES], c, s1, s2, GQA_ROT // 2).astype(BF16)
    v_ref[...] = qkv[:, (nq + nk) * LANES:].astype(BF16)


def _table_index(i, n_real_tiles, tiles_per_seq):
    return jnp.where(i < n_real_tiles, i % tiles_per_seq, tiles_per_seq + i - n_real_tiles)


def _qkv_call(h_sources, g, w, tabs, *, tm, n_real, seq):
    src_rows = tuple(s.shape[0] for s in h_sources)
    ntot = sum(src_rows)
    nrt, tps = n_real // tm, seq // tm
    tab_spec = pl.BlockSpec((tm, LANES), lambda i: (_table_index(i, nrt, tps), 0))
    return pl.pallas_call(
        functools.partial(_qkv_kernel, n_src=len(h_sources), src_rows=src_rows, tm=tm),
        grid=(ntot // tm,),
        in_specs=_source_specs(h_sources, tm) + [
            pl.BlockSpec((1, D_MODEL), lambda i: (0, 0)),
            pl.BlockSpec(w.shape, lambda i: (0, 0)),
            tab_spec, tab_spec, tab_spec,
        ],
        out_specs=[
            pl.BlockSpec((tm, 1024), lambda i: (i, 0)),
            pl.BlockSpec((tm, GQA_KV * LANES), lambda i: (i, 0)),
            pl.BlockSpec((tm, GQA_KV * LANES), lambda i: (i, 0)),
        ],
        out_shape=[
            jax.ShapeDtypeStruct((ntot, 1024), BF16),
            jax.ShapeDtypeStruct((ntot, GQA_KV * LANES), BF16),
            jax.ShapeDtypeStruct((ntot, GQA_KV * LANES), BF16),
        ],
        compiler_params=_params(("parallel",)),
        name="qkv_rope",
    )(*h_sources, g, w, *tabs)


def _gqa_attend(sink_ref, q_ref, o_ref, k_pieces, v_pieces, bias):
    nq = q_ref.shape[0]
    nkeys = bias.shape[1]
    have = sum(p.shape[0] for p in k_pieces)
    row = lax.broadcasted_iota(I32, (2 * nq, 1), 0)
    lane_o = lax.broadcasted_iota(I32, (2 * nq, LANES), 1)
    lane_k = lax.broadcasted_iota(I32, (nkeys, LANES), 1)
    pad = [jnp.zeros((nkeys - have, LANES), BF16)] if nkeys > have else []

    def scores(kv):
        sl = slice(kv * LANES, (kv + 1) * LANES)
        kcat = jnp.concatenate([p[:, sl] for p in k_pieces] + pad, axis=0)
        zero = jnp.zeros_like(kcat)
        lo = slice(2 * kv * LANES, (2 * kv + 1) * LANES)
        hi = slice((2 * kv + 1) * LANES, (2 * kv + 2) * LANES)
        qp = jnp.concatenate([q_ref[:, lo], q_ref[:, hi]], axis=0)
        return (_dot_nt(qp, jnp.where(lane_k < GQA_HD, kcat, zero)) + bias,
                _dot_nt(qp, jnp.where(lane_k >= GQA_HD, kcat, zero)) + bias)

    def softmax(kv, ss):
        out = []
        for par, s in enumerate(ss):
            head = kv * GQA_GROUP + par
            sink = jnp.where(row < nq, sink_ref[head], sink_ref[head + 2])
            m = jnp.maximum(jnp.max(s, axis=-1, keepdims=True), sink)
            p = jnp.exp2(s - m)
            den = jnp.sum(p, axis=-1, keepdims=True) + jnp.exp2(sink - m)
            out.append((p.astype(BF16), den))
        return out

    def weighted_values(kv, pd):
        sl = slice(kv * LANES, (kv + 1) * LANES)
        vcat = jnp.concatenate([p[:, sl] for p in v_pieces] + pad, axis=0)
        outs = [_dot(p, vcat) / den for p, den in pd]
        o = jnp.where(lane_o < GQA_HD, outs[0], outs[1]).astype(BF16)
        o_ref[:, 2 * kv * LANES:(2 * kv + 1) * LANES] = o[:nq]
        o_ref[:, (2 * kv + 1) * LANES:(2 * kv + 2) * LANES] = o[nq:]

    nxt = scores(0)
    pending = None
    for kv in range(GQA_KV):
        ss = nxt
        if kv + 1 < GQA_KV:
            nxt = scores(kv + 1)
        pd = softmax(kv, ss)
        if pending is not None:
            weighted_values(*pending)
        pending = (kv, pd)
    weighted_values(*pending)


def _win_attn_kernel(sink_ref, q_ref, kp_ref, kc_ref, kn_ref, vp_ref, vc_ref, vn_ref,
                     km_ref, vm_ref, o_ref, *, nb):
    j = pl.program_id(1)
    nkeys = 4 * BLK
    qi = lax.broadcasted_iota(I32, (2 * BLK, nkeys), 0) & (BLK - 1)
    c = lax.broadcasted_iota(I32, (2 * BLK, nkeys), 1)
    neg = jnp.float32(-jnp.inf)
    b_p = jnp.where((c >= qi) & (j > 0), 0.0, neg)
    b_n = jnp.where(((c - 2 * BLK) <= qi) & (j < nb - 1), 0.0, neg)
    b_m = jnp.where(c < 3 * BLK + N_META, 0.0, neg)
    bias = jnp.where(c < BLK, b_p, jnp.where(c < 2 * BLK, 0.0, jnp.where(c < 3 * BLK, b_n, b_m)))
    _gqa_attend(sink_ref, q_ref, o_ref, (kp_ref, kc_ref, kn_ref, km_ref),
                (vp_ref, vc_ref, vn_ref, vm_ref), bias)


def _win_attn_call(sink, q, k, v, *, batch, seq, n_real):
    ntot = q.shape[0]
    nb = seq // BLK
    mrow = n_real // N_META

    def qmap(b, j, s):
        return (b * nb + j, 0)

    def pmap(b, j, s):
        return (b * nb + jnp.maximum(j - 1, 0), 0)

    def nmap(b, j, s):
        return (b * nb + jnp.minimum(j + 1, nb - 1), 0)

    def mmap(b, j, s):
        return (mrow + b, 0)

    kvw = GQA_KV * LANES
    kv_spec = lambda f: pl.BlockSpec((BLK, kvw), f)
    grid_spec = pltpu.PrefetchScalarGridSpec(
        num_scalar_prefetch=1,
        grid=(batch, nb),
        in_specs=[
            pl.BlockSpec((BLK, 1024), qmap),
            kv_spec(pmap), kv_spec(qmap), kv_spec(nmap),
            kv_spec(pmap), kv_spec(qmap), kv_spec(nmap),
            pl.BlockSpec((N_META, kvw), mmap),
            pl.BlockSpec((N_META, kvw), mmap),
        ],
        out_specs=pl.BlockSpec((BLK, 1024), qmap),
    )
    return pl.pallas_call(
        functools.partial(_win_attn_kernel, nb=nb),
        grid_spec=grid_spec,
        out_shape=jax.ShapeDtypeStruct((n_real, 1024), BF16),
        compiler_params=_params(("parallel", "parallel")),
        name="win_attn",
    )(sink, q, k, k, k, v, v, v, k, v)


def _win_meta_kernel(sink_ref, q_ref, km_ref, kc_ref, vm_ref, vc_ref, o_ref, *, batch):
    i = pl.program_id(0)

    @pl.when(i < batch)
    def _():
        nkeys = 2 * BLK
        qp = lax.broadcasted_iota(I32, (2 * N_META, nkeys), 0) & (N_META - 1)
        c = lax.broadcasted_iota(I32, (2 * N_META, nkeys), 1)
        neg = jnp.float32(-jnp.inf)
        b_c = jnp.where((N_META + c - qp) <= BLK, 0.0, neg)
        bias = jnp.where(c < BLK, b_c, jnp.where(c < BLK + N_META, 0.0, neg))
        _gqa_attend(sink_ref, q_ref, o_ref, (kc_ref, km_ref), (vc_ref, vm_ref), bias)

    @pl.when(i >= batch)
    def _():
        o_ref[...] = jnp.zeros_like(o_ref)


def _win_meta_call(sink, q, k, v, *, batch, seq, n_real):
    ntot = q.shape[0]
    nb = seq // BLK
    mrow = n_real // N_META
    ntail_blocks = (ntot - n_real) // N_META

    def tmap(i, s):
        return (mrow + i, 0)

    def cmap(i, s):
        return (jnp.minimum(i, batch - 1) * nb, 0)

    grid_spec = pltpu.PrefetchScalarGridSpec(
        num_scalar_prefetch=1,
        grid=(ntail_blocks,),
        in_specs=[
            pl.BlockSpec((N_META, 1024), tmap),
            pl.BlockSpec((N_META, GQA_KV * LANES), tmap),
            pl.BlockSpec((BLK, GQA_KV * LANES), cmap),
            pl.BlockSpec((N_META, GQA_KV * LANES), tmap),
            pl.BlockSpec((BLK, GQA_KV * LANES), cmap),
        ],
        out_specs=pl.BlockSpec((N_META, 1024), lambda i, s: (i, 0)),
    )
    return pl.pallas_call(
        functools.partial(_win_meta_kernel, batch=batch),
        grid_spec=grid_spec,
        out_shape=jax.ShapeDtypeStruct((ntot - n_real, 1024), BF16),
        compiler_params=_params(("parallel",)),
        name="win_attn_meta",
    )(sink, q, k, k, v, v)


def _attn_out_tile(o_real_ref, o_tail_ref, n_real_tiles):
    is_real = pl.program_id(0) < n_real_tiles
    return jnp.where(is_real, o_real_ref[...], o_tail_ref[...])


def _attn_out_specs(tm, n_real_tiles):
    return [
        pl.BlockSpec((tm, 1024), lambda i: (jnp.minimum(i, n_real_tiles - 1), 0)),
        pl.BlockSpec((tm, 1024), lambda i: (jnp.maximum(i - n_real_tiles, 0), 0)),
    ]


def _wo_kernel(*refs, n_real_tiles, n_src, src_rows, tm):
    o_real_ref, o_tail_ref, w_ref = refs[:3]
    h_refs = refs[3:3 + n_src]
    g_ref, hout_ref, xn_ref = refs[3 + n_src:]
    o = _attn_out_tile(o_real_ref, o_tail_ref, n_real_tiles)
    hn = _dot(o, w_ref[...]) + _select_source(h_refs, src_rows, tm)
    hout_ref[...] = hn
    xn_ref[...] = _rms(hn, g_ref[...]).astype(BF16)


def _wo_call(o_real, o_tail, w, h_sources, g, *, tm):
    src_rows = tuple(s.shape[0] for s in h_sources)
    ntot = sum(src_rows)
    nrt = o_real.shape[0] // tm
    row = lambda i: (i, 0)
    fix = lambda i: (0, 0)
    return pl.pallas_call(
        functools.partial(_wo_kernel, n_real_tiles=nrt, n_src=len(h_sources), src_rows=src_rows, tm=tm),
        grid=(ntot // tm,),
        in_specs=_attn_out_specs(tm, nrt) + [pl.BlockSpec((1024, D_MODEL), fix)]
        + _source_specs(h_sources, tm) + [
            pl.BlockSpec((1, D_MODEL), fix),
        ],
        out_specs=[pl.BlockSpec((tm, D_MODEL), row), pl.BlockSpec((tm, D_MODEL), row)],
        out_shape=[jax.ShapeDtypeStruct((ntot, D_MODEL), F32),
                   jax.ShapeDtypeStruct((ntot, D_MODEL), BF16)],
        compiler_params=_params(("parallel",)),
        name="wo_res_norm",
    )(o_real, o_tail, w, *h_sources, g)


XPK_ROWS = D_MODEL // 2 // LANES
YS_ROWS = D_MODEL // LANES


ROW_SPLIT = 2


def _store_token_major(ref, x, row0=0):
    t, n = x.shape
    k = n // LANES
    for s in range(k):
        ref[pl.ds(row0 * k + s, t, stride=k), :] = x[:, s * LANES:(s + 1) * LANES]


def _load_token_major(ref, k):
    t = ref.shape[0] // k
    return [ref[pl.ds(s, t, stride=k), :] for s in range(k)]


def _pack_bf16_pair(a, b):
    ra = pltpu.bitcast(a.astype(BF16).astype(F32), U32)
    rb = pltpu.bitcast(b.astype(BF16).astype(F32), U32)
    return ra | (rb >> 16)


def _unpack_bf16_pair(p):
    a = pltpu.bitcast(p & jnp.uint32(0xFFFF0000), F32).astype(BF16)
    b = pltpu.bitcast(p << 16, F32).astype(BF16)
    return a, b


def _wo_router_kernel(o_real_ref, o_tail_ref, w_ref, h_ref, g_ref, r_ref,
                      hout_ref, xpk_ref, route_ref, cnt_ref, carry_ref, *, n_real_tiles):
    i = pl.program_id(0)
    tm = h_ref.shape[0]

    @pl.when(i == 0)
    def _():
        carry_ref[...] = jnp.zeros_like(carry_ref)

    o = _attn_out_tile(o_real_ref, o_tail_ref, n_real_tiles)
    th = tm // ROW_SPLIT
    parts = [slice(p * th, (p + 1) * th) for p in range(ROW_SPLIT)]
    half = D_MODEL // 2
    r2 = r_ref[...]
    lane = lax.broadcasted_iota(I32, (th, LANES), 1)
    rr = lax.broadcasted_iota(I32, (th, th), 0)
    cc = lax.broadcasted_iota(I32, (th, th), 1)
    lower = jnp.where(cc < rr, 1.0, 0.0).astype(BF16)

    ys = []
    for rs in parts:
        hn = _dot(o[rs], w_ref[...]) + h_ref[rs, :]
        hout_ref[rs, :] = hn
        ys.append(_rms(hn, g_ref[...]))
    logits = []
    for p, y in enumerate(ys):
        _store_token_major(xpk_ref, _pack_bf16_pair(y[:, :half], y[:, half:]), row0=p * th)
        yhi = y.astype(BF16)
        ylo = (y - yhi.astype(F32)).astype(BF16)
        t = _dot(yhi, r2)
        logits.append(t[:, :LANES] + t[:, LANES:] + _dot(ylo, r2[:, :LANES]))
    carry = carry_ref[0:1, :]
    for rs, lg in zip(parts, logits):
        lm = jnp.where(lane < N_EXPERTS, lg, -jnp.inf)
        m1 = jnp.max(lm, axis=-1, keepdims=True)
        i1 = jnp.min(jnp.where(lm == m1, lane, LANES), axis=-1, keepdims=True)
        lm2 = jnp.where(lane == i1, -jnp.inf, lm)
        m2 = jnp.max(lm2, axis=-1, keepdims=True)
        i2 = jnp.min(jnp.where(lm2 == m2, lane, LANES), axis=-1, keepdims=True)
        e21 = jnp.exp(m2 - m1)
        g0 = 1.0 / (1.0 + e21)
        g1 = e21 / (1.0 + e21)
        onehot = ((lane == i1) | (lane == i2)).astype(F32)
        before = _dot(lower, onehot.astype(BF16)) + carry
        rank1 = jnp.sum(jnp.where(lane == i1, before, 0.0), axis=-1, keepdims=True)
        rank2 = jnp.sum(jnp.where(lane == i2, before, 0.0), axis=-1, keepdims=True)
        carry = carry + jnp.sum(onehot, axis=0, keepdims=True)
        out = jnp.where(lane == 0, i1.astype(F32), 0.0)
        out = jnp.where(lane == 1, i2.astype(F32), out)
        out = jnp.where(lane == 2, rank1, out)
        out = jnp.where(lane == 3, rank2, out)
        out = jnp.where(lane == 4, g0, out)
        out = jnp.where(lane == 5, g1, out)
        route_ref[rs, :] = out
    carry_ref[...] = jnp.broadcast_to(carry, carry_ref.shape)
    cnt_ref[...] = jnp.broadcast_to(carry, cnt_ref.shape)


def _wo_router_call(o_real, o_tail, w, h, g, r2, *, tm):
    ntot = h.shape[0]
    nrt = o_real.shape[0] // tm
    row = lambda i: (i, 0)
    fix = lambda i: (0, 0)
    return pl.pallas_call(
        functools.partial(_wo_router_kernel, n_real_tiles=nrt),
        grid=(ntot // tm,),
        in_specs=_attn_out_specs(tm, nrt) + [
            pl.BlockSpec((1024, D_MODEL), fix),
            pl.BlockSpec((tm, D_MODEL), row),
            pl.BlockSpec((1, D_MODEL), fix),
            pl.BlockSpec((D_MODEL, 2 * LANES), fix),
        ],
        out_specs=[
            pl.BlockSpec((tm, D_MODEL), row),
            pl.BlockSpec((tm * XPK_ROWS, LANES), row),
            pl.BlockSpec((tm, LANES), row),
            pl.BlockSpec((8, LANES), fix),
        ],
        out_shape=[
            jax.ShapeDtypeStruct((ntot, D_MODEL), F32),
            jax.ShapeDtypeStruct((ntot * XPK_ROWS, LANES), U32),
            jax.ShapeDtypeStruct((ntot, LANES), F32),
            jax.ShapeDtypeStruct((8, LANES), F32),
        ],
        scratch_shapes=[pltpu.VMEM((8, LANES), F32)],
        compiler_params=_params(("arbitrary",)),
        name="wo_res_norm_router",
    )(o_real, o_tail, w, h, g, r2)


def _swiglu_chunk(x, wg, wu, wd):
    g = _dot(x, wg)
    u = _dot(x, wu)
    a = (g * (1.0 / (1.0 + jnp.exp(-g)))) * u
    return _dot(a.astype(BF16), wd)


def _ffn_dense_kernel(x_ref, wg_ref, wu_ref, wd_ref, h_ref, g_ref, hout_ref, xn_ref, *, tf):
    x = x_ref[...]
    hn = h_ref[...]
    for c in range(wg_ref.shape[1] // tf):
        cs = slice(c * tf, (c + 1) * tf)
        hn = hn + _swiglu_chunk(x, wg_ref[:, cs], wu_ref[:, cs], wd_ref[cs, :])
    hout_ref[...] = hn
    xn_ref[...] = _rms(hn, g_ref[...]).astype(BF16)


def _ffn_dense_call(x, wg, wu, wd, h, g, *, tm, tf):
    ntot = x.shape[0]
    fdim = wg.shape[1]
    row = lambda i: (i, 0)
    fix = lambda i: (0, 0)
    once = pl.Buffered(1)
    return pl.pallas_call(
        functools.partial(_ffn_dense_kernel, tf=tf),
        grid=(ntot // tm,),
        in_specs=[
            pl.BlockSpec((tm, D_MODEL), row),
            pl.BlockSpec((D_MODEL, fdim), fix, pipeline_mode=once),
            pl.BlockSpec((D_MODEL, fdim), fix, pipeline_mode=once),
            pl.BlockSpec((fdim, D_MODEL), fix, pipeline_mode=once),
            pl.BlockSpec((tm, D_MODEL), row),
            pl.BlockSpec((1, D_MODEL), fix),
        ],
        out_specs=[pl.BlockSpec((tm, D_MODEL), row), pl.BlockSpec((tm, D_MODEL), row)],
        out_shape=[jax.ShapeDtypeStruct((ntot, D_MODEL), F32),
                   jax.ShapeDtypeStruct((ntot, D_MODEL), BF16)],
        compiler_params=_params(("parallel",)),
        name="ffn_dense",
    )(x, wg, wu, wd, h, g)


def _ffn_expert_kernel(be_ref, nu_ref, xs_ref, wg_ref, wu_ref, wd_ref, ys_ref, *, tf):
    del be_ref
    used = pl.program_id(0) < nu_ref[0]

    @pl.when(used)
    def _():
        halves = [_unpack_bf16_pair(blk) for blk in _load_token_major(xs_ref, XPK_ROWS)]
        x = jnp.concatenate([a for a, _ in halves] + [b for _, b in halves], axis=1)
        acc = None
        for c in range(wg_ref.shape[1] // tf):
            cs = slice(c * tf, (c + 1) * tf)
            y = _swiglu_chunk(x, wg_ref[:, cs], wu_ref[:, cs], wd_ref[cs, :])
            acc = y if acc is None else acc + y
        _store_token_major(ys_ref, acc)

    @pl.when(jnp.logical_not(used))
    def _():
        ys_ref[...] = jnp.zeros_like(ys_ref)


def _ffn_expert_call(block_e, n_used, xs, wg, wu, wd, *, tm, tf):
    rows = xs.shape[0] // XPK_ROWS
    once = pl.Buffered(1)
    wmap = lambda i, be, nu: (be[i], 0, 0)
    grid_spec = pltpu.PrefetchScalarGridSpec(
        num_scalar_prefetch=2,
        grid=(rows // tm,),
        in_specs=[
            pl.BlockSpec((tm * XPK_ROWS, LANES), lambda i, be, nu: (i, 0)),
            pl.BlockSpec((None,) + wg.shape[1:], wmap, pipeline_mode=once),
            pl.BlockSpec((None,) + wu.shape[1:], wmap, pipeline_mode=once),
            pl.BlockSpec((None,) + wd.shape[1:], wmap, pipeline_mode=once),
        ],
        out_specs=pl.BlockSpec((tm * YS_ROWS, LANES), lambda i, be, nu: (i, 0)),
    )
    return pl.pallas_call(
        functools.partial(_ffn_expert_kernel, tf=tf),
        grid_spec=grid_spec,
        out_shape=jax.ShapeDtypeStruct((rows * YS_ROWS, LANES), F32),
        compiler_params=_params(("arbitrary",)),
        name="ffn_expert",
    )(block_e, n_used, xs, wg, wu, wd)


DMA_UNROLL = 8


def _token_copy(src, src_tok, dst, dst_tok, rows, sem):
    return pltpu.make_async_copy(src.at[pl.ds(pl.multiple_of(src_tok * rows, rows), rows)],
                                 dst.at[pl.ds(pl.multiple_of(dst_tok * rows, rows), rows)], sem)


def _dispatch_kernel(dest_ref, x_ref, xs_in_ref, xs_ref, sem):
    del xs_in_ref
    td = x_ref.shape[0] // XPK_ROWS

    def issue(t, carry):
        for s in range(2):
            _token_copy(x_ref, t, xs_ref, dest_ref[0, s, t], XPK_ROWS, sem).start(priority=s)
        return carry

    lax.fori_loop(0, td, issue, 0, unroll=DMA_UNROLL)

    def drain(t, carry):
        for s in range(2):
            _token_copy(x_ref, 0, xs_ref, 0, XPK_ROWS, sem).wait()
        return carry

    lax.fori_loop(0, td, drain, 0, unroll=DMA_UNROLL)


def _dispatch_call(dest, xpk, xs_init, *, td):
    ntot = xpk.shape[0] // XPK_ROWS
    return pl.pallas_call(
        _dispatch_kernel,
        grid=(ntot // td,),
        in_specs=[
            pl.BlockSpec((1, 2, td), lambda i: (i, 0, 0), memory_space=pltpu.SMEM),
            pl.BlockSpec((td * XPK_ROWS, LANES), lambda i: (i, 0)),
            pl.BlockSpec(memory_space=pl.ANY),
        ],
        out_specs=pl.BlockSpec(memory_space=pl.ANY),
        out_shape=jax.ShapeDtypeStruct(xs_init.shape, U32),
        input_output_aliases={2: 0},
        scratch_shapes=[pltpu.SemaphoreType.DMA(())],
        compiler_params=_params(("arbitrary",)),
        name="moe_dispatch",
    )(dest, xpk, xs_init)


def _combine_kernel(dest_ref, ys_ref, h_ref, route_ref, g_ref, out_ref, y0_ref, y1_ref, sem):
    tc = h_ref.shape[0]
    bufs = (y0_ref, y1_ref)

    def issue(t, carry):
        for s in range(2):
            _token_copy(ys_ref, dest_ref[0, s, t], bufs[s], t, YS_ROWS, sem).start(priority=s)
        return carry

    lax.fori_loop(0, tc, issue, 0, unroll=DMA_UNROLL)

    def drain(t, carry):
        for s in range(2):
            _token_copy(ys_ref, 0, bufs[s], 0, YS_ROWS, sem).wait()
        return carry

    lax.fori_loop(0, tc, drain, 0, unroll=DMA_UNROLL)

    r = route_ref[...]
    lane = lax.broadcasted_iota(I32, r.shape, 1)
    g0 = jnp.sum(jnp.where(lane == 4, r, 0.0), axis=-1, keepdims=True)
    g1 = jnp.sum(jnp.where(lane == 5, r, 0.0), axis=-1, keepdims=True)
    y0 = _load_token_major(y0_ref, YS_ROWS)
    y1 = _load_token_major(y1_ref, YS_ROWS)
    moe = jnp.concatenate([a * g0 + b * g1 for a, b in zip(y0, y1)], axis=1)
    out_ref[...] = _rms(h_ref[...] + moe, g_ref[...])


def _combine_call(dest, ys, h, route, g, *, tc, tile0, ntiles):
    return pl.pallas_call(
        _combine_kernel,
        grid=(ntiles,),
        in_specs=[
            pl.BlockSpec((1, 2, tc), lambda i: (tile0 + i, 0, 0), memory_space=pltpu.SMEM),
            pl.BlockSpec(memory_space=pl.ANY),
            pl.BlockSpec((tc, D_MODEL), lambda i: (tile0 + i, 0)),
            pl.BlockSpec((tc, LANES), lambda i: (tile0 + i, 0)),
            pl.BlockSpec((1, D_MODEL), lambda i: (0, 0)),
        ],
        out_specs=pl.BlockSpec((tc, D_MODEL), lambda i: (i, 0)),
        out_shape=jax.ShapeDtypeStruct((ntiles * tc, D_MODEL), F32),
        scratch_shapes=[pltpu.VMEM((tc * YS_ROWS, LANES), F32), pltpu.VMEM((tc * YS_ROWS, LANES), F32),
                        pltpu.SemaphoreType.DMA(())],
        compiler_params=_params(("arbitrary",)),
        name="moe_combine_norm",
    )(dest, ys, h, route, g)


def _mla_proj_kernel(x_ref, wd_ref, qn_ref, kvn_ref, wuq_ref, wuk_ref, wuv_ref, wuvt_ref,
                     c_ref, s1_ref, s2_ref, q_ref, k_ref, v_ref, vt_ref):
    tm = x_ref.shape[0]
    th = tm // ROW_SPLIT
    parts = [slice(p * th, (p + 1) * th) for p in range(ROW_SPLIT)]
    half = MLA_ROPE // 2
    scale = (MLA_QK ** -0.5) * LOG2E
    lats = [_dot(x_ref[rs, :], wd_ref[...]) for rs in parts]
    ups = []
    for lat in lats:
        cq = _rms(lat[:, :MLA_Q_RANK], qn_ref[...]).astype(BF16)
        ckv = _rms(lat[:, MLA_Q_RANK:MLA_Q_RANK + MLA_KV_RANK], kvn_ref[...]).astype(BF16)
        ups.append((_dot(cq, wuq_ref[...]),
                    _dot(ckv, wuk_ref[...]),
                    _dot(ckv, wuv_ref[...]),
                    _dot_nt(wuvt_ref[...], ckv)))
    for rs, lat, (q, k, v, vt) in zip(parts, lats, ups):
        c, s1, s2 = c_ref[rs, :], s1_ref[rs, :], s2_ref[rs, :]
        kr = _rope(lat[:, MLA_Q_RANK + MLA_KV_RANK:], c, s1, s2, half)
        for h in range(MLA_HEADS):
            sl = slice(h * LANES, (h + 1) * LANES)
            q_ref[rs, sl] = (_rope(q[:, sl], c, s1, s2, half) * scale).astype(BF16)
            k_ref[rs, sl] = (k[:, sl] + kr).astype(BF16)
        v_ref[rs, :] = v.astype(BF16)
        vt_ref[:, rs] = vt.astype(BF16)


def _mla_proj_call(x, wd, qn, kvn, wuq, wuk, wuv, tabs, *, tm, n_real, seq):
    ntot = x.shape[0]
    nrt, tps = n_real // tm, seq // tm
    row = lambda i: (i, 0)
    fix = lambda i: (0, 0)
    tab_spec = pl.BlockSpec((tm, LANES), lambda i: (_table_index(i, nrt, tps), 0))
    return pl.pallas_call(
        _mla_proj_kernel,
        grid=(ntot // tm,),
        in_specs=[
            pl.BlockSpec((tm, D_MODEL), row),
            pl.BlockSpec(wd.shape, fix),
            pl.BlockSpec((1, MLA_Q_RANK), fix),
            pl.BlockSpec((1, MLA_KV_RANK), fix),
            pl.BlockSpec(wuq.shape, fix),
            pl.BlockSpec(wuk.shape, fix),
            pl.BlockSpec(wuv.shape, fix),
            pl.BlockSpec(wuv.shape[::-1], fix),
            tab_spec, tab_spec, tab_spec,
        ],
        out_specs=[
            pl.BlockSpec((tm, 2048), row),
            pl.BlockSpec((tm, 2048), row),
            pl.BlockSpec((tm, 1024), row),
            pl.BlockSpec((1024, tm), lambda i: (0, i)),
        ],
        out_shape=[
            jax.ShapeDtypeStruct((ntot, 2048), BF16),
            jax.ShapeDtypeStruct((ntot, 2048), BF16),
            jax.ShapeDtypeStruct((ntot, 1024), BF16),
            jax.ShapeDtypeStruct((1024, ntot), BF16),
        ],
        compiler_params=_params(("parallel",)),
        name="mla_proj",
    )(x, wd, qn, kvn, wuq, wuk, wuv, wuv.T, *tabs)


def _mla_attend(q_ref, kr_ref, km_ref, vr_ref, vm_ref, o_ref):
    for pair in range(MLA_HEADS // 2):
        vs = slice(pair * LANES, (pair + 1) * LANES)
        v_r, v_m = vr_ref[:, vs], vm_ref[:, vs]
        outs = []
        for hh in range(2):
            h = 2 * pair + hh
            hs = slice(h * LANES, (h + 1) * LANES)
            qh = q_ref[:, hs]
            s_r = _dot_nt(qh, kr_ref[:, hs])
            s_m = _dot_nt(qh, km_ref[:, hs])
            m = jnp.maximum(jnp.max(s_r, axis=-1, keepdims=True), jnp.max(s_m, axis=-1, keepdims=True))
            p_r = jnp.exp2(s_r - m)
            p_m = jnp.exp2(s_m - m)
            den = jnp.sum(p_r, axis=-1, keepdims=True) + jnp.sum(p_m, axis=-1, keepdims=True)
            acc = _dot(p_r.astype(BF16), v_r) + _dot(p_m.astype(BF16), v_m)
            outs.append(acc / den)
        lane = lax.broadcasted_iota(I32, outs[0].shape, 1)
        o_ref[:, vs] = jnp.where(lane < MLA_V, outs[0], outs[1]).astype(BF16)


META_PER_BLOCK = LANES // N_META
REDUCE_ROWS = 64


def _col_reduce(x, op):
    rows, n = x.shape
    if rows > REDUCE_ROWS and rows % REDUCE_ROWS == 0:
        x = op(x.reshape(rows // REDUCE_ROWS, REDUCE_ROWS, n), axis=0)
    return op(x, axis=0, keepdims=True)


def _mla_attn_kernel(q_ref, kr_ref, kt_ref, vtr_ref, vtt_ref, o_ref, ot_ref):
    b = pl.program_id(0)
    r = lax.broadcasted_iota(I32, (LANES, 1), 0)
    mine = jnp.right_shift(r, int(math.log2(N_META))) == (b % META_PER_BLOCK)
    mbias = jnp.where(mine, 0.0, -jnp.inf).astype(F32)
    def scores(h):
        hs = slice(h * LANES, (h + 1) * LANES)
        qh = q_ref[:, hs]
        return _dot_nt(kr_ref[:, hs], qh), _dot_nt(kt_ref[:, hs], qh) + mbias

    def weighted_values(h, p_r, p_m, den):
        vs = slice(h * MLA_V, (h + 1) * MLA_V)
        acc = _dot(vtr_ref[vs, :], p_r) + _dot(vtt_ref[vs, :], p_m)
        ot_ref[vs, :] = acc / den

    nxt = scores(0)
    pending = None
    for h in range(MLA_HEADS):
        s_r, s_m = nxt
        if h + 1 < MLA_HEADS:
            nxt = scores(h + 1)
        m = jnp.maximum(_col_reduce(s_r, jnp.max), _col_reduce(s_m, jnp.max))
        p_r = jnp.exp2(s_r - m)
        p_m = jnp.exp2(s_m - m)
        den = _col_reduce(p_r, jnp.sum) + _col_reduce(p_m, jnp.sum)
        if pending is not None:
            weighted_values(*pending)
        pending = (h, p_r.astype(BF16), p_m.astype(BF16), den)
    weighted_values(*pending)
    o_ref[...] = ot_ref[...].T.astype(BF16)


def _mla_attn_call(q, k, vt, *, batch, seq, n_real, tq):
    nq = seq // tq
    tail0 = n_real // LANES
    qmap = lambda b, j: (b * nq + j, 0)
    return pl.pallas_call(
        _mla_attn_kernel,
        grid=(batch, nq),
        in_specs=[
            pl.BlockSpec((tq, 2048), qmap),
            pl.BlockSpec((seq, 2048), lambda b, j: (b, 0)),
            pl.BlockSpec((LANES, 2048), lambda b, j: (tail0 + b // META_PER_BLOCK, 0)),
            pl.BlockSpec((1024, seq), lambda b, j: (0, b)),
            pl.BlockSpec((1024, LANES), lambda b, j: (0, tail0 + b // META_PER_BLOCK)),
        ],
        out_specs=pl.BlockSpec((tq, 1024), qmap),
        out_shape=jax.ShapeDtypeStruct((n_real, 1024), BF16),
        scratch_shapes=[pltpu.VMEM((1024, tq), F32)],
        compiler_params=_params(("parallel", "arbitrary")),
        name="mla_attn",
    )(q, k, k, vt, vt)


def _mla_meta_kernel(q_ref, kr_ref, km_ref, vr_ref, vm_ref, o_ref, *, batch):
    i = pl.program_id(0)

    @pl.when(i < batch)
    def _():
        _mla_attend(q_ref, kr_ref, km_ref, vr_ref, vm_ref, o_ref)

    @pl.when(i >= batch)
    def _():
        o_ref[...] = jnp.zeros_like(o_ref)


def _mla_meta_call(q, k, v, *, batch, seq, n_real):
    ntot = q.shape[0]
    mrow = n_real // N_META
    ntail_blocks = (ntot - n_real) // N_META
    tmap = lambda i: (mrow + i, 0)
    rmap = lambda i: (jnp.minimum(i, batch - 1), 0)
    return pl.pallas_call(
        functools.partial(_mla_meta_kernel, batch=batch),
        grid=(ntail_blocks,),
        in_specs=[
            pl.BlockSpec((N_META, 2048), tmap),
            pl.BlockSpec((seq, 2048), rmap),
            pl.BlockSpec((N_META, 2048), tmap),
            pl.BlockSpec((seq, 1024), rmap),
            pl.BlockSpec((N_META, 1024), tmap),
        ],
        out_specs=pl.BlockSpec((N_META, 1024), lambda i: (i, 0)),
        out_shape=jax.ShapeDtypeStruct((ntot - n_real, 1024), BF16),
        compiler_params=_params(("arbitrary",)),
        name="mla_attn_meta",
    )(q, k, k, v, v)


def _prep_gqa(a_wqkv, a_wo):
    d = a_wqkv.shape[0]
    nq = GQA_HEADS * GQA_HD
    nkv = GQA_KV * GQA_HD
    wq = a_wqkv[:, :nq] * (GQA_HD ** -0.5)

    def dup(w):
        w = w.reshape(d, GQA_KV, 1, GQA_HD)
        return jnp.broadcast_to(w, (d, GQA_KV, 2, GQA_HD)).reshape(d, 2 * nkv)

    w = jnp.concatenate([wq, dup(a_wqkv[:, nq:nq + nkv]), dup(a_wqkv[:, nq + nkv:])], axis=1)
    return w.astype(BF16), a_wo.astype(BF16)


def _prep_mla(b_wdkv, b_wuq, b_wukv):
    d = b_wdkv.shape[0]
    lat = MLA_Q_RANK + MLA_KV_RANK
    z = lambda n: jnp.zeros((d, n), F32)
    wd = jnp.concatenate([b_wdkv[:, :lat], z(MLA_NOPE), b_wdkv[:, lat:], z(LANES - MLA_QK)], axis=1)
    wuq = b_wuq.reshape(MLA_Q_RANK, MLA_HEADS, MLA_QK)
    wuq = jnp.pad(wuq, ((0, 0), (0, 0), (0, LANES - MLA_QK))).reshape(MLA_Q_RANK, MLA_HEADS * LANES)
    wukv = b_wukv.reshape(MLA_KV_RANK, MLA_HEADS, MLA_NOPE + MLA_V)
    wuk = jnp.pad(wukv[:, :, :MLA_NOPE], ((0, 0), (0, 0), (0, LANES - MLA_NOPE)))
    wuk = wuk.reshape(MLA_KV_RANK, MLA_HEADS * LANES)
    wuv = wukv[:, :, MLA_NOPE:].reshape(MLA_KV_RANK, MLA_HEADS * MLA_V)
    return wd.astype(BF16), wuq.astype(BF16), wuk.astype(BF16), wuv.astype(BF16)


def _prep_router(m_router):
    hi = m_router.astype(BF16)
    lo = (m_router - hi.astype(F32)).astype(BF16)
    pad = lambda r: jnp.pad(r, ((0, 0), (0, LANES - N_EXPERTS)))
    return jnp.concatenate([pad(hi), pad(lo)], axis=1)


def _positions(seq, n_tail, n_meta_rows):
    real = N_META + jnp.arange(seq, dtype=F32)
    t = jnp.arange(n_tail)
    tail = jnp.where(t < n_meta_rows, t % N_META, 0).astype(F32)
    return jnp.concatenate([real, tail])


def _trunk_flat(x_groups, meta, norm_mix, norm_ffn, norm_final,
                a_wqkv, a_wo, a_sink,
                b_wdkv, b_qnorm, b_kvnorm, b_wuq, b_wukv, b_wo,
                f_wg, f_wu, f_wd,
                m_router, m_wg, m_wu, m_wd):
    seq = x_groups[0].shape[1]
    batches = [x.shape[0] for x in x_groups]
    batch = sum(batches)
    n_real = batch * seq
    n_meta_rows = batch * N_META
    big = _pick_tile(seq, (1024, 512, 256, 128))
    tp = min(big, 512)
    ntot = -(-(n_real + n_meta_rows) // big) * big
    n_tail = ntot - n_real

    tail = jnp.concatenate([
        jnp.broadcast_to(meta.astype(F32)[None], (batch, N_META, D_MODEL)).reshape(n_meta_rows, D_MODEL),
        jnp.zeros((n_tail - n_meta_rows, D_MODEL), F32)], axis=0)
    h_sources = [x.reshape(-1, D_MODEL) for x in x_groups] + [tail]

    pos = _positions(seq, n_tail, n_meta_rows)
    tabs_a = _rope_lane_tables(pos, GQA_THETA, GQA_ROT, GQA_HD, 0)
    tabs_b = _rope_lane_tables(pos, MLA_THETA, MLA_ROPE, LANES, MLA_NOPE)
    row = lambda v: v.reshape(1, -1).astype(F32)

    wqkv, wo_a = _prep_gqa(a_wqkv[0], a_wo[0])
    q, k, v = _qkv_call(h_sources, row(norm_mix[0]), wqkv, tabs_a, tm=tp, n_real=n_real, seq=seq)
    sink = a_sink[0].astype(F32) * LOG2E
    o = _win_attn_call(sink, q, k, v, batch=batch, seq=seq, n_real=n_real)
    o_tail = _win_meta_call(sink, q, k, v, batch=batch, seq=seq, n_real=n_real)
    h, xn = _wo_call(o, o_tail, wo_a, h_sources, row(norm_ffn[0]), tm=big)
    fdim = f_wg.shape[2]
    tf = _pick_tile(fdim, (512, 256, 128))
    h, xn = _ffn_dense_call(xn, f_wg[0].astype(BF16), f_wu[0].astype(BF16), f_wd[0].astype(BF16),
                            h, row(norm_mix[1]), tm=tp, tf=tf)

    wd, wuq, wuk, wuv = _prep_mla(b_wdkv[0], b_wuq[0], b_wukv[0])
    q, k, v, vt = _mla_proj_call(xn, wd, row(b_qnorm[0]), row(b_kvnorm[0]), wuq, wuk, wuv, tabs_b,
                                 tm=tp, n_real=n_real, seq=seq)
    tq = _pick_tile(seq, (256, 128))
    o = _mla_attn_call(q, k, vt, batch=batch, seq=seq, n_real=n_real, tq=tq)
    o_tail = _mla_meta_call(q, k, v, batch=batch, seq=seq, n_real=n_real)
    tr = tp
    h, xpk, route, counts = _wo_router_call(o, o_tail, b_wo[0].astype(BF16), h, row(norm_ffn[1]),
                                            _prep_router(m_router[0]), tm=tr)

    tmx = big
    counts = counts[0, :N_EXPERTS].astype(I32)
    padded = ((counts + tmx - 1) // tmx) * tmx
    pend = jnp.cumsum(padded)
    pstart = pend - padded
    e_idx = route[:, 0:2].astype(I32)
    rank = route[:, 2:4].astype(I32)
    dest = pstart[e_idx] + rank
    n_assign = 2 * ntot
    n_blocks = -(-n_assign // tmx) + N_EXPERTS
    rows = n_blocks * tmx
    block_e = jnp.clip(jnp.searchsorted(pend, jnp.arange(n_blocks, dtype=I32) * tmx, side="right"),
                       0, N_EXPERTS - 1).astype(I32)
    n_used = (pend[-1] // tmx).astype(I32).reshape(1)
    block_e = jnp.where(jnp.arange(n_blocks) < n_used[0], block_e, block_e[jnp.maximum(n_used[0] - 1, 0)])

    td = tp
    dest_t = dest.reshape(ntot // td, td, 2).transpose(0, 2, 1)
    xs = _dispatch_call(dest_t, xpk, jnp.zeros((rows * XPK_ROWS, LANES), U32), td=td)
    edim = m_wg.shape[3]
    tfx = _pick_tile(edim, (512, 256, 128))
    ys = _ffn_expert_call(block_e, n_used, xs, m_wg[0].astype(BF16), m_wu[0].astype(BF16),
                          m_wd[0].astype(BF16), tm=tmx, tf=tfx)

    outs = []
    tile0 = 0
    for b, x in zip(batches, x_groups):
        ntiles = b * seq // td
        y = _combine_call(dest_t, ys, h, route, row(norm_final), tc=td, tile0=tile0, ntiles=ntiles)
        outs.append(y.reshape(b, seq, D_MODEL))
        tile0 += ntiles
    return tuple(outs)


def kernel(x_prompt, x_sample, meta, norm_mix, norm_ffn, norm_final, a_wqkv, a_wo, a_sink, b_wdkv, b_qnorm, b_kvnorm, b_wuq, b_wukv, b_wo, f_wg, f_wu, f_wd, m_router, m_wg, m_wu, m_wd):
    assert x_prompt.shape[1] == x_sample.shape[1] and x_prompt.shape[1] % BLK == 0
    assert norm_mix.shape[0] == 2, "two layers: windowed GQA + dense FFN, then MLA + MoE"
    return _trunk_flat((x_prompt, x_sample), meta, norm_mix, norm_ffn, norm_final,
                       a_wqkv, a_wo, a_sink,
                       b_wdkv, b_qnorm, b_kvnorm, b_wuq, b_wukv, b_wo,
                       f_wg, f_wu, f_wd,
                       m_router, m_wg, m_wu, m_wd)
```

```python
import functools
import math

import jax
import jax.numpy as jnp
from jax import lax
from jax.experimental import pallas as pl
from jax.experimental.pallas import tpu as pltpu

F32 = jnp.float32
BF16 = jnp.bfloat16
U32 = jnp.uint32
I32 = jnp.int32

D_MODEL = 1024
N_META = 16
RMS_EPS = 1e-6
BLK = 128
GQA_HEADS = 16
GQA_KV = 4
GQA_GROUP = 4
GQA_HD = 64
GQA_ROT = 16
GQA_THETA = 500000.0
MLA_HEADS = 16
MLA_NOPE = 64
MLA_ROPE = 32
MLA_V = 64
MLA_Q_RANK = 384
MLA_KV_RANK = 256
MLA_THETA = 10000.0
MLA_QK = MLA_NOPE + MLA_ROPE
N_EXPERTS = 8
LOG2E = 1.4426950408889634

LANES = 128
VMEM_LIMIT = 56 * 1024 * 1024


def _params(sem, vmem=VMEM_LIMIT):
    return pltpu.CompilerParams(dimension_semantics=sem, vmem_limit_bytes=vmem)


def _rms(x, g):
    ms = jnp.mean(x * x, axis=-1, keepdims=True)
    return x * lax.rsqrt(ms + RMS_EPS) * g


def _dot(a, b):
    return jnp.dot(a, b, preferred_element_type=F32)


def _dot_nt(a, b):
    return lax.dot_general(a, b, (((1,), (1,)), ((), ())), preferred_element_type=F32)


def _pick_tile(n, candidates):
    for c in candidates:
        if n % c == 0:
            return c
    raise ValueError(f"no tile in {candidates} divides {n}")


def _rope_lane_tables(pos, theta, rot, period, offset):
    half = rot // 2
    inv = jnp.power(jnp.float32(theta), -(jnp.arange(0, rot, 2, dtype=F32) / rot))
    ang = pos[:, None] * inv[None, :]
    cos, sin = jnp.cos(ang), jnp.sin(ang)
    lane = jnp.arange(LANES)
    r = (lane % period) - offset
    is_x1 = (r >= 0) & (r < half)
    is_x2 = (r >= half) & (r < rot)
    f = jnp.clip(jnp.where(is_x2, r - half, r), 0, half - 1)
    cos_l = cos[:, f]
    sin_l = sin[:, f]
    c = jnp.where((is_x1 | is_x2)[None, :], cos_l, 1.0)
    s1 = jnp.where(is_x2[None, :], sin_l, 0.0)
    s2 = jnp.where(is_x1[None, :], -sin_l, 0.0)
    return c.astype(F32), s1.astype(F32), s2.astype(F32)


def _rope(x, c, s1, s2, half):
    return x * c + pltpu.roll(x, half, 1) * s1 + pltpu.roll(x, LANES - half, 1) * s2


def _source_specs(sources, tm):
    specs, start = [], 0
    for src in sources:
        n = src.shape[0] // tm
        specs.append(pl.BlockSpec((tm, src.shape[1]),
                                  lambda i, start=start, n=n: (jnp.clip(i - start, 0, n - 1), 0)))
        start += n
    return specs


def _select_source(refs, src_rows, tm):
    i = pl.program_id(0)
    ends, acc = [], 0
    for n in src_rows:
        acc += n // tm
        ends.append(acc)
    x = refs[-1][...]
    for r, end in reversed(list(zip(refs[:-1], ends[:-1]))):
        x = jnp.where(i < end, r[...], x)
    return x


def _qkv_kernel(*refs, n_src, src_rows, tm):
    h_refs = refs[:n_src]
    g_ref, w_ref, c_ref, s1_ref, s2_ref, q_ref, k_ref, v_ref = refs[n_src:]
    xn = _rms(_select_source(h_refs, src_rows, tm), g_ref[...]).astype(BF16)
    qkv = _dot(xn, w_ref[...])
    c, s1, s2 = c_ref[...], s1_ref[...], s2_ref[...]
    nq = GQA_HEADS * GQA_HD // LANES
    nk = GQA_KV
    for i in range(nq):
        q_ref[:, i * LANES:(i + 1) * LANES] = (_rope(
            qkv[:, i * LANES:(i + 1) * LANES], c, s1, s2, GQA_ROT // 2) * LOG2E).astype(BF16)
    for i in range(nk):
        lo = (nq + i) * LANES
        k_ref[:, i * LANES:(i + 1) * LANES] = _rope(
            qkv[:, lo:lo + LANES], c, s1, s2, GQA_ROT // 2).astype(BF16)
    v_ref[...] = qkv[:, (nq + nk) * LANES:].astype(BF16)


def _table_index(i, n_real_tiles, tiles_per_seq):
    return jnp.where(i < n_real_tiles, i % tiles_per_seq, tiles_per_seq + i - n_real_tiles)


def _qkv_call(h_sources, g, w, tabs, *, tm, n_real, seq):
    src_rows = tuple(s.shape[0] for s in h_sources)
    ntot = sum(src_rows)
    nrt, tps = n_real // tm, seq // tm
    tab_spec = pl.BlockSpec((tm, LANES), lambda i: (_table_index(i, nrt, tps), 0))
    return pl.pallas_call(
        functools.partial(_qkv_kernel, n_src=len(h_sources), src_rows=src_rows, tm=tm),
        grid=(ntot // tm,),
        in_specs=_source_specs(h_sources, tm) + [
            pl.BlockSpec((1, D_MODEL), lambda i: (0, 0)),
            pl.BlockSpec(w.shape, lambda i: (0, 0)),
            tab_spec, tab_spec, tab_spec,
        ],
        out_specs=[
            pl.BlockSpec((tm, 1024), lambda i: (i, 0)),
            pl.BlockSpec((tm, GQA_KV * LANES), lambda i: (i, 0)),
            pl.BlockSpec((tm, GQA_KV * LANES), lambda i: (i, 0)),
        ],
        out_shape=[
            jax.ShapeDtypeStruct((ntot, 1024), BF16),
            jax.ShapeDtypeStruct((ntot, GQA_KV * LANES), BF16),
            jax.ShapeDtypeStruct((ntot, GQA_KV * LANES), BF16),
        ],
        compiler_params=_params(("parallel",)),
        name="qkv_rope",
    )(*h_sources, g, w, *tabs)


def _gqa_attend(sink_ref, q_ref, o_ref, k_pieces, v_pieces, bias):
    nq = q_ref.shape[0]
    nkeys = bias.shape[1]
    have = sum(p.shape[0] for p in k_pieces)
    row = lax.broadcasted_iota(I32, (2 * nq, 1), 0)
    lane_o = lax.broadcasted_iota(I32, (2 * nq, LANES), 1)
    lane_k = lax.broadcasted_iota(I32, (nkeys, LANES), 1)
    pad = [jnp.zeros((nkeys - have, LANES), BF16)] if nkeys > have else []

    def scores(kv):
        sl = slice(kv * LANES, (kv + 1) * LANES)
        kcat = jnp.concatenate([p[:, sl] for p in k_pieces] + pad, axis=0)
        zero = jnp.zeros_like(kcat)
        lo = slice(2 * kv * LANES, (2 * kv + 1) * LANES)
        hi = slice((2 * kv + 1) * LANES, (2 * kv + 2) * LANES)
        qp = jnp.concatenate([q_ref[:, lo], q_ref[:, hi]], axis=0)
        return (_dot_nt(qp, jnp.where(lane_k < GQA_HD, kcat, zero)) + bias,
                _dot_nt(qp, jnp.where(lane_k >= GQA_HD, kcat, zero)) + bias)

    def softmax(kv, ss):
        out = []
        for par, s in enumerate(ss):
            head = kv * GQA_GROUP + par
            sink = jnp.where(row < nq, sink_ref[head], sink_ref[head + 2])
            m = jnp.maximum(jnp.max(s, axis=-1, keepdims=True), sink)
            p = jnp.exp2(s - m)
            den = jnp.sum(p, axis=-1, keepdims=True) + jnp.exp2(sink - m)
            out.append((p.astype(BF16), den))
        return out

    def weighted_values(kv, pd):
        sl = slice(kv * LANES, (kv + 1) * LANES)
        vcat = jnp.concatenate([p[:, sl] for p in v_pieces] + pad, axis=0)
        outs = [_dot(p, vcat) / den for p, den in pd]
        o = jnp.where(lane_o < GQA_HD, outs[0], outs[1]).astype(BF16)
        o_ref[:, 2 * kv * LANES:(2 * kv + 1) * LANES] = o[:nq]
        o_ref[:, (2 * kv + 1) * LANES:(2 * kv + 2) * LANES] = o[nq:]

    nxt = scores(0)
    pending = None
    for kv in range(GQA_KV):
        ss = nxt
        if kv + 1 < GQA_KV:
            nxt = scores(kv + 1)
        pd = softmax(kv, ss)
        if pending is not None:
            weighted_values(*pending)
        pending = (kv, pd)
    weighted_values(*pending)


def _win_attn_kernel(sink_ref, q_ref, kp_ref, kc_ref, kn_ref, vp_ref, vc_ref, vn_ref,
                     km_ref, vm_ref, o_ref, *, nb):
    j = pl.program_id(1)
    nkeys = 4 * BLK
    qi = lax.broadcasted_iota(I32, (2 * BLK, nkeys), 0) & (BLK - 1)
    c = lax.broadcasted_iota(I32, (2 * BLK, nkeys), 1)
    neg = jnp.float32(-jnp.inf)
    b_p = jnp.where((c >= qi) & (j > 0), 0.0, neg)
    b_n = jnp.where(((c - 2 * BLK) <= qi) & (j < nb - 1), 0.0, neg)
    b_m = jnp.where(c < 3 * BLK + N_META, 0.0, neg)
    bias = jnp.where(c < BLK, b_p, jnp.where(c < 2 * BLK, 0.0, jnp.where(c < 3 * BLK, b_n, b_m)))
    _gqa_attend(sink_ref, q_ref, o_ref, (kp_ref, kc_ref, kn_ref, km_ref),
                (vp_ref, vc_ref, vn_ref, vm_ref), bias)


def _win_attn_call(sink, q, k, v, *, batch, seq, n_real):
    ntot = q.shape[0]
    nb = seq // BLK
    mrow = n_real // N_META

    def qmap(b, j, s):
        return (b * nb + j, 0)

    def pmap(b, j, s):
        return (b * nb + jnp.maximum(j - 1, 0), 0)

    def nmap(b, j, s):
        return (b * nb + jnp.minimum(j + 1, nb - 1), 0)

    def mmap(b, j, s):
        return (mrow + b, 0)

    kvw = GQA_KV * LANES
    kv_spec = lambda f: pl.BlockSpec((BLK, kvw), f)
    grid_spec = pltpu.PrefetchScalarGridSpec(
        num_scalar_prefetch=1,
        grid=(batch, nb),
        in_specs=[
            pl.BlockSpec((BLK, 1024), qmap),
            kv_spec(pmap), kv_spec(qmap), kv_spec(nmap),
            kv_spec(pmap), kv_spec(qmap), kv_spec(nmap),
            pl.BlockSpec((N_META, kvw), mmap),
            pl.BlockSpec((N_META, kvw), mmap),
        ],
        out_specs=pl.BlockSpec((BLK, 1024), qmap),
    )
    return pl.pallas_call(
        functools.partial(_win_attn_kernel, nb=nb),
        grid_spec=grid_spec,
        out_shape=jax.ShapeDtypeStruct((n_real, 1024), BF16),
        compiler_params=_params(("parallel", "parallel")),
        name="win_attn",
    )(sink, q, k, k, k, v, v, v, k, v)


def _win_meta_kernel(sink_ref, q_ref, km_ref, kc_ref, vm_ref, vc_ref, o_ref, *, batch):
    i = pl.program_id(0)

    @pl.when(i < batch)
    def _():
        nkeys = 2 * BLK
        qp = lax.broadcasted_iota(I32, (2 * N_META, nkeys), 0) & (N_META - 1)
        c = lax.broadcasted_iota(I32, (2 * N_META, nkeys), 1)
        neg = jnp.float32(-jnp.inf)
        b_c = jnp.where((N_META + c - qp) <= BLK, 0.0, neg)
        bias = jnp.where(c < BLK, b_c, jnp.where(c < BLK + N_META, 0.0, neg))
        _gqa_attend(sink_ref, q_ref, o_ref, (kc_ref, km_ref), (vc_ref, vm_ref), bias)

    @pl.when(i >= batch)
    def _():
        o_ref[...] = jnp.zeros_like(o_ref)


def _win_meta_call(sink, q, k, v, *, batch, seq, n_real):
    ntot = q.shape[0]
    nb = seq // BLK
    mrow = n_real // N_META
    ntail_blocks = (ntot - n_real) // N_META

    def tmap(i, s):
        return (mrow + i, 0)

    def cmap(i, s):
        return (jnp.minimum(i, batch - 1) * nb, 0)

    grid_spec = pltpu.PrefetchScalarGridSpec(
        num_scalar_prefetch=1,
        grid=(ntail_blocks,),
        in_specs=[
            pl.BlockSpec((N_META, 1024), tmap),
            pl.BlockSpec((N_META, GQA_KV * LANES), tmap),
            pl.BlockSpec((BLK, GQA_KV * LANES), cmap),
            pl.BlockSpec((N_META, GQA_KV * LANES), tmap),
            pl.BlockSpec((BLK, GQA_KV * LANES), cmap),
        ],
        out_specs=pl.BlockSpec((N_META, 1024), lambda i, s: (i, 0)),
    )
    return pl.pallas_call(
        functools.partial(_win_meta_kernel, batch=batch),
        grid_spec=grid_spec,
        out_shape=jax.ShapeDtypeStruct((ntot - n_real, 1024), BF16),
        compiler_params=_params(("parallel",)),
        name="win_attn_meta",
    )(sink, q, k, k, v, v)


def _attn_out_tile(o_real_ref, o_tail_ref, n_real_tiles):
    is_real = pl.program_id(0) < n_real_tiles
    return jnp.where(is_real, o_real_ref[...], o_tail_ref[...])


def _attn_out_specs(tm, n_real_tiles):
    return [
        pl.BlockSpec((tm, 1024), lambda i: (jnp.minimum(i, n_real_tiles - 1), 0)),
        pl.BlockSpec((tm, 1024), lambda i: (jnp.maximum(i - n_real_tiles, 0), 0)),
    ]


def _wo_kernel(*refs, n_real_tiles, n_src, src_rows, tm):
    o_real_ref, o_tail_ref, w_ref = refs[:3]
    h_refs = refs[3:3 + n_src]
    g_ref, hout_ref, xn_ref = refs[3 + n_src:]
    o = _attn_out_tile(o_real_ref, o_tail_ref, n_real_tiles)
    hn = _dot(o, w_ref[...]) + _select_source(h_refs, src_rows, tm)
    hout_ref[...] = hn
    xn_ref[...] = _rms(hn, g_ref[...]).astype(BF16)


def _wo_call(o_real, o_tail, w, h_sources, g, *, tm):
    src_rows = tuple(s.shape[0] for s in h_sources)
    ntot = sum(src_rows)
    nrt = o_real.shape[0] // tm
    row = lambda i: (i, 0)
    fix = lambda i: (0, 0)
    return pl.pallas_call(
        functools.partial(_wo_kernel, n_real_tiles=nrt, n_src=len(h_sources), src_rows=src_rows, tm=tm),
        grid=(ntot // tm,),
        in_specs=_attn_out_specs(tm, nrt) + [pl.BlockSpec((1024, D_MODEL), fix)]
        + _source_specs(h_sources, tm) + [
            pl.BlockSpec((1, D_MODEL), fix),
        ],
        out_specs=[pl.BlockSpec((tm, D_MODEL), row), pl.BlockSpec((tm, D_MODEL), row)],
        out_shape=[jax.ShapeDtypeStruct((ntot, D_MODEL), F32),
                   jax.ShapeDtypeStruct((ntot, D_MODEL), BF16)],
        compiler_params=_params(("parallel",)),
        name="wo_res_norm",
    )(o_real, o_tail, w, *h_sources, g)


XPK_ROWS = D_MODEL // 2 // LANES
YS_ROWS = D_MODEL // LANES


ROW_SPLIT = 2


def _store_token_major(ref, x, row0=0):
    t, n = x.shape
    k = n // LANES
    for s in range(k):
        ref[pl.ds(row0 * k + s, t, stride=k), :] = x[:, s * LANES:(s + 1) * LANES]


def _load_token_major(ref, k):
    t = ref.shape[0] // k
    return [ref[pl.ds(s, t, stride=k), :] for s in range(k)]


def _pack_bf16_pair(a, b):
    ra = pltpu.bitcast(a.astype(BF16).astype(F32), U32)
    rb = pltpu.bitcast(b.astype(BF16).astype(F32), U32)
    return ra | (rb >> 16)


def _unpack_bf16_pair(p):
    a = pltpu.bitcast(p & jnp.uint32(0xFFFF0000), F32).astype(BF16)
    b = pltpu.bitcast(p << 16, F32).astype(BF16)
    return a, b


def _wo_router_kernel(o_real_ref, o_tail_ref, w_ref, h_ref, g_ref, r_ref,
                      hout_ref, xpk_ref, route_ref, cnt_ref, carry_ref, *, n_real_tiles):
    i = pl.program_id(0)
    tm = h_ref.shape[0]

    @pl.when(i == 0)
    def _():
        carry_ref[...] = jnp.zeros_like(carry_ref)

    o = _attn_out_tile(o_real_ref, o_tail_ref, n_real_tiles)
    th = tm // ROW_SPLIT
    parts = [slice(p * th, (p + 1) * th) for p in range(ROW_SPLIT)]
    half = D_MODEL // 2
    r2 = r_ref[...]
    lane = lax.broadcasted_iota(I32, (th, LANES), 1)
    rr = lax.broadcasted_iota(I32, (th, th), 0)
    cc = lax.broadcasted_iota(I32, (th, th), 1)
    lower = jnp.where(cc < rr, 1.0, 0.0).astype(BF16)

    ys = []
    for rs in parts:
        hn = _dot(o[rs], w_ref[...]) + h_ref[rs, :]
        hout_ref[rs, :] = hn
        ys.append(_rms(hn, g_ref[...]))
    logits = []
    for p, y in enumerate(ys):
        _store_token_major(xpk_ref, _pack_bf16_pair(y[:, :half], y[:, half:]), row0=p * th)
        yhi = y.astype(BF16)
        ylo = (y - yhi.astype(F32)).astype(BF16)
        t = _dot(yhi, r2)
        logits.append(t[:, :LANES] + t[:, LANES:] + _dot(ylo, r2[:, :LANES]))
    carry = carry_ref[0:1, :]
    for rs, lg in zip(parts, logits):
        lm = jnp.where(lane < N_EXPERTS, lg, -jnp.inf)
        m1 = jnp.max(lm, axis=-1, keepdims=True)
        i1 = jnp.min(jnp.where(lm == m1, lane, LANES), axis=-1, keepdims=True)
        lm2 = jnp.where(lane == i1, -jnp.inf, lm)
        m2 = jnp.max(lm2, axis=-1, keepdims=True)
        i2 = jnp.min(jnp.where(lm2 == m2, lane, LANES), axis=-1, keepdims=True)
        e21 = jnp.exp(m2 - m1)
        g0 = 1.0 / (1.0 + e21)
        g1 = e21 / (1.0 + e21)
        onehot = ((lane == i1) | (lane == i2)).astype(F32)
        before = _dot(lower, onehot.astype(BF16)) + carry
        rank1 = jnp.sum(jnp.where(lane == i1, before, 0.0), axis=-1, keepdims=True)
        rank2 = jnp.sum(jnp.where(lane == i2, before, 0.0), axis=-1, keepdims=True)
        carry = carry + jnp.sum(onehot, axis=0, keepdims=True)
        out = jnp.where(lane == 0, i1.astype(F32), 0.0)
        out = jnp.where(lane == 1, i2.astype(F32), out)
        out = jnp.where(lane == 2, rank1, out)
        out = jnp.where(lane == 3, rank2, out)
        out = jnp.where(lane == 4, g0, out)
        out = jnp.where(lane == 5, g1, out)
        route_ref[rs, :] = out
    carry_ref[...] = jnp.broadcast_to(carry, carry_ref.shape)
    cnt_ref[...] = jnp.broadcast_to(carry, cnt_ref.shape)


def _wo_router_call(o_real, o_tail, w, h, g, r2, *, tm):
    ntot = h.shape[0]
    nrt = o_real.shape[0] // tm
    row = lambda i: (i, 0)
    fix = lambda i: (0, 0)
    return pl.pallas_call(
        functools.partial(_wo_router_kernel, n_real_tiles=nrt),
        grid=(ntot // tm,),
        in_specs=_attn_out_specs(tm, nrt) + [
            pl.BlockSpec((1024, D_MODEL), fix),
            pl.BlockSpec((tm, D_MODEL), row),
            pl.BlockSpec((1, D_MODEL), fix),
            pl.BlockSpec((D_MODEL, 2 * LANES), fix),
        ],
        out_specs=[
            pl.BlockSpec((tm, D_MODEL), row),
            pl.BlockSpec((tm * XPK_ROWS, LANES), row),
            pl.BlockSpec((tm, LANES), row),
            pl.BlockSpec((8, LANES), fix),
        ],
        out_shape=[
            jax.ShapeDtypeStruct((ntot, D_MODEL), F32),
            jax.ShapeDtypeStruct((ntot * XPK_ROWS, LANES), U32),
            jax.ShapeDtypeStruct((ntot, LANES), F32),
            jax.ShapeDtypeStruct((8, LANES), F32),
        ],
        scratch_shapes=[pltpu.VMEM((8, LANES), F32)],
        compiler_params=_params(("arbitrary",)),
        name="wo_res_norm_router",
    )(o_real, o_tail, w, h, g, r2)


def _swiglu_chunk(x, wg, wu, wd):
    g = _dot(x, wg)
    u = _dot(x, wu)
    a = (g * (1.0 / (1.0 + jnp.exp(-g)))) * u
    return _dot(a.astype(BF16), wd)


def _ffn_dense_kernel(x_ref, wg_ref, wu_ref, wd_ref, h_ref, g_ref, hout_ref, xn_ref, *, tf):
    x = x_ref[...]
    hn = h_ref[...]
    for c in range(wg_ref.shape[1] // tf):
        cs = slice(c * tf, (c + 1) * tf)
        hn = hn + _swiglu_chunk(x, wg_ref[:, cs], wu_ref[:, cs], wd_ref[cs, :])
    hout_ref[...] = hn
    xn_ref[...] = _rms(hn, g_ref[...]).astype(BF16)


def _ffn_dense_call(x, wg, wu, wd, h, g, *, tm, tf):
    ntot = x.shape[0]
    fdim = wg.shape[1]
    row = lambda i: (i, 0)
    fix = lambda i: (0, 0)
    once = pl.Buffered(1)
    return pl.pallas_call(
        functools.partial(_ffn_dense_kernel, tf=tf),
        grid=(ntot // tm,),
        in_specs=[
            pl.BlockSpec((tm, D_MODEL), row),
            pl.BlockSpec((D_MODEL, fdim), fix, pipeline_mode=once),
            pl.BlockSpec((D_MODEL, fdim), fix, pipeline_mode=once),
            pl.BlockSpec((fdim, D_MODEL), fix, pipeline_mode=once),
            pl.BlockSpec((tm, D_MODEL), row),
            pl.BlockSpec((1, D_MODEL), fix),
        ],
        out_specs=[pl.BlockSpec((tm, D_MODEL), row), pl.BlockSpec((tm, D_MODEL), row)],
        out_shape=[jax.ShapeDtypeStruct((ntot, D_MODEL), F32),
                   jax.ShapeDtypeStruct((ntot, D_MODEL), BF16)],
        compiler_params=_params(("parallel",)),
        name="ffn_dense",
    )(x, wg, wu, wd, h, g)


def _ffn_expert_kernel(be_ref, nu_ref, xs_ref, wg_ref, wu_ref, wd_ref, ys_ref, *, tf):
    del be_ref
    used = pl.program_id(0) < nu_ref[0]

    @pl.when(used)
    def _():
        halves = [_unpack_bf16_pair(blk) for blk in _load_token_major(xs_ref, XPK_ROWS)]
        x = jnp.concatenate([a for a, _ in halves] + [b for _, b in halves], axis=1)
        acc = None
        for c in range(wg_ref.shape[1] // tf):
            cs = slice(c * tf, (c + 1) * tf)
            y = _swiglu_chunk(x, wg_ref[:, cs], wu_ref[:, cs], wd_ref[cs, :])
            acc = y if acc is None else acc + y
        _store_token_major(ys_ref, acc)

    @pl.when(jnp.logical_not(used))
    def _():
        ys_ref[...] = jnp.zeros_like(ys_ref)


def _ffn_expert_call(block_e, n_used, xs, wg, wu, wd, *, tm, tf):
    rows = xs.shape[0] // XPK_ROWS
    once = pl.Buffered(1)
    wmap = lambda i, be, nu: (be[i], 0, 0)
    grid_spec = pltpu.PrefetchScalarGridSpec(
        num_scalar_prefetch=2,
        grid=(rows // tm,),
        in_specs=[
            pl.BlockSpec((tm * XPK_ROWS, LANES), lambda i, be, nu: (i, 0)),
            pl.BlockSpec((None,) + wg.shape[1:], wmap, pipeline_mode=once),
            pl.BlockSpec((None,) + wu.shape[1:], wmap, pipeline_mode=once),
            pl.BlockSpec((None,) + wd.shape[1:], wmap, pipeline_mode=once),
        ],
        out_specs=pl.BlockSpec((tm * YS_ROWS, LANES), lambda i, be, nu: (i, 0)),
    )
    return pl.pallas_call(
        functools.partial(_ffn_expert_kernel, tf=tf),
        grid_spec=grid_spec,
        out_shape=jax.ShapeDtypeStruct((rows * YS_ROWS, LANES), F32),
        compiler_params=_params(("arbitrary",)),
        name="ffn_expert",
    )(block_e, n_used, xs, wg, wu, wd)


DMA_UNROLL = 8


def _token_copy(src, src_tok, dst, dst_tok, rows, sem):
    return pltpu.make_async_copy(src.at[pl.ds(pl.multiple_of(src_tok * rows, rows), rows)],
                                 dst.at[pl.ds(pl.multiple_of(dst_tok * rows, rows), rows)], sem)


def _dispatch_kernel(dest_ref, x_ref, xs_in_ref, xs_ref, sem):
    del xs_in_ref
    td = x_ref.shape[0] // XPK_ROWS

    def issue(t, carry):
        for s in range(2):
            _token_copy(x_ref, t, xs_ref, dest_ref[0, s, t], XPK_ROWS, sem).start(priority=s)
        return carry

    lax.fori_loop(0, td, issue, 0, unroll=DMA_UNROLL)

    def drain(t, carry):
        for s in range(2):
            _token_copy(x_ref, 0, xs_ref, 0, XPK_ROWS, sem).wait()
        return carry

    lax.fori_loop(0, td, drain, 0, unroll=DMA_UNROLL)


def _dispatch_call(dest, xpk, xs_init, *, td):
    ntot = xpk.shape[0] // XPK_ROWS
    return pl.pallas_call(
        _dispatch_kernel,
        grid=(ntot // td,),
        in_specs=[
            pl.BlockSpec((1, 2, td), lambda i: (i, 0, 0), memory_space=pltpu.SMEM),
            pl.BlockSpec((td * XPK_ROWS, LANES), lambda i: (i, 0)),
            pl.BlockSpec(memory_space=pl.ANY),
        ],
        out_specs=pl.BlockSpec(memory_space=pl.ANY),
        out_shape=jax.ShapeDtypeStruct(xs_init.shape, U32),
        input_output_aliases={2: 0},
        scratch_shapes=[pltpu.SemaphoreType.DMA(())],
        compiler_params=_params(("arbitrary",)),
        name="moe_dispatch",
    )(dest, xpk, xs_init)


def _combine_kernel(dest_ref, dest_next_ref, ys_ref, h_ref, route_ref, g_ref, out_ref,
                    y0_ref, y1_ref, sems):
    i = pl.program_id(0)
    n = pl.num_programs(0)
    tc = h_ref.shape[0]
    bufs = (y0_ref, y1_ref)
    slot = i % 2

    def issue(d_ref, sl):
        def body(t, carry):
            for s in range(2):
                _token_copy(ys_ref, d_ref[0, s, t], bufs[s].at[sl], t, YS_ROWS,
                            sems.at[sl]).start(priority=s)
            return carry

        lax.fori_loop(0, tc, body, 0, unroll=DMA_UNROLL)

    @pl.when(i == 0)
    def _():
        issue(dest_ref, 0)

    @pl.when(i + 1 < n)
    def _():
        issue(dest_next_ref, 1 - slot)

    def drain(t, carry):
        for s in range(2):
            _token_copy(ys_ref, 0, bufs[s].at[slot], 0, YS_ROWS, sems.at[slot]).wait()
        return carry

    lax.fori_loop(0, tc, drain, 0, unroll=DMA_UNROLL)

    r = route_ref[...]
    lane = lax.broadcasted_iota(I32, r.shape, 1)
    g0 = jnp.sum(jnp.where(lane == 4, r, 0.0), axis=-1, keepdims=True)
    g1 = jnp.sum(jnp.where(lane == 5, r, 0.0), axis=-1, keepdims=True)
    y0 = _load_token_major(y0_ref.at[slot], YS_ROWS)
    y1 = _load_token_major(y1_ref.at[slot], YS_ROWS)
    moe = jnp.concatenate([a * g0 + b * g1 for a, b in zip(y0, y1)], axis=1)
    out_ref[...] = _rms(h_ref[...] + moe, g_ref[...])


def _combine_call(dest, ys, h, route, g, *, tc, tile0, ntiles):
    return pl.pallas_call(
        _combine_kernel,
        grid=(ntiles,),
        in_specs=[
            pl.BlockSpec((1, 2, tc), lambda i: (tile0 + i, 0, 0), memory_space=pltpu.SMEM),
            pl.BlockSpec((1, 2, tc), lambda i: (tile0 + jnp.minimum(i + 1, ntiles - 1), 0, 0),
                         memory_space=pltpu.SMEM),
            pl.BlockSpec(memory_space=pl.ANY),
            pl.BlockSpec((tc, D_MODEL), lambda i: (tile0 + i, 0)),
            pl.BlockSpec((tc, LANES), lambda i: (tile0 + i, 0)),
            pl.BlockSpec((1, D_MODEL), lambda i: (0, 0)),
        ],
        out_specs=pl.BlockSpec((tc, D_MODEL), lambda i: (i, 0)),
        out_shape=jax.ShapeDtypeStruct((ntiles * tc, D_MODEL), F32),
        scratch_shapes=[pltpu.VMEM((2, tc * YS_ROWS, LANES), F32), pltpu.VMEM((2, tc * YS_ROWS, LANES), F32),
                        pltpu.SemaphoreType.DMA((2,))],
        compiler_params=_params(("arbitrary",)),
        name="moe_combine_norm",
    )(dest, dest, ys, h, route, g)


def _mla_proj_kernel(x_ref, wd_ref, qn_ref, kvn_ref, wuq_ref, wuk_ref, wuv_ref, wuvt_ref,
                     c_ref, s1_ref, s2_ref, q_ref, k_ref, v_ref, vt_ref):
    tm = x_ref.shape[0]
    th = tm // ROW_SPLIT
    parts = [slice(p * th, (p + 1) * th) for p in range(ROW_SPLIT)]
    half = MLA_ROPE // 2
    scale = (MLA_QK ** -0.5) * LOG2E
    lats = [_dot(x_ref[rs, :], wd_ref[...]) for rs in parts]
    ups = []
    for lat in lats:
        cq = _rms(lat[:, :MLA_Q_RANK], qn_ref[...]).astype(BF16)
        ckv = _rms(lat[:, MLA_Q_RANK:MLA_Q_RANK + MLA_KV_RANK], kvn_ref[...]).astype(BF16)
        ups.append((_dot(cq, wuq_ref[...]),
                    _dot(ckv, wuk_ref[...]),
                    _dot(ckv, wuv_ref[...]),
                    _dot_nt(wuvt_ref[...], ckv)))
    for rs, lat, (q, k, v, vt) in zip(parts, lats, ups):
        c, s1, s2 = c_ref[rs, :], s1_ref[rs, :], s2_ref[rs, :]
        kr = _rope(lat[:, MLA_Q_RANK + MLA_KV_RANK:], c, s1, s2, half)
        for h in range(MLA_HEADS):
            sl = slice(h * LANES, (h + 1) * LANES)
            q_ref[rs, sl] = (_rope(q[:, sl], c, s1, s2, half) * scale).astype(BF16)
            k_ref[rs, sl] = (k[:, sl] + kr).astype(BF16)
        v_ref[rs, :] = v.astype(BF16)
        vt_ref[:, rs] = vt.astype(BF16)


def _mla_proj_call(x, wd, qn, kvn, wuq, wuk, wuv, tabs, *, tm, n_real, seq):
    ntot = x.shape[0]
    nrt, tps = n_real // tm, seq // tm
    row = lambda i: (i, 0)
    fix = lambda i: (0, 0)
    tab_spec = pl.BlockSpec((tm, LANES), lambda i: (_table_index(i, nrt, tps), 0))
    return pl.pallas_call(
        _mla_proj_kernel,
        grid=(ntot // tm,),
        in_specs=[
            pl.BlockSpec((tm, D_MODEL), row),
            pl.BlockSpec(wd.shape, fix),
            pl.BlockSpec((1, MLA_Q_RANK), fix),
            pl.BlockSpec((1, MLA_KV_RANK), fix),
            pl.BlockSpec(wuq.shape, fix),
            pl.BlockSpec(wuk.shape, fix),
            pl.BlockSpec(wuv.shape, fix),
            pl.BlockSpec(wuv.shape[::-1], fix),
            tab_spec, tab_spec, tab_spec,
        ],
        out_specs=[
            pl.BlockSpec((tm, 2048), row),
            pl.BlockSpec((tm, 2048), row),
            pl.BlockSpec((tm, 1024), row),
            pl.BlockSpec((1024, tm), lambda i: (0, i)),
        ],
        out_shape=[
            jax.ShapeDtypeStruct((ntot, 2048), BF16),
            jax.ShapeDtypeStruct((ntot, 2048), BF16),
            jax.ShapeDtypeStruct((ntot, 1024), BF16),
            jax.ShapeDtypeStruct((1024, ntot), BF16),
        ],
        compiler_params=_params(("parallel",)),
        name="mla_proj",
    )(x, wd, qn, kvn, wuq, wuk, wuv, wuv.T, *tabs)


def _mla_attend(q_ref, kr_ref, km_ref, vr_ref, vm_ref, o_ref):
    for pair in range(MLA_HEADS // 2):
        vs = slice(pair * LANES, (pair + 1) * LANES)
        v_r, v_m = vr_ref[:, vs], vm_ref[:, vs]
        outs = []
        for hh in range(2):
            h = 2 * pair + hh
            hs = slice(h * LANES, (h + 1) * LANES)
            qh = q_ref[:, hs]
            s_r = _dot_nt(qh, kr_ref[:, hs])
            s_m = _dot_nt(qh, km_ref[:, hs])
            m = jnp.maximum(jnp.max(s_r, axis=-1, keepdims=True), jnp.max(s_m, axis=-1, keepdims=True))
            p_r = jnp.exp2(s_r - m)
            p_m = jnp.exp2(s_m - m)
            den = jnp.sum(p_r, axis=-1, keepdims=True) + jnp.sum(p_m, axis=-1, keepdims=True)
            acc = _dot(p_r.astype(BF16), v_r) + _dot(p_m.astype(BF16), v_m)
            outs.append(acc / den)
        lane = lax.broadcasted_iota(I32, outs[0].shape, 1)
        o_ref[:, vs] = jnp.where(lane < MLA_V, outs[0], outs[1]).astype(BF16)


META_PER_BLOCK = LANES // N_META
REDUCE_ROWS = 64


def _col_reduce(x, op):
    rows, n = x.shape
    if rows > REDUCE_ROWS and rows % REDUCE_ROWS == 0:
        x = op(x.reshape(rows // REDUCE_ROWS, REDUCE_ROWS, n), axis=0)
    return op(x, axis=0, keepdims=True)


def _mla_attn_kernel(q_ref, kr_ref, kt_ref, vtr_ref, vtt_ref, o_ref, ot_ref):
    b = pl.program_id(0)
    r = lax.broadcasted_iota(I32, (LANES, 1), 0)
    mine = jnp.right_shift(r, int(math.log2(N_META))) == (b % META_PER_BLOCK)
    mbias = jnp.where(mine, 0.0, -jnp.inf).astype(F32)
    def scores(h):
        hs = slice(h * LANES, (h + 1) * LANES)
        qh = q_ref[:, hs]
        return _dot_nt(kr_ref[:, hs], qh), _dot_nt(kt_ref[:, hs], qh) + mbias

    def weighted_values(h, p_r, p_m, den):
        vs = slice(h * MLA_V, (h + 1) * MLA_V)
        acc = _dot(vtr_ref[vs, :], p_r) + _dot(vtt_ref[vs, :], p_m)
        ot_ref[vs, :] = acc / den

    nxt = scores(0)
    pending = None
    for h in range(MLA_HEADS):
        s_r, s_m = nxt
        if h + 1 < MLA_HEADS:
            nxt = scores(h + 1)
        m = jnp.maximum(_col_reduce(s_r, jnp.max), _col_reduce(s_m, jnp.max))
        p_r = jnp.exp2(s_r - m)
        p_m = jnp.exp2(s_m - m)
        den = _col_reduce(p_r, jnp.sum) + _col_reduce(p_m, jnp.sum)
        if pending is not None:
            weighted_values(*pending)
        pending = (h, p_r.astype(BF16), p_m.astype(BF16), den)
    weighted_values(*pending)
    o_ref[...] = ot_ref[...].T.astype(BF16)


def _mla_attn_call(q, k, vt, *, batch, seq, n_real, tq):
    nq = seq // tq
    tail0 = n_real // LANES
    qmap = lambda b, j: (b * nq + j, 0)
    return pl.pallas_call(
        _mla_attn_kernel,
        grid=(batch, nq),
        in_specs=[
            pl.BlockSpec((tq, 2048), qmap),
            pl.BlockSpec((seq, 2048), lambda b, j: (b, 0)),
            pl.BlockSpec((LANES, 2048), lambda b, j: (tail0 + b // META_PER_BLOCK, 0)),
            pl.BlockSpec((1024, seq), lambda b, j: (0, b)),
            pl.BlockSpec((1024, LANES), lambda b, j: (0, tail0 + b // META_PER_BLOCK)),
        ],
        out_specs=pl.BlockSpec((tq, 1024), qmap),
        out_shape=jax.ShapeDtypeStruct((n_real, 1024), BF16),
        scratch_shapes=[pltpu.VMEM((1024, tq), F32)],
        compiler_params=_params(("parallel", "arbitrary")),
        name="mla_attn",
    )(q, k, k, vt, vt)


def _mla_meta_kernel(q_ref, kr_ref, km_ref, vr_ref, vm_ref, o_ref, *, batch):
    i = pl.program_id(0)

    @pl.when(i < batch)
    def _():
        _mla_attend(q_ref, kr_ref, km_ref, vr_ref, vm_ref, o_ref)

    @pl.when(i >= batch)
    def _():
        o_ref[...] = jnp.zeros_like(o_ref)


def _mla_meta_call(q, k, v, *, batch, seq, n_real):
    ntot = q.shape[0]
    mrow = n_real // N_META
    ntail_blocks = (ntot - n_real) // N_META
    tmap = lambda i: (mrow + i, 0)
    rmap = lambda i: (jnp.minimum(i, batch - 1), 0)
    return pl.pallas_call(
        functools.partial(_mla_meta_kernel, batch=batch),
        grid=(ntail_blocks,),
        in_specs=[
            pl.BlockSpec((N_META, 2048), tmap),
            pl.BlockSpec((seq, 2048), rmap),
            pl.BlockSpec((N_META, 2048), tmap),
            pl.BlockSpec((seq, 1024), rmap),
            pl.BlockSpec((N_META, 1024), tmap),
        ],
        out_specs=pl.BlockSpec((N_META, 1024), lambda i: (i, 0)),
        out_shape=jax.ShapeDtypeStruct((ntot - n_real, 1024), BF16),
        compiler_params=_params(("arbitrary",)),
        name="mla_attn_meta",
    )(q, k, k, v, v)


def _prep_gqa(a_wqkv, a_wo):
    d = a_wqkv.shape[0]
    nq = GQA_HEADS * GQA_HD
    nkv = GQA_KV * GQA_HD
    wq = a_wqkv[:, :nq] * (GQA_HD ** -0.5)

    def dup(w):
        w = w.reshape(d, GQA_KV, 1, GQA_HD)
        return jnp.broadcast_to(w, (d, GQA_KV, 2, GQA_HD)).reshape(d, 2 * nkv)

    w = jnp.concatenate([wq, dup(a_wqkv[:, nq:nq + nkv]), dup(a_wqkv[:, nq + nkv:])], axis=1)
    return w.astype(BF16), a_wo.astype(BF16)


def _prep_mla(b_wdkv, b_wuq, b_wukv):
    d = b_wdkv.shape[0]
    lat = MLA_Q_RANK + MLA_KV_RANK
    z = lambda n: jnp.zeros((d, n), F32)
    wd = jnp.concatenate([b_wdkv[:, :lat], z(MLA_NOPE), b_wdkv[:, lat:], z(LANES - MLA_QK)], axis=1)
    wuq = b_wuq.reshape(MLA_Q_RANK, MLA_HEADS, MLA_QK)
    wuq = jnp.pad(wuq, ((0, 0), (0, 0), (0, LANES - MLA_QK))).reshape(MLA_Q_RANK, MLA_HEADS * LANES)
    wukv = b_wukv.reshape(MLA_KV_RANK, MLA_HEADS, MLA_NOPE + MLA_V)
    wuk = jnp.pad(wukv[:, :, :MLA_NOPE], ((0, 0), (0, 0), (0, LANES - MLA_NOPE)))
    wuk = wuk.reshape(MLA_KV_RANK, MLA_HEADS * LANES)
    wuv = wukv[:, :, MLA_NOPE:].reshape(MLA_KV_RANK, MLA_HEADS * MLA_V)
    return wd.astype(BF16), wuq.astype(BF16), wuk.astype(BF16), wuv.astype(BF16)


def _prep_router(m_router):
    hi = m_router.astype(BF16)
    lo = (m_router - hi.astype(F32)).astype(BF16)
    pad = lambda r: jnp.pad(r, ((0, 0), (0, LANES - N_EXPERTS)))
    return jnp.concatenate([pad(hi), pad(lo)], axis=1)


def _positions(seq, n_tail, n_meta_rows):
    real = N_META + jnp.arange(seq, dtype=F32)
    t = jnp.arange(n_tail)
    tail = jnp.where(t < n_meta_rows, t % N_META, 0).astype(F32)
    return jnp.concatenate([real, tail])


def _trunk_flat(x_groups, meta, norm_mix, norm_ffn, norm_final,
                a_wqkv, a_wo, a_sink,
                b_wdkv, b_qnorm, b_kvnorm, b_wuq, b_wukv, b_wo,
                f_wg, f_wu, f_wd,
                m_router, m_wg, m_wu, m_wd):
    seq = x_groups[0].shape[1]
    batches = [x.shape[0] for x in x_groups]
    batch = sum(batches)
    n_real = batch * seq
    n_meta_rows = batch * N_META
    big = _pick_tile(seq, (1024, 512, 256, 128))
    tp = min(big, 512)
    ntot = -(-(n_real + n_meta_rows) // big) * big
    n_tail = ntot - n_real

    tail = jnp.concatenate([
        jnp.broadcast_to(meta.astype(F32)[None], (batch, N_META, D_MODEL)).reshape(n_meta_rows, D_MODEL),
        jnp.zeros((n_tail - n_meta_rows, D_MODEL), F32)], axis=0)
    h_sources = [x.reshape(-1, D_MODEL) for x in x_groups] + [tail]

    pos = _positions(seq, n_tail, n_meta_rows)
    tabs_a = _rope_lane_tables(pos, GQA_THETA, GQA_ROT, GQA_HD, 0)
    tabs_b = _rope_lane_tables(pos, MLA_THETA, MLA_ROPE, LANES, MLA_NOPE)
    row = lambda v: v.reshape(1, -1).astype(F32)

    wqkv, wo_a = _prep_gqa(a_wqkv[0], a_wo[0])
    q, k, v = _qkv_call(h_sources, row(norm_mix[0]), wqkv, tabs_a, tm=tp, n_real=n_real, seq=seq)
    sink = a_sink[0].astype(F32) * LOG2E
    o = _win_attn_call(sink, q, k, v, batch=batch, seq=seq, n_real=n_real)
    o_tail = _win_meta_call(sink, q, k, v, batch=batch, seq=seq, n_real=n_real)
    h, xn = _wo_call(o, o_tail, wo_a, h_sources, row(norm_ffn[0]), tm=big)
    fdim = f_wg.shape[2]
    tf = _pick_tile(fdim, (512, 256, 128))
    h, xn = _ffn_dense_call(xn, f_wg[0].astype(BF16), f_wu[0].astype(BF16), f_wd[0].astype(BF16),
                            h, row(norm_mix[1]), tm=tp, tf=tf)

    wd, wuq, wuk, wuv = _prep_mla(b_wdkv[0], b_wuq[0], b_wukv[0])
    q, k, v, vt = _mla_proj_call(xn, wd, row(b_qnorm[0]), row(b_kvnorm[0]), wuq, wuk, wuv, tabs_b,
                                 tm=tp, n_real=n_real, seq=seq)
    tq = _pick_tile(seq, (256, 128))
    o = _mla_attn_call(q, k, vt, batch=batch, seq=seq, n_real=n_real, tq=tq)
    o_tail = _mla_meta_call(q, k, v, batch=batch, seq=seq, n_real=n_real)
    tr = tp
    h, xpk, route, counts = _wo_router_call(o, o_tail, b_wo[0].astype(BF16), h, row(norm_ffn[1]),
                                            _prep_router(m_router[0]), tm=tr)

    tmx = big
    counts = counts[0, :N_EXPERTS].astype(I32)
    padded = ((counts + tmx - 1) // tmx) * tmx
    pend = jnp.cumsum(padded)
    pstart = pend - padded
    e_idx = route[:, 0:2].astype(I32)
    rank = route[:, 2:4].astype(I32)
    dest = pstart[e_idx] + rank
    n_assign = 2 * ntot
    n_blocks = -(-n_assign // tmx) + N_EXPERTS
    rows = n_blocks * tmx
    block_e = jnp.clip(jnp.searchsorted(pend, jnp.arange(n_blocks, dtype=I32) * tmx, side="right"),
                       0, N_EXPERTS - 1).astype(I32)
    n_used = (pend[-1] // tmx).astype(I32).reshape(1)
    block_e = jnp.where(jnp.arange(n_blocks) < n_used[0], block_e, block_e[jnp.maximum(n_used[0] - 1, 0)])

    td = tp
    dest_t = dest.reshape(ntot // td, td, 2).transpose(0, 2, 1)
    xs = _dispatch_call(dest_t, xpk, jnp.zeros((rows * XPK_ROWS, LANES), U32), td=td)
    edim = m_wg.shape[3]
    tfx = _pick_tile(edim, (512, 256, 128))
    ys = _ffn_expert_call(block_e, n_used, xs, m_wg[0].astype(BF16), m_wu[0].astype(BF16),
                          m_wd[0].astype(BF16), tm=tmx, tf=tfx)

    outs = []
    tile0 = 0
    for b, x in zip(batches, x_groups):
        ntiles = b * seq // td
        y = _combine_call(dest_t, ys, h, route, row(norm_final), tc=td, tile0=tile0, ntiles=ntiles)
        outs.append(y.reshape(b, seq, D_MODEL))
        tile0 += ntiles
    return tuple(outs)


def kernel(x_prompt, x_sample, meta, norm_mix, norm_ffn, norm_final, a_wqkv, a_wo, a_sink, b_wdkv, b_qnorm, b_kvnorm, b_wuq, b_wukv, b_wo, f_wg, f_wu, f_wd, m_router, m_wg, m_wu, m_wd):
    assert x_prompt.shape[1] == x_sample.shape[1] and x_prompt.shape[1] % BLK == 0
    assert norm_mix.shape[0] == 2, "two layers: windowed GQA + dense FFN, then MLA + MoE"
    return _trunk_flat((x_prompt, x_sample), meta, norm_mix, norm_ffn, norm_final,
                       a_wqkv, a_wo, a_sink,
                       b_wdkv, b_qnorm, b_kvnorm, b_wuq, b_wukv, b_wo,
                       f_wg, f_wu, f_wd,
                       m_router, m_wg, m_wu, m_wd)
```

```python
import functools
import math

import jax
import jax.numpy as jnp
from jax import lax
from jax.experimental import pallas as pl
from jax.experimental.pallas import tpu as pltpu

F32 = jnp.float32
BF16 = jnp.bfloat16
U32 = jnp.uint32
I32 = jnp.int32

D_MODEL = 1024
N_META = 16
RMS_EPS = 1e-6
BLK = 128
GQA_HEADS = 16
GQA_KV = 4
GQA_GROUP = 4
GQA_HD = 64
GQA_ROT = 16
GQA_THETA = 500000.0
MLA_HEADS = 16
MLA_NOPE = 64
MLA_ROPE = 32
MLA_V = 64
MLA_Q_RANK = 384
MLA_KV_RANK = 256
MLA_THETA = 10000.0
MLA_QK = MLA_NOPE + MLA_ROPE
N_EXPERTS = 8
LOG2E = 1.4426950408889634

LANES = 128
VMEM_LIMIT = 56 * 1024 * 1024


def _params(sem, vmem=VMEM_LIMIT):
    return pltpu.CompilerParams(dimension_semantics=sem, vmem_limit_bytes=vmem)


def _rms(x, g):
    ms = jnp.mean(x * x, axis=-1, keepdims=True)
    return x * lax.rsqrt(ms + RMS_EPS) * g


def _dot(a, b):
    return jnp.dot(a, b, preferred_element_type=F32)


def _dot_nt(a, b):
    return lax.dot_general(a, b, (((1,), (1,)), ((), ())), preferred_element_type=F32)


def _log2(n):
    assert n & (n - 1) == 0, n
    return n.bit_length() - 1


def _pick_tile(n, candidates):
    for c in candidates:
        if n % c == 0:
            return c
    raise ValueError(f"no tile in {candidates} divides {n}")


def _rope_lane_tables(pos, theta, rot, period, offset):
    half = rot // 2
    inv = jnp.power(jnp.float32(theta), -(jnp.arange(0, rot, 2, dtype=F32) / rot))
    ang = pos[:, None] * inv[None, :]
    cos, sin = jnp.cos(ang), jnp.sin(ang)
    lane = jnp.arange(LANES)
    r = (lane % period) - offset
    is_x1 = (r >= 0) & (r < half)
    is_x2 = (r >= half) & (r < rot)
    f = jnp.clip(jnp.where(is_x2, r - half, r), 0, half - 1)
    cos_l = cos[:, f]
    sin_l = sin[:, f]
    c = jnp.where((is_x1 | is_x2)[None, :], cos_l, 1.0)
    s1 = jnp.where(is_x2[None, :], sin_l, 0.0)
    s2 = jnp.where(is_x1[None, :], -sin_l, 0.0)
    return c.astype(F32), s1.astype(F32), s2.astype(F32)


def _rope(x, c, s1, s2, half):
    return x * c + pltpu.roll(x, half, 1) * s1 + pltpu.roll(x, LANES - half, 1) * s2


def _source_specs(sources, tm):
    specs, start = [], 0
    for src in sources:
        n = src.shape[0] // tm
        specs.append(pl.BlockSpec((tm, src.shape[1]),
                                  lambda i, start=start, n=n: (jnp.clip(i - start, 0, n - 1), 0)))
        start += n
    return specs


def _select_source(refs, src_rows, tm):
    i = pl.program_id(0)
    ends, acc = [], 0
    for n in src_rows:
        acc += n // tm
        ends.append(acc)
    x = refs[-1][...]
    for r, end in reversed(list(zip(refs[:-1], ends[:-1]))):
        x = jnp.where(i < end, r[...], x)
    return x


def _qkv_kernel(*refs, n_src, src_rows, tm):
    h_refs = refs[:n_src]
    g_ref, w_ref, c_ref, s1_ref, s2_ref, q_ref, k_ref, v_ref = refs[n_src:]
    xn = _rms(_select_source(h_refs, src_rows, tm), g_ref[...]).astype(BF16)
    qkv = _dot(xn, w_ref[...])
    c, s1, s2 = c_ref[...], s1_ref[...], s2_ref[...]
    nq = GQA_HEADS * GQA_HD // LANES
    nk = GQA_KV
    for i in range(nq):
        q_ref[:, i * LANES:(i + 1) * LANES] = (_rope(
            qkv[:, i * LANES:(i + 1) * LANES], c, s1, s2, GQA_ROT // 2) * LOG2E).astype(BF16)
    for i in range(nk):
        lo = (nq + i) * LANES
        k_ref[:, i * LANES:(i + 1) * LANES] = _rope(
            qkv[:, lo:lo + LANES], c, s1, s2, GQA_ROT // 2).astype(BF16)
    v_ref[...] = qkv[:, (nq + nk) * LANES:].astype(BF16)


def _table_index(i, n_real_tiles, tiles_per_seq):
    return jnp.where(i < n_real_tiles, i % tiles_per_seq, tiles_per_seq + i - n_real_tiles)


def _qkv_call(h_sources, g, w, tabs, *, tm, n_real, seq):
    src_rows = tuple(s.shape[0] for s in h_sources)
    ntot = sum(src_rows)
    nrt, tps = n_real // tm, seq // tm
    tab_spec = pl.BlockSpec((tm, LANES), lambda i: (_table_index(i, nrt, tps), 0))
    return pl.pallas_call(
        functools.partial(_qkv_kernel, n_src=len(h_sources), src_rows=src_rows, tm=tm),
        grid=(ntot // tm,),
        in_specs=_source_specs(h_sources, tm) + [
            pl.BlockSpec((1, D_MODEL), lambda i: (0, 0)),
            pl.BlockSpec(w.shape, lambda i: (0, 0)),
            tab_spec, tab_spec, tab_spec,
        ],
        out_specs=[
            pl.BlockSpec((tm, 1024), lambda i: (i, 0)),
            pl.BlockSpec((tm, GQA_KV * LANES), lambda i: (i, 0)),
            pl.BlockSpec((tm, GQA_KV * LANES), lambda i: (i, 0)),
        ],
        out_shape=[
            jax.ShapeDtypeStruct((ntot, 1024), BF16),
            jax.ShapeDtypeStruct((ntot, GQA_KV * LANES), BF16),
            jax.ShapeDtypeStruct((ntot, GQA_KV * LANES), BF16),
        ],
        compiler_params=_params(("parallel",)),
        name="qkv_rope",
    )(*h_sources, g, w, *tabs)


def _gqa_attend(sink_ref, q_ref, o_ref, k_pieces, v_pieces, bias):
    nq = q_ref.shape[0]
    nkeys = bias.shape[1]
    have = sum(p.shape[0] for p in k_pieces)
    row = lax.broadcasted_iota(I32, (2 * nq, 1), 0)
    lane_o = lax.broadcasted_iota(I32, (2 * nq, LANES), 1)
    lane_k = lax.broadcasted_iota(I32, (nkeys, LANES), 1)
    pad = [jnp.zeros((nkeys - have, LANES), BF16)] if nkeys > have else []

    def scores(kv):
        sl = slice(kv * LANES, (kv + 1) * LANES)
        kcat = jnp.concatenate([p[:, sl] for p in k_pieces] + pad, axis=0)
        zero = jnp.zeros_like(kcat)
        lo = slice(2 * kv * LANES, (2 * kv + 1) * LANES)
        hi = slice((2 * kv + 1) * LANES, (2 * kv + 2) * LANES)
        qp = jnp.concatenate([q_ref[:, lo], q_ref[:, hi]], axis=0)
        return (_dot_nt(qp, jnp.where(lane_k < GQA_HD, kcat, zero)) + bias,
                _dot_nt(qp, jnp.where(lane_k >= GQA_HD, kcat, zero)) + bias)

    def softmax(kv, ss):
        out = []
        for par, s in enumerate(ss):
            head = kv * GQA_GROUP + par
            sink = jnp.where(row < nq, sink_ref[head], sink_ref[head + 2])
            m = jnp.maximum(jnp.max(s, axis=-1, keepdims=True), sink)
            p = jnp.exp2(s - m)
            den = jnp.sum(p, axis=-1, keepdims=True) + jnp.exp2(sink - m)
            out.append((p.astype(BF16), den))
        return out

    def weighted_values(kv, pd):
        sl = slice(kv * LANES, (kv + 1) * LANES)
        vcat = jnp.concatenate([p[:, sl] for p in v_pieces] + pad, axis=0)
        outs = [_dot(p, vcat) / den for p, den in pd]
        o = jnp.where(lane_o < GQA_HD, outs[0], outs[1]).astype(BF16)
        o_ref[:, 2 * kv * LANES:(2 * kv + 1) * LANES] = o[:nq]
        o_ref[:, (2 * kv + 1) * LANES:(2 * kv + 2) * LANES] = o[nq:]

    nxt = scores(0)
    pending = None
    for kv in range(GQA_KV):
        ss = nxt
        if kv + 1 < GQA_KV:
            nxt = scores(kv + 1)
        pd = softmax(kv, ss)
        if pending is not None:
            weighted_values(*pending)
        pending = (kv, pd)
    weighted_values(*pending)


def _win_attn_kernel(sink_ref, q_ref, kp_ref, kc_ref, kn_ref, vp_ref, vc_ref, vn_ref,
                     km_ref, vm_ref, o_ref, *, nb):
    j = pl.program_id(1)
    nkeys = 4 * BLK
    qi = lax.broadcasted_iota(I32, (2 * BLK, nkeys), 0) & (BLK - 1)
    c = lax.broadcasted_iota(I32, (2 * BLK, nkeys), 1)
    neg = jnp.float32(-jnp.inf)
    b_p = jnp.where((c >= qi) & (j > 0), 0.0, neg)
    b_n = jnp.where(((c - 2 * BLK) <= qi) & (j < nb - 1), 0.0, neg)
    b_m = jnp.where(c < 3 * BLK + N_META, 0.0, neg)
    bias = jnp.where(c < BLK, b_p, jnp.where(c < 2 * BLK, 0.0, jnp.where(c < 3 * BLK, b_n, b_m)))
    _gqa_attend(sink_ref, q_ref, o_ref, (kp_ref, kc_ref, kn_ref, km_ref),
                (vp_ref, vc_ref, vn_ref, vm_ref), bias)


def _win_attn_call(sink, q, k, v, *, batch, seq, n_real):
    ntot = q.shape[0]
    nb = seq // BLK
    mrow = n_real // N_META

    def qmap(b, j, s):
        return (b * nb + j, 0)

    def pmap(b, j, s):
        return (b * nb + jnp.maximum(j - 1, 0), 0)

    def nmap(b, j, s):
        return (b * nb + jnp.minimum(j + 1, nb - 1), 0)

    def mmap(b, j, s):
        return (mrow + b, 0)

    kvw = GQA_KV * LANES
    kv_spec = lambda f: pl.BlockSpec((BLK, kvw), f)
    grid_spec = pltpu.PrefetchScalarGridSpec(
        num_scalar_prefetch=1,
        grid=(batch, nb),
        in_specs=[
            pl.BlockSpec((BLK, 1024), qmap),
            kv_spec(pmap), kv_spec(qmap), kv_spec(nmap),
            kv_spec(pmap), kv_spec(qmap), kv_spec(nmap),
            pl.BlockSpec((N_META, kvw), mmap),
            pl.BlockSpec((N_META, kvw), mmap),
        ],
        out_specs=pl.BlockSpec((BLK, 1024), qmap),
    )
    return pl.pallas_call(
        functools.partial(_win_attn_kernel, nb=nb),
        grid_spec=grid_spec,
        out_shape=jax.ShapeDtypeStruct((n_real, 1024), BF16),
        compiler_params=_params(("parallel", "parallel")),
        name="win_attn",
    )(sink, q, k, k, k, v, v, v, k, v)


def _win_meta_kernel(sink_ref, q_ref, km_ref, kc_ref, vm_ref, vc_ref, o_ref, *, batch):
    i = pl.program_id(0)

    @pl.when(i < batch)
    def _():
        nkeys = 2 * BLK
        qp = lax.broadcasted_iota(I32, (2 * N_META, nkeys), 0) & (N_META - 1)
        c = lax.broadcasted_iota(I32, (2 * N_META, nkeys), 1)
        neg = jnp.float32(-jnp.inf)
        b_c = jnp.where((N_META + c - qp) <= BLK, 0.0, neg)
        bias = jnp.where(c < BLK, b_c, jnp.where(c < BLK + N_META, 0.0, neg))
        _gqa_attend(sink_ref, q_ref, o_ref, (kc_ref, km_ref), (vc_ref, vm_ref), bias)

    @pl.when(i >= batch)
    def _():
        o_ref[...] = jnp.zeros_like(o_ref)


def _win_meta_call(sink, q, k, v, *, batch, seq, n_real):
    ntot = q.shape[0]
    nb = seq // BLK
    mrow = n_real // N_META
    ntail_blocks = (ntot - n_real) // N_META

    def tmap(i, s):
        return (mrow + i, 0)

    def cmap(i, s):
        return (jnp.minimum(i, batch - 1) * nb, 0)

    grid_spec = pltpu.PrefetchScalarGridSpec(
        num_scalar_prefetch=1,
        grid=(ntail_blocks,),
        in_specs=[
            pl.BlockSpec((N_META, 1024), tmap),
            pl.BlockSpec((N_META, GQA_KV * LANES), tmap),
            pl.BlockSpec((BLK, GQA_KV * LANES), cmap),
            pl.BlockSpec((N_META, GQA_KV * LANES), tmap),
            pl.BlockSpec((BLK, GQA_KV * LANES), cmap),
        ],
        out_specs=pl.BlockSpec((N_META, 1024), lambda i, s: (i, 0)),
    )
    return pl.pallas_call(
        functools.partial(_win_meta_kernel, batch=batch),
        grid_spec=grid_spec,
        out_shape=jax.ShapeDtypeStruct((ntot - n_real, 1024), BF16),
        compiler_params=_params(("parallel",)),
        name="win_attn_meta",
    )(sink, q, k, k, v, v)


def _attn_out_tile(o_real_ref, o_tail_ref, n_real_tiles):
    is_real = pl.program_id(0) < n_real_tiles
    return jnp.where(is_real, o_real_ref[...], o_tail_ref[...])


def _attn_out_specs(tm, n_real_tiles):
    return [
        pl.BlockSpec((tm, 1024), lambda i: (jnp.minimum(i, n_real_tiles - 1), 0)),
        pl.BlockSpec((tm, 1024), lambda i: (jnp.maximum(i - n_real_tiles, 0), 0)),
    ]


def _wo_kernel(*refs, n_real_tiles, n_src, src_rows, tm):
    o_real_ref, o_tail_ref, w_ref = refs[:3]
    h_refs = refs[3:3 + n_src]
    g_ref, hout_ref, xn_ref = refs[3 + n_src:]
    o = _attn_out_tile(o_real_ref, o_tail_ref, n_real_tiles)
    hn = _dot(o, w_ref[...]) + _select_source(h_refs, src_rows, tm)
    hout_ref[...] = hn
    xn_ref[...] = _rms(hn, g_ref[...]).astype(BF16)


def _wo_call(o_real, o_tail, w, h_sources, g, *, tm):
    src_rows = tuple(s.shape[0] for s in h_sources)
    ntot = sum(src_rows)
    nrt = o_real.shape[0] // tm
    row = lambda i: (i, 0)
    fix = lambda i: (0, 0)
    return pl.pallas_call(
        functools.partial(_wo_kernel, n_real_tiles=nrt, n_src=len(h_sources), src_rows=src_rows, tm=tm),
        grid=(ntot // tm,),
        in_specs=_attn_out_specs(tm, nrt) + [pl.BlockSpec((1024, D_MODEL), fix)]
        + _source_specs(h_sources, tm) + [
            pl.BlockSpec((1, D_MODEL), fix),
        ],
        out_specs=[pl.BlockSpec((tm, D_MODEL), row), pl.BlockSpec((tm, D_MODEL), row)],
        out_shape=[jax.ShapeDtypeStruct((ntot, D_MODEL), F32),
                   jax.ShapeDtypeStruct((ntot, D_MODEL), BF16)],
        compiler_params=_params(("parallel",)),
        name="wo_res_norm",
    )(o_real, o_tail, w, *h_sources, g)


XPK_ROWS = D_MODEL // 2 // LANES
YS_ROWS = D_MODEL // LANES


ROW_SPLIT = 2


def _store_token_major(ref, x, row0=0):
    t, n = x.shape
    k = n // LANES
    for s in range(k):
        ref[pl.ds(row0 * k + s, t, stride=k), :] = x[:, s * LANES:(s + 1) * LANES]


def _load_token_major(ref, k):
    t = ref.shape[0] // k
    return [ref[pl.ds(s, t, stride=k), :] for s in range(k)]


def _pack_bf16_pair(a, b):
    ra = pltpu.bitcast(a.astype(BF16).astype(F32), U32)
    rb = pltpu.bitcast(b.astype(BF16).astype(F32), U32)
    return ra | (rb >> 16)


def _unpack_bf16_pair(p):
    a = pltpu.bitcast(p & jnp.uint32(0xFFFF0000), F32).astype(BF16)
    b = pltpu.bitcast(p << 16, F32).astype(BF16)
    return a, b


def _wo_router_kernel(o_real_ref, o_tail_ref, w_ref, h_ref, g_ref, r_ref,
                      hout_ref, xpk_ref, route_ref, cnt_ref, carry_ref, *, n_real_tiles):
    i = pl.program_id(0)
    tm = h_ref.shape[0]

    @pl.when(i == 0)
    def _():
        carry_ref[...] = jnp.zeros_like(carry_ref)

    o = _attn_out_tile(o_real_ref, o_tail_ref, n_real_tiles)
    th = tm // ROW_SPLIT
    parts = [slice(p * th, (p + 1) * th) for p in range(ROW_SPLIT)]
    half = D_MODEL // 2
    r2 = r_ref[...]
    lane = lax.broadcasted_iota(I32, (th, LANES), 1)
    rr = lax.broadcasted_iota(I32, (th, th), 0)
    cc = lax.broadcasted_iota(I32, (th, th), 1)
    lower = jnp.where(cc < rr, 1.0, 0.0).astype(BF16)

    ys = []
    for rs in parts:
        hn = _dot(o[rs], w_ref[...]) + h_ref[rs, :]
        hout_ref[rs, :] = hn
        ys.append(_rms(hn, g_ref[...]))
    logits = []
    for p, y in enumerate(ys):
        _store_token_major(xpk_ref, _pack_bf16_pair(y[:, :half], y[:, half:]), row0=p * th)
        yhi = y.astype(BF16)
        ylo = (y - yhi.astype(F32)).astype(BF16)
        t = _dot(yhi, r2)
        logits.append(t[:, :LANES] + t[:, LANES:] + _dot(ylo, r2[:, :LANES]))
    carry = carry_ref[0:1, :]
    for rs, lg in zip(parts, logits):
        lm = jnp.where(lane < N_EXPERTS, lg, -jnp.inf)
        m1 = jnp.max(lm, axis=-1, keepdims=True)
        i1 = jnp.min(jnp.where(lm == m1, lane, LANES), axis=-1, keepdims=True)
        lm2 = jnp.where(lane == i1, -jnp.inf, lm)
        m2 = jnp.max(lm2, axis=-1, keepdims=True)
        i2 = jnp.min(jnp.where(lm2 == m2, lane, LANES), axis=-1, keepdims=True)
        e21 = jnp.exp(m2 - m1)
        g0 = 1.0 / (1.0 + e21)
        g1 = e21 / (1.0 + e21)
        onehot = ((lane == i1) | (lane == i2)).astype(F32)
        before = _dot(lower, onehot.astype(BF16)) + carry
        rank1 = jnp.sum(jnp.where(lane == i1, before, 0.0), axis=-1, keepdims=True)
        rank2 = jnp.sum(jnp.where(lane == i2, before, 0.0), axis=-1, keepdims=True)
        carry = carry + jnp.sum(onehot, axis=0, keepdims=True)
        out = jnp.where(lane == 0, i1.astype(F32), 0.0)
        out = jnp.where(lane == 1, i2.astype(F32), out)
        out = jnp.where(lane == 2, rank1, out)
        out = jnp.where(lane == 3, rank2, out)
        out = jnp.where(lane == 4, g0, out)
        out = jnp.where(lane == 5, g1, out)
        route_ref[rs, :] = out
    carry_ref[...] = jnp.broadcast_to(carry, carry_ref.shape)
    cnt_ref[...] = jnp.broadcast_to(carry, cnt_ref.shape)


def _wo_router_call(o_real, o_tail, w, h, g, r2, *, tm):
    ntot = h.shape[0]
    nrt = o_real.shape[0] // tm
    row = lambda i: (i, 0)
    fix = lambda i: (0, 0)
    return pl.pallas_call(
        functools.partial(_wo_router_kernel, n_real_tiles=nrt),
        grid=(ntot // tm,),
        in_specs=_attn_out_specs(tm, nrt) + [
            pl.BlockSpec((1024, D_MODEL), fix),
            pl.BlockSpec((tm, D_MODEL), row),
            pl.BlockSpec((1, D_MODEL), fix),
            pl.BlockSpec((D_MODEL, 2 * LANES), fix),
        ],
        out_specs=[
            pl.BlockSpec((tm, D_MODEL), row),
            pl.BlockSpec((tm * XPK_ROWS, LANES), row),
            pl.BlockSpec((tm, LANES), row),
            pl.BlockSpec((8, LANES), fix),
        ],
        out_shape=[
            jax.ShapeDtypeStruct((ntot, D_MODEL), F32),
            jax.ShapeDtypeStruct((ntot * XPK_ROWS, LANES), U32),
            jax.ShapeDtypeStruct((ntot, LANES), F32),
            jax.ShapeDtypeStruct((8, LANES), F32),
        ],
        scratch_shapes=[pltpu.VMEM((8, LANES), F32)],
        compiler_params=_params(("arbitrary",)),
        name="wo_res_norm_router",
    )(o_real, o_tail, w, h, g, r2)


def _swiglu_chunk(x, wg, wu, wd):
    g = _dot(x, wg)
    u = _dot(x, wu)
    a = (g * (1.0 / (1.0 + jnp.exp(-g)))) * u
    return _dot(a.astype(BF16), wd)


def _ffn_dense_kernel(x_ref, wg_ref, wu_ref, wd_ref, h_ref, g_ref, hout_ref, xn_ref, *, tf):
    x = x_ref[...]
    hn = h_ref[...]
    for c in range(wg_ref.shape[1] // tf):
        cs = slice(c * tf, (c + 1) * tf)
        hn = hn + _swiglu_chunk(x, wg_ref[:, cs], wu_ref[:, cs], wd_ref[cs, :])
    hout_ref[...] = hn
    xn_ref[...] = _rms(hn, g_ref[...]).astype(BF16)


def _ffn_dense_call(x, wg, wu, wd, h, g, *, tm, tf):
    ntot = x.shape[0]
    fdim = wg.shape[1]
    row = lambda i: (i, 0)
    fix = lambda i: (0, 0)
    once = pl.Buffered(1)
    return pl.pallas_call(
        functools.partial(_ffn_dense_kernel, tf=tf),
        grid=(ntot // tm,),
        in_specs=[
            pl.BlockSpec((tm, D_MODEL), row),
            pl.BlockSpec((D_MODEL, fdim), fix, pipeline_mode=once),
            pl.BlockSpec((D_MODEL, fdim), fix, pipeline_mode=once),
            pl.BlockSpec((fdim, D_MODEL), fix, pipeline_mode=once),
            pl.BlockSpec((tm, D_MODEL), row),
            pl.BlockSpec((1, D_MODEL), fix),
        ],
        out_specs=[pl.BlockSpec((tm, D_MODEL), row), pl.BlockSpec((tm, D_MODEL), row)],
        out_shape=[jax.ShapeDtypeStruct((ntot, D_MODEL), F32),
                   jax.ShapeDtypeStruct((ntot, D_MODEL), BF16)],
        compiler_params=_params(("parallel",)),
        name="ffn_dense",
    )(x, wg, wu, wd, h, g)


def _ffn_expert_kernel(be_ref, nu_ref, xs_ref, wg_ref, wu_ref, wd_ref, ys_ref, *, tf):
    del be_ref
    used = pl.program_id(0) < nu_ref[0]

    @pl.when(used)
    def _():
        halves = [_unpack_bf16_pair(blk) for blk in _load_token_major(xs_ref, XPK_ROWS)]
        x = jnp.concatenate([a for a, _ in halves] + [b for _, b in halves], axis=1)
        acc = None
        for c in range(wg_ref.shape[1] // tf):
            cs = slice(c * tf, (c + 1) * tf)
            y = _swiglu_chunk(x, wg_ref[:, cs], wu_ref[:, cs], wd_ref[cs, :])
            acc = y if acc is None else acc + y
        _store_token_major(ys_ref, acc)

    @pl.when(jnp.logical_not(used))
    def _():
        ys_ref[...] = jnp.zeros_like(ys_ref)


def _ffn_expert_call(block_e, n_used, xs, wg, wu, wd, *, tm, tf):
    rows = xs.shape[0] // XPK_ROWS
    once = pl.Buffered(1)
    wmap = lambda i, be, nu: (be[i], 0, 0)
    grid_spec = pltpu.PrefetchScalarGridSpec(
        num_scalar_prefetch=2,
        grid=(rows // tm,),
        in_specs=[
            pl.BlockSpec((tm * XPK_ROWS, LANES), lambda i, be, nu: (i, 0)),
            pl.BlockSpec((None,) + wg.shape[1:], wmap, pipeline_mode=once),
            pl.BlockSpec((None,) + wu.shape[1:], wmap, pipeline_mode=once),
            pl.BlockSpec((None,) + wd.shape[1:], wmap, pipeline_mode=once),
        ],
        out_specs=pl.BlockSpec((tm * YS_ROWS, LANES), lambda i, be, nu: (i, 0)),
    )
    return pl.pallas_call(
        functools.partial(_ffn_expert_kernel, tf=tf),
        grid_spec=grid_spec,
        out_shape=jax.ShapeDtypeStruct((rows * YS_ROWS, LANES), F32),
        compiler_params=_params(("arbitrary",)),
        name="ffn_expert",
    )(block_e, n_used, xs, wg, wu, wd)


DMA_UNROLL = 8


def _token_copy(src, src_tok, dst, dst_tok, rows, sem):
    return pltpu.make_async_copy(src.at[pl.ds(pl.multiple_of(src_tok * rows, rows), rows)],
                                 dst.at[pl.ds(pl.multiple_of(dst_tok * rows, rows), rows)], sem)


def _dispatch_kernel(dest_ref, x_ref, xs_in_ref, xs_ref, sem):
    del xs_in_ref
    td = x_ref.shape[0] // XPK_ROWS

    def issue(t, carry):
        for s in range(2):
            _token_copy(x_ref, t, xs_ref, dest_ref[0, s, t], XPK_ROWS, sem).start(priority=s)
        return carry

    lax.fori_loop(0, td, issue, 0, unroll=DMA_UNROLL)

    def drain(t, carry):
        for s in range(2):
            _token_copy(x_ref, 0, xs_ref, 0, XPK_ROWS, sem).wait()
        return carry

    lax.fori_loop(0, td, drain, 0, unroll=DMA_UNROLL)


def _dispatch_call(dest, xpk, xs_init, *, td):
    ntot = xpk.shape[0] // XPK_ROWS
    return pl.pallas_call(
        _dispatch_kernel,
        grid=(ntot // td,),
        in_specs=[
            pl.BlockSpec((1, 2, td), lambda i: (i, 0, 0), memory_space=pltpu.SMEM),
            pl.BlockSpec((td * XPK_ROWS, LANES), lambda i: (i, 0)),
            pl.BlockSpec(memory_space=pl.ANY),
        ],
        out_specs=pl.BlockSpec(memory_space=pl.ANY),
        out_shape=jax.ShapeDtypeStruct(xs_init.shape, U32),
        input_output_aliases={2: 0},
        scratch_shapes=[pltpu.SemaphoreType.DMA(())],
        compiler_params=_params(("arbitrary",)),
        name="moe_dispatch",
    )(dest, xpk, xs_init)


def _combine_kernel(dest_ref, dest_next_ref, ys_ref, h_ref, route_ref, g_ref, out_ref,
                    y0_ref, y1_ref, sems):
    i = pl.program_id(0)
    n = pl.num_programs(0)
    tc = h_ref.shape[0]
    bufs = (y0_ref, y1_ref)
    slot = i % 2

    def issue(d_ref, sl):
        def body(t, carry):
            for s in range(2):
                _token_copy(ys_ref, d_ref[0, s, t], bufs[s].at[sl], t, YS_ROWS,
                            sems.at[sl]).start(priority=s)
            return carry

        lax.fori_loop(0, tc, body, 0, unroll=DMA_UNROLL)

    @pl.when(i == 0)
    def _():
        issue(dest_ref, 0)

    @pl.when(i + 1 < n)
    def _():
        issue(dest_next_ref, 1 - slot)

    def drain(t, carry):
        for s in range(2):
            _token_copy(ys_ref, 0, bufs[s].at[slot], 0, YS_ROWS, sems.at[slot]).wait()
        return carry

    lax.fori_loop(0, tc, drain, 0, unroll=DMA_UNROLL)

    r = route_ref[...]
    lane = lax.broadcasted_iota(I32, r.shape, 1)
    g0 = jnp.sum(jnp.where(lane == 4, r, 0.0), axis=-1, keepdims=True)
    g1 = jnp.sum(jnp.where(lane == 5, r, 0.0), axis=-1, keepdims=True)
    y0 = _load_token_major(y0_ref.at[slot], YS_ROWS)
    y1 = _load_token_major(y1_ref.at[slot], YS_ROWS)
    moe = jnp.concatenate([a * g0 + b * g1 for a, b in zip(y0, y1)], axis=1)
    out_ref[...] = _rms(h_ref[...] + moe, g_ref[...])


def _combine_call(dest, ys, h, route, g, *, tc, tile0, ntiles):
    return pl.pallas_call(
        _combine_kernel,
        grid=(ntiles,),
        in_specs=[
            pl.BlockSpec((1, 2, tc), lambda i: (tile0 + i, 0, 0), memory_space=pltpu.SMEM),
            pl.BlockSpec((1, 2, tc), lambda i: (tile0 + jnp.minimum(i + 1, ntiles - 1), 0, 0),
                         memory_space=pltpu.SMEM),
            pl.BlockSpec(memory_space=pl.ANY),
            pl.BlockSpec((tc, D_MODEL), lambda i: (tile0 + i, 0)),
            pl.BlockSpec((tc, LANES), lambda i: (tile0 + i, 0)),
            pl.BlockSpec((1, D_MODEL), lambda i: (0, 0)),
        ],
        out_specs=pl.BlockSpec((tc, D_MODEL), lambda i: (i, 0)),
        out_shape=jax.ShapeDtypeStruct((ntiles * tc, D_MODEL), F32),
        scratch_shapes=[pltpu.VMEM((2, tc * YS_ROWS, LANES), F32), pltpu.VMEM((2, tc * YS_ROWS, LANES), F32),
                        pltpu.SemaphoreType.DMA((2,))],
        compiler_params=_params(("arbitrary",)),
        name="moe_combine_norm",
    )(dest, dest, ys, h, route, g)


def _mla_proj_kernel(x_ref, wd_ref, qn_ref, kvn_ref, wuq_ref, wuk_ref, wuvt_ref,
                     c_ref, s1_ref, s2_ref, q_ref, k_ref, vt_ref):
    tm = x_ref.shape[0]
    th = tm // ROW_SPLIT
    parts = [slice(p * th, (p + 1) * th) for p in range(ROW_SPLIT)]
    half = MLA_ROPE // 2
    scale = (MLA_QK ** -0.5) * LOG2E
    lats = [_dot(x_ref[rs, :], wd_ref[...]) for rs in parts]
    ups = []
    for lat in lats:
        cq = _rms(lat[:, :MLA_Q_RANK], qn_ref[...]).astype(BF16)
        ckv = _rms(lat[:, MLA_Q_RANK:MLA_Q_RANK + MLA_KV_RANK], kvn_ref[...]).astype(BF16)
        ups.append((_dot(cq, wuq_ref[...]),
                    _dot(ckv, wuk_ref[...]),
                    _dot_nt(wuvt_ref[...], ckv)))
    for rs, lat, (q, k, vt) in zip(parts, lats, ups):
        c, s1, s2 = c_ref[rs, :], s1_ref[rs, :], s2_ref[rs, :]
        kr = _rope(lat[:, MLA_Q_RANK + MLA_KV_RANK:], c, s1, s2, half)
        for h in range(MLA_HEADS):
            sl = slice(h * LANES, (h + 1) * LANES)
            q_ref[rs, sl] = (_rope(q[:, sl], c, s1, s2, half) * scale).astype(BF16)
            k_ref[rs, sl] = (k[:, sl] + kr).astype(BF16)
        vt_ref[:, rs] = vt.astype(BF16)


def _mla_proj_call(x, wd, qn, kvn, wuq, wuk, wuv, tabs, *, tm, n_real, seq):
    ntot = x.shape[0]
    nrt, tps = n_real // tm, seq // tm
    row = lambda i: (i, 0)
    fix = lambda i: (0, 0)
    tab_spec = pl.BlockSpec((tm, LANES), lambda i: (_table_index(i, nrt, tps), 0))
    return pl.pallas_call(
        _mla_proj_kernel,
        grid=(ntot // tm,),
        in_specs=[
            pl.BlockSpec((tm, D_MODEL), row),
            pl.BlockSpec(wd.shape, fix),
            pl.BlockSpec((1, MLA_Q_RANK), fix),
            pl.BlockSpec((1, MLA_KV_RANK), fix),
            pl.BlockSpec(wuq.shape, fix),
            pl.BlockSpec(wuk.shape, fix),
            pl.BlockSpec(wuv.shape[::-1], fix),
            tab_spec, tab_spec, tab_spec,
        ],
        out_specs=[
            pl.BlockSpec((tm, 2048), row),
            pl.BlockSpec((tm, 2048), row),
            pl.BlockSpec((1024, tm), lambda i: (0, i)),
        ],
        out_shape=[
            jax.ShapeDtypeStruct((ntot, 2048), BF16),
            jax.ShapeDtypeStruct((ntot, 2048), BF16),
            jax.ShapeDtypeStruct((1024, ntot), BF16),
        ],
        compiler_params=_params(("parallel",)),
        name="mla_proj",
    )(x, wd, qn, kvn, wuq, wuk, wuv.T, *tabs)


META_PER_BLOCK = LANES // N_META
REDUCE_ROWS = 64
SCORES_AHEAD = 1
VALUES_BEHIND = 1


def _col_reduce(x, op):
    rows, n = x.shape
    if rows > REDUCE_ROWS and rows % REDUCE_ROWS == 0:
        x = op(x.reshape(rows // REDUCE_ROWS, REDUCE_ROWS, n), axis=0)
    return op(x, axis=0, keepdims=True)


def _mla_attn_kernel(q_ref, kr_ref, kt_ref, vtr_ref, vtt_ref, o_ref, ot_ref):
    b = pl.program_id(0)
    r = lax.broadcasted_iota(I32, (LANES, 1), 0)
    mine = jnp.right_shift(r, _log2(N_META)) == (b % META_PER_BLOCK)
    mbias = jnp.where(mine, 0.0, -jnp.inf).astype(F32)
    def scores(h):
        hs = slice(h * LANES, (h + 1) * LANES)
        qh = q_ref[:, hs]
        return _dot_nt(kr_ref[:, hs], qh), _dot_nt(kt_ref[:, hs], qh) + mbias

    def weighted_values(h, p_r, p_m, den):
        vs = slice(h * MLA_V, (h + 1) * MLA_V)
        acc = _dot(vtr_ref[vs, :], p_r) + _dot(vtt_ref[vs, :], p_m)
        ot_ref[vs, :] = acc / den

    ahead = [scores(h) for h in range(SCORES_AHEAD)]
    pending = []
    for h in range(MLA_HEADS):
        s_r, s_m = ahead.pop(0)
        if h + SCORES_AHEAD < MLA_HEADS:
            ahead.append(scores(h + SCORES_AHEAD))
        m = jnp.maximum(_col_reduce(s_r, jnp.max), _col_reduce(s_m, jnp.max))
        p_r = jnp.exp2(s_r - m)
        p_m = jnp.exp2(s_m - m)
        den = _col_reduce(p_r, jnp.sum) + _col_reduce(p_m, jnp.sum)
        pending.append((h, p_r.astype(BF16), p_m.astype(BF16), den))
        if len(pending) > VALUES_BEHIND:
            weighted_values(*pending.pop(0))
    for item in pending:
        weighted_values(*item)
    o_ref[...] = ot_ref[...].T.astype(BF16)


def _mla_attn_call(q, k, vt, *, batch, seq, n_real, tq):
    nq = seq // tq
    tail0 = n_real // LANES
    qmap = lambda b, j: (b * nq + j, 0)
    return pl.pallas_call(
        _mla_attn_kernel,
        grid=(batch, nq),
        in_specs=[
            pl.BlockSpec((tq, 2048), qmap),
            pl.BlockSpec((seq, 2048), lambda b, j: (b, 0)),
            pl.BlockSpec((LANES, 2048), lambda b, j: (tail0 + b // META_PER_BLOCK, 0)),
            pl.BlockSpec((1024, seq), lambda b, j: (0, b)),
            pl.BlockSpec((1024, LANES), lambda b, j: (0, tail0 + b // META_PER_BLOCK)),
        ],
        out_specs=pl.BlockSpec((tq, 1024), qmap),
        out_shape=jax.ShapeDtypeStruct((n_real, 1024), BF16),
        scratch_shapes=[pltpu.VMEM((1024, tq), F32)],
        compiler_params=_params(("parallel", "arbitrary")),
        name="mla_attn",
    )(q, k, k, vt, vt)


def _mla_meta_kernel(q_ref, kr_ref, kt_ref, vtr_ref, vtt_ref, o_ref, *, batch):
    i = pl.program_id(0)
    ncol = MLA_HEADS * N_META

    @pl.when(i < batch)
    def _():
        qi = lax.broadcasted_iota(I32, (N_META, ncol), 0)
        ci = lax.broadcasted_iota(I32, (N_META, ncol), 1)
        spread = jnp.where((ci & (N_META - 1)) == qi, 1.0, 0.0).astype(BF16)
        w = lax.dot_general(q_ref[...], spread, (((0,), (0,)), ((), ())),
                            preferred_element_type=F32)
        wr = lax.broadcasted_iota(I32, w.shape, 0)
        wc = lax.broadcasted_iota(I32, w.shape, 1)
        on_diag = jnp.right_shift(wr, _log2(LANES)) == jnp.right_shift(wc, _log2(N_META))
        w = jnp.where(on_diag, w, 0.0).astype(BF16)
        r = lax.broadcasted_iota(I32, (LANES, 1), 0)
        mine = jnp.right_shift(r, _log2(N_META)) == (i % META_PER_BLOCK)
        mbias = jnp.where(mine, 0.0, -jnp.inf).astype(F32)
        s_r = _dot(kr_ref[...], w)
        s_m = _dot(kt_ref[...], w) + mbias
        m = jnp.maximum(_col_reduce(s_r, jnp.max), _col_reduce(s_m, jnp.max))
        p_r = jnp.exp2(s_r - m)
        p_m = jnp.exp2(s_m - m)
        den = _col_reduce(p_r, jnp.sum) + _col_reduce(p_m, jnp.sum)
        ot = (_dot(vtr_ref[...], p_r.astype(BF16)) + _dot(vtt_ref[...], p_m.astype(BF16))) / den
        o_all = ot.T
        lane = lax.broadcasted_iota(I32, (N_META, 1024), 1)
        o = jnp.zeros((N_META, 1024), F32)
        for h in range(MLA_HEADS):
            rows = o_all[h * N_META:(h + 1) * N_META, :]
            o = jnp.where(jnp.right_shift(lane, _log2(MLA_V)) == h, rows, o)
        o_ref[...] = o.astype(BF16)

    @pl.when(i >= batch)
    def _():
        o_ref[...] = jnp.zeros_like(o_ref)


def _mla_meta_call(q, k, vt, *, batch, seq, n_real):
    ntot = q.shape[0]
    mrow = n_real // N_META
    tail0 = n_real // LANES
    ntail_blocks = (ntot - n_real) // N_META
    last = (batch - 1) // META_PER_BLOCK
    bmap = lambda i: jnp.minimum(i, batch - 1)
    tmap = lambda i: jnp.minimum(i // META_PER_BLOCK, last)
    return pl.pallas_call(
        functools.partial(_mla_meta_kernel, batch=batch),
        grid=(ntail_blocks,),
        in_specs=[
            pl.BlockSpec((N_META, 2048), lambda i: (mrow + i, 0)),
            pl.BlockSpec((seq, 2048), lambda i: (bmap(i), 0)),
            pl.BlockSpec((LANES, 2048), lambda i: (tail0 + tmap(i), 0)),
            pl.BlockSpec((1024, seq), lambda i: (0, bmap(i))),
            pl.BlockSpec((1024, LANES), lambda i: (0, tail0 + tmap(i))),
        ],
        out_specs=pl.BlockSpec((N_META, 1024), lambda i: (i, 0)),
        out_shape=jax.ShapeDtypeStruct((ntot - n_real, 1024), BF16),
        compiler_params=_params(("arbitrary",)),
        name="mla_attn_meta",
    )(q, k, k, vt, vt)


def _prep_gqa(a_wqkv, a_wo):
    d = a_wqkv.shape[0]
    nq = GQA_HEADS * GQA_HD
    nkv = GQA_KV * GQA_HD
    wq = a_wqkv[:, :nq] * (GQA_HD ** -0.5)

    def dup(w):
        w = w.reshape(d, GQA_KV, 1, GQA_HD)
        return jnp.broadcast_to(w, (d, GQA_KV, 2, GQA_HD)).reshape(d, 2 * nkv)

    w = jnp.concatenate([wq, dup(a_wqkv[:, nq:nq + nkv]), dup(a_wqkv[:, nq + nkv:])], axis=1)
    return w.astype(BF16), a_wo.astype(BF16)


def _prep_mla(b_wdkv, b_wuq, b_wukv):
    d = b_wdkv.shape[0]
    lat = MLA_Q_RANK + MLA_KV_RANK
    z = lambda n: jnp.zeros((d, n), F32)
    wd = jnp.concatenate([b_wdkv[:, :lat], z(MLA_NOPE), b_wdkv[:, lat:], z(LANES - MLA_QK)], axis=1)
    wuq = b_wuq.reshape(MLA_Q_RANK, MLA_HEADS, MLA_QK)
    wuq = jnp.pad(wuq, ((0, 0), (0, 0), (0, LANES - MLA_QK))).reshape(MLA_Q_RANK, MLA_HEADS * LANES)
    wukv = b_wukv.reshape(MLA_KV_RANK, MLA_HEADS, MLA_NOPE + MLA_V)
    wuk = jnp.pad(wukv[:, :, :MLA_NOPE], ((0, 0), (0, 0), (0, LANES - MLA_NOPE)))
    wuk = wuk.reshape(MLA_KV_RANK, MLA_HEADS * LANES)
    wuv = wukv[:, :, MLA_NOPE:].reshape(MLA_KV_RANK, MLA_HEADS * MLA_V)
    return wd.astype(BF16), wuq.astype(BF16), wuk.astype(BF16), wuv.astype(BF16)


def _prep_router(m_router):
    hi = m_router.astype(BF16)
    lo = (m_router - hi.astype(F32)).astype(BF16)
    pad = lambda r: jnp.pad(r, ((0, 0), (0, LANES - N_EXPERTS)))
    return jnp.concatenate([pad(hi), pad(lo)], axis=1)


def _positions(seq, n_tail, n_meta_rows):
    real = N_META + jnp.arange(seq, dtype=F32)
    t = jnp.arange(n_tail)
    tail = jnp.where(t < n_meta_rows, t % N_META, 0).astype(F32)
    return jnp.concatenate([real, tail])


def _trunk_flat(x_groups, meta, norm_mix, norm_ffn, norm_final,
                a_wqkv, a_wo, a_sink,
                b_wdkv, b_qnorm, b_kvnorm, b_wuq, b_wukv, b_wo,
                f_wg, f_wu, f_wd,
                m_router, m_wg, m_wu, m_wd):
    seq = x_groups[0].shape[1]
    batches = [x.shape[0] for x in x_groups]
    batch = sum(batches)
    n_real = batch * seq
    n_meta_rows = batch * N_META
    big = _pick_tile(seq, (1024, 512, 256, 128))
    tp = min(big, 512)
    ntot = -(-(n_real + n_meta_rows) // big) * big
    n_tail = ntot - n_real

    tail = jnp.concatenate([
        jnp.broadcast_to(meta.astype(F32)[None], (batch, N_META, D_MODEL)).reshape(n_meta_rows, D_MODEL),
        jnp.zeros((n_tail - n_meta_rows, D_MODEL), F32)], axis=0)
    h_sources = [x.reshape(-1, D_MODEL) for x in x_groups] + [tail]

    pos = _positions(seq, n_tail, n_meta_rows)
    tabs_a = _rope_lane_tables(pos, GQA_THETA, GQA_ROT, GQA_HD, 0)
    tabs_b = _rope_lane_tables(pos, MLA_THETA, MLA_ROPE, LANES, MLA_NOPE)
    row = lambda v: v.reshape(1, -1).astype(F32)

    wqkv, wo_a = _prep_gqa(a_wqkv[0], a_wo[0])
    q, k, v = _qkv_call(h_sources, row(norm_mix[0]), wqkv, tabs_a, tm=tp, n_real=n_real, seq=seq)
    sink = a_sink[0].astype(F32) * LOG2E
    o = _win_attn_call(sink, q, k, v, batch=batch, seq=seq, n_real=n_real)
    o_tail = _win_meta_call(sink, q, k, v, batch=batch, seq=seq, n_real=n_real)
    h, xn = _wo_call(o, o_tail, wo_a, h_sources, row(norm_ffn[0]), tm=big)
    fdim = f_wg.shape[2]
    tf = _pick_tile(fdim, (512, 256, 128))
    h, xn = _ffn_dense_call(xn, f_wg[0].astype(BF16), f_wu[0].astype(BF16), f_wd[0].astype(BF16),
                            h, row(norm_mix[1]), tm=tp, tf=tf)

    wd, wuq, wuk, wuv = _prep_mla(b_wdkv[0], b_wuq[0], b_wukv[0])
    q, k, vt = _mla_proj_call(xn, wd, row(b_qnorm[0]), row(b_kvnorm[0]), wuq, wuk, wuv, tabs_b,
                              tm=tp, n_real=n_real, seq=seq)
    tq = _pick_tile(seq, (256, 128))
    o = _mla_attn_call(q, k, vt, batch=batch, seq=seq, n_real=n_real, tq=tq)
    o_tail = _mla_meta_call(q, k, vt, batch=batch, seq=seq, n_real=n_real)
    tr = tp
    h, xpk, route, counts = _wo_router_call(o, o_tail, b_wo[0].astype(BF16), h, row(norm_ffn[1]),
                                            _prep_router(m_router[0]), tm=tr)

    tmx = big
    counts = counts[0, :N_EXPERTS].astype(I32)
    padded = ((counts + tmx - 1) // tmx) * tmx
    pend = jnp.cumsum(padded)
    pstart = pend - padded
    e_idx = route[:, 0:2].astype(I32)
    rank = route[:, 2:4].astype(I32)
    dest = pstart[e_idx] + rank
    n_assign = 2 * ntot
    n_blocks = -(-n_assign // tmx) + N_EXPERTS
    rows = n_blocks * tmx
    block_e = jnp.clip(jnp.searchsorted(pend, jnp.arange(n_blocks, dtype=I32) * tmx, side="right"),
                       0, N_EXPERTS - 1).astype(I32)
    n_used = (pend[-1] // tmx).astype(I32).reshape(1)
    block_e = jnp.where(jnp.arange(n_blocks) < n_used[0], block_e, block_e[jnp.maximum(n_used[0] - 1, 0)])

    td = tp
    dest_t = dest.reshape(ntot // td, td, 2).transpose(0, 2, 1)
    xs = _dispatch_call(dest_t, xpk, jnp.zeros((rows * XPK_ROWS, LANES), U32), td=td)
    edim = m_wg.shape[3]
    tfx = _pick_tile(edim, (512, 256, 128))
    ys = _ffn_expert_call(block_e, n_used, xs, m_wg[0].astype(BF16), m_wu[0].astype(BF16),
                          m_wd[0].astype(BF16), tm=tmx, tf=tfx)

    outs = []
    tile0 = 0
    for b, x in zip(batches, x_groups):
        ntiles = b * seq // td
        y = _combine_call(dest_t, ys, h, route, row(norm_final), tc=td, tile0=tile0, ntiles=ntiles)
        outs.append(y.reshape(b, seq, D_MODEL))
        tile0 += ntiles
    return tuple(outs)


def kernel(x_prompt, x_sample, meta, norm_mix, norm_ffn, norm_final, a_wqkv, a_wo, a_sink, b_wdkv, b_qnorm, b_kvnorm, b_wuq, b_wukv, b_wo, f_wg, f_wu, f_wd, m_router, m_wg, m_wu, m_wd):
    assert x_prompt.shape[1] == x_sample.shape[1] and x_prompt.shape[1] % BLK == 0
    assert norm_mix.shape[0] == 2, "two layers: windowed GQA + dense FFN, then MLA + MoE"
    return _trunk_flat((x_prompt, x_sample), meta, norm_mix, norm_ffn, norm_final,
                       a_wqkv, a_wo, a_sink,
                       b_wdkv, b_qnorm, b_kvnorm, b_wuq, b_wukv, b_wo,
                       f_wg, f_wu, f_wd,
                       m_router, m_wg, m_wu, m_wd)
```

```python
import functools
import math

import jax
import jax.numpy as jnp
from jax import lax
from jax.experimental import pallas as pl
from jax.experimental.pallas import tpu as pltpu

F32 = jnp.float32
BF16 = jnp.bfloat16
U32 = jnp.uint32
I32 = jnp.int32

D_MODEL = 1024
N_META = 16
RMS_EPS = 1e-6
BLK = 128
GQA_HEADS = 16
GQA_KV = 4
GQA_GROUP = 4
GQA_HD = 64
GQA_ROT = 16
GQA_THETA = 500000.0
MLA_HEADS = 16
MLA_NOPE = 64
MLA_ROPE = 32
MLA_V = 64
MLA_Q_RANK = 384
MLA_KV_RANK = 256
MLA_THETA = 10000.0
MLA_QK = MLA_NOPE + MLA_ROPE
N_EXPERTS = 8
LOG2E = 1.4426950408889634

LANES = 128
VMEM_LIMIT = 56 * 1024 * 1024


def _params(sem, vmem=VMEM_LIMIT):
    return pltpu.CompilerParams(dimension_semantics=sem, vmem_limit_bytes=vmem)


def _rms(x, g):
    ms = jnp.mean(x * x, axis=-1, keepdims=True)
    return x * lax.rsqrt(ms + RMS_EPS) * g


def _dot(a, b):
    return jnp.dot(a, b, preferred_element_type=F32)


def _dot_nt(a, b):
    return lax.dot_general(a, b, (((1,), (1,)), ((), ())), preferred_element_type=F32)


def _log2(n):
    assert n & (n - 1) == 0, n
    return n.bit_length() - 1


def _pick_tile(n, candidates):
    for c in candidates:
        if n % c == 0:
            return c
    raise ValueError(f"no tile in {candidates} divides {n}")


def _rope_lane_tables(pos, theta, rot, period, offset):
    half = rot // 2
    inv = jnp.power(jnp.float32(theta), -(jnp.arange(0, rot, 2, dtype=F32) / rot))
    ang = pos[:, None] * inv[None, :]
    cos, sin = jnp.cos(ang), jnp.sin(ang)
    lane = jnp.arange(LANES)
    r = (lane % period) - offset
    is_x1 = (r >= 0) & (r < half)
    is_x2 = (r >= half) & (r < rot)
    f = jnp.clip(jnp.where(is_x2, r - half, r), 0, half - 1)
    cos_l = cos[:, f]
    sin_l = sin[:, f]
    c = jnp.where((is_x1 | is_x2)[None, :], cos_l, 1.0)
    s1 = jnp.where(is_x2[None, :], sin_l, 0.0)
    s2 = jnp.where(is_x1[None, :], -sin_l, 0.0)
    return c.astype(F32), s1.astype(F32), s2.astype(F32)


def _rope(x, c, s1, s2, half):
    return x * c + pltpu.roll(x, half, 1) * s1 + pltpu.roll(x, LANES - half, 1) * s2


def _source_specs(sources, tm):
    specs, start = [], 0
    for src in sources:
        n = src.shape[0] // tm
        specs.append(pl.BlockSpec((tm, src.shape[1]),
                                  lambda i, start=start, n=n: (jnp.clip(i - start, 0, n - 1), 0)))
        start += n
    return specs


def _select_source(refs, src_rows, tm):
    i = pl.program_id(0)
    ends, acc = [], 0
    for n in src_rows:
        acc += n // tm
        ends.append(acc)
    x = refs[-1][...]
    for r, end in reversed(list(zip(refs[:-1], ends[:-1]))):
        x = jnp.where(i < end, r[...], x)
    return x


def _qkv_kernel(*refs, n_src, src_rows, tm):
    h_refs = refs[:n_src]
    g_ref, w_ref, c_ref, s1_ref, s2_ref, q_ref, k_ref, v_ref = refs[n_src:]
    xn = _rms(_select_source(h_refs, src_rows, tm), g_ref[...]).astype(BF16)
    qkv = _dot(xn, w_ref[...])
    c, s1, s2 = c_ref[...], s1_ref[...], s2_ref[...]
    nq = GQA_HEADS * GQA_HD // LANES
    nk = GQA_KV
    for i in range(nq):
        q_ref[:, i * LANES:(i + 1) * LANES] = (_rope(
            qkv[:, i * LANES:(i + 1) * LANES], c, s1, s2, GQA_ROT // 2) * LOG2E).astype(BF16)
    for i in range(nk):
        lo = (nq + i) * LANES
        k_ref[:, i * LANES:(i + 1) * LANES] = _rope(
            qkv[:, lo:lo + LANES], c, s1, s2, GQA_ROT // 2).astype(BF16)
    v_ref[...] = qkv[:, (nq + nk) * LANES:].astype(BF16)


def _table_index(i, n_real_tiles, tiles_per_seq):
    return jnp.where(i < n_real_tiles, i % tiles_per_seq, tiles_per_seq + i - n_real_tiles)


def _qkv_call(h_sources, g, w, tabs, *, tm, n_real, seq):
    src_rows = tuple(s.shape[0] for s in h_sources)
    ntot = sum(src_rows)
    nrt, tps = n_real // tm, seq // tm
    tab_spec = pl.BlockSpec((tm, LANES), lambda i: (_table_index(i, nrt, tps), 0))
    return pl.pallas_call(
        functools.partial(_qkv_kernel, n_src=len(h_sources), src_rows=src_rows, tm=tm),
        grid=(ntot // tm,),
        in_specs=_source_specs(h_sources, tm) + [
            pl.BlockSpec((1, D_MODEL), lambda i: (0, 0)),
            pl.BlockSpec(w.shape, lambda i: (0, 0)),
            tab_spec, tab_spec, tab_spec,
        ],
        out_specs=[
            pl.BlockSpec((tm, 1024), lambda i: (i, 0)),
            pl.BlockSpec((tm, GQA_KV * LANES), lambda i: (i, 0)),
            pl.BlockSpec((tm, GQA_KV * LANES), lambda i: (i, 0)),
        ],
        out_shape=[
            jax.ShapeDtypeStruct((ntot, 1024), BF16),
            jax.ShapeDtypeStruct((ntot, GQA_KV * LANES), BF16),
            jax.ShapeDtypeStruct((ntot, GQA_KV * LANES), BF16),
        ],
        compiler_params=_params(("parallel",)),
        name="qkv_rope",
    )(*h_sources, g, w, *tabs)


def _gqa_attend(sink_ref, q_ref, o_ref, groups):
    def pad_of(bias, pieces):
        have = sum(p.shape[0] for p in pieces)
        return [jnp.zeros((bias.shape[1] - have, LANES), BF16)] if bias.shape[1] > have else []

    def scores(g, kv):
        rows, k_pieces, _, bias = groups[g]
        sl = slice(kv * LANES, (kv + 1) * LANES)
        kcat = jnp.concatenate([p[:, sl] for p in k_pieces] + pad_of(bias, k_pieces), axis=0)
        zero = jnp.zeros_like(kcat)
        lane_k = lax.broadcasted_iota(I32, kcat.shape, 1)
        lo = slice(2 * kv * LANES, (2 * kv + 1) * LANES)
        hi = slice((2 * kv + 1) * LANES, (2 * kv + 2) * LANES)
        qp = jnp.concatenate([q_ref[rows, lo], q_ref[rows, hi]], axis=0)
        return (_dot_nt(qp, jnp.where(lane_k < GQA_HD, kcat, zero)) + bias,
                _dot_nt(qp, jnp.where(lane_k >= GQA_HD, kcat, zero)) + bias)

    def softmax(g, kv, ss):
        nq = groups[g][3].shape[0] // 2
        row = lax.broadcasted_iota(I32, (2 * nq, 1), 0)
        out = []
        for par, s in enumerate(ss):
            head = kv * GQA_GROUP + par
            sink = jnp.where(row < nq, sink_ref[head], sink_ref[head + 2])
            m = jnp.maximum(jnp.max(s, axis=-1, keepdims=True), sink)
            p = jnp.exp2(s - m)
            den = jnp.sum(p, axis=-1, keepdims=True) + jnp.exp2(sink - m)
            out.append((p.astype(BF16), den))
        return out

    def weighted_values(g, kv, pd):
        rows, _, v_pieces, bias = groups[g]
        nq = bias.shape[0] // 2
        sl = slice(kv * LANES, (kv + 1) * LANES)
        vcat = jnp.concatenate([p[:, sl] for p in v_pieces] + pad_of(bias, v_pieces), axis=0)
        outs = [_dot(p, vcat) / den for p, den in pd]
        lane_o = lax.broadcasted_iota(I32, outs[0].shape, 1)
        o = jnp.where(lane_o < GQA_HD, outs[0], outs[1]).astype(BF16)
        o_ref[rows, 2 * kv * LANES:(2 * kv + 1) * LANES] = o[:nq]
        o_ref[rows, (2 * kv + 1) * LANES:(2 * kv + 2) * LANES] = o[nq:]

    units = [(g, kv) for g in range(len(groups)) for kv in range(GQA_KV)]
    nxt = scores(*units[0])
    pending = None
    for idx, unit in enumerate(units):
        ss = nxt
        if idx + 1 < len(units):
            nxt = scores(*units[idx + 1])
        pd = softmax(*unit, ss)
        if pending is not None:
            weighted_values(*pending)
        pending = (*unit, pd)
    weighted_values(*pending)


def _window_bias(prev_ok, next_ok):
    nkeys = 4 * BLK
    qi = lax.broadcasted_iota(I32, (2 * BLK, nkeys), 0) & (BLK - 1)
    c = lax.broadcasted_iota(I32, (2 * BLK, nkeys), 1)
    neg = jnp.float32(-jnp.inf)
    b_p = jnp.where((c >= qi) & prev_ok, 0.0, neg)
    b_n = jnp.where(((c - 2 * BLK) <= qi) & next_ok, 0.0, neg)
    b_m = jnp.where(c < 3 * BLK + N_META, 0.0, neg)
    return jnp.where(c < BLK, b_p, jnp.where(c < 2 * BLK, 0.0, jnp.where(c < 3 * BLK, b_n, b_m)))


def _win_attn_kernel(sink_ref, q_ref, *refs, nb, qb):
    k_blk, km_ref = refs[:qb + 2], refs[qb + 2]
    v_blk, vm_ref = refs[qb + 3:2 * qb + 5], refs[2 * qb + 5]
    o_ref = refs[2 * qb + 6]
    j0 = pl.program_id(1) * qb
    groups = []
    for t in range(qb):
        bias = _window_bias(j0 + t > 0, j0 + t < nb - 1)
        groups.append((slice(t * BLK, (t + 1) * BLK),
                       (k_blk[t], k_blk[t + 1], k_blk[t + 2], km_ref),
                       (v_blk[t], v_blk[t + 1], v_blk[t + 2], vm_ref), bias))
    _gqa_attend(sink_ref, q_ref, o_ref, groups)


def _win_attn_call(sink, q, k, v, *, batch, seq, n_real):
    nb = seq // BLK
    qb = _pick_tile(nb, (4, 2, 1))
    mrow = n_real // N_META
    kvw = GQA_KV * LANES

    def qmap(b, j, s):
        return (b * (nb // qb) + j, 0)

    def blk_spec(t):
        return pl.BlockSpec((BLK, kvw),
                            lambda b, j, s: (b * nb + jnp.clip(j * qb + t - 1, 0, nb - 1), 0))

    meta_spec = pl.BlockSpec((N_META, kvw), lambda b, j, s: (mrow + b, 0))
    kv_specs = [blk_spec(t) for t in range(qb + 2)] + [meta_spec]
    grid_spec = pltpu.PrefetchScalarGridSpec(
        num_scalar_prefetch=1,
        grid=(batch, nb // qb),
        in_specs=[pl.BlockSpec((qb * BLK, 1024), qmap)] + kv_specs + kv_specs,
        out_specs=pl.BlockSpec((qb * BLK, 1024), qmap),
    )
    n_kv = qb + 3
    return pl.pallas_call(
        functools.partial(_win_attn_kernel, nb=nb, qb=qb),
        grid_spec=grid_spec,
        out_shape=jax.ShapeDtypeStruct((n_real, 1024), BF16),
        compiler_params=_params(("parallel", "parallel")),
        name="win_attn",
    )(sink, q, *([k] * n_kv), *([v] * n_kv))


def _win_meta_kernel(sink_ref, q_ref, km_ref, kc_ref, vm_ref, vc_ref, o_ref, *, batch):
    i = pl.program_id(0)

    @pl.when(i < batch)
    def _():
        nkeys = 2 * BLK
        qp = lax.broadcasted_iota(I32, (2 * N_META, nkeys), 0) & (N_META - 1)
        c = lax.broadcasted_iota(I32, (2 * N_META, nkeys), 1)
        neg = jnp.float32(-jnp.inf)
        b_c = jnp.where((N_META + c - qp) <= BLK, 0.0, neg)
        bias = jnp.where(c < BLK, b_c, jnp.where(c < BLK + N_META, 0.0, neg))
        _gqa_attend(sink_ref, q_ref, o_ref,
                    [(slice(0, N_META), (kc_ref, km_ref), (vc_ref, vm_ref), bias)])

    @pl.when(i >= batch)
    def _():
        o_ref[...] = jnp.zeros_like(o_ref)


def _win_meta_call(sink, q, k, v, *, batch, seq, n_real):
    ntot = q.shape[0]
    nb = seq // BLK
    mrow = n_real // N_META
    ntail_blocks = (ntot - n_real) // N_META

    def tmap(i, s):
        return (mrow + i, 0)

    def cmap(i, s):
        return (jnp.minimum(i, batch - 1) * nb, 0)

    grid_spec = pltpu.PrefetchScalarGridSpec(
        num_scalar_prefetch=1,
        grid=(ntail_blocks,),
        in_specs=[
            pl.BlockSpec((N_META, 1024), tmap),
            pl.BlockSpec((N_META, GQA_KV * LANES), tmap),
            pl.BlockSpec((BLK, GQA_KV * LANES), cmap),
            pl.BlockSpec((N_META, GQA_KV * LANES), tmap),
            pl.BlockSpec((BLK, GQA_KV * LANES), cmap),
        ],
        out_specs=pl.BlockSpec((N_META, 1024), lambda i, s: (i, 0)),
    )
    return pl.pallas_call(
        functools.partial(_win_meta_kernel, batch=batch),
        grid_spec=grid_spec,
        out_shape=jax.ShapeDtypeStruct((ntot - n_real, 1024), BF16),
        compiler_params=_params(("parallel",)),
        name="win_attn_meta",
    )(sink, q, k, k, v, v)


def _attn_out_tile(o_real_ref, o_tail_ref, n_real_tiles):
    is_real = pl.program_id(0) < n_real_tiles
    return jnp.where(is_real, o_real_ref[...], o_tail_ref[...])


def _attn_out_specs(tm, n_real_tiles):
    return [
        pl.BlockSpec((tm, 1024), lambda i: (jnp.minimum(i, n_real_tiles - 1), 0)),
        pl.BlockSpec((tm, 1024), lambda i: (jnp.maximum(i - n_real_tiles, 0), 0)),
    ]


def _wo_kernel(*refs, n_real_tiles, n_src, src_rows, tm):
    o_real_ref, o_tail_ref, w_ref = refs[:3]
    h_refs = refs[3:3 + n_src]
    g_ref, hout_ref, xn_ref = refs[3 + n_src:]
    o = _attn_out_tile(o_real_ref, o_tail_ref, n_real_tiles)
    hn = _dot(o, w_ref[...]) + _select_source(h_refs, src_rows, tm)
    hout_ref[...] = hn
    xn_ref[...] = _rms(hn, g_ref[...]).astype(BF16)


def _wo_call(o_real, o_tail, w, h_sources, g, *, tm):
    src_rows = tuple(s.shape[0] for s in h_sources)
    ntot = sum(src_rows)
    nrt = o_real.shape[0] // tm
    row = lambda i: (i, 0)
    fix = lambda i: (0, 0)
    return pl.pallas_call(
        functools.partial(_wo_kernel, n_real_tiles=nrt, n_src=len(h_sources), src_rows=src_rows, tm=tm),
        grid=(ntot // tm,),
        in_specs=_attn_out_specs(tm, nrt) + [pl.BlockSpec((1024, D_MODEL), fix)]
        + _source_specs(h_sources, tm) + [
            pl.BlockSpec((1, D_MODEL), fix),
        ],
        out_specs=[pl.BlockSpec((tm, D_MODEL), row), pl.BlockSpec((tm, D_MODEL), row)],
        out_shape=[jax.ShapeDtypeStruct((ntot, D_MODEL), F32),
                   jax.ShapeDtypeStruct((ntot, D_MODEL), BF16)],
        compiler_params=_params(("parallel",)),
        name="wo_res_norm",
    )(o_real, o_tail, w, *h_sources, g)


XPK_ROWS = D_MODEL // 2 // LANES
YS_ROWS = D_MODEL // LANES


ROW_SPLIT = 2


def _store_token_major(ref, x, row0=0):
    t, n = x.shape
    k = n // LANES
    for s in range(k):
        ref[pl.ds(row0 * k + s, t, stride=k), :] = x[:, s * LANES:(s + 1) * LANES]


def _load_token_major(ref, k):
    t = ref.shape[0] // k
    return [ref[pl.ds(s, t, stride=k), :] for s in range(k)]


def _pack_bf16_pair(a, b):
    ra = pltpu.bitcast(a.astype(BF16).astype(F32), U32)
    rb = pltpu.bitcast(b.astype(BF16).astype(F32), U32)
    return ra | (rb >> 16)


def _unpack_bf16_pair(p):
    a = pltpu.bitcast(p & jnp.uint32(0xFFFF0000), F32).astype(BF16)
    b = pltpu.bitcast(p << 16, F32).astype(BF16)
    return a, b


def _wo_router_kernel(o_real_ref, o_tail_ref, w_ref, h_ref, g_ref, r_ref,
                      hout_ref, xpk_ref, route_ref, cnt_ref, carry_ref, *, n_real_tiles):
    i = pl.program_id(0)
    tm = h_ref.shape[0]

    @pl.when(i == 0)
    def _():
        carry_ref[...] = jnp.zeros_like(carry_ref)

    o = _attn_out_tile(o_real_ref, o_tail_ref, n_real_tiles)
    th = tm // ROW_SPLIT
    parts = [slice(p * th, (p + 1) * th) for p in range(ROW_SPLIT)]
    half = D_MODEL // 2
    r2 = r_ref[...]
    lane = lax.broadcasted_iota(I32, (th, LANES), 1)
    rr = lax.broadcasted_iota(I32, (th, th), 0)
    cc = lax.broadcasted_iota(I32, (th, th), 1)
    lower = jnp.where(cc < rr, 1.0, 0.0).astype(BF16)

    ys = []
    for rs in parts:
        hn = _dot(o[rs], w_ref[...]) + h_ref[rs, :]
        hout_ref[rs, :] = hn
        ys.append(_rms(hn, g_ref[...]))
    logits = []
    for p, y in enumerate(ys):
        _store_token_major(xpk_ref, _pack_bf16_pair(y[:, :half], y[:, half:]), row0=p * th)
        yhi = y.astype(BF16)
        ylo = (y - yhi.astype(F32)).astype(BF16)
        t = _dot(yhi, r2)
        logits.append(t[:, :LANES] + t[:, LANES:] + _dot(ylo, r2[:, :LANES]))
    carry = carry_ref[0:1, :]
    for rs, lg in zip(parts, logits):
        lm = jnp.where(lane < N_EXPERTS, lg, -jnp.inf)
        m1 = jnp.max(lm, axis=-1, keepdims=True)
        i1 = jnp.min(jnp.where(lm == m1, lane, LANES), axis=-1, keepdims=True)
        lm2 = jnp.where(lane == i1, -jnp.inf, lm)
        m2 = jnp.max(lm2, axis=-1, keepdims=True)
        i2 = jnp.min(jnp.where(lm2 == m2, lane, LANES), axis=-1, keepdims=True)
        e21 = jnp.exp(m2 - m1)
        g0 = 1.0 / (1.0 + e21)
        g1 = e21 / (1.0 + e21)
        onehot = ((lane == i1) | (lane == i2)).astype(F32)
        before = _dot(lower, onehot.astype(BF16)) + carry
        rank1 = jnp.sum(jnp.where(lane == i1, before, 0.0), axis=-1, keepdims=True)
        rank2 = jnp.sum(jnp.where(lane == i2, before, 0.0), axis=-1, keepdims=True)
        carry = carry + jnp.sum(onehot, axis=0, keepdims=True)
        out = jnp.where(lane == 0, i1.astype(F32), 0.0)
        out = jnp.where(lane == 1, i2.astype(F32), out)
        out = jnp.where(lane == 2, rank1, out)
        out = jnp.where(lane == 3, rank2, out)
        out = jnp.where(lane == 4, g0, out)
        out = jnp.where(lane == 5, g1, out)
        route_ref[rs, :] = out
    carry_ref[...] = jnp.broadcast_to(carry, carry_ref.shape)
    cnt_ref[...] = jnp.broadcast_to(carry, cnt_ref.shape)


def _wo_router_call(o_real, o_tail, w, h, g, r2, *, tm):
    ntot = h.shape[0]
    nrt = o_real.shape[0] // tm
    row = lambda i: (i, 0)
    fix = lambda i: (0, 0)
    return pl.pallas_call(
        functools.partial(_wo_router_kernel, n_real_tiles=nrt),
        grid=(ntot // tm,),
        in_specs=_attn_out_specs(tm, nrt) + [
            pl.BlockSpec((1024, D_MODEL), fix),
            pl.BlockSpec((tm, D_MODEL), row),
            pl.BlockSpec((1, D_MODEL), fix),
            pl.BlockSpec((D_MODEL, 2 * LANES), fix),
        ],
        out_specs=[
            pl.BlockSpec((tm, D_MODEL), row),
            pl.BlockSpec((tm * XPK_ROWS, LANES), row),
            pl.BlockSpec((tm, LANES), row),
            pl.BlockSpec((8, LANES), fix),
        ],
        out_shape=[
            jax.ShapeDtypeStruct((ntot, D_MODEL), F32),
            jax.ShapeDtypeStruct((ntot * XPK_ROWS, LANES), U32),
            jax.ShapeDtypeStruct((ntot, LANES), F32),
            jax.ShapeDtypeStruct((8, LANES), F32),
        ],
        scratch_shapes=[pltpu.VMEM((8, LANES), F32)],
        compiler_params=_params(("arbitrary",)),
        name="wo_res_norm_router",
    )(o_real, o_tail, w, h, g, r2)


def _swiglu_chunk(x, wg, wu, wd):
    g = _dot(x, wg)
    u = _dot(x, wu)
    a = (g * (1.0 / (1.0 + jnp.exp(-g)))) * u
    return _dot(a.astype(BF16), wd)


def _ffn_dense_kernel(x_ref, wg_ref, wu_ref, wd_ref, h_ref, g_ref, hout_ref, xn_ref, *, tf):
    x = x_ref[...]
    hn = h_ref[...]
    for c in range(wg_ref.shape[1] // tf):
        cs = slice(c * tf, (c + 1) * tf)
        hn = hn + _swiglu_chunk(x, wg_ref[:, cs], wu_ref[:, cs], wd_ref[cs, :])
    hout_ref[...] = hn
    xn_ref[...] = _rms(hn, g_ref[...]).astype(BF16)


def _ffn_dense_call(x, wg, wu, wd, h, g, *, tm, tf):
    ntot = x.shape[0]
    fdim = wg.shape[1]
    row = lambda i: (i, 0)
    fix = lambda i: (0, 0)
    once = pl.Buffered(1)
    return pl.pallas_call(
        functools.partial(_ffn_dense_kernel, tf=tf),
        grid=(ntot // tm,),
        in_specs=[
            pl.BlockSpec((tm, D_MODEL), row),
            pl.BlockSpec((D_MODEL, fdim), fix, pipeline_mode=once),
            pl.BlockSpec((D_MODEL, fdim), fix, pipeline_mode=once),
            pl.BlockSpec((fdim, D_MODEL), fix, pipeline_mode=once),
            pl.BlockSpec((tm, D_MODEL), row),
            pl.BlockSpec((1, D_MODEL), fix),
        ],
        out_specs=[pl.BlockSpec((tm, D_MODEL), row), pl.BlockSpec((tm, D_MODEL), row)],
        out_shape=[jax.ShapeDtypeStruct((ntot, D_MODEL), F32),
                   jax.ShapeDtypeStruct((ntot, D_MODEL), BF16)],
        compiler_params=_params(("parallel",)),
        name="ffn_dense",
    )(x, wg, wu, wd, h, g)


def _ffn_expert_kernel(be_ref, nu_ref, xs_ref, wg_ref, wu_ref, wd_ref, ys_ref, *, tf):
    del be_ref
    used = pl.program_id(0) < nu_ref[0]

    @pl.when(used)
    def _():
        halves = [_unpack_bf16_pair(blk) for blk in _load_token_major(xs_ref, XPK_ROWS)]
        x = jnp.concatenate([a for a, _ in halves] + [b for _, b in halves], axis=1)
        acc = None
        for c in range(wg_ref.shape[1] // tf):
            cs = slice(c * tf, (c + 1) * tf)
            y = _swiglu_chunk(x, wg_ref[:, cs], wu_ref[:, cs], wd_ref[cs, :])
            acc = y if acc is None else acc + y
        _store_token_major(ys_ref, acc)

    @pl.when(jnp.logical_not(used))
    def _():
        ys_ref[...] = jnp.zeros_like(ys_ref)


def _ffn_expert_call(block_e, n_used, xs, wg, wu, wd, *, tm, tf):
    rows = xs.shape[0] // XPK_ROWS
    once = pl.Buffered(1)
    wmap = lambda i, be, nu: (be[i], 0, 0)
    grid_spec = pltpu.PrefetchScalarGridSpec(
        num_scalar_prefetch=2,
        grid=(rows // tm,),
        in_specs=[
            pl.BlockSpec((tm * XPK_ROWS, LANES), lambda i, be, nu: (i, 0)),
            pl.BlockSpec((None,) + wg.shape[1:], wmap, pipeline_mode=once),
            pl.BlockSpec((None,) + wu.shape[1:], wmap, pipeline_mode=once),
            pl.BlockSpec((None,) + wd.shape[1:], wmap, pipeline_mode=once),
        ],
        out_specs=pl.BlockSpec((tm * YS_ROWS, LANES), lambda i, be, nu: (i, 0)),
    )
    return pl.pallas_call(
        functools.partial(_ffn_expert_kernel, tf=tf),
        grid_spec=grid_spec,
        out_shape=jax.ShapeDtypeStruct((rows * YS_ROWS, LANES), F32),
        compiler_params=_params(("arbitrary",)),
        name="ffn_expert",
    )(block_e, n_used, xs, wg, wu, wd)


DMA_UNROLL = 8


def _token_copy(src, src_tok, dst, dst_tok, rows, sem):
    return pltpu.make_async_copy(src.at[pl.ds(pl.multiple_of(src_tok * rows, rows), rows)],
                                 dst.at[pl.ds(pl.multiple_of(dst_tok * rows, rows), rows)], sem)


def _dispatch_kernel(dest_ref, x_ref, xs_in_ref, xs_ref, sem):
    del xs_in_ref
    td = x_ref.shape[0] // XPK_ROWS

    def issue(t, carry):
        for s in range(2):
            _token_copy(x_ref, t, xs_ref, dest_ref[0, s, t], XPK_ROWS, sem).start(priority=s)
        return carry

    lax.fori_loop(0, td, issue, 0, unroll=DMA_UNROLL)

    def drain(t, carry):
        for s in range(2):
            _token_copy(x_ref, 0, xs_ref, 0, XPK_ROWS, sem).wait()
        return carry

    lax.fori_loop(0, td, drain, 0, unroll=DMA_UNROLL)


def _dispatch_call(dest, xpk, xs_init, *, td):
    ntot = xpk.shape[0] // XPK_ROWS
    return pl.pallas_call(
        _dispatch_kernel,
        grid=(ntot // td,),
        in_specs=[
            pl.BlockSpec((1, 2, td), lambda i: (i, 0, 0), memory_space=pltpu.SMEM),
            pl.BlockSpec((td * XPK_ROWS, LANES), lambda i: (i, 0)),
            pl.BlockSpec(memory_space=pl.ANY),
        ],
        out_specs=pl.BlockSpec(memory_space=pl.ANY),
        out_shape=jax.ShapeDtypeStruct(xs_init.shape, U32),
        input_output_aliases={2: 0},
        scratch_shapes=[pltpu.SemaphoreType.DMA(())],
        compiler_params=_params(("arbitrary",)),
        name="moe_dispatch",
    )(dest, xpk, xs_init)


def _combine_kernel(dest_ref, dest_next_ref, ys_ref, h_ref, route_ref, g_ref, out_ref,
                    y0_ref, y1_ref, sems):
    i = pl.program_id(0)
    n = pl.num_programs(0)
    tc = h_ref.shape[0]
    bufs = (y0_ref, y1_ref)
    slot = i % 2

    def issue(d_ref, sl):
        def body(t, carry):
            for s in range(2):
                _token_copy(ys_ref, d_ref[0, s, t], bufs[s].at[sl], t, YS_ROWS,
                            sems.at[sl]).start(priority=s)
            return carry

        lax.fori_loop(0, tc, body, 0, unroll=DMA_UNROLL)

    @pl.when(i == 0)
    def _():
        issue(dest_ref, 0)

    @pl.when(i + 1 < n)
    def _():
        issue(dest_next_ref, 1 - slot)

    def drain(t, carry):
        for s in range(2):
            _token_copy(ys_ref, 0, bufs[s].at[slot], 0, YS_ROWS, sems.at[slot]).wait()
        return carry

    lax.fori_loop(0, tc, drain, 0, unroll=DMA_UNROLL)

    r = route_ref[...]
    lane = lax.broadcasted_iota(I32, r.shape, 1)
    g0 = jnp.sum(jnp.where(lane == 4, r, 0.0), axis=-1, keepdims=True)
    g1 = jnp.sum(jnp.where(lane == 5, r, 0.0), axis=-1, keepdims=True)
    y0 = _load_token_major(y0_ref.at[slot], YS_ROWS)
    y1 = _load_token_major(y1_ref.at[slot], YS_ROWS)
    moe = jnp.concatenate([a * g0 + b * g1 for a, b in zip(y0, y1)], axis=1)
    out_ref[...] = _rms(h_ref[...] + moe, g_ref[...])


def _combine_call(dest, ys, h, route, g, *, tc, tile0, ntiles):
    return pl.pallas_call(
        _combine_kernel,
        grid=(ntiles,),
        in_specs=[
            pl.BlockSpec((1, 2, tc), lambda i: (tile0 + i, 0, 0), memory_space=pltpu.SMEM),
            pl.BlockSpec((1, 2, tc), lambda i: (tile0 + jnp.minimum(i + 1, ntiles - 1), 0, 0),
                         memory_space=pltpu.SMEM),
            pl.BlockSpec(memory_space=pl.ANY),
            pl.BlockSpec((tc, D_MODEL), lambda i: (tile0 + i, 0)),
            pl.BlockSpec((tc, LANES), lambda i: (tile0 + i, 0)),
            pl.BlockSpec((1, D_MODEL), lambda i: (0, 0)),
        ],
        out_specs=pl.BlockSpec((tc, D_MODEL), lambda i: (i, 0)),
        out_shape=jax.ShapeDtypeStruct((ntiles * tc, D_MODEL), F32),
        scratch_shapes=[pltpu.VMEM((2, tc * YS_ROWS, LANES), F32), pltpu.VMEM((2, tc * YS_ROWS, LANES), F32),
                        pltpu.SemaphoreType.DMA((2,))],
        compiler_params=_params(("arbitrary",)),
        name="moe_combine_norm",
    )(dest, dest, ys, h, route, g)


def _mla_proj_kernel(x_ref, wd_ref, qn_ref, kvn_ref, wuq_ref, wuk_ref, wuvt_ref,
                     c_ref, s1_ref, s2_ref, q_ref, k_ref, vt_ref):
    tm = x_ref.shape[0]
    th = tm // ROW_SPLIT
    parts = [slice(p * th, (p + 1) * th) for p in range(ROW_SPLIT)]
    half = MLA_ROPE // 2
    scale = (MLA_QK ** -0.5) * LOG2E
    lats = [_dot(x_ref[rs, :], wd_ref[...]) for rs in parts]
    ups = []
    for lat in lats:
        cq = _rms(lat[:, :MLA_Q_RANK], qn_ref[...]).astype(BF16)
        ckv = _rms(lat[:, MLA_Q_RANK:MLA_Q_RANK + MLA_KV_RANK], kvn_ref[...]).astype(BF16)
        ups.append((_dot(cq, wuq_ref[...]),
                    _dot(ckv, wuk_ref[...]),
                    _dot_nt(wuvt_ref[...], ckv)))
    for rs, lat, (q, k, vt) in zip(parts, lats, ups):
        c, s1, s2 = c_ref[rs, :], s1_ref[rs, :], s2_ref[rs, :]
        kr = _rope(lat[:, MLA_Q_RANK + MLA_KV_RANK:], c, s1, s2, half)
        for h in range(MLA_HEADS):
            sl = slice(h * LANES, (h + 1) * LANES)
            q_ref[rs, sl] = (_rope(q[:, sl], c, s1, s2, half) * scale).astype(BF16)
            k_ref[rs, sl] = (k[:, sl] + kr).astype(BF16)
        vt_ref[:, rs] = vt.astype(BF16)


def _mla_proj_call(x, wd, qn, kvn, wuq, wuk, wuv, tabs, *, tm, n_real, seq):
    ntot = x.shape[0]
    nrt, tps = n_real // tm, seq // tm
    row = lambda i: (i, 0)
    fix = lambda i: (0, 0)
    tab_spec = pl.BlockSpec((tm, LANES), lambda i: (_table_index(i, nrt, tps), 0))
    return pl.pallas_call(
        _mla_proj_kernel,
        grid=(ntot // tm,),
        in_specs=[
            pl.BlockSpec((tm, D_MODEL), row),
            pl.BlockSpec(wd.shape, fix),
            pl.BlockSpec((1, MLA_Q_RANK), fix),
            pl.BlockSpec((1, MLA_KV_RANK), fix),
            pl.BlockSpec(wuq.shape, fix),
            pl.BlockSpec(wuk.shape, fix),
            pl.BlockSpec(wuv.shape[::-1], fix),
            tab_spec, tab_spec, tab_spec,
        ],
        out_specs=[
            pl.BlockSpec((tm, 2048), row),
            pl.BlockSpec((tm, 2048), row),
            pl.BlockSpec((1024, tm), lambda i: (0, i)),
        ],
        out_shape=[
            jax.ShapeDtypeStruct((ntot, 2048), BF16),
            jax.ShapeDtypeStruct((ntot, 2048), BF16),
            jax.ShapeDtypeStruct((1024, ntot), BF16),
        ],
        compiler_params=_params(("parallel",)),
        name="mla_proj",
    )(x, wd, qn, kvn, wuq, wuk, wuv.T, *tabs)


META_PER_BLOCK = LANES // N_META
REDUCE_ROWS = 64
SCORES_AHEAD = 1
VALUES_BEHIND = 1


def _col_reduce(x, op):
    rows, n = x.shape
    if rows > REDUCE_ROWS and rows % REDUCE_ROWS == 0:
        x = op(x.reshape(rows // REDUCE_ROWS, REDUCE_ROWS, n), axis=0)
    return op(x, axis=0, keepdims=True)


def _mla_attn_kernel(q_ref, kr_ref, kt_ref, vtr_ref, vtt_ref, o_ref, ot_ref):
    b = pl.program_id(0)
    r = lax.broadcasted_iota(I32, (LANES, 1), 0)
    mine = jnp.right_shift(r, _log2(N_META)) == (b % META_PER_BLOCK)
    mbias = jnp.where(mine, 0.0, -jnp.inf).astype(F32)
    def scores(h):
        hs = slice(h * LANES, (h + 1) * LANES)
        qh = q_ref[:, hs]
        return _dot_nt(kr_ref[:, hs], qh), _dot_nt(kt_ref[:, hs], qh) + mbias

    def weighted_values(h, p_r, p_m, den):
        vs = slice(h * MLA_V, (h + 1) * MLA_V)
        acc = _dot(vtr_ref[vs, :], p_r) + _dot(vtt_ref[vs, :], p_m)
        ot_ref[vs, :] = acc / den

    ahead = [scores(h) for h in range(SCORES_AHEAD)]
    pending = []
    for h in range(MLA_HEADS):
        s_r, s_m = ahead.pop(0)
        if h + SCORES_AHEAD < MLA_HEADS:
            ahead.append(scores(h + SCORES_AHEAD))
        m = jnp.maximum(_col_reduce(s_r, jnp.max), _col_reduce(s_m, jnp.max))
        p_r = jnp.exp2(s_r - m)
        p_m = jnp.exp2(s_m - m)
        den = _col_reduce(p_r, jnp.sum) + _col_reduce(p_m, jnp.sum)
        pending.append((h, p_r.astype(BF16), p_m.astype(BF16), den))
        if len(pending) > VALUES_BEHIND:
            weighted_values(*pending.pop(0))
    for item in pending:
        weighted_values(*item)
    o_ref[...] = ot_ref[...].T.astype(BF16)


def _mla_attn_call(q, k, vt, *, batch, seq, n_real, tq):
    nq = seq // tq
    tail0 = n_real // LANES
    qmap = lambda b, j: (b * nq + j, 0)
    return pl.pallas_call(
        _mla_attn_kernel,
        grid=(batch, nq),
        in_specs=[
            pl.BlockSpec((tq, 2048), qmap),
            pl.BlockSpec((seq, 2048), lambda b, j: (b, 0)),
            pl.BlockSpec((LANES, 2048), lambda b, j: (tail0 + b // META_PER_BLOCK, 0)),
            pl.BlockSpec((1024, seq), lambda b, j: (0, b)),
            pl.BlockSpec((1024, LANES), lambda b, j: (0, tail0 + b // META_PER_BLOCK)),
        ],
        out_specs=pl.BlockSpec((tq, 1024), qmap),
        out_shape=jax.ShapeDtypeStruct((n_real, 1024), BF16),
        scratch_shapes=[pltpu.VMEM((1024, tq), F32)],
        compiler_params=_params(("parallel", "arbitrary")),
        name="mla_attn",
    )(q, k, k, vt, vt)


def _mla_meta_kernel(q_ref, kr_ref, kt_ref, vtr_ref, vtt_ref, o_ref, *, batch):
    i = pl.program_id(0)
    ncol = MLA_HEADS * N_META

    @pl.when(i < batch)
    def _():
        qi = lax.broadcasted_iota(I32, (N_META, ncol), 0)
        ci = lax.broadcasted_iota(I32, (N_META, ncol), 1)
        spread = jnp.where((ci & (N_META - 1)) == qi, 1.0, 0.0).astype(BF16)
        w = lax.dot_general(q_ref[...], spread, (((0,), (0,)), ((), ())),
                            preferred_element_type=F32)
        wr = lax.broadcasted_iota(I32, w.shape, 0)
        wc = lax.broadcasted_iota(I32, w.shape, 1)
        on_diag = jnp.right_shift(wr, _log2(LANES)) == jnp.right_shift(wc, _log2(N_META))
        w = jnp.where(on_diag, w, 0.0).astype(BF16)
        r = lax.broadcasted_iota(I32, (LANES, 1), 0)
        mine = jnp.right_shift(r, _log2(N_META)) == (i % META_PER_BLOCK)
        mbias = jnp.where(mine, 0.0, -jnp.inf).astype(F32)
        s_r = _dot(kr_ref[...], w)
        s_m = _dot(kt_ref[...], w) + mbias
        m = jnp.maximum(_col_reduce(s_r, jnp.max), _col_reduce(s_m, jnp.max))
        p_r = jnp.exp2(s_r - m)
        p_m = jnp.exp2(s_m - m)
        den = _col_reduce(p_r, jnp.sum) + _col_reduce(p_m, jnp.sum)
        ot = (_dot(vtr_ref[...], p_r.astype(BF16)) + _dot(vtt_ref[...], p_m.astype(BF16))) / den
        o_all = ot.T
        lane = lax.broadcasted_iota(I32, (N_META, 1024), 1)
        o = jnp.zeros((N_META, 1024), F32)
        for h in range(MLA_HEADS):
            rows = o_all[h * N_META:(h + 1) * N_META, :]
            o = jnp.where(jnp.right_shift(lane, _log2(MLA_V)) == h, rows, o)
        o_ref[...] = o.astype(BF16)

    @pl.when(i >= batch)
    def _():
        o_ref[...] = jnp.zeros_like(o_ref)


def _mla_meta_call(q, k, vt, *, batch, seq, n_real):
    ntot = q.shape[0]
    mrow = n_real // N_META
    tail0 = n_real // LANES
    ntail_blocks = (ntot - n_real) // N_META
    last = (batch - 1) // META_PER_BLOCK
    bmap = lambda i: jnp.minimum(i, batch - 1)
    tmap = lambda i: jnp.minimum(i // META_PER_BLOCK, last)
    return pl.pallas_call(
        functools.partial(_mla_meta_kernel, batch=batch),
        grid=(ntail_blocks,),
        in_specs=[
            pl.BlockSpec((N_META, 2048), lambda i: (mrow + i, 0)),
            pl.BlockSpec((seq, 2048), lambda i: (bmap(i), 0)),
            pl.BlockSpec((LANES, 2048), lambda i: (tail0 + tmap(i), 0)),
            pl.BlockSpec((1024, seq), lambda i: (0, bmap(i))),
            pl.BlockSpec((1024, LANES), lambda i: (0, tail0 + tmap(i))),
        ],
        out_specs=pl.BlockSpec((N_META, 1024), lambda i: (i, 0)),
        out_shape=jax.ShapeDtypeStruct((ntot - n_real, 1024), BF16),
        compiler_params=_params(("arbitrary",)),
        name="mla_attn_meta",
    )(q, k, k, vt, vt)


def _prep_gqa(a_wqkv, a_wo):
    d = a_wqkv.shape[0]
    nq = GQA_HEADS * GQA_HD
    nkv = GQA_KV * GQA_HD
    wq = a_wqkv[:, :nq] * (GQA_HD ** -0.5)

    def dup(w):
        w = w.reshape(d, GQA_KV, 1, GQA_HD)
        return jnp.broadcast_to(w, (d, GQA_KV, 2, GQA_HD)).reshape(d, 2 * nkv)

    w = jnp.concatenate([wq, dup(a_wqkv[:, nq:nq + nkv]), dup(a_wqkv[:, nq + nkv:])], axis=1)
    return w.astype(BF16), a_wo.astype(BF16)


def _prep_mla(b_wdkv, b_wuq, b_wukv):
    d = b_wdkv.shape[0]
    lat = MLA_Q_RANK + MLA_KV_RANK
    z = lambda n: jnp.zeros((d, n), F32)
    wd = jnp.concatenate([b_wdkv[:, :lat], z(MLA_NOPE), b_wdkv[:, lat:], z(LANES - MLA_QK)], axis=1)
    wuq = b_wuq.reshape(MLA_Q_RANK, MLA_HEADS, MLA_QK)
    wuq = jnp.pad(wuq, ((0, 0), (0, 0), (0, LANES - MLA_QK))).reshape(MLA_Q_RANK, MLA_HEADS * LANES)
    wukv = b_wukv.reshape(MLA_KV_RANK, MLA_HEADS, MLA_NOPE + MLA_V)
    wuk = jnp.pad(wukv[:, :, :MLA_NOPE], ((0, 0), (0, 0), (0, LANES - MLA_NOPE)))
    wuk = wuk.reshape(MLA_KV_RANK, MLA_HEADS * LANES)
    wuv = wukv[:, :, MLA_NOPE:].reshape(MLA_KV_RANK, MLA_HEADS * MLA_V)
    return wd.astype(BF16), wuq.astype(BF16), wuk.astype(BF16), wuv.astype(BF16)


def _prep_router(m_router):
    hi = m_router.astype(BF16)
    lo = (m_router - hi.astype(F32)).astype(BF16)
    pad = lambda r: jnp.pad(r, ((0, 0), (0, LANES - N_EXPERTS)))
    return jnp.concatenate([pad(hi), pad(lo)], axis=1)


def _positions(seq, n_tail, n_meta_rows):
    real = N_META + jnp.arange(seq, dtype=F32)
    t = jnp.arange(n_tail)
    tail = jnp.where(t < n_meta_rows, t % N_META, 0).astype(F32)
    return jnp.concatenate([real, tail])


def _trunk_flat(x_groups, meta, norm_mix, norm_ffn, norm_final,
                a_wqkv, a_wo, a_sink,
                b_wdkv, b_qnorm, b_kvnorm, b_wuq, b_wukv, b_wo,
                f_wg, f_wu, f_wd,
                m_router, m_wg, m_wu, m_wd):
    seq = x_groups[0].shape[1]
    batches = [x.shape[0] for x in x_groups]
    batch = sum(batches)
    n_real = batch * seq
    n_meta_rows = batch * N_META
    big = _pick_tile(seq, (1024, 512, 256, 128))
    tp = min(big, 512)
    ntot = -(-(n_real + n_meta_rows) // big) * big
    n_tail = ntot - n_real

    tail = jnp.concatenate([
        jnp.broadcast_to(meta.astype(F32)[None], (batch, N_META, D_MODEL)).reshape(n_meta_rows, D_MODEL),
        jnp.zeros((n_tail - n_meta_rows, D_MODEL), F32)], axis=0)
    h_sources = [x.reshape(-1, D_MODEL) for x in x_groups] + [tail]

    pos = _positions(seq, n_tail, n_meta_rows)
    tabs_a = _rope_lane_tables(pos, GQA_THETA, GQA_ROT, GQA_HD, 0)
    tabs_b = _rope_lane_tables(pos, MLA_THETA, MLA_ROPE, LANES, MLA_NOPE)
    row = lambda v: v.reshape(1, -1).astype(F32)

    wqkv, wo_a = _prep_gqa(a_wqkv[0], a_wo[0])
    q, k, v = _qkv_call(h_sources, row(norm_mix[0]), wqkv, tabs_a, tm=tp, n_real=n_real, seq=seq)
    sink = a_sink[0].astype(F32) * LOG2E
    o = _win_attn_call(sink, q, k, v, batch=batch, seq=seq, n_real=n_real)
    o_tail = _win_meta_call(sink, q, k, v, batch=batch, seq=seq, n_real=n_real)
    h, xn = _wo_call(o, o_tail, wo_a, h_sources, row(norm_ffn[0]), tm=big)
    fdim = f_wg.shape[2]
    tf = _pick_tile(fdim, (512, 256, 128))
    h, xn = _ffn_dense_call(xn, f_wg[0].astype(BF16), f_wu[0].astype(BF16), f_wd[0].astype(BF16),
                            h, row(norm_mix[1]), tm=tp, tf=tf)

    wd, wuq, wuk, wuv = _prep_mla(b_wdkv[0], b_wuq[0], b_wukv[0])
    q, k, vt = _mla_proj_call(xn, wd, row(b_qnorm[0]), row(b_kvnorm[0]), wuq, wuk, wuv, tabs_b,
                              tm=tp, n_real=n_real, seq=seq)
    tq = _pick_tile(seq, (256, 128))
    o = _mla_attn_call(q, k, vt, batch=batch, seq=seq, n_real=n_real, tq=tq)
    o_tail = _mla_meta_call(q, k, vt, batch=batch, seq=seq, n_real=n_real)
    tr = tp
    h, xpk, route, counts = _wo_router_call(o, o_tail, b_wo[0].astype(BF16), h, row(norm_ffn[1]),
                                            _prep_router(m_router[0]), tm=tr)

    tmx = big
    counts = counts[0, :N_EXPERTS].astype(I32)
    padded = ((counts + tmx - 1) // tmx) * tmx
    pend = jnp.cumsum(padded)
    pstart = pend - padded
    e_idx = route[:, 0:2].astype(I32)
    rank = route[:, 2:4].astype(I32)
    dest = pstart[e_idx] + rank
    n_assign = 2 * ntot
    n_blocks = -(-n_assign // tmx) + N_EXPERTS
    rows = n_blocks * tmx
    block_e = jnp.clip(jnp.searchsorted(pend, jnp.arange(n_blocks, dtype=I32) * tmx, side="right"),
                       0, N_EXPERTS - 1).astype(I32)
    n_used = (pend[-1] // tmx).astype(I32).reshape(1)
    block_e = jnp.where(jnp.arange(n_blocks) < n_used[0], block_e, block_e[jnp.maximum(n_used[0] - 1, 0)])

    td = tp
    dest_t = dest.reshape(ntot // td, td, 2).transpose(0, 2, 1)
    xs = _dispatch_call(dest_t, xpk, jnp.zeros((rows * XPK_ROWS, LANES), U32), td=td)
    edim = m_wg.shape[3]
    tfx = _pick_tile(edim, (512, 256, 128))
    ys = _ffn_expert_call(block_e, n_used, xs, m_wg[0].astype(BF16), m_wu[0].astype(BF16),
                          m_wd[0].astype(BF16), tm=tmx, tf=tfx)

    outs = []
    tile0 = 0
    for b, x in zip(batches, x_groups):
        ntiles = b * seq // td
        y = _combine_call(dest_t, ys, h, route, row(norm_final), tc=td, tile0=tile0, ntiles=ntiles)
        outs.append(y.reshape(b, seq, D_MODEL))
        tile0 += ntiles
    return tuple(outs)


def kernel(x_prompt, x_sample, meta, norm_mix, norm_ffn, norm_final, a_wqkv, a_wo, a_sink, b_wdkv, b_qnorm, b_kvnorm, b_wuq, b_wukv, b_wo, f_wg, f_wu, f_wd, m_router, m_wg, m_wu, m_wd):
    assert x_prompt.shape[1] == x_sample.shape[1] and x_prompt.shape[1] % BLK == 0
    assert norm_mix.shape[0] == 2, "two layers: windowed GQA + dense FFN, then MLA + MoE"
    return _trunk_flat((x_prompt, x_sample), meta, norm_mix, norm_ffn, norm_final,
                       a_wqkv, a_wo, a_sink,
                       b_wdkv, b_qnorm, b_kvnorm, b_wuq, b_wukv, b_wo,
                       f_wg, f_wu, f_wd,
                       m_router, m_wg, m_wu, m_wd)
```

```python
import functools
import math

import jax
import jax.numpy as jnp
from jax import lax
from jax.experimental import pallas as pl
from jax.experimental.pallas import tpu as pltpu

F32 = jnp.float32
BF16 = jnp.bfloat16
U32 = jnp.uint32
I32 = jnp.int32

D_MODEL = 1024
N_META = 16
RMS_EPS = 1e-6
BLK = 128
GQA_HEADS = 16
GQA_KV = 4
GQA_GROUP = 4
GQA_HD = 64
GQA_ROT = 16
GQA_THETA = 500000.0
MLA_HEADS = 16
MLA_NOPE = 64
MLA_ROPE = 32
MLA_V = 64
MLA_Q_RANK = 384
MLA_KV_RANK = 256
MLA_THETA = 10000.0
MLA_QK = MLA_NOPE + MLA_ROPE
N_EXPERTS = 8
LOG2E = 1.4426950408889634

LANES = 128
VMEM_LIMIT = 56 * 1024 * 1024


def _params(sem, vmem=VMEM_LIMIT):
    return pltpu.CompilerParams(dimension_semantics=sem, vmem_limit_bytes=vmem)


def _rms(x, g):
    ms = jnp.mean(x * x, axis=-1, keepdims=True)
    return x * lax.rsqrt(ms + RMS_EPS) * g


def _dot(a, b):
    return jnp.dot(a, b, preferred_element_type=F32)


def _dot_nt(a, b):
    return lax.dot_general(a, b, (((1,), (1,)), ((), ())), preferred_element_type=F32)


def _log2(n):
    assert n & (n - 1) == 0, n
    return n.bit_length() - 1


def _pick_tile(n, candidates):
    for c in candidates:
        if n % c == 0:
            return c
    raise ValueError(f"no tile in {candidates} divides {n}")


def _rope_lane_tables(pos, theta, rot, period, offset):
    half = rot // 2
    inv = jnp.power(jnp.float32(theta), -(jnp.arange(0, rot, 2, dtype=F32) / rot))
    ang = pos[:, None] * inv[None, :]
    cos, sin = jnp.cos(ang), jnp.sin(ang)
    lane = jnp.arange(LANES)
    r = (lane % period) - offset
    is_x1 = (r >= 0) & (r < half)
    is_x2 = (r >= half) & (r < rot)
    f = jnp.clip(jnp.where(is_x2, r - half, r), 0, half - 1)
    cos_l = cos[:, f]
    sin_l = sin[:, f]
    c = jnp.where((is_x1 | is_x2)[None, :], cos_l, 1.0)
    s1 = jnp.where(is_x2[None, :], sin_l, 0.0)
    s2 = jnp.where(is_x1[None, :], -sin_l, 0.0)
    return c.astype(F32), s1.astype(F32), s2.astype(F32)


def _rope(x, c, s1, s2, half):
    return x * c + pltpu.roll(x, half, 1) * s1 + pltpu.roll(x, LANES - half, 1) * s2


def _source_specs(sources, tm):
    specs, start = [], 0
    for src in sources:
        n = src.shape[0] // tm
        specs.append(pl.BlockSpec((tm, src.shape[1]),
                                  lambda i, start=start, n=n: (jnp.clip(i - start, 0, n - 1), 0)))
        start += n
    return specs


def _select_source(refs, src_rows, tm):
    i = pl.program_id(0)
    ends, acc = [], 0
    for n in src_rows:
        acc += n // tm
        ends.append(acc)
    x = refs[-1][...]
    for r, end in reversed(list(zip(refs[:-1], ends[:-1]))):
        x = jnp.where(i < end, r[...], x)
    return x


def _qkv_kernel(*refs, n_src, src_rows, tm):
    h_refs = refs[:n_src]
    g_ref, w_ref, c_ref, s1_ref, s2_ref, q_ref, k_ref, v_ref = refs[n_src:]
    xn = _rms(_select_source(h_refs, src_rows, tm), g_ref[...]).astype(BF16)
    qkv = _dot(xn, w_ref[...])
    c, s1, s2 = c_ref[...], s1_ref[...], s2_ref[...]
    nq = GQA_HEADS * GQA_HD // LANES
    nk = GQA_KV
    for i in range(nq):
        q_ref[:, i * LANES:(i + 1) * LANES] = (_rope(
            qkv[:, i * LANES:(i + 1) * LANES], c, s1, s2, GQA_ROT // 2) * LOG2E).astype(BF16)
    for i in range(nk):
        lo = (nq + i) * LANES
        k_ref[:, i * LANES:(i + 1) * LANES] = _rope(
            qkv[:, lo:lo + LANES], c, s1, s2, GQA_ROT // 2).astype(BF16)
    v_ref[...] = qkv[:, (nq + nk) * LANES:].astype(BF16)


def _table_index(i, n_real_tiles, tiles_per_seq):
    return jnp.where(i < n_real_tiles, i % tiles_per_seq, tiles_per_seq + i - n_real_tiles)


def _qkv_call(h_sources, g, w, tabs, *, tm, n_real, seq):
    src_rows = tuple(s.shape[0] for s in h_sources)
    ntot = sum(src_rows)
    nrt, tps = n_real // tm, seq // tm
    tab_spec = pl.BlockSpec((tm, LANES), lambda i: (_table_index(i, nrt, tps), 0))
    return pl.pallas_call(
        functools.partial(_qkv_kernel, n_src=len(h_sources), src_rows=src_rows, tm=tm),
        grid=(ntot // tm,),
        in_specs=_source_specs(h_sources, tm) + [
            pl.BlockSpec((1, D_MODEL), lambda i: (0, 0)),
            pl.BlockSpec(w.shape, lambda i: (0, 0)),
            tab_spec, tab_spec, tab_spec,
        ],
        out_specs=[
            pl.BlockSpec((tm, 1024), lambda i: (i, 0)),
            pl.BlockSpec((tm, GQA_KV * LANES), lambda i: (i, 0)),
            pl.BlockSpec((tm, GQA_KV * LANES), lambda i: (i, 0)),
        ],
        out_shape=[
            jax.ShapeDtypeStruct((ntot, 1024), BF16),
            jax.ShapeDtypeStruct((ntot, GQA_KV * LANES), BF16),
            jax.ShapeDtypeStruct((ntot, GQA_KV * LANES), BF16),
        ],
        compiler_params=_params(("parallel",)),
        name="qkv_rope",
    )(*h_sources, g, w, *tabs)


def _gqa_attend(sink_ref, q_ref, o_ref, groups):
    def pad_of(bias, pieces):
        have = sum(p.shape[0] for p in pieces)
        return [jnp.zeros((bias.shape[1] - have, LANES), BF16)] if bias.shape[1] > have else []

    def scores(g, kv):
        rows, k_pieces, _, bias = groups[g]
        sl = slice(kv * LANES, (kv + 1) * LANES)
        kcat = jnp.concatenate([p[:, sl] for p in k_pieces] + pad_of(bias, k_pieces), axis=0)
        zero = jnp.zeros_like(kcat)
        lane_k = lax.broadcasted_iota(I32, kcat.shape, 1)
        lo = slice(2 * kv * LANES, (2 * kv + 1) * LANES)
        hi = slice((2 * kv + 1) * LANES, (2 * kv + 2) * LANES)
        qp = jnp.concatenate([q_ref[rows, lo], q_ref[rows, hi]], axis=0)
        return (_dot_nt(qp, jnp.where(lane_k < GQA_HD, kcat, zero)) + bias,
                _dot_nt(qp, jnp.where(lane_k >= GQA_HD, kcat, zero)) + bias)

    def softmax(g, kv, ss):
        nq = groups[g][3].shape[0] // 2
        row = lax.broadcasted_iota(I32, (2 * nq, 1), 0)
        out = []
        for par, s in enumerate(ss):
            head = kv * GQA_GROUP + par
            sink = jnp.where(row < nq, sink_ref[head], sink_ref[head + 2])
            m = jnp.maximum(jnp.max(s, axis=-1, keepdims=True), sink)
            p = jnp.exp2(s - m)
            den = jnp.sum(p, axis=-1, keepdims=True) + jnp.exp2(sink - m)
            out.append((p.astype(BF16), den))
        return out

    def weighted_values(g, kv, pd):
        rows, _, v_pieces, bias = groups[g]
        nq = bias.shape[0] // 2
        sl = slice(kv * LANES, (kv + 1) * LANES)
        vcat = jnp.concatenate([p[:, sl] for p in v_pieces] + pad_of(bias, v_pieces), axis=0)
        outs = [_dot(p, vcat) / den for p, den in pd]
        lane_o = lax.broadcasted_iota(I32, outs[0].shape, 1)
        o = jnp.where(lane_o < GQA_HD, outs[0], outs[1]).astype(BF16)
        o_ref[rows, 2 * kv * LANES:(2 * kv + 1) * LANES] = o[:nq]
        o_ref[rows, (2 * kv + 1) * LANES:(2 * kv + 2) * LANES] = o[nq:]

    units = [(g, kv) for g in range(len(groups)) for kv in range(GQA_KV)]
    nxt = scores(*units[0])
    pending = None
    for idx, unit in enumerate(units):
        ss = nxt
        if idx + 1 < len(units):
            nxt = scores(*units[idx + 1])
        pd = softmax(*unit, ss)
        if pending is not None:
            weighted_values(*pending)
        pending = (*unit, pd)
    weighted_values(*pending)


def _window_bias(prev_ok, next_ok):
    nkeys = 4 * BLK
    qi = lax.broadcasted_iota(I32, (2 * BLK, nkeys), 0) & (BLK - 1)
    c = lax.broadcasted_iota(I32, (2 * BLK, nkeys), 1)
    neg = jnp.float32(-jnp.inf)
    b_p = jnp.where((c >= qi) & prev_ok, 0.0, neg)
    b_n = jnp.where(((c - 2 * BLK) <= qi) & next_ok, 0.0, neg)
    b_m = jnp.where(c < 3 * BLK + N_META, 0.0, neg)
    return jnp.where(c < BLK, b_p, jnp.where(c < 2 * BLK, 0.0, jnp.where(c < 3 * BLK, b_n, b_m)))


def _win_attn_kernel(sink_ref, q_ref, *refs, nb, qb):
    k_blk, km_ref = refs[:qb + 2], refs[qb + 2]
    v_blk, vm_ref = refs[qb + 3:2 * qb + 5], refs[2 * qb + 5]
    o_ref = refs[2 * qb + 6]
    j0 = pl.program_id(1) * qb
    groups = []
    for t in range(qb):
        bias = _window_bias(j0 + t > 0, j0 + t < nb - 1)
        groups.append((slice(t * BLK, (t + 1) * BLK),
                       (k_blk[t], k_blk[t + 1], k_blk[t + 2], km_ref),
                       (v_blk[t], v_blk[t + 1], v_blk[t + 2], vm_ref), bias))
    _gqa_attend(sink_ref, q_ref, o_ref, groups)


def _win_attn_call(sink, q, k, v, *, batch, seq, n_real):
    nb = seq // BLK
    qb = _pick_tile(nb, (4, 2, 1))
    mrow = n_real // N_META
    kvw = GQA_KV * LANES

    def qmap(b, j, s):
        return (b * (nb // qb) + j, 0)

    def blk_spec(t):
        return pl.BlockSpec((BLK, kvw),
                            lambda b, j, s: (b * nb + jnp.clip(j * qb + t - 1, 0, nb - 1), 0))

    meta_spec = pl.BlockSpec((N_META, kvw), lambda b, j, s: (mrow + b, 0))
    kv_specs = [blk_spec(t) for t in range(qb + 2)] + [meta_spec]
    grid_spec = pltpu.PrefetchScalarGridSpec(
        num_scalar_prefetch=1,
        grid=(batch, nb // qb),
        in_specs=[pl.BlockSpec((qb * BLK, 1024), qmap)] + kv_specs + kv_specs,
        out_specs=pl.BlockSpec((qb * BLK, 1024), qmap),
    )
    n_kv = qb + 3
    return pl.pallas_call(
        functools.partial(_win_attn_kernel, nb=nb, qb=qb),
        grid_spec=grid_spec,
        out_shape=jax.ShapeDtypeStruct((n_real, 1024), BF16),
        compiler_params=_params(("parallel", "parallel")),
        name="win_attn",
    )(sink, q, *([k] * n_kv), *([v] * n_kv))


def _win_meta_kernel(sink_ref, q_ref, km_ref, kc_ref, vm_ref, vc_ref, o_ref, *, batch):
    i = pl.program_id(0)

    @pl.when(i < batch)
    def _():
        nkeys = 2 * BLK
        qp = lax.broadcasted_iota(I32, (2 * N_META, nkeys), 0) & (N_META - 1)
        c = lax.broadcasted_iota(I32, (2 * N_META, nkeys), 1)
        neg = jnp.float32(-jnp.inf)
        b_c = jnp.where((N_META + c - qp) <= BLK, 0.0, neg)
        bias = jnp.where(c < BLK, b_c, jnp.where(c < BLK + N_META, 0.0, neg))
        _gqa_attend(sink_ref, q_ref, o_ref,
                    [(slice(0, N_META), (kc_ref, km_ref), (vc_ref, vm_ref), bias)])

    @pl.when(i >= batch)
    def _():
        o_ref[...] = jnp.zeros_like(o_ref)


def _win_meta_call(sink, q, k, v, *, batch, seq, n_real):
    ntot = q.shape[0]
    nb = seq // BLK
    mrow = n_real // N_META
    ntail_blocks = (ntot - n_real) // N_META

    def tmap(i, s):
        return (mrow + i, 0)

    def cmap(i, s):
        return (jnp.minimum(i, batch - 1) * nb, 0)

    grid_spec = pltpu.PrefetchScalarGridSpec(
        num_scalar_prefetch=1,
        grid=(ntail_blocks,),
        in_specs=[
            pl.BlockSpec((N_META, 1024), tmap),
            pl.BlockSpec((N_META, GQA_KV * LANES), tmap),
            pl.BlockSpec((BLK, GQA_KV * LANES), cmap),
            pl.BlockSpec((N_META, GQA_KV * LANES), tmap),
            pl.BlockSpec((BLK, GQA_KV * LANES), cmap),
        ],
        out_specs=pl.BlockSpec((N_META, 1024), lambda i, s: (i, 0)),
    )
    return pl.pallas_call(
        functools.partial(_win_meta_kernel, batch=batch),
        grid_spec=grid_spec,
        out_shape=jax.ShapeDtypeStruct((ntot - n_real, 1024), BF16),
        compiler_params=_params(("parallel",)),
        name="win_attn_meta",
    )(sink, q, k, k, v, v)


def _attn_out_tile(o_real_ref, o_tail_ref, n_real_tiles):
    is_real = pl.program_id(0) < n_real_tiles
    return jnp.where(is_real, o_real_ref[...], o_tail_ref[...])


def _attn_out_specs(tm, n_real_tiles):
    return [
        pl.BlockSpec((tm, 1024), lambda i: (jnp.minimum(i, n_real_tiles - 1), 0)),
        pl.BlockSpec((tm, 1024), lambda i: (jnp.maximum(i - n_real_tiles, 0), 0)),
    ]


def _wo_kernel(*refs, n_real_tiles, n_src, src_rows, tm):
    o_real_ref, o_tail_ref, w_ref = refs[:3]
    h_refs = refs[3:3 + n_src]
    g_ref, hout_ref, xn_ref = refs[3 + n_src:]
    o = _attn_out_tile(o_real_ref, o_tail_ref, n_real_tiles)
    hn = _dot(o, w_ref[...]) + _select_source(h_refs, src_rows, tm)
    hout_ref[...] = hn
    xn_ref[...] = _rms(hn, g_ref[...]).astype(BF16)


def _wo_call(o_real, o_tail, w, h_sources, g, *, tm):
    src_rows = tuple(s.shape[0] for s in h_sources)
    ntot = sum(src_rows)
    nrt = o_real.shape[0] // tm
    row = lambda i: (i, 0)
    fix = lambda i: (0, 0)
    return pl.pallas_call(
        functools.partial(_wo_kernel, n_real_tiles=nrt, n_src=len(h_sources), src_rows=src_rows, tm=tm),
        grid=(ntot // tm,),
        in_specs=_attn_out_specs(tm, nrt) + [pl.BlockSpec((1024, D_MODEL), fix)]
        + _source_specs(h_sources, tm) + [
            pl.BlockSpec((1, D_MODEL), fix),
        ],
        out_specs=[pl.BlockSpec((tm, D_MODEL), row), pl.BlockSpec((tm, D_MODEL), row)],
        out_shape=[jax.ShapeDtypeStruct((ntot, D_MODEL), F32),
                   jax.ShapeDtypeStruct((ntot, D_MODEL), BF16)],
        compiler_params=_params(("parallel",)),
        name="wo_res_norm",
    )(o_real, o_tail, w, *h_sources, g)


XPK_ROWS = D_MODEL // 2 // LANES
YS_ROWS = D_MODEL // LANES


ROW_SPLIT = 2


def _store_token_major(ref, x, row0=0):
    t, n = x.shape
    k = n // LANES
    for s in range(k):
        ref[pl.ds(row0 * k + s, t, stride=k), :] = x[:, s * LANES:(s + 1) * LANES]


def _load_token_major(ref, k):
    t = ref.shape[0] // k
    return [ref[pl.ds(s, t, stride=k), :] for s in range(k)]


def _pack_bf16_pair(a, b):
    ra = pltpu.bitcast(a.astype(BF16).astype(F32), U32)
    rb = pltpu.bitcast(b.astype(BF16).astype(F32), U32)
    return ra | (rb >> 16)


def _unpack_bf16_pair(p):
    a = pltpu.bitcast(p & jnp.uint32(0xFFFF0000), F32).astype(BF16)
    b = pltpu.bitcast(p << 16, F32).astype(BF16)
    return a, b


def _wo_router_kernel(o_real_ref, o_tail_ref, w_ref, h_ref, g_ref, r_ref,
                      hout_ref, xpk_ref, route_ref, cnt_ref, carry_ref, *, n_real_tiles):
    i = pl.program_id(0)
    tm = h_ref.shape[0]

    @pl.when(i == 0)
    def _():
        carry_ref[...] = jnp.zeros_like(carry_ref)

    o = _attn_out_tile(o_real_ref, o_tail_ref, n_real_tiles)
    th = tm // ROW_SPLIT
    parts = [slice(p * th, (p + 1) * th) for p in range(ROW_SPLIT)]
    half = D_MODEL // 2
    r2 = r_ref[...]
    lane = lax.broadcasted_iota(I32, (th, LANES), 1)
    rr = lax.broadcasted_iota(I32, (th, th), 0)
    cc = lax.broadcasted_iota(I32, (th, th), 1)
    lower = jnp.where(cc < rr, 1.0, 0.0).astype(BF16)

    ys = []
    for rs in parts:
        hn = _dot(o[rs], w_ref[...]) + h_ref[rs, :]
        hout_ref[rs, :] = hn
        ys.append(_rms(hn, g_ref[...]))
    logits = []
    for p, y in enumerate(ys):
        _store_token_major(xpk_ref, _pack_bf16_pair(y[:, :half], y[:, half:]), row0=p * th)
        yhi = y.astype(BF16)
        ylo = (y - yhi.astype(F32)).astype(BF16)
        t = _dot(yhi, r2)
        logits.append(t[:, :LANES] + t[:, LANES:] + _dot(ylo, r2[:, :LANES]))
    carry = carry_ref[0:1, :]
    for rs, lg in zip(parts, logits):
        lm = jnp.where(lane < N_EXPERTS, lg, -jnp.inf)
        m1 = jnp.max(lm, axis=-1, keepdims=True)
        i1 = jnp.min(jnp.where(lm == m1, lane, LANES), axis=-1, keepdims=True)
        lm2 = jnp.where(lane == i1, -jnp.inf, lm)
        m2 = jnp.max(lm2, axis=-1, keepdims=True)
        i2 = jnp.min(jnp.where(lm2 == m2, lane, LANES), axis=-1, keepdims=True)
        e21 = jnp.exp(m2 - m1)
        g0 = 1.0 / (1.0 + e21)
        g1 = e21 / (1.0 + e21)
        onehot = ((lane == i1) | (lane == i2)).astype(F32)
        before = _dot(lower, onehot.astype(BF16)) + carry
        rank1 = jnp.sum(jnp.where(lane == i1, before, 0.0), axis=-1, keepdims=True)
        rank2 = jnp.sum(jnp.where(lane == i2, before, 0.0), axis=-1, keepdims=True)
        carry = carry + jnp.sum(onehot, axis=0, keepdims=True)
        out = jnp.where(lane == 0, i1.astype(F32), 0.0)
        out = jnp.where(lane == 1, i2.astype(F32), out)
        out = jnp.where(lane == 2, rank1, out)
        out = jnp.where(lane == 3, rank2, out)
        out = jnp.where(lane == 4, g0, out)
        out = jnp.where(lane == 5, g1, out)
        route_ref[rs, :] = out
    carry_ref[...] = jnp.broadcast_to(carry, carry_ref.shape)
    cnt_ref[...] = jnp.broadcast_to(carry, cnt_ref.shape)


def _wo_router_call(o_real, o_tail, w, h, g, r2, *, tm):
    ntot = h.shape[0]
    nrt = o_real.shape[0] // tm
    row = lambda i: (i, 0)
    fix = lambda i: (0, 0)
    return pl.pallas_call(
        functools.partial(_wo_router_kernel, n_real_tiles=nrt),
        grid=(ntot // tm,),
        in_specs=_attn_out_specs(tm, nrt) + [
            pl.BlockSpec((1024, D_MODEL), fix),
            pl.BlockSpec((tm, D_MODEL), row),
            pl.BlockSpec((1, D_MODEL), fix),
            pl.BlockSpec((D_MODEL, 2 * LANES), fix),
        ],
        out_specs=[
            pl.BlockSpec((tm, D_MODEL), row),
            pl.BlockSpec((tm * XPK_ROWS, LANES), row),
            pl.BlockSpec((tm, LANES), row),
            pl.BlockSpec((8, LANES), fix),
        ],
        out_shape=[
            jax.ShapeDtypeStruct((ntot, D_MODEL), F32),
            jax.ShapeDtypeStruct((ntot * XPK_ROWS, LANES), U32),
            jax.ShapeDtypeStruct((ntot, LANES), F32),
            jax.ShapeDtypeStruct((8, LANES), F32),
        ],
        scratch_shapes=[pltpu.VMEM((8, LANES), F32)],
        compiler_params=_params(("arbitrary",)),
        name="wo_res_norm_router",
    )(o_real, o_tail, w, h, g, r2)


def _swiglu_chunk(x, wg, wu, wd):
    g = _dot(x, wg)
    u = _dot(x, wu)
    a = (g * (1.0 / (1.0 + jnp.exp(-g)))) * u
    return _dot(a.astype(BF16), wd)


def _ffn_dense_kernel(x_ref, wg_ref, wu_ref, wd_ref, h_ref, g_ref, hout_ref, xn_ref, *, tf):
    x = x_ref[...]
    hn = h_ref[...]
    for c in range(wg_ref.shape[1] // tf):
        cs = slice(c * tf, (c + 1) * tf)
        hn = hn + _swiglu_chunk(x, wg_ref[:, cs], wu_ref[:, cs], wd_ref[cs, :])
    hout_ref[...] = hn
    xn_ref[...] = _rms(hn, g_ref[...]).astype(BF16)


def _ffn_dense_call(x, wg, wu, wd, h, g, *, tm, tf):
    ntot = x.shape[0]
    fdim = wg.shape[1]
    row = lambda i: (i, 0)
    fix = lambda i: (0, 0)
    once = pl.Buffered(1)
    return pl.pallas_call(
        functools.partial(_ffn_dense_kernel, tf=tf),
        grid=(ntot // tm,),
        in_specs=[
            pl.BlockSpec((tm, D_MODEL), row),
            pl.BlockSpec((D_MODEL, fdim), fix, pipeline_mode=once),
            pl.BlockSpec((D_MODEL, fdim), fix, pipeline_mode=once),
            pl.BlockSpec((fdim, D_MODEL), fix, pipeline_mode=once),
            pl.BlockSpec((tm, D_MODEL), row),
            pl.BlockSpec((1, D_MODEL), fix),
        ],
        out_specs=[pl.BlockSpec((tm, D_MODEL), row), pl.BlockSpec((tm, D_MODEL), row)],
        out_shape=[jax.ShapeDtypeStruct((ntot, D_MODEL), F32),
                   jax.ShapeDtypeStruct((ntot, D_MODEL), BF16)],
        compiler_params=_params(("parallel",)),
        name="ffn_dense",
    )(x, wg, wu, wd, h, g)


def _ffn_expert_kernel(be_ref, nu_ref, xs_ref, wg_ref, wu_ref, wd_ref, ys_ref, *, tf):
    del be_ref
    used = pl.program_id(0) < nu_ref[0]

    @pl.when(used)
    def _():
        halves = [_unpack_bf16_pair(blk) for blk in _load_token_major(xs_ref, XPK_ROWS)]
        x = jnp.concatenate([a for a, _ in halves] + [b for _, b in halves], axis=1)
        acc = None
        for c in range(wg_ref.shape[1] // tf):
            cs = slice(c * tf, (c + 1) * tf)
            y = _swiglu_chunk(x, wg_ref[:, cs], wu_ref[:, cs], wd_ref[cs, :])
            acc = y if acc is None else acc + y
        _store_token_major(ys_ref, acc)

    @pl.when(jnp.logical_not(used))
    def _():
        ys_ref[...] = jnp.zeros_like(ys_ref)


def _ffn_expert_call(block_e, n_used, xs, wg, wu, wd, *, tm, tf):
    rows = xs.shape[0] // XPK_ROWS
    once = pl.Buffered(1)
    wmap = lambda i, be, nu: (be[i], 0, 0)
    grid_spec = pltpu.PrefetchScalarGridSpec(
        num_scalar_prefetch=2,
        grid=(rows // tm,),
        in_specs=[
            pl.BlockSpec((tm * XPK_ROWS, LANES), lambda i, be, nu: (i, 0)),
            pl.BlockSpec((None,) + wg.shape[1:], wmap, pipeline_mode=once),
            pl.BlockSpec((None,) + wu.shape[1:], wmap, pipeline_mode=once),
            pl.BlockSpec((None,) + wd.shape[1:], wmap, pipeline_mode=once),
        ],
        out_specs=pl.BlockSpec((tm * YS_ROWS, LANES), lambda i, be, nu: (i, 0)),
    )
    return pl.pallas_call(
        functools.partial(_ffn_expert_kernel, tf=tf),
        grid_spec=grid_spec,
        out_shape=jax.ShapeDtypeStruct((rows * YS_ROWS, LANES), F32),
        compiler_params=_params(("arbitrary",)),
        name="ffn_expert",
    )(block_e, n_used, xs, wg, wu, wd)


DMA_UNROLL = 8


def _token_copy(src, src_tok, dst, dst_tok, rows, sem):
    return pltpu.make_async_copy(src.at[pl.ds(pl.multiple_of(src_tok * rows, rows), rows)],
                                 dst.at[pl.ds(pl.multiple_of(dst_tok * rows, rows), rows)], sem)


def _dispatch_kernel(dest_ref, x_ref, xs_in_ref, xs_ref, sem):
    del xs_in_ref
    td = x_ref.shape[0] // XPK_ROWS

    def issue(t, carry):
        for s in range(2):
            _token_copy(x_ref, t, xs_ref, dest_ref[0, s, t], XPK_ROWS, sem).start(priority=s)
        return carry

    lax.fori_loop(0, td, issue, 0, unroll=DMA_UNROLL)

    def drain(t, carry):
        for s in range(2):
            _token_copy(x_ref, 0, xs_ref, 0, XPK_ROWS, sem).wait()
        return carry

    lax.fori_loop(0, td, drain, 0, unroll=DMA_UNROLL)


def _dispatch_call(dest, xpk, xs_init, *, td):
    ntot = xpk.shape[0] // XPK_ROWS
    return pl.pallas_call(
        _dispatch_kernel,
        grid=(ntot // td,),
        in_specs=[
            pl.BlockSpec((1, 2, td), lambda i: (i, 0, 0), memory_space=pltpu.SMEM),
            pl.BlockSpec((td * XPK_ROWS, LANES), lambda i: (i, 0)),
            pl.BlockSpec(memory_space=pl.ANY),
        ],
        out_specs=pl.BlockSpec(memory_space=pl.ANY),
        out_shape=jax.ShapeDtypeStruct(xs_init.shape, U32),
        input_output_aliases={2: 0},
        scratch_shapes=[pltpu.SemaphoreType.DMA(())],
        compiler_params=_params(("arbitrary",)),
        name="moe_dispatch",
    )(dest, xpk, xs_init)


def _combine_kernel(dest_ref, dest_next_ref, ys_ref, h_ref, route_ref, g_ref, out_ref,
                    y0_ref, y1_ref, sems):
    i = pl.program_id(0)
    n = pl.num_programs(0)
    tc = h_ref.shape[0]
    bufs = (y0_ref, y1_ref)
    slot = i % 2

    def issue(d_ref, sl):
        def body(t, carry):
            for s in range(2):
                _token_copy(ys_ref, d_ref[0, s, t], bufs[s].at[sl], t, YS_ROWS,
                            sems.at[sl]).start(priority=s)
            return carry

        lax.fori_loop(0, tc, body, 0, unroll=DMA_UNROLL)

    @pl.when(i == 0)
    def _():
        issue(dest_ref, 0)

    @pl.when(i + 1 < n)
    def _():
        issue(dest_next_ref, 1 - slot)

    def drain(t, carry):
        for s in range(2):
            _token_copy(ys_ref, 0, bufs[s].at[slot], 0, YS_ROWS, sems.at[slot]).wait()
        return carry

    lax.fori_loop(0, tc, drain, 0, unroll=DMA_UNROLL)

    r = route_ref[...]
    lane = lax.broadcasted_iota(I32, r.shape, 1)
    g0 = jnp.sum(jnp.where(lane == 4, r, 0.0), axis=-1, keepdims=True)
    g1 = jnp.sum(jnp.where(lane == 5, r, 0.0), axis=-1, keepdims=True)
    y0 = _load_token_major(y0_ref.at[slot], YS_ROWS)
    y1 = _load_token_major(y1_ref.at[slot], YS_ROWS)
    moe = jnp.concatenate([a * g0 + b * g1 for a, b in zip(y0, y1)], axis=1)
    out_ref[...] = _rms(h_ref[...] + moe, g_ref[...])


def _combine_call(dest, ys, h, route, g, *, tc, tile0, ntiles):
    return pl.pallas_call(
        _combine_kernel,
        grid=(ntiles,),
        in_specs=[
            pl.BlockSpec((1, 2, tc), lambda i: (tile0 + i, 0, 0), memory_space=pltpu.SMEM),
            pl.BlockSpec((1, 2, tc), lambda i: (tile0 + jnp.minimum(i + 1, ntiles - 1), 0, 0),
                         memory_space=pltpu.SMEM),
            pl.BlockSpec(memory_space=pl.ANY),
            pl.BlockSpec((tc, D_MODEL), lambda i: (tile0 + i, 0)),
            pl.BlockSpec((tc, LANES), lambda i: (tile0 + i, 0)),
            pl.BlockSpec((1, D_MODEL), lambda i: (0, 0)),
        ],
        out_specs=pl.BlockSpec((tc, D_MODEL), lambda i: (i, 0)),
        out_shape=jax.ShapeDtypeStruct((ntiles * tc, D_MODEL), F32),
        scratch_shapes=[pltpu.VMEM((2, tc * YS_ROWS, LANES), F32), pltpu.VMEM((2, tc * YS_ROWS, LANES), F32),
                        pltpu.SemaphoreType.DMA((2,))],
        compiler_params=_params(("arbitrary",)),
        name="moe_combine_norm",
    )(dest, dest, ys, h, route, g)


def _mla_proj_kernel(x_ref, wd_ref, qn_ref, kvn_ref, wuq_ref, wuk_ref, wuvt_ref,
                     c_ref, s1_ref, s2_ref, q_ref, k_ref, vt_ref):
    tm = x_ref.shape[0]
    th = tm // ROW_SPLIT
    parts = [slice(p * th, (p + 1) * th) for p in range(ROW_SPLIT)]
    half = MLA_ROPE // 2
    scale = (MLA_QK ** -0.5) * LOG2E
    lats = [_dot(x_ref[rs, :], wd_ref[...]) for rs in parts]
    ups = []
    for lat in lats:
        cq = _rms(lat[:, :MLA_Q_RANK], qn_ref[...]).astype(BF16)
        ckv = _rms(lat[:, MLA_Q_RANK:MLA_Q_RANK + MLA_KV_RANK], kvn_ref[...]).astype(BF16)
        ups.append((_dot(cq, wuq_ref[...]),
                    _dot(ckv, wuk_ref[...]),
                    _dot_nt(wuvt_ref[...], ckv)))
    for rs, lat, (q, k, vt) in zip(parts, lats, ups):
        c, s1, s2 = c_ref[rs, :], s1_ref[rs, :], s2_ref[rs, :]
        kr = _rope(lat[:, MLA_Q_RANK + MLA_KV_RANK:], c, s1, s2, half)
        for h in range(MLA_HEADS):
            sl = slice(h * LANES, (h + 1) * LANES)
            q_ref[rs, sl] = (_rope(q[:, sl], c, s1, s2, half) * scale).astype(BF16)
            k_ref[rs, sl] = (k[:, sl] + kr).astype(BF16)
        vt_ref[:, rs] = vt.astype(BF16)


def _mla_proj_call(x, wd, qn, kvn, wuq, wuk, wuv, tabs, *, tm, n_real, seq):
    ntot = x.shape[0]
    nrt, tps = n_real // tm, seq // tm
    row = lambda i: (i, 0)
    fix = lambda i: (0, 0)
    tab_spec = pl.BlockSpec((tm, LANES), lambda i: (_table_index(i, nrt, tps), 0))
    return pl.pallas_call(
        _mla_proj_kernel,
        grid=(ntot // tm,),
        in_specs=[
            pl.BlockSpec((tm, D_MODEL), row),
            pl.BlockSpec(wd.shape, fix),
            pl.BlockSpec((1, MLA_Q_RANK), fix),
            pl.BlockSpec((1, MLA_KV_RANK), fix),
            pl.BlockSpec(wuq.shape, fix),
            pl.BlockSpec(wuk.shape, fix),
            pl.BlockSpec(wuv.shape[::-1], fix),
            tab_spec, tab_spec, tab_spec,
        ],
        out_specs=[
            pl.BlockSpec((tm, 2048), row),
            pl.BlockSpec((tm, 2048), row),
            pl.BlockSpec((1024, tm), lambda i: (0, i)),
        ],
        out_shape=[
            jax.ShapeDtypeStruct((ntot, 2048), BF16),
            jax.ShapeDtypeStruct((ntot, 2048), BF16),
            jax.ShapeDtypeStruct((1024, ntot), BF16),
        ],
        compiler_params=_params(("parallel",)),
        name="mla_proj",
    )(x, wd, qn, kvn, wuq, wuk, wuv.T, *tabs)


META_PER_BLOCK = LANES // N_META
REDUCE_ROWS = 64
SCORES_AHEAD = 1
VALUES_BEHIND = 1
MLA_QUERY_TILE = 256


def _col_reduce(x, op):
    rows, n = x.shape
    if rows > REDUCE_ROWS and rows % REDUCE_ROWS == 0:
        x = op(x.reshape(rows // REDUCE_ROWS, REDUCE_ROWS, n), axis=0)
    return op(x, axis=0, keepdims=True)


def _mla_attn_kernel(q_ref, kr_ref, kt_ref, vtr_ref, vtt_ref, o_ref, ot_ref):
    b = pl.program_id(0)
    r = lax.broadcasted_iota(I32, (LANES, 1), 0)
    mine = jnp.right_shift(r, _log2(N_META)) == (b % META_PER_BLOCK)
    mbias = jnp.where(mine, 0.0, -jnp.inf).astype(F32)
    tq = min(MLA_QUERY_TILE, q_ref.shape[0])
    units = [(g, h) for g in range(q_ref.shape[0] // tq) for h in range(MLA_HEADS)]

    def scores(g, h):
        hs = slice(h * LANES, (h + 1) * LANES)
        qh = q_ref[g * tq:(g + 1) * tq, hs]
        return _dot_nt(kr_ref[:, hs], qh), _dot_nt(kt_ref[:, hs], qh) + mbias

    def weighted_values(g, h, p_r, p_m, den):
        vs = slice(h * MLA_V, (h + 1) * MLA_V)
        acc = _dot(vtr_ref[vs, :], p_r) + _dot(vtt_ref[vs, :], p_m)
        ot_ref[vs, g * tq:(g + 1) * tq] = acc / den

    ahead = [scores(*u) for u in units[:SCORES_AHEAD]]
    pending = []
    for idx, unit in enumerate(units):
        s_r, s_m = ahead.pop(0)
        if idx + SCORES_AHEAD < len(units):
            ahead.append(scores(*units[idx + SCORES_AHEAD]))
        m = jnp.maximum(_col_reduce(s_r, jnp.max), _col_reduce(s_m, jnp.max))
        p_r = jnp.exp2(s_r - m)
        p_m = jnp.exp2(s_m - m)
        den = _col_reduce(p_r, jnp.sum) + _col_reduce(p_m, jnp.sum)
        pending.append((*unit, p_r.astype(BF16), p_m.astype(BF16), den))
        if len(pending) > VALUES_BEHIND:
            weighted_values(*pending.pop(0))
    for item in pending:
        weighted_values(*item)
    o_ref[...] = ot_ref[...].T.astype(BF16)


def _mla_attn_call(q, k, vt, *, batch, seq, n_real, tq):
    nq = seq // tq
    tail0 = n_real // LANES
    qmap = lambda b, j: (b * nq + j, 0)
    return pl.pallas_call(
        _mla_attn_kernel,
        grid=(batch, nq),
        in_specs=[
            pl.BlockSpec((tq, 2048), qmap),
            pl.BlockSpec((seq, 2048), lambda b, j: (b, 0)),
            pl.BlockSpec((LANES, 2048), lambda b, j: (tail0 + b // META_PER_BLOCK, 0)),
            pl.BlockSpec((1024, seq), lambda b, j: (0, b)),
            pl.BlockSpec((1024, LANES), lambda b, j: (0, tail0 + b // META_PER_BLOCK)),
        ],
        out_specs=pl.BlockSpec((tq, 1024), qmap),
        out_shape=jax.ShapeDtypeStruct((n_real, 1024), BF16),
        scratch_shapes=[pltpu.VMEM((1024, tq), F32)],
        compiler_params=_params(("parallel", "arbitrary")),
        name="mla_attn",
    )(q, k, k, vt, vt)


def _mla_meta_kernel(q_ref, kr_ref, kt_ref, vtr_ref, vtt_ref, o_ref, *, batch):
    i = pl.program_id(0)
    ncol = MLA_HEADS * N_META

    @pl.when(i < batch)
    def _():
        qi = lax.broadcasted_iota(I32, (N_META, ncol), 0)
        ci = lax.broadcasted_iota(I32, (N_META, ncol), 1)
        spread = jnp.where((ci & (N_META - 1)) == qi, 1.0, 0.0).astype(BF16)
        w = lax.dot_general(q_ref[...], spread, (((0,), (0,)), ((), ())),
                            preferred_element_type=F32)
        wr = lax.broadcasted_iota(I32, w.shape, 0)
        wc = lax.broadcasted_iota(I32, w.shape, 1)
        on_diag = jnp.right_shift(wr, _log2(LANES)) == jnp.right_shift(wc, _log2(N_META))
        w = jnp.where(on_diag, w, 0.0).astype(BF16)
        r = lax.broadcasted_iota(I32, (LANES, 1), 0)
        mine = jnp.right_shift(r, _log2(N_META)) == (i % META_PER_BLOCK)
        mbias = jnp.where(mine, 0.0, -jnp.inf).astype(F32)
        s_r = _dot(kr_ref[...], w)
        s_m = _dot(kt_ref[...], w) + mbias
        m = jnp.maximum(_col_reduce(s_r, jnp.max), _col_reduce(s_m, jnp.max))
        p_r = jnp.exp2(s_r - m)
        p_m = jnp.exp2(s_m - m)
        den = _col_reduce(p_r, jnp.sum) + _col_reduce(p_m, jnp.sum)
        ot = (_dot(vtr_ref[...], p_r.astype(BF16)) + _dot(vtt_ref[...], p_m.astype(BF16))) / den
        o_all = ot.T
        lane = lax.broadcasted_iota(I32, (N_META, 1024), 1)
        o = jnp.zeros((N_META, 1024), F32)
        for h in range(MLA_HEADS):
            rows = o_all[h * N_META:(h + 1) * N_META, :]
            o = jnp.where(jnp.right_shift(lane, _log2(MLA_V)) == h, rows, o)
        o_ref[...] = o.astype(BF16)

    @pl.when(i >= batch)
    def _():
        o_ref[...] = jnp.zeros_like(o_ref)


def _mla_meta_call(q, k, vt, *, batch, seq, n_real):
    ntot = q.shape[0]
    mrow = n_real // N_META
    tail0 = n_real // LANES
    ntail_blocks = (ntot - n_real) // N_META
    last = (batch - 1) // META_PER_BLOCK
    bmap = lambda i: jnp.minimum(i, batch - 1)
    tmap = lambda i: jnp.minimum(i // META_PER_BLOCK, last)
    return pl.pallas_call(
        functools.partial(_mla_meta_kernel, batch=batch),
        grid=(ntail_blocks,),
        in_specs=[
            pl.BlockSpec((N_META, 2048), lambda i: (mrow + i, 0)),
            pl.BlockSpec((seq, 2048), lambda i: (bmap(i), 0)),
            pl.BlockSpec((LANES, 2048), lambda i: (tail0 + tmap(i), 0)),
            pl.BlockSpec((1024, seq), lambda i: (0, bmap(i))),
            pl.BlockSpec((1024, LANES), lambda i: (0, tail0 + tmap(i))),
        ],
        out_specs=pl.BlockSpec((N_META, 1024), lambda i: (i, 0)),
        out_shape=jax.ShapeDtypeStruct((ntot - n_real, 1024), BF16),
        compiler_params=_params(("arbitrary",)),
        name="mla_attn_meta",
    )(q, k, k, vt, vt)


def _prep_gqa(a_wqkv, a_wo):
    d = a_wqkv.shape[0]
    nq = GQA_HEADS * GQA_HD
    nkv = GQA_KV * GQA_HD
    wq = a_wqkv[:, :nq] * (GQA_HD ** -0.5)

    def dup(w):
        w = w.reshape(d, GQA_KV, 1, GQA_HD)
        return jnp.broadcast_to(w, (d, GQA_KV, 2, GQA_HD)).reshape(d, 2 * nkv)

    w = jnp.concatenate([wq, dup(a_wqkv[:, nq:nq + nkv]), dup(a_wqkv[:, nq + nkv:])], axis=1)
    return w.astype(BF16), a_wo.astype(BF16)


def _prep_mla(b_wdkv, b_wuq, b_wukv):
    d = b_wdkv.shape[0]
    lat = MLA_Q_RANK + MLA_KV_RANK
    z = lambda n: jnp.zeros((d, n), F32)
    wd = jnp.concatenate([b_wdkv[:, :lat], z(MLA_NOPE), b_wdkv[:, lat:], z(LANES - MLA_QK)], axis=1)
    wuq = b_wuq.reshape(MLA_Q_RANK, MLA_HEADS, MLA_QK)
    wuq = jnp.pad(wuq, ((0, 0), (0, 0), (0, LANES - MLA_QK))).reshape(MLA_Q_RANK, MLA_HEADS * LANES)
    wukv = b_wukv.reshape(MLA_KV_RANK, MLA_HEADS, MLA_NOPE + MLA_V)
    wuk = jnp.pad(wukv[:, :, :MLA_NOPE], ((0, 0), (0, 0), (0, LANES - MLA_NOPE)))
    wuk = wuk.reshape(MLA_KV_RANK, MLA_HEADS * LANES)
    wuv = wukv[:, :, MLA_NOPE:].reshape(MLA_KV_RANK, MLA_HEADS * MLA_V)
    return wd.astype(BF16), wuq.astype(BF16), wuk.astype(BF16), wuv.astype(BF16)


def _prep_router(m_router):
    hi = m_router.astype(BF16)
    lo = (m_router - hi.astype(F32)).astype(BF16)
    pad = lambda r: jnp.pad(r, ((0, 0), (0, LANES - N_EXPERTS)))
    return jnp.concatenate([pad(hi), pad(lo)], axis=1)


def _positions(seq, n_tail, n_meta_rows):
    real = N_META + jnp.arange(seq, dtype=F32)
    t = jnp.arange(n_tail)
    tail = jnp.where(t < n_meta_rows, t % N_META, 0).astype(F32)
    return jnp.concatenate([real, tail])


def _trunk_flat(x_groups, meta, norm_mix, norm_ffn, norm_final,
                a_wqkv, a_wo, a_sink,
                b_wdkv, b_qnorm, b_kvnorm, b_wuq, b_wukv, b_wo,
                f_wg, f_wu, f_wd,
                m_router, m_wg, m_wu, m_wd):
    seq = x_groups[0].shape[1]
    batches = [x.shape[0] for x in x_groups]
    batch = sum(batches)
    n_real = batch * seq
    n_meta_rows = batch * N_META
    big = _pick_tile(seq, (1024, 512, 256, 128))
    tp = min(big, 512)
    ntot = -(-(n_real + n_meta_rows) // big) * big
    n_tail = ntot - n_real

    tail = jnp.concatenate([
        jnp.broadcast_to(meta.astype(F32)[None], (batch, N_META, D_MODEL)).reshape(n_meta_rows, D_MODEL),
        jnp.zeros((n_tail - n_meta_rows, D_MODEL), F32)], axis=0)
    h_sources = [x.reshape(-1, D_MODEL) for x in x_groups] + [tail]

    pos = _positions(seq, n_tail, n_meta_rows)
    tabs_a = _rope_lane_tables(pos, GQA_THETA, GQA_ROT, GQA_HD, 0)
    tabs_b = _rope_lane_tables(pos, MLA_THETA, MLA_ROPE, LANES, MLA_NOPE)
    row = lambda v: v.reshape(1, -1).astype(F32)

    wqkv, wo_a = _prep_gqa(a_wqkv[0], a_wo[0])
    q, k, v = _qkv_call(h_sources, row(norm_mix[0]), wqkv, tabs_a, tm=tp, n_real=n_real, seq=seq)
    sink = a_sink[0].astype(F32) * LOG2E
    o = _win_attn_call(sink, q, k, v, batch=batch, seq=seq, n_real=n_real)
    o_tail = _win_meta_call(sink, q, k, v, batch=batch, seq=seq, n_real=n_real)
    h, xn = _wo_call(o, o_tail, wo_a, h_sources, row(norm_ffn[0]), tm=big)
    fdim = f_wg.shape[2]
    tf = _pick_tile(fdim, (512, 256, 128))
    h, xn = _ffn_dense_call(xn, f_wg[0].astype(BF16), f_wu[0].astype(BF16), f_wd[0].astype(BF16),
                            h, row(norm_mix[1]), tm=tp, tf=tf)

    wd, wuq, wuk, wuv = _prep_mla(b_wdkv[0], b_wuq[0], b_wukv[0])
    q, k, vt = _mla_proj_call(xn, wd, row(b_qnorm[0]), row(b_kvnorm[0]), wuq, wuk, wuv, tabs_b,
                              tm=tp, n_real=n_real, seq=seq)
    tq = _pick_tile(seq, (512, 256, 128))
    o = _mla_attn_call(q, k, vt, batch=batch, seq=seq, n_real=n_real, tq=tq)
    o_tail = _mla_meta_call(q, k, vt, batch=batch, seq=seq, n_real=n_real)
    tr = tp
    h, xpk, route, counts = _wo_router_call(o, o_tail, b_wo[0].astype(BF16), h, row(norm_ffn[1]),
                                            _prep_router(m_router[0]), tm=tr)

    tmx = big
    counts = counts[0, :N_EXPERTS].astype(I32)
    padded = ((counts + tmx - 1) // tmx) * tmx
    pend = jnp.cumsum(padded)
    pstart = pend - padded
    e_idx = route[:, 0:2].astype(I32)
    rank = route[:, 2:4].astype(I32)
    dest = pstart[e_idx] + rank
    n_assign = 2 * ntot
    n_blocks = -(-n_assign // tmx) + N_EXPERTS
    rows = n_blocks * tmx
    block_e = jnp.clip(jnp.searchsorted(pend, jnp.arange(n_blocks, dtype=I32) * tmx, side="right"),
                       0, N_EXPERTS - 1).astype(I32)
    n_used = (pend[-1] // tmx).astype(I32).reshape(1)
    block_e = jnp.where(jnp.arange(n_blocks) < n_used[0], block_e, block_e[jnp.maximum(n_used[0] - 1, 0)])

    td = tp
    dest_t = dest.reshape(ntot // td, td, 2).transpose(0, 2, 1)
    xs = _dispatch_call(dest_t, xpk, jnp.zeros((rows * XPK_ROWS, LANES), U32), td=td)
    edim = m_wg.shape[3]
    tfx = _pick_tile(edim, (512, 256, 128))
    ys = _ffn_expert_call(block_e, n_used, xs, m_wg[0].astype(BF16), m_wu[0].astype(BF16),
                          m_wd[0].astype(BF16), tm=tmx, tf=tfx)

    outs = []
    tile0 = 0
    for b, x in zip(batches, x_groups):
        ntiles = b * seq // td
        y = _combine_call(dest_t, ys, h, route, row(norm_final), tc=td, tile0=tile0, ntiles=ntiles)
        outs.append(y.reshape(b, seq, D_MODEL))
        tile0 += ntiles
    return tuple(outs)


def kernel(x_prompt, x_sample, meta, norm_mix, norm_ffn, norm_final, a_wqkv, a_wo, a_sink, b_wdkv, b_qnorm, b_kvnorm, b_wuq, b_wukv, b_wo, f_wg, f_wu, f_wd, m_router, m_wg, m_wu, m_wd):
    assert x_prompt.shape[1] == x_sample.shape[1] and x_prompt.shape[1] % BLK == 0
    assert norm_mix.shape[0] == 2, "two layers: windowed GQA + dense FFN, then MLA + MoE"
    return _trunk_flat((x_prompt, x_sample), meta, norm_mix, norm_ffn, norm_final,
                       a_wqkv, a_wo, a_sink,
                       b_wdkv, b_qnorm, b_kvnorm, b_wuq, b_wukv, b_wo,
                       f_wg, f_wu, f_wd,
                       m_router, m_wg, m_wu, m_wd)
```

```python
import functools
import math

import jax
import jax.numpy as jnp
from jax import lax
from jax.experimental import pallas as pl
from jax.experimental.pallas import tpu as pltpu

F32 = jnp.float32
BF16 = jnp.bfloat16
U32 = jnp.uint32
I32 = jnp.int32

D_MODEL = 1024
N_META = 16
RMS_EPS = 1e-6
BLK = 128
GQA_HEADS = 16
GQA_KV = 4
GQA_GROUP = 4
GQA_HD = 64
GQA_ROT = 16
GQA_THETA = 500000.0
MLA_HEADS = 16
MLA_NOPE = 64
MLA_ROPE = 32
MLA_V = 64
MLA_Q_RANK = 384
MLA_KV_RANK = 256
MLA_THETA = 10000.0
MLA_QK = MLA_NOPE + MLA_ROPE
N_EXPERTS = 8
LOG2E = 1.4426950408889634

LANES = 128
VMEM_LIMIT = 56 * 1024 * 1024


def _params(sem, vmem=VMEM_LIMIT):
    return pltpu.CompilerParams(dimension_semantics=sem, vmem_limit_bytes=vmem)


def _rms(x, g):
    ms = jnp.mean(x * x, axis=-1, keepdims=True)
    return x * lax.rsqrt(ms + RMS_EPS) * g


def _dot(a, b):
    return jnp.dot(a, b, preferred_element_type=F32)


def _dot_nt(a, b):
    return lax.dot_general(a, b, (((1,), (1,)), ((), ())), preferred_element_type=F32)


def _log2(n):
    assert n & (n - 1) == 0, n
    return n.bit_length() - 1


def _pick_tile(n, candidates):
    for c in candidates:
        if n % c == 0:
            return c
    raise ValueError(f"no tile in {candidates} divides {n}")


def _rope_lane_tables(pos, theta, rot, period, offset):
    half = rot // 2
    inv = jnp.power(jnp.float32(theta), -(jnp.arange(0, rot, 2, dtype=F32) / rot))
    ang = pos[:, None] * inv[None, :]
    cos, sin = jnp.cos(ang), jnp.sin(ang)
    lane = jnp.arange(LANES)
    r = (lane % period) - offset
    is_x1 = (r >= 0) & (r < half)
    is_x2 = (r >= half) & (r < rot)
    f = jnp.clip(jnp.where(is_x2, r - half, r), 0, half - 1)
    cos_l = cos[:, f]
    sin_l = sin[:, f]
    c = jnp.where((is_x1 | is_x2)[None, :], cos_l, 1.0)
    s1 = jnp.where(is_x2[None, :], sin_l, 0.0)
    s2 = jnp.where(is_x1[None, :], -sin_l, 0.0)
    return c.astype(F32), s1.astype(F32), s2.astype(F32)


def _rope(x, c, s1, s2, half):
    return x * c + pltpu.roll(x, half, 1) * s1 + pltpu.roll(x, LANES - half, 1) * s2


def _source_specs(sources, tm):
    specs, start = [], 0
    for src in sources:
        n = src.shape[0] // tm
        specs.append(pl.BlockSpec((tm, src.shape[1]),
                                  lambda i, start=start, n=n: (jnp.clip(i - start, 0, n - 1), 0)))
        start += n
    return specs


def _select_source(refs, src_rows, tm):
    i = pl.program_id(0)
    ends, acc = [], 0
    for n in src_rows:
        acc += n // tm
        ends.append(acc)
    x = refs[-1][...]
    for r, end in reversed(list(zip(refs[:-1], ends[:-1]))):
        x = jnp.where(i < end, r[...], x)
    return x


def _qkv_kernel(*refs, n_src, src_rows, tm):
    h_refs = refs[:n_src]
    g_ref, w_ref, c_ref, s1_ref, s2_ref, q_ref, k_ref, v_ref = refs[n_src:]
    xn = _rms(_select_source(h_refs, src_rows, tm), g_ref[...]).astype(BF16)
    qkv = _dot(xn, w_ref[...])
    c, s1, s2 = c_ref[...], s1_ref[...], s2_ref[...]
    nq = GQA_HEADS * GQA_HD // LANES
    nk = GQA_KV
    for i in range(nq):
        q_ref[:, i * LANES:(i + 1) * LANES] = (_rope(
            qkv[:, i * LANES:(i + 1) * LANES], c, s1, s2, GQA_ROT // 2) * LOG2E).astype(BF16)
    for i in range(nk):
        lo = (nq + i) * LANES
        k_ref[:, i * LANES:(i + 1) * LANES] = _rope(
            qkv[:, lo:lo + LANES], c, s1, s2, GQA_ROT // 2).astype(BF16)
    v_ref[...] = qkv[:, (nq + nk) * LANES:].astype(BF16)


def _table_index(i, n_real_tiles, tiles_per_seq):
    return jnp.where(i < n_real_tiles, i % tiles_per_seq, tiles_per_seq + i - n_real_tiles)


def _qkv_call(h_sources, g, w, tabs, *, tm, n_real, seq):
    src_rows = tuple(s.shape[0] for s in h_sources)
    ntot = sum(src_rows)
    nrt, tps = n_real // tm, seq // tm
    tab_spec = pl.BlockSpec((tm, LANES), lambda i: (_table_index(i, nrt, tps), 0))
    return pl.pallas_call(
        functools.partial(_qkv_kernel, n_src=len(h_sources), src_rows=src_rows, tm=tm),
        grid=(ntot // tm,),
        in_specs=_source_specs(h_sources, tm) + [
            pl.BlockSpec((1, D_MODEL), lambda i: (0, 0)),
            pl.BlockSpec(w.shape, lambda i: (0, 0)),
            tab_spec, tab_spec, tab_spec,
        ],
        out_specs=[
            pl.BlockSpec((tm, 1024), lambda i: (i, 0)),
            pl.BlockSpec((tm, GQA_KV * LANES), lambda i: (i, 0)),
            pl.BlockSpec((tm, GQA_KV * LANES), lambda i: (i, 0)),
        ],
        out_shape=[
            jax.ShapeDtypeStruct((ntot, 1024), BF16),
            jax.ShapeDtypeStruct((ntot, GQA_KV * LANES), BF16),
            jax.ShapeDtypeStruct((ntot, GQA_KV * LANES), BF16),
        ],
        compiler_params=_params(("parallel",)),
        name="qkv_rope",
    )(*h_sources, g, w, *tabs)


def _gqa_attend(sink_ref, q_ref, o_ref, groups):
    def pad_of(bias, pieces):
        have = sum(p.shape[0] for p in pieces)
        return [jnp.zeros((bias.shape[1] - have, LANES), BF16)] if bias.shape[1] > have else []

    def scores(g, kv):
        rows, k_pieces, _, bias = groups[g]
        sl = slice(kv * LANES, (kv + 1) * LANES)
        kcat = jnp.concatenate([p[:, sl] for p in k_pieces] + pad_of(bias, k_pieces), axis=0)
        zero = jnp.zeros_like(kcat)
        lane_k = lax.broadcasted_iota(I32, kcat.shape, 1)
        lo = slice(2 * kv * LANES, (2 * kv + 1) * LANES)
        hi = slice((2 * kv + 1) * LANES, (2 * kv + 2) * LANES)
        qp = jnp.concatenate([q_ref[rows, lo], q_ref[rows, hi]], axis=0)
        return (_dot_nt(qp, jnp.where(lane_k < GQA_HD, kcat, zero)) + bias,
                _dot_nt(qp, jnp.where(lane_k >= GQA_HD, kcat, zero)) + bias)

    def softmax(g, kv, ss):
        nq = groups[g][3].shape[0] // 2
        row = lax.broadcasted_iota(I32, (2 * nq, 1), 0)
        out = []
        for par, s in enumerate(ss):
            head = kv * GQA_GROUP + par
            sink = jnp.where(row < nq, sink_ref[head], sink_ref[head + 2])
            m = jnp.maximum(jnp.max(s, axis=-1, keepdims=True), sink)
            p = jnp.exp2(s - m)
            den = jnp.sum(p, axis=-1, keepdims=True) + jnp.exp2(sink - m)
            out.append((p.astype(BF16), den))
        return out

    def weighted_values(g, kv, pd):
        rows, _, v_pieces, bias = groups[g]
        nq = bias.shape[0] // 2
        sl = slice(kv * LANES, (kv + 1) * LANES)
        vcat = jnp.concatenate([p[:, sl] for p in v_pieces] + pad_of(bias, v_pieces), axis=0)
        outs = [_dot(p, vcat) / den for p, den in pd]
        lane_o = lax.broadcasted_iota(I32, outs[0].shape, 1)
        o = jnp.where(lane_o < GQA_HD, outs[0], outs[1]).astype(BF16)
        o_ref[rows, 2 * kv * LANES:(2 * kv + 1) * LANES] = o[:nq]
        o_ref[rows, (2 * kv + 1) * LANES:(2 * kv + 2) * LANES] = o[nq:]

    units = [(g, kv) for g in range(len(groups)) for kv in range(GQA_KV)]
    nxt = scores(*units[0])
    pending = None
    for idx, unit in enumerate(units):
        ss = nxt
        if idx + 1 < len(units):
            nxt = scores(*units[idx + 1])
        pd = softmax(*unit, ss)
        if pending is not None:
            weighted_values(*pending)
        pending = (*unit, pd)
    weighted_values(*pending)


def _window_bias(prev_ok, next_ok):
    nkeys = 4 * BLK
    qi = lax.broadcasted_iota(I32, (2 * BLK, nkeys), 0) & (BLK - 1)
    c = lax.broadcasted_iota(I32, (2 * BLK, nkeys), 1)
    neg = jnp.float32(-jnp.inf)
    b_p = jnp.where((c >= qi) & prev_ok, 0.0, neg)
    b_n = jnp.where(((c - 2 * BLK) <= qi) & next_ok, 0.0, neg)
    b_m = jnp.where(c < 3 * BLK + N_META, 0.0, neg)
    return jnp.where(c < BLK, b_p, jnp.where(c < 2 * BLK, 0.0, jnp.where(c < 3 * BLK, b_n, b_m)))


def _win_attn_kernel(sink_ref, q_ref, *refs, nb, qb):
    k_blk, km_ref = refs[:qb + 2], refs[qb + 2]
    v_blk, vm_ref = refs[qb + 3:2 * qb + 5], refs[2 * qb + 5]
    o_ref = refs[2 * qb + 6]
    j0 = pl.program_id(1) * qb
    groups = []
    for t in range(qb):
        bias = _window_bias(j0 + t > 0, j0 + t < nb - 1)
        groups.append((slice(t * BLK, (t + 1) * BLK),
                       (k_blk[t], k_blk[t + 1], k_blk[t + 2], km_ref),
                       (v_blk[t], v_blk[t + 1], v_blk[t + 2], vm_ref), bias))
    _gqa_attend(sink_ref, q_ref, o_ref, groups)


def _win_attn_call(sink, q, k, v, *, batch, seq, n_real):
    nb = seq // BLK
    qb = _pick_tile(nb, (4, 2, 1))
    mrow = n_real // N_META
    kvw = GQA_KV * LANES

    def qmap(b, j, s):
        return (b * (nb // qb) + j, 0)

    def blk_spec(t):
        return pl.BlockSpec((BLK, kvw),
                            lambda b, j, s: (b * nb + jnp.clip(j * qb + t - 1, 0, nb - 1), 0))

    meta_spec = pl.BlockSpec((N_META, kvw), lambda b, j, s: (mrow + b, 0))
    kv_specs = [blk_spec(t) for t in range(qb + 2)] + [meta_spec]
    grid_spec = pltpu.PrefetchScalarGridSpec(
        num_scalar_prefetch=1,
        grid=(batch, nb // qb),
        in_specs=[pl.BlockSpec((qb * BLK, 1024), qmap)] + kv_specs + kv_specs,
        out_specs=pl.BlockSpec((qb * BLK, 1024), qmap),
    )
    n_kv = qb + 3
    return pl.pallas_call(
        functools.partial(_win_attn_kernel, nb=nb, qb=qb),
        grid_spec=grid_spec,
        out_shape=jax.ShapeDtypeStruct((n_real, 1024), BF16),
        compiler_params=_params(("parallel", "parallel")),
        name="win_attn",
    )(sink, q, *([k] * n_kv), *([v] * n_kv))


def _win_meta_kernel(sink_ref, q_ref, km_ref, kc_ref, vm_ref, vc_ref, o_ref, *, batch):
    i = pl.program_id(0)

    @pl.when(i < batch)
    def _():
        nkeys = 2 * BLK
        qp = lax.broadcasted_iota(I32, (2 * N_META, nkeys), 0) & (N_META - 1)
        c = lax.broadcasted_iota(I32, (2 * N_META, nkeys), 1)
        neg = jnp.float32(-jnp.inf)
        b_c = jnp.where((N_META + c - qp) <= BLK, 0.0, neg)
        bias = jnp.where(c < BLK, b_c, jnp.where(c < BLK + N_META, 0.0, neg))
        _gqa_attend(sink_ref, q_ref, o_ref,
                    [(slice(0, N_META), (kc_ref, km_ref), (vc_ref, vm_ref), bias)])

    @pl.when(i >= batch)
    def _():
        o_ref[...] = jnp.zeros_like(o_ref)


def _win_meta_call(sink, q, k, v, *, batch, seq, n_real):
    ntot = q.shape[0]
    nb = seq // BLK
    mrow = n_real // N_META
    ntail_blocks = (ntot - n_real) // N_META

    def tmap(i, s):
        return (mrow + i, 0)

    def cmap(i, s):
        return (jnp.minimum(i, batch - 1) * nb, 0)

    grid_spec = pltpu.PrefetchScalarGridSpec(
        num_scalar_prefetch=1,
        grid=(ntail_blocks,),
        in_specs=[
            pl.BlockSpec((N_META, 1024), tmap),
            pl.BlockSpec((N_META, GQA_KV * LANES), tmap),
            pl.BlockSpec((BLK, GQA_KV * LANES), cmap),
            pl.BlockSpec((N_META, GQA_KV * LANES), tmap),
            pl.BlockSpec((BLK, GQA_KV * LANES), cmap),
        ],
        out_specs=pl.BlockSpec((N_META, 1024), lambda i, s: (i, 0)),
    )
    return pl.pallas_call(
        functools.partial(_win_meta_kernel, batch=batch),
        grid_spec=grid_spec,
        out_shape=jax.ShapeDtypeStruct((ntot - n_real, 1024), BF16),
        compiler_params=_params(("parallel",)),
        name="win_attn_meta",
    )(sink, q, k, k, v, v)


def _attn_out_tile(o_real_ref, o_tail_ref, n_real_tiles):
    is_real = pl.program_id(0) < n_real_tiles
    return jnp.where(is_real, o_real_ref[...], o_tail_ref[...])


def _attn_out_specs(tm, n_real_tiles):
    return [
        pl.BlockSpec((tm, 1024), lambda i: (jnp.minimum(i, n_real_tiles - 1), 0)),
        pl.BlockSpec((tm, 1024), lambda i: (jnp.maximum(i - n_real_tiles, 0), 0)),
    ]


def _wo_kernel(*refs, n_real_tiles, n_src, src_rows, tm):
    o_real_ref, o_tail_ref, w_ref = refs[:3]
    h_refs = refs[3:3 + n_src]
    g_ref, hout_ref, xn_ref = refs[3 + n_src:]
    o = _attn_out_tile(o_real_ref, o_tail_ref, n_real_tiles)
    hn = _dot(o, w_ref[...]) + _select_source(h_refs, src_rows, tm)
    hout_ref[...] = hn
    xn_ref[...] = _rms(hn, g_ref[...]).astype(BF16)


def _wo_call(o_real, o_tail, w, h_sources, g, *, tm):
    src_rows = tuple(s.shape[0] for s in h_sources)
    ntot = sum(src_rows)
    nrt = o_real.shape[0] // tm
    row = lambda i: (i, 0)
    fix = lambda i: (0, 0)
    return pl.pallas_call(
        functools.partial(_wo_kernel, n_real_tiles=nrt, n_src=len(h_sources), src_rows=src_rows, tm=tm),
        grid=(ntot // tm,),
        in_specs=_attn_out_specs(tm, nrt) + [pl.BlockSpec((1024, D_MODEL), fix)]
        + _source_specs(h_sources, tm) + [
            pl.BlockSpec((1, D_MODEL), fix),
        ],
        out_specs=[pl.BlockSpec((tm, D_MODEL), row), pl.BlockSpec((tm, D_MODEL), row)],
        out_shape=[jax.ShapeDtypeStruct((ntot, D_MODEL), F32),
                   jax.ShapeDtypeStruct((ntot, D_MODEL), BF16)],
        compiler_params=_params(("parallel",)),
        name="wo_res_norm",
    )(o_real, o_tail, w, *h_sources, g)


XPK_ROWS = D_MODEL // 2 // LANES
YS_ROWS = D_MODEL // LANES


ROW_SPLIT = 2
MLA_PROJ_SPLIT = 4


def _store_token_major(ref, x, row0=0):
    t, n = x.shape
    k = n // LANES
    for s in range(k):
        ref[pl.ds(row0 * k + s, t, stride=k), :] = x[:, s * LANES:(s + 1) * LANES]


def _load_token_major(ref, k):
    t = ref.shape[0] // k
    return [ref[pl.ds(s, t, stride=k), :] for s in range(k)]


def _pack_bf16_pair(a, b):
    ra = pltpu.bitcast(a.astype(BF16).astype(F32), U32)
    rb = pltpu.bitcast(b.astype(BF16).astype(F32), U32)
    return ra | (rb >> 16)


def _unpack_bf16_pair(p):
    a = pltpu.bitcast(p & jnp.uint32(0xFFFF0000), F32).astype(BF16)
    b = pltpu.bitcast(p << 16, F32).astype(BF16)
    return a, b


def _wo_router_kernel(o_real_ref, o_tail_ref, w_ref, h_ref, g_ref, r_ref,
                      hout_ref, xpk_ref, route_ref, cnt_ref, carry_ref, *, n_real_tiles):
    i = pl.program_id(0)
    tm = h_ref.shape[0]

    @pl.when(i == 0)
    def _():
        carry_ref[...] = jnp.zeros_like(carry_ref)

    o = _attn_out_tile(o_real_ref, o_tail_ref, n_real_tiles)
    th = tm // ROW_SPLIT
    parts = [slice(p * th, (p + 1) * th) for p in range(ROW_SPLIT)]
    half = D_MODEL // 2
    r2 = r_ref[...]
    lane = lax.broadcasted_iota(I32, (th, LANES), 1)
    rr = lax.broadcasted_iota(I32, (th, th), 0)
    cc = lax.broadcasted_iota(I32, (th, th), 1)
    lower = jnp.where(cc < rr, 1.0, 0.0).astype(BF16)

    ys = []
    for rs in parts:
        hn = _dot(o[rs], w_ref[...]) + h_ref[rs, :]
        hout_ref[rs, :] = hn
        ys.append(_rms(hn, g_ref[...]))
    logits = []
    for p, y in enumerate(ys):
        _store_token_major(xpk_ref, _pack_bf16_pair(y[:, :half], y[:, half:]), row0=p * th)
        yhi = y.astype(BF16)
        ylo = (y - yhi.astype(F32)).astype(BF16)
        t = _dot(yhi, r2)
        logits.append(t[:, :LANES] + t[:, LANES:] + _dot(ylo, r2[:, :LANES]))
    carry = carry_ref[0:1, :]
    for rs, lg in zip(parts, logits):
        lm = jnp.where(lane < N_EXPERTS, lg, -jnp.inf)
        m1 = jnp.max(lm, axis=-1, keepdims=True)
        i1 = jnp.min(jnp.where(lm == m1, lane, LANES), axis=-1, keepdims=True)
        lm2 = jnp.where(lane == i1, -jnp.inf, lm)
        m2 = jnp.max(lm2, axis=-1, keepdims=True)
        i2 = jnp.min(jnp.where(lm2 == m2, lane, LANES), axis=-1, keepdims=True)
        e21 = jnp.exp(m2 - m1)
        g0 = 1.0 / (1.0 + e21)
        g1 = e21 / (1.0 + e21)
        onehot = ((lane == i1) | (lane == i2)).astype(F32)
        before = _dot(lower, onehot.astype(BF16)) + carry
        rank1 = jnp.sum(jnp.where(lane == i1, before, 0.0), axis=-1, keepdims=True)
        rank2 = jnp.sum(jnp.where(lane == i2, before, 0.0), axis=-1, keepdims=True)
        carry = carry + jnp.sum(onehot, axis=0, keepdims=True)
        out = jnp.where(lane == 0, i1.astype(F32), 0.0)
        out = jnp.where(lane == 1, i2.astype(F32), out)
        out = jnp.where(lane == 2, rank1, out)
        out = jnp.where(lane == 3, rank2, out)
        out = jnp.where(lane == 4, g0, out)
        out = jnp.where(lane == 5, g1, out)
        route_ref[rs, :] = out
    carry_ref[...] = jnp.broadcast_to(carry, carry_ref.shape)
    cnt_ref[...] = jnp.broadcast_to(carry, cnt_ref.shape)


def _wo_router_call(o_real, o_tail, w, h, g, r2, *, tm):
    ntot = h.shape[0]
    nrt = o_real.shape[0] // tm
    row = lambda i: (i, 0)
    fix = lambda i: (0, 0)
    return pl.pallas_call(
        functools.partial(_wo_router_kernel, n_real_tiles=nrt),
        grid=(ntot // tm,),
        in_specs=_attn_out_specs(tm, nrt) + [
            pl.BlockSpec((1024, D_MODEL), fix),
            pl.BlockSpec((tm, D_MODEL), row),
            pl.BlockSpec((1, D_MODEL), fix),
            pl.BlockSpec((D_MODEL, 2 * LANES), fix),
        ],
        out_specs=[
            pl.BlockSpec((tm, D_MODEL), row),
            pl.BlockSpec((tm * XPK_ROWS, LANES), row),
            pl.BlockSpec((tm, LANES), row),
            pl.BlockSpec((8, LANES), fix),
        ],
        out_shape=[
            jax.ShapeDtypeStruct((ntot, D_MODEL), F32),
            jax.ShapeDtypeStruct((ntot * XPK_ROWS, LANES), U32),
            jax.ShapeDtypeStruct((ntot, LANES), F32),
            jax.ShapeDtypeStruct((8, LANES), F32),
        ],
        scratch_shapes=[pltpu.VMEM((8, LANES), F32)],
        compiler_params=_params(("arbitrary",)),
        name="wo_res_norm_router",
    )(o_real, o_tail, w, h, g, r2)


def _swiglu_chunk(x, wg, wu, wd):
    g = _dot(x, wg)
    u = _dot(x, wu)
    a = (g * (1.0 / (1.0 + jnp.exp(-g)))) * u
    return _dot(a.astype(BF16), wd)


def _ffn_dense_kernel(x_ref, wg_ref, wu_ref, wd_ref, h_ref, g_ref, hout_ref, xn_ref, *, tf):
    x = x_ref[...]
    hn = h_ref[...]
    for c in range(wg_ref.shape[1] // tf):
        cs = slice(c * tf, (c + 1) * tf)
        hn = hn + _swiglu_chunk(x, wg_ref[:, cs], wu_ref[:, cs], wd_ref[cs, :])
    hout_ref[...] = hn
    xn_ref[...] = _rms(hn, g_ref[...]).astype(BF16)


def _ffn_dense_call(x, wg, wu, wd, h, g, *, tm, tf):
    ntot = x.shape[0]
    fdim = wg.shape[1]
    row = lambda i: (i, 0)
    fix = lambda i: (0, 0)
    once = pl.Buffered(1)
    return pl.pallas_call(
        functools.partial(_ffn_dense_kernel, tf=tf),
        grid=(ntot // tm,),
        in_specs=[
            pl.BlockSpec((tm, D_MODEL), row),
            pl.BlockSpec((D_MODEL, fdim), fix, pipeline_mode=once),
            pl.BlockSpec((D_MODEL, fdim), fix, pipeline_mode=once),
            pl.BlockSpec((fdim, D_MODEL), fix, pipeline_mode=once),
            pl.BlockSpec((tm, D_MODEL), row),
            pl.BlockSpec((1, D_MODEL), fix),
        ],
        out_specs=[pl.BlockSpec((tm, D_MODEL), row), pl.BlockSpec((tm, D_MODEL), row)],
        out_shape=[jax.ShapeDtypeStruct((ntot, D_MODEL), F32),
                   jax.ShapeDtypeStruct((ntot, D_MODEL), BF16)],
        compiler_params=_params(("parallel",)),
        name="ffn_dense",
    )(x, wg, wu, wd, h, g)


def _ffn_expert_kernel(be_ref, nu_ref, xs_ref, wg_ref, wu_ref, wd_ref, ys_ref, *, tf):
    del be_ref
    used = pl.program_id(0) < nu_ref[0]

    @pl.when(used)
    def _():
        halves = [_unpack_bf16_pair(blk) for blk in _load_token_major(xs_ref, XPK_ROWS)]
        x = jnp.concatenate([a for a, _ in halves] + [b for _, b in halves], axis=1)
        acc = None
        for c in range(wg_ref.shape[1] // tf):
            cs = slice(c * tf, (c + 1) * tf)
            y = _swiglu_chunk(x, wg_ref[:, cs], wu_ref[:, cs], wd_ref[cs, :])
            acc = y if acc is None else acc + y
        _store_token_major(ys_ref, acc)

    @pl.when(jnp.logical_not(used))
    def _():
        ys_ref[...] = jnp.zeros_like(ys_ref)


def _ffn_expert_call(block_e, n_used, xs, wg, wu, wd, *, tm, tf):
    rows = xs.shape[0] // XPK_ROWS
    once = pl.Buffered(1)
    wmap = lambda i, be, nu: (be[i], 0, 0)
    grid_spec = pltpu.PrefetchScalarGridSpec(
        num_scalar_prefetch=2,
        grid=(rows // tm,),
        in_specs=[
            pl.BlockSpec((tm * XPK_ROWS, LANES), lambda i, be, nu: (i, 0)),
            pl.BlockSpec((None,) + wg.shape[1:], wmap, pipeline_mode=once),
            pl.BlockSpec((None,) + wu.shape[1:], wmap, pipeline_mode=once),
            pl.BlockSpec((None,) + wd.shape[1:], wmap, pipeline_mode=once),
        ],
        out_specs=pl.BlockSpec((tm * YS_ROWS, LANES), lambda i, be, nu: (i, 0)),
    )
    return pl.pallas_call(
        functools.partial(_ffn_expert_kernel, tf=tf),
        grid_spec=grid_spec,
        out_shape=jax.ShapeDtypeStruct((rows * YS_ROWS, LANES), F32),
        compiler_params=_params(("arbitrary",)),
        name="ffn_expert",
    )(block_e, n_used, xs, wg, wu, wd)


DMA_UNROLL = 8


def _token_copy(src, src_tok, dst, dst_tok, rows, sem):
    return pltpu.make_async_copy(src.at[pl.ds(pl.multiple_of(src_tok * rows, rows), rows)],
                                 dst.at[pl.ds(pl.multiple_of(dst_tok * rows, rows), rows)], sem)


def _dispatch_kernel(dest_ref, x_ref, xs_in_ref, xs_ref, sem):
    del xs_in_ref
    td = x_ref.shape[0] // XPK_ROWS

    def issue(t, carry):
        for s in range(2):
            _token_copy(x_ref, t, xs_ref, dest_ref[0, s, t], XPK_ROWS, sem).start(priority=s)
        return carry

    lax.fori_loop(0, td, issue, 0, unroll=DMA_UNROLL)

    def drain(t, carry):
        for s in range(2):
            _token_copy(x_ref, 0, xs_ref, 0, XPK_ROWS, sem).wait()
        return carry

    lax.fori_loop(0, td, drain, 0, unroll=DMA_UNROLL)


def _dispatch_call(dest, xpk, xs_init, *, td):
    ntot = xpk.shape[0] // XPK_ROWS
    return pl.pallas_call(
        _dispatch_kernel,
        grid=(ntot // td,),
        in_specs=[
            pl.BlockSpec((1, 2, td), lambda i: (i, 0, 0), memory_space=pltpu.SMEM),
            pl.BlockSpec((td * XPK_ROWS, LANES), lambda i: (i, 0)),
            pl.BlockSpec(memory_space=pl.ANY),
        ],
        out_specs=pl.BlockSpec(memory_space=pl.ANY),
        out_shape=jax.ShapeDtypeStruct(xs_init.shape, U32),
        input_output_aliases={2: 0},
        scratch_shapes=[pltpu.SemaphoreType.DMA(())],
        compiler_params=_params(("arbitrary",)),
        name="moe_dispatch",
    )(dest, xpk, xs_init)


def _combine_kernel(dest_ref, dest_next_ref, ys_ref, h_ref, route_ref, g_ref, out_ref,
                    y0_ref, y1_ref, sems):
    i = pl.program_id(0)
    n = pl.num_programs(0)
    tc = h_ref.shape[0]
    bufs = (y0_ref, y1_ref)
    slot = i % 2

    def issue(d_ref, sl):
        def body(t, carry):
            for s in range(2):
                _token_copy(ys_ref, d_ref[0, s, t], bufs[s].at[sl], t, YS_ROWS,
                            sems.at[sl]).start(priority=s)
            return carry

        lax.fori_loop(0, tc, body, 0, unroll=DMA_UNROLL)

    @pl.when(i == 0)
    def _():
        issue(dest_ref, 0)

    @pl.when(i + 1 < n)
    def _():
        issue(dest_next_ref, 1 - slot)

    def drain(t, carry):
        for s in range(2):
            _token_copy(ys_ref, 0, bufs[s].at[slot], 0, YS_ROWS, sems.at[slot]).wait()
        return carry

    lax.fori_loop(0, tc, drain, 0, unroll=DMA_UNROLL)

    r = route_ref[...]
    lane = lax.broadcasted_iota(I32, r.shape, 1)
    g0 = jnp.sum(jnp.where(lane == 4, r, 0.0), axis=-1, keepdims=True)
    g1 = jnp.sum(jnp.where(lane == 5, r, 0.0), axis=-1, keepdims=True)
    y0 = _load_token_major(y0_ref.at[slot], YS_ROWS)
    y1 = _load_token_major(y1_ref.at[slot], YS_ROWS)
    moe = jnp.concatenate([a * g0 + b * g1 for a, b in zip(y0, y1)], axis=1)
    out_ref[...] = _rms(h_ref[...] + moe, g_ref[...])


def _combine_call(dest, ys, h, route, g, *, tc, tile0, ntiles):
    return pl.pallas_call(
        _combine_kernel,
        grid=(ntiles,),
        in_specs=[
            pl.BlockSpec((1, 2, tc), lambda i: (tile0 + i, 0, 0), memory_space=pltpu.SMEM),
            pl.BlockSpec((1, 2, tc), lambda i: (tile0 + jnp.minimum(i + 1, ntiles - 1), 0, 0),
                         memory_space=pltpu.SMEM),
            pl.BlockSpec(memory_space=pl.ANY),
            pl.BlockSpec((tc, D_MODEL), lambda i: (tile0 + i, 0)),
            pl.BlockSpec((tc, LANES), lambda i: (tile0 + i, 0)),
            pl.BlockSpec((1, D_MODEL), lambda i: (0, 0)),
        ],
        out_specs=pl.BlockSpec((tc, D_MODEL), lambda i: (i, 0)),
        out_shape=jax.ShapeDtypeStruct((ntiles * tc, D_MODEL), F32),
        scratch_shapes=[pltpu.VMEM((2, tc * YS_ROWS, LANES), F32), pltpu.VMEM((2, tc * YS_ROWS, LANES), F32),
                        pltpu.SemaphoreType.DMA((2,))],
        compiler_params=_params(("arbitrary",)),
        name="moe_combine_norm",
    )(dest, dest, ys, h, route, g)


def _mla_proj_kernel(x_ref, wd_ref, qn_ref, kvn_ref, wuq_ref, wuk_ref, wuvt_ref,
                     c_ref, s1_ref, s2_ref, q_ref, k_ref, vt_ref):
    tm = x_ref.shape[0]
    th = tm // MLA_PROJ_SPLIT
    parts = [slice(p * th, (p + 1) * th) for p in range(MLA_PROJ_SPLIT)]
    half = MLA_ROPE // 2
    scale = (MLA_QK ** -0.5) * LOG2E
    lats = [_dot(x_ref[rs, :], wd_ref[...]) for rs in parts]
    ups = []
    for lat in lats:
        cq = _rms(lat[:, :MLA_Q_RANK], qn_ref[...]).astype(BF16)
        ckv = _rms(lat[:, MLA_Q_RANK:MLA_Q_RANK + MLA_KV_RANK], kvn_ref[...]).astype(BF16)
        ups.append((_dot(cq, wuq_ref[...]),
                    _dot(ckv, wuk_ref[...]),
                    _dot_nt(wuvt_ref[...], ckv)))
    for rs, lat, (q, k, vt) in zip(parts, lats, ups):
        c, s1, s2 = c_ref[rs, :], s1_ref[rs, :], s2_ref[rs, :]
        kr = _rope(lat[:, MLA_Q_RANK + MLA_KV_RANK:], c, s1, s2, half)
        for h in range(MLA_HEADS):
            sl = slice(h * LANES, (h + 1) * LANES)
            q_ref[rs, sl] = (_rope(q[:, sl], c, s1, s2, half) * scale).astype(BF16)
            k_ref[rs, sl] = (k[:, sl] + kr).astype(BF16)
        vt_ref[:, rs] = vt.astype(BF16)


def _mla_proj_call(x, wd, qn, kvn, wuq, wuk, wuv, tabs, *, tm, n_real, seq):
    ntot = x.shape[0]
    nrt, tps = n_real // tm, seq // tm
    row = lambda i: (i, 0)
    fix = lambda i: (0, 0)
    tab_spec = pl.BlockSpec((tm, LANES), lambda i: (_table_index(i, nrt, tps), 0))
    return pl.pallas_call(
        _mla_proj_kernel,
        grid=(ntot // tm,),
        in_specs=[
            pl.BlockSpec((tm, D_MODEL), row),
            pl.BlockSpec(wd.shape, fix),
            pl.BlockSpec((1, MLA_Q_RANK), fix),
            pl.BlockSpec((1, MLA_KV_RANK), fix),
            pl.BlockSpec(wuq.shape, fix),
            pl.BlockSpec(wuk.shape, fix),
            pl.BlockSpec(wuv.shape[::-1], fix),
            tab_spec, tab_spec, tab_spec,
        ],
        out_specs=[
            pl.BlockSpec((tm, 2048), row),
            pl.BlockSpec((tm, 2048), row),
            pl.BlockSpec((1024, tm), lambda i: (0, i)),
        ],
        out_shape=[
            jax.ShapeDtypeStruct((ntot, 2048), BF16),
            jax.ShapeDtypeStruct((ntot, 2048), BF16),
            jax.ShapeDtypeStruct((1024, ntot), BF16),
        ],
        compiler_params=_params(("parallel",)),
        name="mla_proj",
    )(x, wd, qn, kvn, wuq, wuk, wuv.T, *tabs)


META_PER_BLOCK = LANES // N_META
REDUCE_ROWS = 64
SCORES_AHEAD = 1
VALUES_BEHIND = 1
MLA_QUERY_TILE = 256


def _col_reduce(x, op):
    rows, n = x.shape
    if rows > REDUCE_ROWS and rows % REDUCE_ROWS == 0:
        x = op(x.reshape(rows // REDUCE_ROWS, REDUCE_ROWS, n), axis=0)
    return op(x, axis=0, keepdims=True)


def _mla_attn_kernel(q_ref, kr_ref, kt_ref, vtr_ref, vtt_ref, o_ref, ot_ref):
    b = pl.program_id(0)
    r = lax.broadcasted_iota(I32, (LANES, 1), 0)
    mine = jnp.right_shift(r, _log2(N_META)) == (b % META_PER_BLOCK)
    mbias = jnp.where(mine, 0.0, -jnp.inf).astype(F32)
    tq = min(MLA_QUERY_TILE, q_ref.shape[0])
    units = [(g, h) for g in range(q_ref.shape[0] // tq) for h in range(MLA_HEADS)]

    def scores(g, h):
        hs = slice(h * LANES, (h + 1) * LANES)
        qh = q_ref[g * tq:(g + 1) * tq, hs]
        return _dot_nt(kr_ref[:, hs], qh), _dot_nt(kt_ref[:, hs], qh) + mbias

    def weighted_values(g, h, p_r, p_m, den):
        vs = slice(h * MLA_V, (h + 1) * MLA_V)
        acc = _dot(vtr_ref[vs, :], p_r) + _dot(vtt_ref[vs, :], p_m)
        ot_ref[vs, g * tq:(g + 1) * tq] = acc / den

    ahead = [scores(*u) for u in units[:SCORES_AHEAD]]
    pending = []
    for idx, unit in enumerate(units):
        s_r, s_m = ahead.pop(0)
        if idx + SCORES_AHEAD < len(units):
            ahead.append(scores(*units[idx + SCORES_AHEAD]))
        m = jnp.maximum(_col_reduce(s_r, jnp.max), _col_reduce(s_m, jnp.max))
        p_r = jnp.exp2(s_r - m)
        p_m = jnp.exp2(s_m - m)
        den = _col_reduce(p_r, jnp.sum) + _col_reduce(p_m, jnp.sum)
        pending.append((*unit, p_r.astype(BF16), p_m.astype(BF16), den))
        if len(pending) > VALUES_BEHIND:
            weighted_values(*pending.pop(0))
    for item in pending:
        weighted_values(*item)
    o_ref[...] = ot_ref[...].T.astype(BF16)


def _mla_attn_call(q, k, vt, *, batch, seq, n_real, tq):
    nq = seq // tq
    tail0 = n_real // LANES
    qmap = lambda b, j: (b * nq + j, 0)
    return pl.pallas_call(
        _mla_attn_kernel,
        grid=(batch, nq),
        in_specs=[
            pl.BlockSpec((tq, 2048), qmap),
            pl.BlockSpec((seq, 2048), lambda b, j: (b, 0)),
            pl.BlockSpec((LANES, 2048), lambda b, j: (tail0 + b // META_PER_BLOCK, 0)),
            pl.BlockSpec((1024, seq), lambda b, j: (0, b)),
            pl.BlockSpec((1024, LANES), lambda b, j: (0, tail0 + b // META_PER_BLOCK)),
        ],
        out_specs=pl.BlockSpec((tq, 1024), qmap),
        out_shape=jax.ShapeDtypeStruct((n_real, 1024), BF16),
        scratch_shapes=[pltpu.VMEM((1024, tq), F32)],
        compiler_params=_params(("parallel", "arbitrary")),
        name="mla_attn",
    )(q, k, k, vt, vt)


def _mla_meta_kernel(q_ref, kr_ref, kt_ref, vtr_ref, vtt_ref, o_ref, *, batch):
    i = pl.program_id(0)
    ncol = MLA_HEADS * N_META

    @pl.when(i < batch)
    def _():
        qi = lax.broadcasted_iota(I32, (N_META, ncol), 0)
        ci = lax.broadcasted_iota(I32, (N_META, ncol), 1)
        spread = jnp.where((ci & (N_META - 1)) == qi, 1.0, 0.0).astype(BF16)
        w = lax.dot_general(q_ref[...], spread, (((0,), (0,)), ((), ())),
                            preferred_element_type=F32)
        wr = lax.broadcasted_iota(I32, w.shape, 0)
        wc = lax.broadcasted_iota(I32, w.shape, 1)
        on_diag = jnp.right_shift(wr, _log2(LANES)) == jnp.right_shift(wc, _log2(N_META))
        w = jnp.where(on_diag, w, 0.0).astype(BF16)
        r = lax.broadcasted_iota(I32, (LANES, 1), 0)
        mine = jnp.right_shift(r, _log2(N_META)) == (i % META_PER_BLOCK)
        mbias = jnp.where(mine, 0.0, -jnp.inf).astype(F32)
        s_r = _dot(kr_ref[...], w)
        s_m = _dot(kt_ref[...], w) + mbias
        m = jnp.maximum(_col_reduce(s_r, jnp.max), _col_reduce(s_m, jnp.max))
        p_r = jnp.exp2(s_r - m)
        p_m = jnp.exp2(s_m - m)
        den = _col_reduce(p_r, jnp.sum) + _col_reduce(p_m, jnp.sum)
        ot = (_dot(vtr_ref[...], p_r.astype(BF16)) + _dot(vtt_ref[...], p_m.astype(BF16))) / den
        o_all = ot.T
        lane = lax.broadcasted_iota(I32, (N_META, 1024), 1)
        o = jnp.zeros((N_META, 1024), F32)
        for h in range(MLA_HEADS):
            rows = o_all[h * N_META:(h + 1) * N_META, :]
            o = jnp.where(jnp.right_shift(lane, _log2(MLA_V)) == h, rows, o)
        o_ref[...] = o.astype(BF16)

    @pl.when(i >= batch)
    def _():
        o_ref[...] = jnp.zeros_like(o_ref)


def _mla_meta_call(q, k, vt, *, batch, seq, n_real):
    ntot = q.shape[0]
    mrow = n_real // N_META
    tail0 = n_real // LANES
    ntail_blocks = (ntot - n_real) // N_META
    last = (batch - 1) // META_PER_BLOCK
    bmap = lambda i: jnp.minimum(i, batch - 1)
    tmap = lambda i: jnp.minimum(i // META_PER_BLOCK, last)
    return pl.pallas_call(
        functools.partial(_mla_meta_kernel, batch=batch),
        grid=(ntail_blocks,),
        in_specs=[
            pl.BlockSpec((N_META, 2048), lambda i: (mrow + i, 0)),
            pl.BlockSpec((seq, 2048), lambda i: (bmap(i), 0)),
            pl.BlockSpec((LANES, 2048), lambda i: (tail0 + tmap(i), 0)),
            pl.BlockSpec((1024, seq), lambda i: (0, bmap(i))),
            pl.BlockSpec((1024, LANES), lambda i: (0, tail0 + tmap(i))),
        ],
        out_specs=pl.BlockSpec((N_META, 1024), lambda i: (i, 0)),
        out_shape=jax.ShapeDtypeStruct((ntot - n_real, 1024), BF16),
        compiler_params=_params(("arbitrary",)),
        name="mla_attn_meta",
    )(q, k, k, vt, vt)


def _prep_gqa(a_wqkv, a_wo):
    d = a_wqkv.shape[0]
    nq = GQA_HEADS * GQA_HD
    nkv = GQA_KV * GQA_HD
    wq = a_wqkv[:, :nq] * (GQA_HD ** -0.5)

    def dup(w):
        w = w.reshape(d, GQA_KV, 1, GQA_HD)
        return jnp.broadcast_to(w, (d, GQA_KV, 2, GQA_HD)).reshape(d, 2 * nkv)

    w = jnp.concatenate([wq, dup(a_wqkv[:, nq:nq + nkv]), dup(a_wqkv[:, nq + nkv:])], axis=1)
    return w.astype(BF16), a_wo.astype(BF16)


def _prep_mla(b_wdkv, b_wuq, b_wukv):
    d = b_wdkv.shape[0]
    lat = MLA_Q_RANK + MLA_KV_RANK
    z = lambda n: jnp.zeros((d, n), F32)
    wd = jnp.concatenate([b_wdkv[:, :lat], z(MLA_NOPE), b_wdkv[:, lat:], z(LANES - MLA_QK)], axis=1)
    wuq = b_wuq.reshape(MLA_Q_RANK, MLA_HEADS, MLA_QK)
    wuq = jnp.pad(wuq, ((0, 0), (0, 0), (0, LANES - MLA_QK))).reshape(MLA_Q_RANK, MLA_HEADS * LANES)
    wukv = b_wukv.reshape(MLA_KV_RANK, MLA_HEADS, MLA_NOPE + MLA_V)
    wuk = jnp.pad(wukv[:, :, :MLA_NOPE], ((0, 0), (0, 0), (0, LANES - MLA_NOPE)))
    wuk = wuk.reshape(MLA_KV_RANK, MLA_HEADS * LANES)
    wuv = wukv[:, :, MLA_NOPE:].reshape(MLA_KV_RANK, MLA_HEADS * MLA_V)
    return wd.astype(BF16), wuq.astype(BF16), wuk.astype(BF16), wuv.astype(BF16)


def _prep_router(m_router):
    hi = m_router.astype(BF16)
    lo = (m_router - hi.astype(F32)).astype(BF16)
    pad = lambda r: jnp.pad(r, ((0, 0), (0, LANES - N_EXPERTS)))
    return jnp.concatenate([pad(hi), pad(lo)], axis=1)


def _positions(seq, n_tail, n_meta_rows):
    real = N_META + jnp.arange(seq, dtype=F32)
    t = jnp.arange(n_tail)
    tail = jnp.where(t < n_meta_rows, t % N_META, 0).astype(F32)
    return jnp.concatenate([real, tail])


def _trunk_flat(x_groups, meta, norm_mix, norm_ffn, norm_final,
                a_wqkv, a_wo, a_sink,
                b_wdkv, b_qnorm, b_kvnorm, b_wuq, b_wukv, b_wo,
                f_wg, f_wu, f_wd,
                m_router, m_wg, m_wu, m_wd):
    seq = x_groups[0].shape[1]
    batches = [x.shape[0] for x in x_groups]
    batch = sum(batches)
    n_real = batch * seq
    n_meta_rows = batch * N_META
    big = _pick_tile(seq, (1024, 512, 256, 128))
    tp = min(big, 512)
    ntot = -(-(n_real + n_meta_rows) // big) * big
    n_tail = ntot - n_real

    tail = jnp.concatenate([
        jnp.broadcast_to(meta.astype(F32)[None], (batch, N_META, D_MODEL)).reshape(n_meta_rows, D_MODEL),
        jnp.zeros((n_tail - n_meta_rows, D_MODEL), F32)], axis=0)
    h_sources = [x.reshape(-1, D_MODEL) for x in x_groups] + [tail]

    pos = _positions(seq, n_tail, n_meta_rows)
    tabs_a = _rope_lane_tables(pos, GQA_THETA, GQA_ROT, GQA_HD, 0)
    tabs_b = _rope_lane_tables(pos, MLA_THETA, MLA_ROPE, LANES, MLA_NOPE)
    row = lambda v: v.reshape(1, -1).astype(F32)

    wqkv, wo_a = _prep_gqa(a_wqkv[0], a_wo[0])
    q, k, v = _qkv_call(h_sources, row(norm_mix[0]), wqkv, tabs_a, tm=tp, n_real=n_real, seq=seq)
    sink = a_sink[0].astype(F32) * LOG2E
    o = _win_attn_call(sink, q, k, v, batch=batch, seq=seq, n_real=n_real)
    o_tail = _win_meta_call(sink, q, k, v, batch=batch, seq=seq, n_real=n_real)
    h, xn = _wo_call(o, o_tail, wo_a, h_sources, row(norm_ffn[0]), tm=big)
    fdim = f_wg.shape[2]
    tf = _pick_tile(fdim, (512, 256, 128))
    h, xn = _ffn_dense_call(xn, f_wg[0].astype(BF16), f_wu[0].astype(BF16), f_wd[0].astype(BF16),
                            h, row(norm_mix[1]), tm=tp, tf=tf)

    wd, wuq, wuk, wuv = _prep_mla(b_wdkv[0], b_wuq[0], b_wukv[0])
    q, k, vt = _mla_proj_call(xn, wd, row(b_qnorm[0]), row(b_kvnorm[0]), wuq, wuk, wuv, tabs_b,
                              tm=tp, n_real=n_real, seq=seq)
    tq = _pick_tile(seq, (512, 256, 128))
    o = _mla_attn_call(q, k, vt, batch=batch, seq=seq, n_real=n_real, tq=tq)
    o_tail = _mla_meta_call(q, k, vt, batch=batch, seq=seq, n_real=n_real)
    tr = tp
    h, xpk, route, counts = _wo_router_call(o, o_tail, b_wo[0].astype(BF16), h, row(norm_ffn[1]),
                                            _prep_router(m_router[0]), tm=tr)

    tmx = big
    counts = counts[0, :N_EXPERTS].astype(I32)
    padded = ((counts + tmx - 1) // tmx) * tmx
    pend = jnp.cumsum(padded)
    pstart = pend - padded
    e_idx = route[:, 0:2].astype(I32)
    rank = route[:, 2:4].astype(I32)
    dest = pstart[e_idx] + rank
    n_assign = 2 * ntot
    n_blocks = -(-n_assign // tmx) + N_EXPERTS
    rows = n_blocks * tmx
    block_e = jnp.clip(jnp.searchsorted(pend, jnp.arange(n_blocks, dtype=I32) * tmx, side="right"),
                       0, N_EXPERTS - 1).astype(I32)
    n_used = (pend[-1] // tmx).astype(I32).reshape(1)
    block_e = jnp.where(jnp.arange(n_blocks) < n_used[0], block_e, block_e[jnp.maximum(n_used[0] - 1, 0)])

    td = tp
    dest_t = dest.reshape(ntot // td, td, 2).transpose(0, 2, 1)
    xs = _dispatch_call(dest_t, xpk, jnp.zeros((rows * XPK_ROWS, LANES), U32), td=td)
    edim = m_wg.shape[3]
    tfx = _pick_tile(edim, (512, 256, 128))
    ys = _ffn_expert_call(block_e, n_used, xs, m_wg[0].astype(BF16), m_wu[0].astype(BF16),
                          m_wd[0].astype(BF16), tm=tmx, tf=tfx)

    outs = []
    tile0 = 0
    for b, x in zip(batches, x_groups):
        ntiles = b * seq // td
        y = _combine_call(dest_t, ys, h, route, row(norm_final), tc=td, tile0=tile0, ntiles=ntiles)
        outs.append(y.reshape(b, seq, D_MODEL))
        tile0 += ntiles
    return tuple(outs)


def kernel(x_prompt, x_sample, meta, norm_mix, norm_ffn, norm_final, a_wqkv, a_wo, a_sink, b_wdkv, b_qnorm, b_kvnorm, b_wuq, b_wukv, b_wo, f_wg, f_wu, f_wd, m_router, m_wg, m_wu, m_wd):
    assert x_prompt.shape[1] == x_sample.shape[1] and x_prompt.shape[1] % BLK == 0
    assert norm_mix.shape[0] == 2, "two layers: windowed GQA + dense FFN, then MLA + MoE"
    return _trunk_flat((x_prompt, x_sample), meta, norm_mix, norm_ffn, norm_final,
                       a_wqkv, a_wo, a_sink,
                       b_wdkv, b_qnorm, b_kvnorm, b_wuq, b_wukv, b_wo,
                       f_wg, f_wu, f_wd,
                       m_router, m_wg, m_wu, m_wd)
```

```python
import functools
import math

import jax
import jax.numpy as jnp
from jax import lax
from jax.experimental import pallas as pl
from jax.experimental.pallas import tpu as pltpu

F32 = jnp.float32
BF16 = jnp.bfloat16
U32 = jnp.uint32
I32 = jnp.int32

D_MODEL = 1024
N_META = 16
RMS_EPS = 1e-6
BLK = 128
GQA_HEADS = 16
GQA_KV = 4
GQA_GROUP = 4
GQA_HD = 64
GQA_ROT = 16
GQA_THETA = 500000.0
MLA_HEADS = 16
MLA_NOPE = 64
MLA_ROPE = 32
MLA_V = 64
MLA_Q_RANK = 384
MLA_KV_RANK = 256
MLA_THETA = 10000.0
MLA_QK = MLA_NOPE + MLA_ROPE
N_EXPERTS = 8
LOG2E = 1.4426950408889634

LANES = 128
VMEM_LIMIT = 56 * 1024 * 1024


def _params(sem, vmem=VMEM_LIMIT):
    return pltpu.CompilerParams(dimension_semantics=sem, vmem_limit_bytes=vmem)


def _rms(x, g):
    ms = jnp.mean(x * x, axis=-1, keepdims=True)
    return x * lax.rsqrt(ms + RMS_EPS) * g


def _dot(a, b):
    return jnp.dot(a, b, preferred_element_type=F32)


def _dot_nt(a, b):
    return lax.dot_general(a, b, (((1,), (1,)), ((), ())), preferred_element_type=F32)


def _log2(n):
    assert n & (n - 1) == 0, n
    return n.bit_length() - 1


def _pick_tile(n, candidates):
    for c in candidates:
        if n % c == 0:
            return c
    raise ValueError(f"no tile in {candidates} divides {n}")


def _rope_lane_tables(pos, theta, rot, period, offset):
    half = rot // 2
    inv = jnp.power(jnp.float32(theta), -(jnp.arange(0, rot, 2, dtype=F32) / rot))
    ang = pos[:, None] * inv[None, :]
    cos, sin = jnp.cos(ang), jnp.sin(ang)
    lane = jnp.arange(LANES)
    r = (lane % period) - offset
    is_x1 = (r >= 0) & (r < half)
    is_x2 = (r >= half) & (r < rot)
    f = jnp.clip(jnp.where(is_x2, r - half, r), 0, half - 1)
    cos_l = cos[:, f]
    sin_l = sin[:, f]
    c = jnp.where((is_x1 | is_x2)[None, :], cos_l, 1.0)
    s1 = jnp.where(is_x2[None, :], sin_l, 0.0)
    s2 = jnp.where(is_x1[None, :], -sin_l, 0.0)
    return c.astype(F32), s1.astype(F32), s2.astype(F32)


def _rope(x, c, s1, s2, half):
    return x * c + pltpu.roll(x, half, 1) * s1 + pltpu.roll(x, LANES - half, 1) * s2


def _source_specs(sources, tm, tile_of=lambda *grid: grid[0]):
    specs, start = [], 0
    for src in sources:
        n = src.shape[0] // tm
        specs.append(pl.BlockSpec(
            (tm, src.shape[1]),
            lambda *grid, start=start, n=n: (jnp.clip(tile_of(*grid) - start, 0, n - 1), 0)))
        start += n
    return specs


def _select_source(refs, src_rows, tm, tile=None):
    i = pl.program_id(0) if tile is None else tile
    ends, acc = [], 0
    for n in src_rows:
        acc += n // tm
        ends.append(acc)
    x = refs[-1][...]
    for r, end in reversed(list(zip(refs[:-1], ends[:-1]))):
        x = jnp.where(i < end, r[...], x)
    return x


def _qkv_kernel(*refs, n_src, src_rows, tm):
    h_refs = refs[:n_src]
    g_ref, w_ref, c_ref, s1_ref, s2_ref, q_ref, k_ref, v_ref = refs[n_src:]
    xn = _rms(_select_source(h_refs, src_rows, tm), g_ref[...]).astype(BF16)
    qkv = _dot(xn, w_ref[...])
    c, s1, s2 = c_ref[...], s1_ref[...], s2_ref[...]
    nq = GQA_HEADS * GQA_HD // LANES
    nk = GQA_KV
    for i in range(nq):
        q_ref[:, i * LANES:(i + 1) * LANES] = (_rope(
            qkv[:, i * LANES:(i + 1) * LANES], c, s1, s2, GQA_ROT // 2) * LOG2E).astype(BF16)
    for i in range(nk):
        lo = (nq + i) * LANES
        k_ref[:, i * LANES:(i + 1) * LANES] = _rope(
            qkv[:, lo:lo + LANES], c, s1, s2, GQA_ROT // 2).astype(BF16)
    v_ref[...] = qkv[:, (nq + nk) * LANES:].astype(BF16)


def _table_index(i, n_real_tiles, tiles_per_seq):
    return jnp.where(i < n_real_tiles, i % tiles_per_seq, tiles_per_seq + i - n_real_tiles)


def _qkv_call(h_sources, g, w, tabs, *, tm, n_real, seq):
    src_rows = tuple(s.shape[0] for s in h_sources)
    ntot = sum(src_rows)
    nrt, tps = n_real // tm, seq // tm
    tab_spec = pl.BlockSpec((tm, LANES), lambda i: (_table_index(i, nrt, tps), 0))
    return pl.pallas_call(
        functools.partial(_qkv_kernel, n_src=len(h_sources), src_rows=src_rows, tm=tm),
        grid=(ntot // tm,),
        in_specs=_source_specs(h_sources, tm) + [
            pl.BlockSpec((1, D_MODEL), lambda i: (0, 0)),
            pl.BlockSpec(w.shape, lambda i: (0, 0)),
            tab_spec, tab_spec, tab_spec,
        ],
        out_specs=[
            pl.BlockSpec((tm, 1024), lambda i: (i, 0)),
            pl.BlockSpec((tm, GQA_KV * LANES), lambda i: (i, 0)),
            pl.BlockSpec((tm, GQA_KV * LANES), lambda i: (i, 0)),
        ],
        out_shape=[
            jax.ShapeDtypeStruct((ntot, 1024), BF16),
            jax.ShapeDtypeStruct((ntot, GQA_KV * LANES), BF16),
            jax.ShapeDtypeStruct((ntot, GQA_KV * LANES), BF16),
        ],
        compiler_params=_params(("parallel",)),
        name="qkv_rope",
    )(*h_sources, g, w, *tabs)


def _gqa_attend(sink_ref, q_ref, o_ref, groups):
    def pad_of(bias, pieces):
        have = sum(p.shape[0] for p in pieces)
        return [jnp.zeros((bias.shape[1] - have, LANES), BF16)] if bias.shape[1] > have else []

    def scores(g, kv):
        rows, k_pieces, _, bias = groups[g]
        sl = slice(kv * LANES, (kv + 1) * LANES)
        kcat = jnp.concatenate([p[:, sl] for p in k_pieces] + pad_of(bias, k_pieces), axis=0)
        zero = jnp.zeros_like(kcat)
        lane_k = lax.broadcasted_iota(I32, kcat.shape, 1)
        lo = slice(2 * kv * LANES, (2 * kv + 1) * LANES)
        hi = slice((2 * kv + 1) * LANES, (2 * kv + 2) * LANES)
        qp = jnp.concatenate([q_ref[rows, lo], q_ref[rows, hi]], axis=0)
        return (_dot_nt(qp, jnp.where(lane_k < GQA_HD, kcat, zero)) + bias,
                _dot_nt(qp, jnp.where(lane_k >= GQA_HD, kcat, zero)) + bias)

    def softmax(g, kv, ss):
        nq = groups[g][3].shape[0] // 2
        row = lax.broadcasted_iota(I32, (2 * nq, 1), 0)
        out = []
        for par, s in enumerate(ss):
            head = kv * GQA_GROUP + par
            sink = jnp.where(row < nq, sink_ref[head], sink_ref[head + 2])
            m = jnp.maximum(jnp.max(s, axis=-1, keepdims=True), sink)
            p = jnp.exp2(s - m)
            den = jnp.sum(p, axis=-1, keepdims=True) + jnp.exp2(sink - m)
            out.append((p.astype(BF16), den))
        return out

    def weighted_values(g, kv, pd):
        rows, _, v_pieces, bias = groups[g]
        nq = bias.shape[0] // 2
        sl = slice(kv * LANES, (kv + 1) * LANES)
        vcat = jnp.concatenate([p[:, sl] for p in v_pieces] + pad_of(bias, v_pieces), axis=0)
        outs = [_dot(p, vcat) / den for p, den in pd]
        lane_o = lax.broadcasted_iota(I32, outs[0].shape, 1)
        o = jnp.where(lane_o < GQA_HD, outs[0], outs[1]).astype(BF16)
        o_ref[rows, 2 * kv * LANES:(2 * kv + 1) * LANES] = o[:nq]
        o_ref[rows, (2 * kv + 1) * LANES:(2 * kv + 2) * LANES] = o[nq:]

    units = [(g, kv) for g in range(len(groups)) for kv in range(GQA_KV)]
    nxt = scores(*units[0])
    pending = None
    for idx, unit in enumerate(units):
        ss = nxt
        if idx + 1 < len(units):
            nxt = scores(*units[idx + 1])
        pd = softmax(*unit, ss)
        if pending is not None:
            weighted_values(*pending)
        pending = (*unit, pd)
    weighted_values(*pending)


def _window_bias(prev_ok, next_ok):
    nkeys = 4 * BLK
    qi = lax.broadcasted_iota(I32, (2 * BLK, nkeys), 0) & (BLK - 1)
    c = lax.broadcasted_iota(I32, (2 * BLK, nkeys), 1)
    neg = jnp.float32(-jnp.inf)
    b_p = jnp.where((c >= qi) & prev_ok, 0.0, neg)
    b_n = jnp.where(((c - 2 * BLK) <= qi) & next_ok, 0.0, neg)
    b_m = jnp.where(c < 3 * BLK + N_META, 0.0, neg)
    return jnp.where(c < BLK, b_p, jnp.where(c < 2 * BLK, 0.0, jnp.where(c < 3 * BLK, b_n, b_m)))


def _win_attn_kernel(sink_ref, q_ref, *refs, nb, qb, n_src, src_rows):
    k_blk, km_ref = refs[:qb + 2], refs[qb + 2]
    v_blk, vm_ref = refs[qb + 3:2 * qb + 5], refs[2 * qb + 5]
    rest = refs[2 * qb + 6:]
    wo_ref, h_refs, g_ref = rest[0], rest[1:1 + n_src], rest[1 + n_src]
    hout_ref, xn_ref, o_scr = rest[2 + n_src:]
    j0 = pl.program_id(1) * qb
    groups = []
    for t in range(qb):
        bias = _window_bias(j0 + t > 0, j0 + t < nb - 1)
        groups.append((slice(t * BLK, (t + 1) * BLK),
                       (k_blk[t], k_blk[t + 1], k_blk[t + 2], km_ref),
                       (v_blk[t], v_blk[t + 1], v_blk[t + 2], vm_ref), bias))
    _gqa_attend(sink_ref, q_ref, o_scr, groups)
    tile = pl.program_id(0) * (nb // qb) + pl.program_id(1)
    hn = _dot(o_scr[...], wo_ref[...]) + _select_source(h_refs, src_rows, qb * BLK, tile)
    hout_ref[...] = hn
    xn_ref[...] = _rms(hn, g_ref[...]).astype(BF16)


def _win_attn_call(sink, q, k, v, wo, h_sources, g, *, batch, seq, n_real):
    nb = seq // BLK
    qb = _pick_tile(nb, (4, 2, 1))
    mrow = n_real // N_META
    kvw = GQA_KV * LANES
    tile_of = lambda b, j, s: b * (nb // qb) + j

    def qmap(b, j, s):
        return (tile_of(b, j, s), 0)

    def blk_spec(t):
        return pl.BlockSpec((BLK, kvw),
                            lambda b, j, s: (b * nb + jnp.clip(j * qb + t - 1, 0, nb - 1), 0))

    meta_spec = pl.BlockSpec((N_META, kvw), lambda b, j, s: (mrow + b, 0))
    kv_specs = [blk_spec(t) for t in range(qb + 2)] + [meta_spec]
    fix = lambda b, j, s: (0, 0)
    grid_spec = pltpu.PrefetchScalarGridSpec(
        num_scalar_prefetch=1,
        grid=(batch, nb // qb),
        in_specs=[pl.BlockSpec((qb * BLK, 1024), qmap)] + kv_specs + kv_specs
        + [pl.BlockSpec((1024, D_MODEL), fix)] + _source_specs(h_sources, qb * BLK, tile_of)
        + [pl.BlockSpec((1, D_MODEL), fix)],
        out_specs=[pl.BlockSpec((qb * BLK, D_MODEL), qmap), pl.BlockSpec((qb * BLK, D_MODEL), qmap)],
        scratch_shapes=[pltpu.VMEM((qb * BLK, 1024), BF16)],
    )
    n_kv = qb + 3
    return pl.pallas_call(
        functools.partial(_win_attn_kernel, nb=nb, qb=qb, n_src=len(h_sources),
                          src_rows=tuple(s.shape[0] for s in h_sources)),
        grid_spec=grid_spec,
        out_shape=[jax.ShapeDtypeStruct((n_real, D_MODEL), F32),
                   jax.ShapeDtypeStruct((n_real, D_MODEL), BF16)],
        compiler_params=_params(("parallel", "parallel")),
        name="win_attn_wo",
    )(sink, q, *([k] * n_kv), *([v] * n_kv), wo, *h_sources, g)


def _win_meta_kernel(sink_ref, q_ref, km_ref, kc_ref, vm_ref, vc_ref, o_ref, *, batch):
    i = pl.program_id(0)

    @pl.when(i < batch)
    def _():
        nkeys = 2 * BLK
        qp = lax.broadcasted_iota(I32, (2 * N_META, nkeys), 0) & (N_META - 1)
        c = lax.broadcasted_iota(I32, (2 * N_META, nkeys), 1)
        neg = jnp.float32(-jnp.inf)
        b_c = jnp.where((N_META + c - qp) <= BLK, 0.0, neg)
        bias = jnp.where(c < BLK, b_c, jnp.where(c < BLK + N_META, 0.0, neg))
        _gqa_attend(sink_ref, q_ref, o_ref,
                    [(slice(0, N_META), (kc_ref, km_ref), (vc_ref, vm_ref), bias)])

    @pl.when(i >= batch)
    def _():
        o_ref[...] = jnp.zeros_like(o_ref)


def _win_meta_call(sink, q, k, v, *, batch, seq, n_real):
    ntot = q.shape[0]
    nb = seq // BLK
    mrow = n_real // N_META
    ntail_blocks = (ntot - n_real) // N_META

    def tmap(i, s):
        return (mrow + i, 0)

    def cmap(i, s):
        return (jnp.minimum(i, batch - 1) * nb, 0)

    grid_spec = pltpu.PrefetchScalarGridSpec(
        num_scalar_prefetch=1,
        grid=(ntail_blocks,),
        in_specs=[
            pl.BlockSpec((N_META, 1024), tmap),
            pl.BlockSpec((N_META, GQA_KV * LANES), tmap),
            pl.BlockSpec((BLK, GQA_KV * LANES), cmap),
            pl.BlockSpec((N_META, GQA_KV * LANES), tmap),
            pl.BlockSpec((BLK, GQA_KV * LANES), cmap),
        ],
        out_specs=pl.BlockSpec((N_META, 1024), lambda i, s: (i, 0)),
    )
    return pl.pallas_call(
        functools.partial(_win_meta_kernel, batch=batch),
        grid_spec=grid_spec,
        out_shape=jax.ShapeDtypeStruct((ntot - n_real, 1024), BF16),
        compiler_params=_params(("parallel",)),
        name="win_attn_meta",
    )(sink, q, k, k, v, v)


def _attn_out_tile(o_real_ref, o_tail_ref, n_real_tiles):
    is_real = pl.program_id(0) < n_real_tiles
    return jnp.where(is_real, o_real_ref[...], o_tail_ref[...])


def _attn_out_specs(tm, n_real_tiles):
    return [
        pl.BlockSpec((tm, 1024), lambda i: (jnp.minimum(i, n_real_tiles - 1), 0)),
        pl.BlockSpec((tm, 1024), lambda i: (jnp.maximum(i - n_real_tiles, 0), 0)),
    ]


def _wo_kernel(o_ref, w_ref, h_ref, g_ref, hout_ref, xn_ref):
    hn = _dot(o_ref[...], w_ref[...]) + h_ref[...]
    hout_ref[...] = hn
    xn_ref[...] = _rms(hn, g_ref[...]).astype(BF16)


def _wo_call(o, w, h, g, *, tm):
    n = o.shape[0]
    row = lambda i: (i, 0)
    fix = lambda i: (0, 0)
    return pl.pallas_call(
        _wo_kernel,
        grid=(n // tm,),
        in_specs=[
            pl.BlockSpec((tm, 1024), row),
            pl.BlockSpec((1024, D_MODEL), fix),
            pl.BlockSpec((tm, D_MODEL), row),
            pl.BlockSpec((1, D_MODEL), fix),
        ],
        out_specs=[pl.BlockSpec((tm, D_MODEL), row), pl.BlockSpec((tm, D_MODEL), row)],
        out_shape=[jax.ShapeDtypeStruct((n, D_MODEL), F32),
                   jax.ShapeDtypeStruct((n, D_MODEL), BF16)],
        compiler_params=_params(("parallel",)),
        name="wo_res_norm_tail",
    )(o, w, h, g)


XPK_ROWS = D_MODEL // 2 // LANES
YS_ROWS = D_MODEL // LANES


ROW_SPLIT = 2
MLA_PROJ_SPLIT = 4


def _store_token_major(ref, x, row0=0):
    t, n = x.shape
    k = n // LANES
    for s in range(k):
        ref[pl.ds(row0 * k + s, t, stride=k), :] = x[:, s * LANES:(s + 1) * LANES]


def _load_token_major(ref, k):
    t = ref.shape[0] // k
    return [ref[pl.ds(s, t, stride=k), :] for s in range(k)]


def _pack_bf16_pair(a, b):
    ra = pltpu.bitcast(a.astype(BF16).astype(F32), U32)
    rb = pltpu.bitcast(b.astype(BF16).astype(F32), U32)
    return ra | (rb >> 16)


def _unpack_bf16_pair(p):
    a = pltpu.bitcast(p & jnp.uint32(0xFFFF0000), F32).astype(BF16)
    b = pltpu.bitcast(p << 16, F32).astype(BF16)
    return a, b


def _wo_router_kernel(o_real_ref, o_tail_ref, w_ref, h_ref, g_ref, r_ref,
                      hout_ref, xpk_ref, route_ref, cnt_ref, carry_ref, *, n_real_tiles):
    i = pl.program_id(0)
    tm = h_ref.shape[0]

    @pl.when(i == 0)
    def _():
        carry_ref[...] = jnp.zeros_like(carry_ref)

    o = _attn_out_tile(o_real_ref, o_tail_ref, n_real_tiles)
    th = tm // ROW_SPLIT
    parts = [slice(p * th, (p + 1) * th) for p in range(ROW_SPLIT)]
    half = D_MODEL // 2
    r2 = r_ref[...]
    lane = lax.broadcasted_iota(I32, (th, LANES), 1)
    rr = lax.broadcasted_iota(I32, (th, th), 0)
    cc = lax.broadcasted_iota(I32, (th, th), 1)
    lower = jnp.where(cc < rr, 1.0, 0.0).astype(BF16)

    ys = []
    for rs in parts:
        hn = _dot(o[rs], w_ref[...]) + h_ref[rs, :]
        hout_ref[rs, :] = hn
        ys.append(_rms(hn, g_ref[...]))
    logits = []
    for p, y in enumerate(ys):
        _store_token_major(xpk_ref, _pack_bf16_pair(y[:, :half], y[:, half:]), row0=p * th)
        yhi = y.astype(BF16)
        ylo = (y - yhi.astype(F32)).astype(BF16)
        t = _dot(yhi, r2)
        logits.append(t[:, :LANES] + t[:, LANES:] + _dot(ylo, r2[:, :LANES]))
    carry = carry_ref[0:1, :]
    for rs, lg in zip(parts, logits):
        lm = jnp.where(lane < N_EXPERTS, lg, -jnp.inf)
        m1 = jnp.max(lm, axis=-1, keepdims=True)
        i1 = jnp.min(jnp.where(lm == m1, lane, LANES), axis=-1, keepdims=True)
        lm2 = jnp.where(lane == i1, -jnp.inf, lm)
        m2 = jnp.max(lm2, axis=-1, keepdims=True)
        i2 = jnp.min(jnp.where(lm2 == m2, lane, LANES), axis=-1, keepdims=True)
        e21 = jnp.exp(m2 - m1)
        g0 = 1.0 / (1.0 + e21)
        g1 = e21 / (1.0 + e21)
        onehot = ((lane == i1) | (lane == i2)).astype(F32)
        before = _dot(lower, onehot.astype(BF16)) + carry
        rank1 = jnp.sum(jnp.where(lane == i1, before, 0.0), axis=-1, keepdims=True)
        rank2 = jnp.sum(jnp.where(lane == i2, before, 0.0), axis=-1, keepdims=True)
        carry = carry + jnp.sum(onehot, axis=0, keepdims=True)
        out = jnp.where(lane == 0, i1.astype(F32), 0.0)
        out = jnp.where(lane == 1, i2.astype(F32), out)
        out = jnp.where(lane == 2, rank1, out)
        out = jnp.where(lane == 3, rank2, out)
        out = jnp.where(lane == 4, g0, out)
        out = jnp.where(lane == 5, g1, out)
        route_ref[rs, :] = out
    carry_ref[...] = jnp.broadcast_to(carry, carry_ref.shape)
    cnt_ref[...] = jnp.broadcast_to(carry, cnt_ref.shape)


def _wo_router_call(o_real, o_tail, w, h, g, r2, *, tm):
    ntot = h.shape[0]
    nrt = o_real.shape[0] // tm
    row = lambda i: (i, 0)
    fix = lambda i: (0, 0)
    return pl.pallas_call(
        functools.partial(_wo_router_kernel, n_real_tiles=nrt),
        grid=(ntot // tm,),
        in_specs=_attn_out_specs(tm, nrt) + [
            pl.BlockSpec((1024, D_MODEL), fix),
            pl.BlockSpec((tm, D_MODEL), row),
            pl.BlockSpec((1, D_MODEL), fix),
            pl.BlockSpec((D_MODEL, 2 * LANES), fix),
        ],
        out_specs=[
            pl.BlockSpec((tm, D_MODEL), row),
            pl.BlockSpec((tm * XPK_ROWS, LANES), row),
            pl.BlockSpec((tm, LANES), row),
            pl.BlockSpec((8, LANES), fix),
        ],
        out_shape=[
            jax.ShapeDtypeStruct((ntot, D_MODEL), F32),
            jax.ShapeDtypeStruct((ntot * XPK_ROWS, LANES), U32),
            jax.ShapeDtypeStruct((ntot, LANES), F32),
            jax.ShapeDtypeStruct((8, LANES), F32),
        ],
        scratch_shapes=[pltpu.VMEM((8, LANES), F32)],
        compiler_params=_params(("arbitrary",)),
        name="wo_res_norm_router",
    )(o_real, o_tail, w, h, g, r2)


def _swiglu_chunk(x, wg, wu, wd):
    g = _dot(x, wg)
    u = _dot(x, wu)
    a = (g * (1.0 / (1.0 + jnp.exp(-g)))) * u
    return _dot(a.astype(BF16), wd)


def _ffn_dense_kernel(*refs, n_src, src_rows, tm, tf):
    x_refs, h_refs = refs[:n_src], refs[n_src:2 * n_src]
    wg_ref, wu_ref, wd_ref, g_ref, hout_ref, xn_ref = refs[2 * n_src:]
    x = _select_source(x_refs, src_rows, tm)
    hn = _select_source(h_refs, src_rows, tm)
    for c in range(wg_ref.shape[1] // tf):
        cs = slice(c * tf, (c + 1) * tf)
        hn = hn + _swiglu_chunk(x, wg_ref[:, cs], wu_ref[:, cs], wd_ref[cs, :])
    hout_ref[...] = hn
    xn_ref[...] = _rms(hn, g_ref[...]).astype(BF16)


def _ffn_dense_call(x_sources, wg, wu, wd, h_sources, g, *, tm, tf):
    src_rows = tuple(s.shape[0] for s in h_sources)
    ntot = sum(src_rows)
    fdim = wg.shape[1]
    row = lambda i: (i, 0)
    fix = lambda i: (0, 0)
    once = pl.Buffered(1)
    return pl.pallas_call(
        functools.partial(_ffn_dense_kernel, n_src=len(h_sources), src_rows=src_rows, tm=tm, tf=tf),
        grid=(ntot // tm,),
        in_specs=_source_specs(x_sources, tm) + _source_specs(h_sources, tm) + [
            pl.BlockSpec((D_MODEL, fdim), fix, pipeline_mode=once),
            pl.BlockSpec((D_MODEL, fdim), fix, pipeline_mode=once),
            pl.BlockSpec((fdim, D_MODEL), fix, pipeline_mode=once),
            pl.BlockSpec((1, D_MODEL), fix),
        ],
        out_specs=[pl.BlockSpec((tm, D_MODEL), row), pl.BlockSpec((tm, D_MODEL), row)],
        out_shape=[jax.ShapeDtypeStruct((ntot, D_MODEL), F32),
                   jax.ShapeDtypeStruct((ntot, D_MODEL), BF16)],
        compiler_params=_params(("parallel",)),
        name="ffn_dense",
    )(*x_sources, *h_sources, wg, wu, wd, g)


def _ffn_expert_kernel(be_ref, nu_ref, xs_ref, wg_ref, wu_ref, wd_ref, ys_ref, *, tf):
    del be_ref
    used = pl.program_id(0) < nu_ref[0]

    @pl.when(used)
    def _():
        halves = [_unpack_bf16_pair(blk) for blk in _load_token_major(xs_ref, XPK_ROWS)]
        x = jnp.concatenate([a for a, _ in halves] + [b for _, b in halves], axis=1)
        acc = None
        for c in range(wg_ref.shape[1] // tf):
            cs = slice(c * tf, (c + 1) * tf)
            y = _swiglu_chunk(x, wg_ref[:, cs], wu_ref[:, cs], wd_ref[cs, :])
            acc = y if acc is None else acc + y
        _store_token_major(ys_ref, acc)

    @pl.when(jnp.logical_not(used))
    def _():
        ys_ref[...] = jnp.zeros_like(ys_ref)


def _ffn_expert_call(block_e, n_used, xs, wg, wu, wd, *, tm, tf):
    rows = xs.shape[0] // XPK_ROWS
    once = pl.Buffered(1)
    wmap = lambda i, be, nu: (be[i], 0, 0)
    grid_spec = pltpu.PrefetchScalarGridSpec(
        num_scalar_prefetch=2,
        grid=(rows // tm,),
        in_specs=[
            pl.BlockSpec((tm * XPK_ROWS, LANES), lambda i, be, nu: (i, 0)),
            pl.BlockSpec((None,) + wg.shape[1:], wmap, pipeline_mode=once),
            pl.BlockSpec((None,) + wu.shape[1:], wmap, pipeline_mode=once),
            pl.BlockSpec((None,) + wd.shape[1:], wmap, pipeline_mode=once),
        ],
        out_specs=pl.BlockSpec((tm * YS_ROWS, LANES), lambda i, be, nu: (i, 0)),
    )
    return pl.pallas_call(
        functools.partial(_ffn_expert_kernel, tf=tf),
        grid_spec=grid_spec,
        out_shape=jax.ShapeDtypeStruct((rows * YS_ROWS, LANES), F32),
        compiler_params=_params(("arbitrary",)),
        name="ffn_expert",
    )(block_e, n_used, xs, wg, wu, wd)


DMA_UNROLL = 8


def _token_copy(src, src_tok, dst, dst_tok, rows, sem):
    return pltpu.make_async_copy(src.at[pl.ds(pl.multiple_of(src_tok * rows, rows), rows)],
                                 dst.at[pl.ds(pl.multiple_of(dst_tok * rows, rows), rows)], sem)


def _dispatch_kernel(dest_ref, x_ref, xs_in_ref, xs_ref, sem):
    del xs_in_ref
    td = x_ref.shape[0] // XPK_ROWS

    def issue(t, carry):
        for s in range(2):
            _token_copy(x_ref, t, xs_ref, dest_ref[0, s, t], XPK_ROWS, sem).start(priority=s)
        return carry

    lax.fori_loop(0, td, issue, 0, unroll=DMA_UNROLL)

    def drain(t, carry):
        for s in range(2):
            _token_copy(x_ref, 0, xs_ref, 0, XPK_ROWS, sem).wait()
        return carry

    lax.fori_loop(0, td, drain, 0, unroll=DMA_UNROLL)


def _dispatch_call(dest, xpk, xs_init, *, td):
    ntot = xpk.shape[0] // XPK_ROWS
    return pl.pallas_call(
        _dispatch_kernel,
        grid=(ntot // td,),
        in_specs=[
            pl.BlockSpec((1, 2, td), lambda i: (i, 0, 0), memory_space=pltpu.SMEM),
            pl.BlockSpec((td * XPK_ROWS, LANES), lambda i: (i, 0)),
            pl.BlockSpec(memory_space=pl.ANY),
        ],
        out_specs=pl.BlockSpec(memory_space=pl.ANY),
        out_shape=jax.ShapeDtypeStruct(xs_init.shape, U32),
        input_output_aliases={2: 0},
        scratch_shapes=[pltpu.SemaphoreType.DMA(())],
        compiler_params=_params(("arbitrary",)),
        name="moe_dispatch",
    )(dest, xpk, xs_init)


def _combine_kernel(dest_ref, dest_next_ref, ys_ref, h_ref, route_ref, g_ref, out_ref,
                    y0_ref, y1_ref, sems):
    i = pl.program_id(0)
    n = pl.num_programs(0)
    tc = h_ref.shape[0]
    bufs = (y0_ref, y1_ref)
    slot = i % 2

    def issue(d_ref, sl):
        def body(t, carry):
            for s in range(2):
                _token_copy(ys_ref, d_ref[0, s, t], bufs[s].at[sl], t, YS_ROWS,
                            sems.at[sl]).start(priority=s)
            return carry

        lax.fori_loop(0, tc, body, 0, unroll=DMA_UNROLL)

    @pl.when(i == 0)
    def _():
        issue(dest_ref, 0)

    @pl.when(i + 1 < n)
    def _():
        issue(dest_next_ref, 1 - slot)

    def drain(t, carry):
        for s in range(2):
            _token_copy(ys_ref, 0, bufs[s].at[slot], 0, YS_ROWS, sems.at[slot]).wait()
        return carry

    lax.fori_loop(0, tc, drain, 0, unroll=DMA_UNROLL)

    r = route_ref[...]
    lane = lax.broadcasted_iota(I32, r.shape, 1)
    g0 = jnp.sum(jnp.where(lane == 4, r, 0.0), axis=-1, keepdims=True)
    g1 = jnp.sum(jnp.where(lane == 5, r, 0.0), axis=-1, keepdims=True)
    y0 = _load_token_major(y0_ref.at[slot], YS_ROWS)
    y1 = _load_token_major(y1_ref.at[slot], YS_ROWS)
    moe = jnp.concatenate([a * g0 + b * g1 for a, b in zip(y0, y1)], axis=1)
    out_ref[...] = _rms(h_ref[...] + moe, g_ref[...])


def _combine_call(dest, ys, h, route, g, *, tc, tile0, ntiles):
    return pl.pallas_call(
        _combine_kernel,
        grid=(ntiles,),
        in_specs=[
            pl.BlockSpec((1, 2, tc), lambda i: (tile0 + i, 0, 0), memory_space=pltpu.SMEM),
            pl.BlockSpec((1, 2, tc), lambda i: (tile0 + jnp.minimum(i + 1, ntiles - 1), 0, 0),
                         memory_space=pltpu.SMEM),
            pl.BlockSpec(memory_space=pl.ANY),
            pl.BlockSpec((tc, D_MODEL), lambda i: (tile0 + i, 0)),
            pl.BlockSpec((tc, LANES), lambda i: (tile0 + i, 0)),
            pl.BlockSpec((1, D_MODEL), lambda i: (0, 0)),
        ],
        out_specs=pl.BlockSpec((tc, D_MODEL), lambda i: (i, 0)),
        out_shape=jax.ShapeDtypeStruct((ntiles * tc, D_MODEL), F32),
        scratch_shapes=[pltpu.VMEM((2, tc * YS_ROWS, LANES), F32), pltpu.VMEM((2, tc * YS_ROWS, LANES), F32),
                        pltpu.SemaphoreType.DMA((2,))],
        compiler_params=_params(("arbitrary",)),
        name="moe_combine_norm",
    )(dest, dest, ys, h, route, g)


def _mla_proj_kernel(x_ref, wd_ref, qn_ref, kvn_ref, wuq_ref, wuk_ref, wuvt_ref,
                     c_ref, s1_ref, s2_ref, q_ref, k_ref, vt_ref):
    tm = x_ref.shape[0]
    th = tm // MLA_PROJ_SPLIT
    parts = [slice(p * th, (p + 1) * th) for p in range(MLA_PROJ_SPLIT)]
    half = MLA_ROPE // 2
    scale = (MLA_QK ** -0.5) * LOG2E
    lats = [_dot(x_ref[rs, :], wd_ref[...]) for rs in parts]
    ups = []
    for lat in lats:
        cq = _rms(lat[:, :MLA_Q_RANK], qn_ref[...]).astype(BF16)
        ckv = _rms(lat[:, MLA_Q_RANK:MLA_Q_RANK + MLA_KV_RANK], kvn_ref[...]).astype(BF16)
        ups.append((_dot(cq, wuq_ref[...]),
                    _dot(ckv, wuk_ref[...]),
                    _dot_nt(wuvt_ref[...], ckv)))
    for rs, lat, (q, k, vt) in zip(parts, lats, ups):
        c, s1, s2 = c_ref[rs, :], s1_ref[rs, :], s2_ref[rs, :]
        kr = _rope(lat[:, MLA_Q_RANK + MLA_KV_RANK:], c, s1, s2, half)
        for h in range(MLA_HEADS):
            sl = slice(h * LANES, (h + 1) * LANES)
            q_ref[rs, sl] = (_rope(q[:, sl], c, s1, s2, half) * scale).astype(BF16)
            k_ref[rs, sl] = (k[:, sl] + kr).astype(BF16)
        vt_ref[:, rs] = vt.astype(BF16)


def _mla_proj_call(x, wd, qn, kvn, wuq, wuk, wuv, tabs, *, tm, n_real, seq):
    ntot = x.shape[0]
    nrt, tps = n_real // tm, seq // tm
    row = lambda i: (i, 0)
    fix = lambda i: (0, 0)
    tab_spec = pl.BlockSpec((tm, LANES), lambda i: (_table_index(i, nrt, tps), 0))
    return pl.pallas_call(
        _mla_proj_kernel,
        grid=(ntot // tm,),
        in_specs=[
            pl.BlockSpec((tm, D_MODEL), row),
            pl.BlockSpec(wd.shape, fix),
            pl.BlockSpec((1, MLA_Q_RANK), fix),
            pl.BlockSpec((1, MLA_KV_RANK), fix),
            pl.BlockSpec(wuq.shape, fix),
            pl.BlockSpec(wuk.shape, fix),
            pl.BlockSpec(wuv.shape[::-1], fix),
            tab_spec, tab_spec, tab_spec,
        ],
        out_specs=[
            pl.BlockSpec((tm, 2048), row),
            pl.BlockSpec((tm, 2048), row),
            pl.BlockSpec((1024, tm), lambda i: (0, i)),
        ],
        out_shape=[
            jax.ShapeDtypeStruct((ntot, 2048), BF16),
            jax.ShapeDtypeStruct((ntot, 2048), BF16),
            jax.ShapeDtypeStruct((1024, ntot), BF16),
        ],
        compiler_params=_params(("parallel",)),
        name="mla_proj",
    )(x, wd, qn, kvn, wuq, wuk, wuv.T, *tabs)


META_PER_BLOCK = LANES // N_META
REDUCE_ROWS = 64
SCORES_AHEAD = 1
VALUES_BEHIND = 1
MLA_QUERY_TILE = 256


def _col_reduce(x, op):
    rows, n = x.shape
    if rows > REDUCE_ROWS and rows % REDUCE_ROWS == 0:
        x = op(x.reshape(rows // REDUCE_ROWS, REDUCE_ROWS, n), axis=0)
    return op(x, axis=0, keepdims=True)


def _mla_attn_kernel(q_ref, kr_ref, kt_ref, vtr_ref, vtt_ref, o_ref, ot_ref):
    b = pl.program_id(0)
    r = lax.broadcasted_iota(I32, (LANES, 1), 0)
    mine = jnp.right_shift(r, _log2(N_META)) == (b % META_PER_BLOCK)
    mbias = jnp.where(mine, 0.0, -jnp.inf).astype(F32)
    tq = min(MLA_QUERY_TILE, q_ref.shape[0])
    units = [(g, h) for g in range(q_ref.shape[0] // tq) for h in range(MLA_HEADS)]

    def scores(g, h):
        hs = slice(h * LANES, (h + 1) * LANES)
        qh = q_ref[g * tq:(g + 1) * tq, hs]
        return _dot_nt(kr_ref[:, hs], qh), _dot_nt(kt_ref[:, hs], qh) + mbias

    def weighted_values(g, h, p_r, p_m, den):
        vs = slice(h * MLA_V, (h + 1) * MLA_V)
        acc = _dot(vtr_ref[vs, :], p_r) + _dot(vtt_ref[vs, :], p_m)
        ot_ref[vs, g * tq:(g + 1) * tq] = acc / den

    ahead = [scores(*u) for u in units[:SCORES_AHEAD]]
    pending = []
    for idx, unit in enumerate(units):
        s_r, s_m = ahead.pop(0)
        if idx + SCORES_AHEAD < len(units):
            ahead.append(scores(*units[idx + SCORES_AHEAD]))
        m = jnp.maximum(_col_reduce(s_r, jnp.max), _col_reduce(s_m, jnp.max))
        p_r = jnp.exp2(s_r - m)
        p_m = jnp.exp2(s_m - m)
        den = _col_reduce(p_r, jnp.sum) + _col_reduce(p_m, jnp.sum)
        pending.append((*unit, p_r.astype(BF16), p_m.astype(BF16), den))
        if len(pending) > VALUES_BEHIND:
            weighted_values(*pending.pop(0))
    for item in pending:
        weighted_values(*item)
    o_ref[...] = ot_ref[...].T.astype(BF16)


def _mla_attn_call(q, k, vt, *, batch, seq, n_real, tq):
    nq = seq // tq
    tail0 = n_real // LANES
    qmap = lambda b, j: (b * nq + j, 0)
    return pl.pallas_call(
        _mla_attn_kernel,
        grid=(batch, nq),
        in_specs=[
            pl.BlockSpec((tq, 2048), qmap),
            pl.BlockSpec((seq, 2048), lambda b, j: (b, 0)),
            pl.BlockSpec((LANES, 2048), lambda b, j: (tail0 + b // META_PER_BLOCK, 0)),
            pl.BlockSpec((1024, seq), lambda b, j: (0, b)),
            pl.BlockSpec((1024, LANES), lambda b, j: (0, tail0 + b // META_PER_BLOCK)),
        ],
        out_specs=pl.BlockSpec((tq, 1024), qmap),
        out_shape=jax.ShapeDtypeStruct((n_real, 1024), BF16),
        scratch_shapes=[pltpu.VMEM((1024, tq), F32)],
        compiler_params=_params(("parallel", "arbitrary")),
        name="mla_attn",
    )(q, k, k, vt, vt)


def _mla_meta_kernel(q_ref, kr_ref, kt_ref, vtr_ref, vtt_ref, o_ref, *, batch):
    i = pl.program_id(0)
    ncol = MLA_HEADS * N_META

    @pl.when(i < batch)
    def _():
        qi = lax.broadcasted_iota(I32, (N_META, ncol), 0)
        ci = lax.broadcasted_iota(I32, (N_META, ncol), 1)
        spread = jnp.where((ci & (N_META - 1)) == qi, 1.0, 0.0).astype(BF16)
        w = lax.dot_general(q_ref[...], spread, (((0,), (0,)), ((), ())),
                            preferred_element_type=F32)
        wr = lax.broadcasted_iota(I32, w.shape, 0)
        wc = lax.broadcasted_iota(I32, w.shape, 1)
        on_diag = jnp.right_shift(wr, _log2(LANES)) == jnp.right_shift(wc, _log2(N_META))
        w = jnp.where(on_diag, w, 0.0).astype(BF16)
        r = lax.broadcasted_iota(I32, (LANES, 1), 0)
        mine = jnp.right_shift(r, _log2(N_META)) == (i % META_PER_BLOCK)
        mbias = jnp.where(mine, 0.0, -jnp.inf).astype(F32)
        s_r = _dot(kr_ref[...], w)
        s_m = _dot(kt_ref[...], w) + mbias
        m = jnp.maximum(_col_reduce(s_r, jnp.max), _col_reduce(s_m, jnp.max))
        p_r = jnp.exp2(s_r - m)
        p_m = jnp.exp2(s_m - m)
        den = _col_reduce(p_r, jnp.sum) + _col_reduce(p_m, jnp.sum)
        ot = (_dot(vtr_ref[...], p_r.astype(BF16)) + _dot(vtt_ref[...], p_m.astype(BF16))) / den
        o_all = ot.T
        lane = lax.broadcasted_iota(I32, (N_META, 1024), 1)
        o = jnp.zeros((N_META, 1024), F32)
        for h in range(MLA_HEADS):
            rows = o_all[h * N_META:(h + 1) * N_META, :]
            o = jnp.where(jnp.right_shift(lane, _log2(MLA_V)) == h, rows, o)
        o_ref[...] = o.astype(BF16)

    @pl.when(i >= batch)
    def _():
        o_ref[...] = jnp.zeros_like(o_ref)


def _mla_meta_call(q, k, vt, *, batch, seq, n_real):
    ntot = q.shape[0]
    mrow = n_real // N_META
    tail0 = n_real // LANES
    ntail_blocks = (ntot - n_real) // N_META
    last = (batch - 1) // META_PER_BLOCK
    bmap = lambda i: jnp.minimum(i, batch - 1)
    tmap = lambda i: jnp.minimum(i // META_PER_BLOCK, last)
    return pl.pallas_call(
        functools.partial(_mla_meta_kernel, batch=batch),
        grid=(ntail_blocks,),
        in_specs=[
            pl.BlockSpec((N_META, 2048), lambda i: (mrow + i, 0)),
            pl.BlockSpec((seq, 2048), lambda i: (bmap(i), 0)),
            pl.BlockSpec((LANES, 2048), lambda i: (tail0 + tmap(i), 0)),
            pl.BlockSpec((1024, seq), lambda i: (0, bmap(i))),
            pl.BlockSpec((1024, LANES), lambda i: (0, tail0 + tmap(i))),
        ],
        out_specs=pl.BlockSpec((N_META, 1024), lambda i: (i, 0)),
        out_shape=jax.ShapeDtypeStruct((ntot - n_real, 1024), BF16),
        compiler_params=_params(("arbitrary",)),
        name="mla_attn_meta",
    )(q, k, k, vt, vt)


def _prep_gqa(a_wqkv, a_wo):
    d = a_wqkv.shape[0]
    nq = GQA_HEADS * GQA_HD
    nkv = GQA_KV * GQA_HD
    wq = a_wqkv[:, :nq] * (GQA_HD ** -0.5)

    def dup(w):
        w = w.reshape(d, GQA_KV, 1, GQA_HD)
        return jnp.broadcast_to(w, (d, GQA_KV, 2, GQA_HD)).reshape(d, 2 * nkv)

    w = jnp.concatenate([wq, dup(a_wqkv[:, nq:nq + nkv]), dup(a_wqkv[:, nq + nkv:])], axis=1)
    return w.astype(BF16), a_wo.astype(BF16)


def _prep_mla(b_wdkv, b_wuq, b_wukv):
    d = b_wdkv.shape[0]
    lat = MLA_Q_RANK + MLA_KV_RANK
    z = lambda n: jnp.zeros((d, n), F32)
    wd = jnp.concatenate([b_wdkv[:, :lat], z(MLA_NOPE), b_wdkv[:, lat:], z(LANES - MLA_QK)], axis=1)
    wuq = b_wuq.reshape(MLA_Q_RANK, MLA_HEADS, MLA_QK)
    wuq = jnp.pad(wuq, ((0, 0), (0, 0), (0, LANES - MLA_QK))).reshape(MLA_Q_RANK, MLA_HEADS * LANES)
    wukv = b_wukv.reshape(MLA_KV_RANK, MLA_HEADS, MLA_NOPE + MLA_V)
    wuk = jnp.pad(wukv[:, :, :MLA_NOPE], ((0, 0), (0, 0), (0, LANES - MLA_NOPE)))
    wuk = wuk.reshape(MLA_KV_RANK, MLA_HEADS * LANES)
    wuv = wukv[:, :, MLA_NOPE:].reshape(MLA_KV_RANK, MLA_HEADS * MLA_V)
    return wd.astype(BF16), wuq.astype(BF16), wuk.astype(BF16), wuv.astype(BF16)


def _prep_router(m_router):
    hi = m_router.astype(BF16)
    lo = (m_router - hi.astype(F32)).astype(BF16)
    pad = lambda r: jnp.pad(r, ((0, 0), (0, LANES - N_EXPERTS)))
    return jnp.concatenate([pad(hi), pad(lo)], axis=1)


def _positions(seq, n_tail, n_meta_rows):
    real = N_META + jnp.arange(seq, dtype=F32)
    t = jnp.arange(n_tail)
    tail = jnp.where(t < n_meta_rows, t % N_META, 0).astype(F32)
    return jnp.concatenate([real, tail])


def _trunk_flat(x_groups, meta, norm_mix, norm_ffn, norm_final,
                a_wqkv, a_wo, a_sink,
                b_wdkv, b_qnorm, b_kvnorm, b_wuq, b_wukv, b_wo,
                f_wg, f_wu, f_wd,
                m_router, m_wg, m_wu, m_wd):
    seq = x_groups[0].shape[1]
    batches = [x.shape[0] for x in x_groups]
    batch = sum(batches)
    n_real = batch * seq
    n_meta_rows = batch * N_META
    big = _pick_tile(seq, (1024, 512, 256, 128))
    tp = min(big, 512)
    ntot = -(-(n_real + n_meta_rows) // big) * big
    n_tail = ntot - n_real

    tail = jnp.concatenate([
        jnp.broadcast_to(meta.astype(F32)[None], (batch, N_META, D_MODEL)).reshape(n_meta_rows, D_MODEL),
        jnp.zeros((n_tail - n_meta_rows, D_MODEL), F32)], axis=0)
    h_sources = [x.reshape(-1, D_MODEL) for x in x_groups] + [tail]

    pos = _positions(seq, n_tail, n_meta_rows)
    tabs_a = _rope_lane_tables(pos, GQA_THETA, GQA_ROT, GQA_HD, 0)
    tabs_b = _rope_lane_tables(pos, MLA_THETA, MLA_ROPE, LANES, MLA_NOPE)
    row = lambda v: v.reshape(1, -1).astype(F32)

    wqkv, wo_a = _prep_gqa(a_wqkv[0], a_wo[0])
    q, k, v = _qkv_call(h_sources, row(norm_mix[0]), wqkv, tabs_a, tm=tp, n_real=n_real, seq=seq)
    sink = a_sink[0].astype(F32) * LOG2E
    h_real, xn_real = _win_attn_call(sink, q, k, v, wo_a, h_sources[:-1], row(norm_ffn[0]),
                                     batch=batch, seq=seq, n_real=n_real)
    o_tail = _win_meta_call(sink, q, k, v, batch=batch, seq=seq, n_real=n_real)
    h_tail, xn_tail = _wo_call(o_tail, wo_a, tail, row(norm_ffn[0]), tm=tp)
    fdim = f_wg.shape[2]
    tf = _pick_tile(fdim, (512, 256, 128))
    h, xn = _ffn_dense_call([xn_real, xn_tail], f_wg[0].astype(BF16), f_wu[0].astype(BF16),
                            f_wd[0].astype(BF16), [h_real, h_tail], row(norm_mix[1]), tm=tp, tf=tf)

    wd, wuq, wuk, wuv = _prep_mla(b_wdkv[0], b_wuq[0], b_wukv[0])
    q, k, vt = _mla_proj_call(xn, wd, row(b_qnorm[0]), row(b_kvnorm[0]), wuq, wuk, wuv, tabs_b,
                              tm=tp, n_real=n_real, seq=seq)
    tq = _pick_tile(seq, (512, 256, 128))
    o = _mla_attn_call(q, k, vt, batch=batch, seq=seq, n_real=n_real, tq=tq)
    o_tail = _mla_meta_call(q, k, vt, batch=batch, seq=seq, n_real=n_real)
    tr = tp
    h, xpk, route, counts = _wo_router_call(o, o_tail, b_wo[0].astype(BF16), h, row(norm_ffn[1]),
                                            _prep_router(m_router[0]), tm=tr)

    tmx = big
    counts = counts[0, :N_EXPERTS].astype(I32)
    padded = ((counts + tmx - 1) // tmx) * tmx
    pend = jnp.cumsum(padded)
    pstart = pend - padded
    e_idx = route[:, 0:2].astype(I32)
    rank = route[:, 2:4].astype(I32)
    dest = pstart[e_idx] + rank
    n_assign = 2 * ntot
    n_blocks = -(-n_assign // tmx) + N_EXPERTS
    rows = n_blocks * tmx
    block_e = jnp.clip(jnp.searchsorted(pend, jnp.arange(n_blocks, dtype=I32) * tmx, side="right"),
                       0, N_EXPERTS - 1).astype(I32)
    n_used = (pend[-1] // tmx).astype(I32).reshape(1)
    block_e = jnp.where(jnp.arange(n_blocks) < n_used[0], block_e, block_e[jnp.maximum(n_used[0] - 1, 0)])

    td = tp
    dest_t = dest.reshape(ntot // td, td, 2).transpose(0, 2, 1)
    xs = _dispatch_call(dest_t, xpk, jnp.zeros((rows * XPK_ROWS, LANES), U32), td=td)
    edim = m_wg.shape[3]
    tfx = _pick_tile(edim, (512, 256, 128))
    ys = _ffn_expert_call(block_e, n_used, xs, m_wg[0].astype(BF16), m_wu[0].astype(BF16),
                          m_wd[0].astype(BF16), tm=tmx, tf=tfx)

    outs = []
    tile0 = 0
    for b, x in zip(batches, x_groups):
        ntiles = b * seq // td
        y = _combine_call(dest_t, ys, h, route, row(norm_final), tc=td, tile0=tile0, ntiles=ntiles)
        outs.append(y.reshape(b, seq, D_MODEL))
        tile0 += ntiles
    return tuple(outs)


def kernel(x_prompt, x_sample, meta, norm_mix, norm_ffn, norm_final, a_wqkv, a_wo, a_sink, b_wdkv, b_qnorm, b_kvnorm, b_wuq, b_wukv, b_wo, f_wg, f_wu, f_wd, m_router, m_wg, m_wu, m_wd):
    assert x_prompt.shape[1] == x_sample.shape[1] and x_prompt.shape[1] % BLK == 0
    assert norm_mix.shape[0] == 2, "two layers: windowed GQA + dense FFN, then MLA + MoE"
    return _trunk_flat((x_prompt, x_sample), meta, norm_mix, norm_ffn, norm_final,
                       a_wqkv, a_wo, a_sink,
                       b_wdkv, b_qnorm, b_kvnorm, b_wuq, b_wukv, b_wo,
                       f_wg, f_wu, f_wd,
                       m_router, m_wg, m_wu, m_wd)
```

```python
import functools
import math

import jax
import jax.numpy as jnp
from jax import lax
from jax.experimental import pallas as pl
from jax.experimental.pallas import tpu as pltpu

F32 = jnp.float32
BF16 = jnp.bfloat16
U32 = jnp.uint32
I32 = jnp.int32

D_MODEL = 1024
N_META = 16
RMS_EPS = 1e-6
BLK = 128
GQA_HEADS = 16
GQA_KV = 4
GQA_GROUP = 4
GQA_HD = 64
GQA_ROT = 16
GQA_THETA = 500000.0
MLA_HEADS = 16
MLA_NOPE = 64
MLA_ROPE = 32
MLA_V = 64
MLA_Q_RANK = 384
MLA_KV_RANK = 256
MLA_THETA = 10000.0
MLA_QK = MLA_NOPE + MLA_ROPE
N_EXPERTS = 8
LOG2E = 1.4426950408889634

LANES = 128
VMEM_LIMIT = 56 * 1024 * 1024


def _params(sem, vmem=VMEM_LIMIT):
    return pltpu.CompilerParams(dimension_semantics=sem, vmem_limit_bytes=vmem)


def _rms(x, g):
    ms = jnp.mean(x * x, axis=-1, keepdims=True)
    return x * lax.rsqrt(ms + RMS_EPS) * g


def _dot(a, b):
    return jnp.dot(a, b, preferred_element_type=F32)


def _dot_nt(a, b):
    return lax.dot_general(a, b, (((1,), (1,)), ((), ())), preferred_element_type=F32)


def _log2(n):
    assert n & (n - 1) == 0, n
    return n.bit_length() - 1


def _pick_tile(n, candidates):
    for c in candidates:
        if n % c == 0:
            return c
    raise ValueError(f"no tile in {candidates} divides {n}")


def _rope_lane_tables(pos, theta, rot, period, offset):
    half = rot // 2
    inv = jnp.power(jnp.float32(theta), -(jnp.arange(0, rot, 2, dtype=F32) / rot))
    ang = pos[:, None] * inv[None, :]
    cos, sin = jnp.cos(ang), jnp.sin(ang)
    lane = jnp.arange(LANES)
    r = (lane % period) - offset
    is_x1 = (r >= 0) & (r < half)
    is_x2 = (r >= half) & (r < rot)
    f = jnp.clip(jnp.where(is_x2, r - half, r), 0, half - 1)
    cos_l = cos[:, f]
    sin_l = sin[:, f]
    c = jnp.where((is_x1 | is_x2)[None, :], cos_l, 1.0)
    s1 = jnp.where(is_x2[None, :], sin_l, 0.0)
    s2 = jnp.where(is_x1[None, :], -sin_l, 0.0)
    return c.astype(F32), s1.astype(F32), s2.astype(F32)


def _rope(x, c, s1, s2, half):
    return x * c + pltpu.roll(x, half, 1) * s1 + pltpu.roll(x, LANES - half, 1) * s2


def _source_specs(sources, tm, tile_of=lambda *grid: grid[0]):
    specs, start = [], 0
    for src in sources:
        n = src.shape[0] // tm
        specs.append(pl.BlockSpec(
            (tm, src.shape[1]),
            lambda *grid, start=start, n=n: (jnp.clip(tile_of(*grid) - start, 0, n - 1), 0)))
        start += n
    return specs


def _select_source(refs, src_rows, tm, tile=None):
    i = pl.program_id(0) if tile is None else tile
    ends, acc = [], 0
    for n in src_rows:
        acc += n // tm
        ends.append(acc)
    x = refs[-1][...]
    for r, end in reversed(list(zip(refs[:-1], ends[:-1]))):
        x = jnp.where(i < end, r[...], x)
    return x


def _qkv_kernel(*refs, n_src, src_rows, tm):
    h_refs = refs[:n_src]
    g_ref, w_ref, c_ref, s1_ref, s2_ref, q_ref, k_ref, v_ref = refs[n_src:]
    xn = _rms(_select_source(h_refs, src_rows, tm), g_ref[...]).astype(BF16)
    qkv = _dot(xn, w_ref[...])
    c, s1, s2 = c_ref[...], s1_ref[...], s2_ref[...]
    nq = GQA_HEADS * GQA_HD // LANES
    nk = GQA_KV
    for i in range(nq):
        q_ref[:, i * LANES:(i + 1) * LANES] = (_rope(
            qkv[:, i * LANES:(i + 1) * LANES], c, s1, s2, GQA_ROT // 2) * LOG2E).astype(BF16)
    for i in range(nk):
        lo = (nq + i) * LANES
        k_ref[:, i * LANES:(i + 1) * LANES] = _rope(
            qkv[:, lo:lo + LANES], c, s1, s2, GQA_ROT // 2).astype(BF16)
    v_ref[...] = qkv[:, (nq + nk) * LANES:].astype(BF16)


def _table_index(i, n_real_tiles, tiles_per_seq):
    return jnp.where(i < n_real_tiles, i % tiles_per_seq, tiles_per_seq + i - n_real_tiles)


def _qkv_call(h_sources, g, w, tabs, *, tm, n_real, seq):
    src_rows = tuple(s.shape[0] for s in h_sources)
    ntot = sum(src_rows)
    nrt, tps = n_real // tm, seq // tm
    tab_spec = pl.BlockSpec((tm, LANES), lambda i: (_table_index(i, nrt, tps), 0))
    return pl.pallas_call(
        functools.partial(_qkv_kernel, n_src=len(h_sources), src_rows=src_rows, tm=tm),
        grid=(ntot // tm,),
        in_specs=_source_specs(h_sources, tm) + [
            pl.BlockSpec((1, D_MODEL), lambda i: (0, 0)),
            pl.BlockSpec(w.shape, lambda i: (0, 0)),
            tab_spec, tab_spec, tab_spec,
        ],
        out_specs=[
            pl.BlockSpec((tm, 1024), lambda i: (i, 0)),
            pl.BlockSpec((tm, GQA_KV * LANES), lambda i: (i, 0)),
            pl.BlockSpec((tm, GQA_KV * LANES), lambda i: (i, 0)),
        ],
        out_shape=[
            jax.ShapeDtypeStruct((ntot, 1024), BF16),
            jax.ShapeDtypeStruct((ntot, GQA_KV * LANES), BF16),
            jax.ShapeDtypeStruct((ntot, GQA_KV * LANES), BF16),
        ],
        compiler_params=_params(("parallel",)),
        name="qkv_rope",
    )(*h_sources, g, w, *tabs)


def _gqa_attend(sink_ref, q_ref, o_ref, groups):
    def pad_of(bias, pieces):
        have = sum(p.shape[0] for p in pieces)
        return [jnp.zeros((bias.shape[1] - have, LANES), BF16)] if bias.shape[1] > have else []

    def scores(g, kv):
        rows, k_pieces, _, bias = groups[g]
        sl = slice(kv * LANES, (kv + 1) * LANES)
        kcat = jnp.concatenate([p[:, sl] for p in k_pieces] + pad_of(bias, k_pieces), axis=0)
        zero = jnp.zeros_like(kcat)
        lane_k = lax.broadcasted_iota(I32, kcat.shape, 1)
        lo = slice(2 * kv * LANES, (2 * kv + 1) * LANES)
        hi = slice((2 * kv + 1) * LANES, (2 * kv + 2) * LANES)
        qp = jnp.concatenate([q_ref[rows, lo], q_ref[rows, hi]], axis=0)
        return (_dot_nt(qp, jnp.where(lane_k < GQA_HD, kcat, zero)) + bias,
                _dot_nt(qp, jnp.where(lane_k >= GQA_HD, kcat, zero)) + bias)

    def softmax(g, kv, ss):
        nq = groups[g][3].shape[0] // 2
        row = lax.broadcasted_iota(I32, (2 * nq, 1), 0)
        out = []
        for par, s in enumerate(ss):
            head = kv * GQA_GROUP + par
            sink = jnp.where(row < nq, sink_ref[head], sink_ref[head + 2])
            m = jnp.maximum(jnp.max(s, axis=-1, keepdims=True), sink)
            p = jnp.exp2(s - m)
            den = jnp.sum(p, axis=-1, keepdims=True) + jnp.exp2(sink - m)
            out.append((p.astype(BF16), den))
        return out

    def weighted_values(g, kv, pd):
        rows, _, v_pieces, bias = groups[g]
        nq = bias.shape[0] // 2
        sl = slice(kv * LANES, (kv + 1) * LANES)
        vcat = jnp.concatenate([p[:, sl] for p in v_pieces] + pad_of(bias, v_pieces), axis=0)
        outs = [_dot(p, vcat) / den for p, den in pd]
        lane_o = lax.broadcasted_iota(I32, outs[0].shape, 1)
        o = jnp.where(lane_o < GQA_HD, outs[0], outs[1]).astype(BF16)
        o_ref[rows, 2 * kv * LANES:(2 * kv + 1) * LANES] = o[:nq]
        o_ref[rows, (2 * kv + 1) * LANES:(2 * kv + 2) * LANES] = o[nq:]

    units = [(g, kv) for g in range(len(groups)) for kv in range(GQA_KV)]
    nxt = scores(*units[0])
    pending = None
    for idx, unit in enumerate(units):
        ss = nxt
        if idx + 1 < len(units):
            nxt = scores(*units[idx + 1])
        pd = softmax(*unit, ss)
        if pending is not None:
            weighted_values(*pending)
        pending = (*unit, pd)
    weighted_values(*pending)


def _window_bias(prev_ok, next_ok):
    nkeys = 4 * BLK
    qi = lax.broadcasted_iota(I32, (2 * BLK, nkeys), 0) & (BLK - 1)
    c = lax.broadcasted_iota(I32, (2 * BLK, nkeys), 1)
    neg = jnp.float32(-jnp.inf)
    b_p = jnp.where((c >= qi) & prev_ok, 0.0, neg)
    b_n = jnp.where(((c - 2 * BLK) <= qi) & next_ok, 0.0, neg)
    b_m = jnp.where(c < 3 * BLK + N_META, 0.0, neg)
    return jnp.where(c < BLK, b_p, jnp.where(c < 2 * BLK, 0.0, jnp.where(c < 3 * BLK, b_n, b_m)))


def _win_attn_kernel(sink_ref, q_ref, *refs, nb, qb, n_src, src_rows):
    k_blk, km_ref = refs[:qb + 2], refs[qb + 2]
    v_blk, vm_ref = refs[qb + 3:2 * qb + 5], refs[2 * qb + 5]
    rest = refs[2 * qb + 6:]
    wo_ref, h_refs, g_ref = rest[0], rest[1:1 + n_src], rest[1 + n_src]
    hout_ref, xn_ref, o_scr = rest[2 + n_src:]
    j0 = pl.program_id(1) * qb
    groups = []
    for t in range(qb):
        bias = _window_bias(j0 + t > 0, j0 + t < nb - 1)
        groups.append((slice(t * BLK, (t + 1) * BLK),
                       (k_blk[t], k_blk[t + 1], k_blk[t + 2], km_ref),
                       (v_blk[t], v_blk[t + 1], v_blk[t + 2], vm_ref), bias))
    _gqa_attend(sink_ref, q_ref, o_scr, groups)
    tile = pl.program_id(0) * (nb // qb) + pl.program_id(1)
    hn = _dot(o_scr[...], wo_ref[...]) + _select_source(h_refs, src_rows, qb * BLK, tile)
    hout_ref[...] = hn
    xn_ref[...] = _rms(hn, g_ref[...]).astype(BF16)


def _win_attn_call(sink, q, k, v, wo, h_sources, g, *, batch, seq, n_real):
    nb = seq // BLK
    qb = _pick_tile(nb, (4, 2, 1))
    mrow = n_real // N_META
    kvw = GQA_KV * LANES
    tile_of = lambda b, j, s: b * (nb // qb) + j

    def qmap(b, j, s):
        return (tile_of(b, j, s), 0)

    def blk_spec(t):
        return pl.BlockSpec((BLK, kvw),
                            lambda b, j, s: (b * nb + jnp.clip(j * qb + t - 1, 0, nb - 1), 0))

    meta_spec = pl.BlockSpec((N_META, kvw), lambda b, j, s: (mrow + b, 0))
    kv_specs = [blk_spec(t) for t in range(qb + 2)] + [meta_spec]
    fix = lambda b, j, s: (0, 0)
    grid_spec = pltpu.PrefetchScalarGridSpec(
        num_scalar_prefetch=1,
        grid=(batch, nb // qb),
        in_specs=[pl.BlockSpec((qb * BLK, 1024), qmap)] + kv_specs + kv_specs
        + [pl.BlockSpec((1024, D_MODEL), fix)] + _source_specs(h_sources, qb * BLK, tile_of)
        + [pl.BlockSpec((1, D_MODEL), fix)],
        out_specs=[pl.BlockSpec((qb * BLK, D_MODEL), qmap), pl.BlockSpec((qb * BLK, D_MODEL), qmap)],
        scratch_shapes=[pltpu.VMEM((qb * BLK, 1024), BF16)],
    )
    n_kv = qb + 3
    return pl.pallas_call(
        functools.partial(_win_attn_kernel, nb=nb, qb=qb, n_src=len(h_sources),
                          src_rows=tuple(s.shape[0] for s in h_sources)),
        grid_spec=grid_spec,
        out_shape=[jax.ShapeDtypeStruct((n_real, D_MODEL), F32),
                   jax.ShapeDtypeStruct((n_real, D_MODEL), BF16)],
        compiler_params=_params(("parallel", "parallel")),
        name="win_attn_wo",
    )(sink, q, *([k] * n_kv), *([v] * n_kv), wo, *h_sources, g)


def _win_meta_kernel(sink_ref, q_ref, km_ref, kc_ref, vm_ref, vc_ref, o_ref, *, batch):
    i = pl.program_id(0)

    @pl.when(i < batch)
    def _():
        nkeys = 2 * BLK
        qp = lax.broadcasted_iota(I32, (2 * N_META, nkeys), 0) & (N_META - 1)
        c = lax.broadcasted_iota(I32, (2 * N_META, nkeys), 1)
        neg = jnp.float32(-jnp.inf)
        b_c = jnp.where((N_META + c - qp) <= BLK, 0.0, neg)
        bias = jnp.where(c < BLK, b_c, jnp.where(c < BLK + N_META, 0.0, neg))
        _gqa_attend(sink_ref, q_ref, o_ref,
                    [(slice(0, N_META), (kc_ref, km_ref), (vc_ref, vm_ref), bias)])

    @pl.when(i >= batch)
    def _():
        o_ref[...] = jnp.zeros_like(o_ref)


def _win_meta_call(sink, q, k, v, *, batch, seq, n_real):
    ntot = q.shape[0]
    nb = seq // BLK
    mrow = n_real // N_META
    ntail_blocks = (ntot - n_real) // N_META

    def tmap(i, s):
        return (mrow + i, 0)

    def cmap(i, s):
        return (jnp.minimum(i, batch - 1) * nb, 0)

    grid_spec = pltpu.PrefetchScalarGridSpec(
        num_scalar_prefetch=1,
        grid=(ntail_blocks,),
        in_specs=[
            pl.BlockSpec((N_META, 1024), tmap),
            pl.BlockSpec((N_META, GQA_KV * LANES), tmap),
            pl.BlockSpec((BLK, GQA_KV * LANES), cmap),
            pl.BlockSpec((N_META, GQA_KV * LANES), tmap),
            pl.BlockSpec((BLK, GQA_KV * LANES), cmap),
        ],
        out_specs=pl.BlockSpec((N_META, 1024), lambda i, s: (i, 0)),
    )
    return pl.pallas_call(
        functools.partial(_win_meta_kernel, batch=batch),
        grid_spec=grid_spec,
        out_shape=jax.ShapeDtypeStruct((ntot - n_real, 1024), BF16),
        compiler_params=_params(("parallel",)),
        name="win_attn_meta",
    )(sink, q, k, k, v, v)


def _attn_out_tile(o_real_ref, o_tail_ref, n_real_tiles):
    is_real = pl.program_id(0) < n_real_tiles
    return jnp.where(is_real, o_real_ref[...], o_tail_ref[...])


def _attn_out_specs(tm, n_real_tiles):
    return [
        pl.BlockSpec((tm, 1024), lambda i: (jnp.minimum(i, n_real_tiles - 1), 0)),
        pl.BlockSpec((tm, 1024), lambda i: (jnp.maximum(i - n_real_tiles, 0), 0)),
    ]


def _wo_kernel(o_ref, w_ref, h_ref, g_ref, hout_ref, xn_ref):
    hn = _dot(o_ref[...], w_ref[...]) + h_ref[...]
    hout_ref[...] = hn
    xn_ref[...] = _rms(hn, g_ref[...]).astype(BF16)


def _wo_call(o, w, h, g, *, tm):
    n = o.shape[0]
    row = lambda i: (i, 0)
    fix = lambda i: (0, 0)
    return pl.pallas_call(
        _wo_kernel,
        grid=(n // tm,),
        in_specs=[
            pl.BlockSpec((tm, 1024), row),
            pl.BlockSpec((1024, D_MODEL), fix),
            pl.BlockSpec((tm, D_MODEL), row),
            pl.BlockSpec((1, D_MODEL), fix),
        ],
        out_specs=[pl.BlockSpec((tm, D_MODEL), row), pl.BlockSpec((tm, D_MODEL), row)],
        out_shape=[jax.ShapeDtypeStruct((n, D_MODEL), F32),
                   jax.ShapeDtypeStruct((n, D_MODEL), BF16)],
        compiler_params=_params(("parallel",)),
        name="wo_res_norm_tail",
    )(o, w, h, g)


XPK_ROWS = D_MODEL // 2 // LANES
YS_ROWS = D_MODEL // LANES


ROW_SPLIT = 2
MLA_PROJ_SPLIT = 4


def _store_token_major(ref, x, row0=0):
    t, n = x.shape
    k = n // LANES
    for s in range(k):
        ref[pl.ds(row0 * k + s, t, stride=k), :] = x[:, s * LANES:(s + 1) * LANES]


def _load_token_major(ref, k):
    t = ref.shape[0] // k
    return [ref[pl.ds(s, t, stride=k), :] for s in range(k)]


def _pack_bf16_pair(a, b):
    ra = pltpu.bitcast(a.astype(BF16).astype(F32), U32)
    rb = pltpu.bitcast(b.astype(BF16).astype(F32), U32)
    return ra | (rb >> 16)


def _unpack_bf16_pair(p):
    a = pltpu.bitcast(p & jnp.uint32(0xFFFF0000), F32).astype(BF16)
    b = pltpu.bitcast(p << 16, F32).astype(BF16)
    return a, b


def _wo_router_kernel(o_real_ref, o_tail_ref, w_ref, h_ref, g_ref, r_ref,
                      hout_ref, xpk_ref, route_ref, cnt_ref, carry_ref, *, n_real_tiles):
    i = pl.program_id(0)
    tm = h_ref.shape[0]

    @pl.when(i == 0)
    def _():
        carry_ref[...] = jnp.zeros_like(carry_ref)

    o = _attn_out_tile(o_real_ref, o_tail_ref, n_real_tiles)
    th = tm // ROW_SPLIT
    parts = [slice(p * th, (p + 1) * th) for p in range(ROW_SPLIT)]
    half = D_MODEL // 2
    r2 = r_ref[...]
    lane = lax.broadcasted_iota(I32, (th, LANES), 1)
    rr = lax.broadcasted_iota(I32, (th, th), 0)
    cc = lax.broadcasted_iota(I32, (th, th), 1)
    lower = jnp.where(cc < rr, 1.0, 0.0).astype(BF16)

    ys = []
    for rs in parts:
        hn = _dot(o[rs], w_ref[...]) + h_ref[rs, :]
        hout_ref[rs, :] = hn
        ys.append(_rms(hn, g_ref[...]))
    logits = []
    for p, y in enumerate(ys):
        _store_token_major(xpk_ref, _pack_bf16_pair(y[:, :half], y[:, half:]), row0=p * th)
        yhi = y.astype(BF16)
        ylo = (y - yhi.astype(F32)).astype(BF16)
        t = _dot(yhi, r2)
        logits.append(t[:, :LANES] + t[:, LANES:] + _dot(ylo, r2[:, :LANES]))
    carry = carry_ref[0:1, :]
    for rs, lg in zip(parts, logits):
        lm = jnp.where(lane < N_EXPERTS, lg, -jnp.inf)
        m1 = jnp.max(lm, axis=-1, keepdims=True)
        i1 = jnp.min(jnp.where(lm == m1, lane, LANES), axis=-1, keepdims=True)
        lm2 = jnp.where(lane == i1, -jnp.inf, lm)
        m2 = jnp.max(lm2, axis=-1, keepdims=True)
        i2 = jnp.min(jnp.where(lm2 == m2, lane, LANES), axis=-1, keepdims=True)
        e21 = jnp.exp(m2 - m1)
        g0 = 1.0 / (1.0 + e21)
        g1 = e21 / (1.0 + e21)
        onehot = ((lane == i1) | (lane == i2)).astype(F32)
        before = _dot(lower, onehot.astype(BF16)) + carry
        rank1 = jnp.sum(jnp.where(lane == i1, before, 0.0), axis=-1, keepdims=True)
        rank2 = jnp.sum(jnp.where(lane == i2, before, 0.0), axis=-1, keepdims=True)
        carry = carry + jnp.sum(onehot, axis=0, keepdims=True)
        out = jnp.where(lane == 0, i1.astype(F32), 0.0)
        out = jnp.where(lane == 1, i2.astype(F32), out)
        out = jnp.where(lane == 2, rank1, out)
        out = jnp.where(lane == 3, rank2, out)
        out = jnp.where(lane == 4, g0, out)
        out = jnp.where(lane == 5, g1, out)
        route_ref[rs, :] = out
    carry_ref[...] = jnp.broadcast_to(carry, carry_ref.shape)
    cnt_ref[...] = jnp.broadcast_to(carry, cnt_ref.shape)


def _wo_router_call(o_real, o_tail, w, h, g, r2, *, tm):
    ntot = h.shape[0]
    nrt = o_real.shape[0] // tm
    row = lambda i: (i, 0)
    fix = lambda i: (0, 0)
    return pl.pallas_call(
        functools.partial(_wo_router_kernel, n_real_tiles=nrt),
        grid=(ntot // tm,),
        in_specs=_attn_out_specs(tm, nrt) + [
            pl.BlockSpec((1024, D_MODEL), fix),
            pl.BlockSpec((tm, D_MODEL), row),
            pl.BlockSpec((1, D_MODEL), fix),
            pl.BlockSpec((D_MODEL, 2 * LANES), fix),
        ],
        out_specs=[
            pl.BlockSpec((tm, D_MODEL), row),
            pl.BlockSpec((tm * XPK_ROWS, LANES), row),
            pl.BlockSpec((tm, LANES), row),
            pl.BlockSpec((8, LANES), fix),
        ],
        out_shape=[
            jax.ShapeDtypeStruct((ntot, D_MODEL), F32),
            jax.ShapeDtypeStruct((ntot * XPK_ROWS, LANES), U32),
            jax.ShapeDtypeStruct((ntot, LANES), F32),
            jax.ShapeDtypeStruct((8, LANES), F32),
        ],
        scratch_shapes=[pltpu.VMEM((8, LANES), F32)],
        compiler_params=_params(("arbitrary",)),
        name="wo_res_norm_router",
    )(o_real, o_tail, w, h, g, r2)


def _swiglu_chunk(x, wg, wu, wd):
    g = _dot(x, wg)
    u = _dot(x, wu)
    a = (g * (1.0 / (1.0 + jnp.exp(-g)))) * u
    return _dot(a.astype(BF16), wd)


def _ffn_dense_kernel(*refs, n_src, src_rows, tm, tf):
    x_refs, h_refs = refs[:n_src], refs[n_src:2 * n_src]
    wg_ref, wu_ref, wd_ref, g_ref, hout_ref, xn_ref = refs[2 * n_src:]
    x = _select_source(x_refs, src_rows, tm)
    hn = _select_source(h_refs, src_rows, tm)
    for c in range(wg_ref.shape[1] // tf):
        cs = slice(c * tf, (c + 1) * tf)
        hn = hn + _swiglu_chunk(x, wg_ref[:, cs], wu_ref[:, cs], wd_ref[cs, :])
    hout_ref[...] = hn
    xn_ref[...] = _rms(hn, g_ref[...]).astype(BF16)


def _ffn_dense_call(x_sources, wg, wu, wd, h_sources, g, *, tm, tf):
    src_rows = tuple(s.shape[0] for s in h_sources)
    ntot = sum(src_rows)
    fdim = wg.shape[1]
    row = lambda i: (i, 0)
    fix = lambda i: (0, 0)
    once = pl.Buffered(1)
    return pl.pallas_call(
        functools.partial(_ffn_dense_kernel, n_src=len(h_sources), src_rows=src_rows, tm=tm, tf=tf),
        grid=(ntot // tm,),
        in_specs=_source_specs(x_sources, tm) + _source_specs(h_sources, tm) + [
            pl.BlockSpec((D_MODEL, fdim), fix, pipeline_mode=once),
            pl.BlockSpec((D_MODEL, fdim), fix, pipeline_mode=once),
            pl.BlockSpec((fdim, D_MODEL), fix, pipeline_mode=once),
            pl.BlockSpec((1, D_MODEL), fix),
        ],
        out_specs=[pl.BlockSpec((tm, D_MODEL), row), pl.BlockSpec((tm, D_MODEL), row)],
        out_shape=[jax.ShapeDtypeStruct((ntot, D_MODEL), F32),
                   jax.ShapeDtypeStruct((ntot, D_MODEL), BF16)],
        compiler_params=_params(("parallel",)),
        name="ffn_dense",
    )(*x_sources, *h_sources, wg, wu, wd, g)


def _ffn_expert_kernel(be_ref, nu_ref, xs_ref, wg_ref, wu_ref, wd_ref, ys_ref, *, tf):
    del be_ref
    used = pl.program_id(0) < nu_ref[0]

    @pl.when(used)
    def _():
        halves = [_unpack_bf16_pair(blk) for blk in _load_token_major(xs_ref, XPK_ROWS)]
        x = jnp.concatenate([a for a, _ in halves] + [b for _, b in halves], axis=1)
        acc = None
        for c in range(wg_ref.shape[1] // tf):
            cs = slice(c * tf, (c + 1) * tf)
            y = _swiglu_chunk(x, wg_ref[:, cs], wu_ref[:, cs], wd_ref[cs, :])
            acc = y if acc is None else acc + y
        _store_token_major(ys_ref, acc)

    @pl.when(jnp.logical_not(used))
    def _():
        ys_ref[...] = jnp.zeros_like(ys_ref)


def _ffn_expert_call(block_e, n_used, xs, wg, wu, wd, *, tm, tf):
    rows = xs.shape[0] // XPK_ROWS
    once = pl.Buffered(1)
    wmap = lambda i, be, nu: (be[i], 0, 0)
    grid_spec = pltpu.PrefetchScalarGridSpec(
        num_scalar_prefetch=2,
        grid=(rows // tm,),
        in_specs=[
            pl.BlockSpec((tm * XPK_ROWS, LANES), lambda i, be, nu: (i, 0)),
            pl.BlockSpec((None,) + wg.shape[1:], wmap, pipeline_mode=once),
            pl.BlockSpec((None,) + wu.shape[1:], wmap, pipeline_mode=once),
            pl.BlockSpec((None,) + wd.shape[1:], wmap, pipeline_mode=once),
        ],
        out_specs=pl.BlockSpec((tm * YS_ROWS, LANES), lambda i, be, nu: (i, 0)),
    )
    return pl.pallas_call(
        functools.partial(_ffn_expert_kernel, tf=tf),
        grid_spec=grid_spec,
        out_shape=jax.ShapeDtypeStruct((rows * YS_ROWS, LANES), F32),
        compiler_params=_params(("arbitrary",)),
        name="ffn_expert",
    )(block_e, n_used, xs, wg, wu, wd)


DMA_UNROLL = 8


def _token_copy(src, src_tok, dst, dst_tok, rows, sem):
    return pltpu.make_async_copy(src.at[pl.ds(pl.multiple_of(src_tok * rows, rows), rows)],
                                 dst.at[pl.ds(pl.multiple_of(dst_tok * rows, rows), rows)], sem)


def _dispatch_kernel(dest_ref, x_ref, xs_in_ref, xs_ref, sems, *, td):
    del xs_in_ref
    i = pl.program_id(0)
    n = pl.num_programs(0)
    slot = i % 2

    def issue(t, carry):
        for s in range(2):
            _token_copy(x_ref, i * td + t, xs_ref, dest_ref[0, s, t], XPK_ROWS,
                        sems.at[slot]).start(priority=s)
        return carry

    lax.fori_loop(0, td, issue, 0, unroll=DMA_UNROLL)

    def drain_slot(sl):
        def drain(t, carry):
            for s in range(2):
                _token_copy(x_ref, 0, xs_ref, 0, XPK_ROWS, sems.at[sl]).wait()
            return carry

        lax.fori_loop(0, td, drain, 0, unroll=DMA_UNROLL)

    @pl.when(i > 0)
    def _():
        drain_slot(1 - slot)

    @pl.when(i == n - 1)
    def _():
        drain_slot(slot)


def _dispatch_call(dest, xpk, xs_init, *, td):
    ntot = xpk.shape[0] // XPK_ROWS
    return pl.pallas_call(
        functools.partial(_dispatch_kernel, td=td),
        grid=(ntot // td,),
        in_specs=[
            pl.BlockSpec((1, 2, td), lambda i: (i, 0, 0), memory_space=pltpu.SMEM),
            pl.BlockSpec(memory_space=pl.ANY),
            pl.BlockSpec(memory_space=pl.ANY),
        ],
        out_specs=pl.BlockSpec(memory_space=pl.ANY),
        out_shape=jax.ShapeDtypeStruct(xs_init.shape, U32),
        input_output_aliases={2: 0},
        scratch_shapes=[pltpu.SemaphoreType.DMA((2,))],
        compiler_params=_params(("arbitrary",)),
        name="moe_dispatch",
    )(dest, xpk, xs_init)


def _combine_kernel(dest_ref, dest_next_ref, ys_ref, h_ref, route_ref, g_ref, out_ref,
                    y0_ref, y1_ref, sems):
    i = pl.program_id(0)
    n = pl.num_programs(0)
    tc = h_ref.shape[0]
    bufs = (y0_ref, y1_ref)
    slot = i % 2

    def issue(d_ref, sl):
        def body(t, carry):
            for s in range(2):
                _token_copy(ys_ref, d_ref[0, s, t], bufs[s].at[sl], t, YS_ROWS,
                            sems.at[sl]).start(priority=s)
            return carry

        lax.fori_loop(0, tc, body, 0, unroll=DMA_UNROLL)

    @pl.when(i == 0)
    def _():
        issue(dest_ref, 0)

    @pl.when(i + 1 < n)
    def _():
        issue(dest_next_ref, 1 - slot)

    def drain(t, carry):
        for s in range(2):
            _token_copy(ys_ref, 0, bufs[s].at[slot], 0, YS_ROWS, sems.at[slot]).wait()
        return carry

    lax.fori_loop(0, tc, drain, 0, unroll=DMA_UNROLL)

    r = route_ref[...]
    lane = lax.broadcasted_iota(I32, r.shape, 1)
    g0 = jnp.sum(jnp.where(lane == 4, r, 0.0), axis=-1, keepdims=True)
    g1 = jnp.sum(jnp.where(lane == 5, r, 0.0), axis=-1, keepdims=True)
    y0 = _load_token_major(y0_ref.at[slot], YS_ROWS)
    y1 = _load_token_major(y1_ref.at[slot], YS_ROWS)
    moe = jnp.concatenate([a * g0 + b * g1 for a, b in zip(y0, y1)], axis=1)
    out_ref[...] = _rms(h_ref[...] + moe, g_ref[...])


def _combine_call(dest, ys, h, route, g, *, tc, tile0, ntiles):
    return pl.pallas_call(
        _combine_kernel,
        grid=(ntiles,),
        in_specs=[
            pl.BlockSpec((1, 2, tc), lambda i: (tile0 + i, 0, 0), memory_space=pltpu.SMEM),
            pl.BlockSpec((1, 2, tc), lambda i: (tile0 + jnp.minimum(i + 1, ntiles - 1), 0, 0),
                         memory_space=pltpu.SMEM),
            pl.BlockSpec(memory_space=pl.ANY),
            pl.BlockSpec((tc, D_MODEL), lambda i: (tile0 + i, 0)),
            pl.BlockSpec((tc, LANES), lambda i: (tile0 + i, 0)),
            pl.BlockSpec((1, D_MODEL), lambda i: (0, 0)),
        ],
        out_specs=pl.BlockSpec((tc, D_MODEL), lambda i: (i, 0)),
        out_shape=jax.ShapeDtypeStruct((ntiles * tc, D_MODEL), F32),
        scratch_shapes=[pltpu.VMEM((2, tc * YS_ROWS, LANES), F32), pltpu.VMEM((2, tc * YS_ROWS, LANES), F32),
                        pltpu.SemaphoreType.DMA((2,))],
        compiler_params=_params(("arbitrary",)),
        name="moe_combine_norm",
    )(dest, dest, ys, h, route, g)


def _mla_proj_kernel(x_ref, wd_ref, qn_ref, kvn_ref, wuq_ref, wuk_ref, wuvt_ref,
                     c_ref, s1_ref, s2_ref, q_ref, k_ref, vt_ref):
    tm = x_ref.shape[0]
    th = tm // MLA_PROJ_SPLIT
    parts = [slice(p * th, (p + 1) * th) for p in range(MLA_PROJ_SPLIT)]
    half = MLA_ROPE // 2
    scale = (MLA_QK ** -0.5) * LOG2E
    lats = [_dot(x_ref[rs, :], wd_ref[...]) for rs in parts]
    ups = []
    for lat in lats:
        cq = _rms(lat[:, :MLA_Q_RANK], qn_ref[...]).astype(BF16)
        ckv = _rms(lat[:, MLA_Q_RANK:MLA_Q_RANK + MLA_KV_RANK], kvn_ref[...]).astype(BF16)
        ups.append((_dot(cq, wuq_ref[...]),
                    _dot(ckv, wuk_ref[...]),
                    _dot_nt(wuvt_ref[...], ckv)))
    for rs, lat, (q, k, vt) in zip(parts, lats, ups):
        c, s1, s2 = c_ref[rs, :], s1_ref[rs, :], s2_ref[rs, :]
        kr = _rope(lat[:, MLA_Q_RANK + MLA_KV_RANK:], c, s1, s2, half)
        for h in range(MLA_HEADS):
            sl = slice(h * LANES, (h + 1) * LANES)
            q_ref[rs, sl] = (_rope(q[:, sl], c, s1, s2, half) * scale).astype(BF16)
            k_ref[rs, sl] = (k[:, sl] + kr).astype(BF16)
        vt_ref[:, rs] = vt.astype(BF16)


def _mla_proj_call(x, wd, qn, kvn, wuq, wuk, wuv, tabs, *, tm, n_real, seq):
    ntot = x.shape[0]
    nrt, tps = n_real // tm, seq // tm
    row = lambda i: (i, 0)
    fix = lambda i: (0, 0)
    tab_spec = pl.BlockSpec((tm, LANES), lambda i: (_table_index(i, nrt, tps), 0))
    return pl.pallas_call(
        _mla_proj_kernel,
        grid=(ntot // tm,),
        in_specs=[
            pl.BlockSpec((tm, D_MODEL), row),
            pl.BlockSpec(wd.shape, fix),
            pl.BlockSpec((1, MLA_Q_RANK), fix),
            pl.BlockSpec((1, MLA_KV_RANK), fix),
            pl.BlockSpec(wuq.shape, fix),
            pl.BlockSpec(wuk.shape, fix),
            pl.BlockSpec(wuv.shape[::-1], fix),
            tab_spec, tab_spec, tab_spec,
        ],
        out_specs=[
            pl.BlockSpec((tm, 2048), row),
            pl.BlockSpec((tm, 2048), row),
            pl.BlockSpec((1024, tm), lambda i: (0, i)),
        ],
        out_shape=[
            jax.ShapeDtypeStruct((ntot, 2048), BF16),
            jax.ShapeDtypeStruct((ntot, 2048), BF16),
            jax.ShapeDtypeStruct((1024, ntot), BF16),
        ],
        compiler_params=_params(("parallel",)),
        name="mla_proj",
    )(x, wd, qn, kvn, wuq, wuk, wuv.T, *tabs)


META_PER_BLOCK = LANES // N_META
REDUCE_ROWS = 64
SCORES_AHEAD = 1
VALUES_BEHIND = 1
MLA_QUERY_TILE = 256


def _col_reduce(x, op):
    rows, n = x.shape
    if rows > REDUCE_ROWS and rows % REDUCE_ROWS == 0:
        x = op(x.reshape(rows // REDUCE_ROWS, REDUCE_ROWS, n), axis=0)
    return op(x, axis=0, keepdims=True)


def _mla_attn_kernel(q_ref, kr_ref, kt_ref, vtr_ref, vtt_ref, o_ref, ot_ref):
    b = pl.program_id(0)
    r = lax.broadcasted_iota(I32, (LANES, 1), 0)
    mine = jnp.right_shift(r, _log2(N_META)) == (b % META_PER_BLOCK)
    mbias = jnp.where(mine, 0.0, -jnp.inf).astype(F32)
    tq = min(MLA_QUERY_TILE, q_ref.shape[0])
    units = [(g, h) for g in range(q_ref.shape[0] // tq) for h in range(MLA_HEADS)]

    def scores(g, h):
        hs = slice(h * LANES, (h + 1) * LANES)
        qh = q_ref[g * tq:(g + 1) * tq, hs]
        return _dot_nt(kr_ref[:, hs], qh), _dot_nt(kt_ref[:, hs], qh) + mbias

    def weighted_values(g, h, p_r, p_m, den):
        vs = slice(h * MLA_V, (h + 1) * MLA_V)
        acc = _dot(vtr_ref[vs, :], p_r) + _dot(vtt_ref[vs, :], p_m)
        ot_ref[vs, g * tq:(g + 1) * tq] = acc / den

    ahead = [scores(*u) for u in units[:SCORES_AHEAD]]
    pending = []
    for idx, unit in enumerate(units):
        s_r, s_m = ahead.pop(0)
        if idx + SCORES_AHEAD < len(units):
            ahead.append(scores(*units[idx + SCORES_AHEAD]))
        m = jnp.maximum(_col_reduce(s_r, jnp.max), _col_reduce(s_m, jnp.max))
        p_r = jnp.exp2(s_r - m)
        p_m = jnp.exp2(s_m - m)
        den = _col_reduce(p_r, jnp.sum) + _col_reduce(p_m, jnp.sum)
        pending.append((*unit, p_r.astype(BF16), p_m.astype(BF16), den))
        if len(pending) > VALUES_BEHIND:
            weighted_values(*pending.pop(0))
    for item in pending:
        weighted_values(*item)
    o_ref[...] = ot_ref[...].T.astype(BF16)


def _mla_attn_call(q, k, vt, *, batch, seq, n_real, tq):
    nq = seq // tq
    tail0 = n_real // LANES
    qmap = lambda b, j: (b * nq + j, 0)
    return pl.pallas_call(
        _mla_attn_kernel,
        grid=(batch, nq),
        in_specs=[
            pl.BlockSpec((tq, 2048), qmap),
            pl.BlockSpec((seq, 2048), lambda b, j: (b, 0)),
            pl.BlockSpec((LANES, 2048), lambda b, j: (tail0 + b // META_PER_BLOCK, 0)),
            pl.BlockSpec((1024, seq), lambda b, j: (0, b)),
            pl.BlockSpec((1024, LANES), lambda b, j: (0, tail0 + b // META_PER_BLOCK)),
        ],
        out_specs=pl.BlockSpec((tq, 1024), qmap),
        out_shape=jax.ShapeDtypeStruct((n_real, 1024), BF16),
        scratch_shapes=[pltpu.VMEM((1024, tq), F32)],
        compiler_params=_params(("parallel", "arbitrary")),
        name="mla_attn",
    )(q, k, k, vt, vt)


def _mla_meta_kernel(q_ref, kr_ref, kt_ref, vtr_ref, vtt_ref, o_ref, *, batch):
    i = pl.program_id(0)
    ncol = MLA_HEADS * N_META

    @pl.when(i < batch)
    def _():
        qi = lax.broadcasted_iota(I32, (N_META, ncol), 0)
        ci = lax.broadcasted_iota(I32, (N_META, ncol), 1)
        spread = jnp.where((ci & (N_META - 1)) == qi, 1.0, 0.0).astype(BF16)
        w = lax.dot_general(q_ref[...], spread, (((0,), (0,)), ((), ())),
                            preferred_element_type=F32)
        wr = lax.broadcasted_iota(I32, w.shape, 0)
        wc = lax.broadcasted_iota(I32, w.shape, 1)
        on_diag = jnp.right_shift(wr, _log2(LANES)) == jnp.right_shift(wc, _log2(N_META))
        w = jnp.where(on_diag, w, 0.0).astype(BF16)
        r = lax.broadcasted_iota(I32, (LANES, 1), 0)
        mine = jnp.right_shift(r, _log2(N_META)) == (i % META_PER_BLOCK)
        mbias = jnp.where(mine, 0.0, -jnp.inf).astype(F32)
        s_r = _dot(kr_ref[...], w)
        s_m = _dot(kt_ref[...], w) + mbias
        m = jnp.maximum(_col_reduce(s_r, jnp.max), _col_reduce(s_m, jnp.max))
        p_r = jnp.exp2(s_r - m)
        p_m = jnp.exp2(s_m - m)
        den = _col_reduce(p_r, jnp.sum) + _col_reduce(p_m, jnp.sum)
        ot = (_dot(vtr_ref[...], p_r.astype(BF16)) + _dot(vtt_ref[...], p_m.astype(BF16))) / den
        o_all = ot.T
        lane = lax.broadcasted_iota(I32, (N_META, 1024), 1)
        o = jnp.zeros((N_META, 1024), F32)
        for h in range(MLA_HEADS):
            rows = o_all[h * N_META:(h + 1) * N_META, :]
            o = jnp.where(jnp.right_shift(lane, _log2(MLA_V)) == h, rows, o)
        o_ref[...] = o.astype(BF16)

    @pl.when(i >= batch)
    def _():
        o_ref[...] = jnp.zeros_like(o_ref)


def _mla_meta_call(q, k, vt, *, batch, seq, n_real):
    ntot = q.shape[0]
    mrow = n_real // N_META
    tail0 = n_real // LANES
    ntail_blocks = (ntot - n_real) // N_META
    last = (batch - 1) // META_PER_BLOCK
    bmap = lambda i: jnp.minimum(i, batch - 1)
    tmap = lambda i: jnp.minimum(i // META_PER_BLOCK, last)
    return pl.pallas_call(
        functools.partial(_mla_meta_kernel, batch=batch),
        grid=(ntail_blocks,),
        in_specs=[
            pl.BlockSpec((N_META, 2048), lambda i: (mrow + i, 0)),
            pl.BlockSpec((seq, 2048), lambda i: (bmap(i), 0)),
            pl.BlockSpec((LANES, 2048), lambda i: (tail0 + tmap(i), 0)),
            pl.BlockSpec((1024, seq), lambda i: (0, bmap(i))),
            pl.BlockSpec((1024, LANES), lambda i: (0, tail0 + tmap(i))),
        ],
        out_specs=pl.BlockSpec((N_META, 1024), lambda i: (i, 0)),
        out_shape=jax.ShapeDtypeStruct((ntot - n_real, 1024), BF16),
        compiler_params=_params(("arbitrary",)),
        name="mla_attn_meta",
    )(q, k, k, vt, vt)


def _prep_gqa(a_wqkv, a_wo):
    d = a_wqkv.shape[0]
    nq = GQA_HEADS * GQA_HD
    nkv = GQA_KV * GQA_HD
    wq = a_wqkv[:, :nq] * (GQA_HD ** -0.5)

    def dup(w):
        w = w.reshape(d, GQA_KV, 1, GQA_HD)
        return jnp.broadcast_to(w, (d, GQA_KV, 2, GQA_HD)).reshape(d, 2 * nkv)

    w = jnp.concatenate([wq, dup(a_wqkv[:, nq:nq + nkv]), dup(a_wqkv[:, nq + nkv:])], axis=1)
    return w.astype(BF16), a_wo.astype(BF16)


def _prep_mla(b_wdkv, b_wuq, b_wukv):
    d = b_wdkv.shape[0]
    lat = MLA_Q_RANK + MLA_KV_RANK
    z = lambda n: jnp.zeros((d, n), F32)
    wd = jnp.concatenate([b_wdkv[:, :lat], z(MLA_NOPE), b_wdkv[:, lat:], z(LANES - MLA_QK)], axis=1)
    wuq = b_wuq.reshape(MLA_Q_RANK, MLA_HEADS, MLA_QK)
    wuq = jnp.pad(wuq, ((0, 0), (0, 0), (0, LANES - MLA_QK))).reshape(MLA_Q_RANK, MLA_HEADS * LANES)
    wukv = b_wukv.reshape(MLA_KV_RANK, MLA_HEADS, MLA_NOPE + MLA_V)
    wuk = jnp.pad(wukv[:, :, :MLA_NOPE], ((0, 0), (0, 0), (0, LANES - MLA_NOPE)))
    wuk = wuk.reshape(MLA_KV_RANK, MLA_HEADS * LANES)
    wuv = wukv[:, :, MLA_NOPE:].reshape(MLA_KV_RANK, MLA_HEADS * MLA_V)
    return wd.astype(BF16), wuq.astype(BF16), wuk.astype(BF16), wuv.astype(BF16)


def _prep_router(m_router):
    hi = m_router.astype(BF16)
    lo = (m_router - hi.astype(F32)).astype(BF16)
    pad = lambda r: jnp.pad(r, ((0, 0), (0, LANES - N_EXPERTS)))
    return jnp.concatenate([pad(hi), pad(lo)], axis=1)


def _positions(seq, n_tail, n_meta_rows):
    real = N_META + jnp.arange(seq, dtype=F32)
    t = jnp.arange(n_tail)
    tail = jnp.where(t < n_meta_rows, t % N_META, 0).astype(F32)
    return jnp.concatenate([real, tail])


def _trunk_flat(x_groups, meta, norm_mix, norm_ffn, norm_final,
                a_wqkv, a_wo, a_sink,
                b_wdkv, b_qnorm, b_kvnorm, b_wuq, b_wukv, b_wo,
                f_wg, f_wu, f_wd,
                m_router, m_wg, m_wu, m_wd):
    seq = x_groups[0].shape[1]
    batches = [x.shape[0] for x in x_groups]
    batch = sum(batches)
    n_real = batch * seq
    n_meta_rows = batch * N_META
    big = _pick_tile(seq, (1024, 512, 256, 128))
    tp = min(big, 512)
    ntot = -(-(n_real + n_meta_rows) // big) * big
    n_tail = ntot - n_real

    tail = jnp.concatenate([
        jnp.broadcast_to(meta.astype(F32)[None], (batch, N_META, D_MODEL)).reshape(n_meta_rows, D_MODEL),
        jnp.zeros((n_tail - n_meta_rows, D_MODEL), F32)], axis=0)
    h_sources = [x.reshape(-1, D_MODEL) for x in x_groups] + [tail]

    pos = _positions(seq, n_tail, n_meta_rows)
    tabs_a = _rope_lane_tables(pos, GQA_THETA, GQA_ROT, GQA_HD, 0)
    tabs_b = _rope_lane_tables(pos, MLA_THETA, MLA_ROPE, LANES, MLA_NOPE)
    row = lambda v: v.reshape(1, -1).astype(F32)

    wqkv, wo_a = _prep_gqa(a_wqkv[0], a_wo[0])
    q, k, v = _qkv_call(h_sources, row(norm_mix[0]), wqkv, tabs_a, tm=tp, n_real=n_real, seq=seq)
    sink = a_sink[0].astype(F32) * LOG2E
    h_real, xn_real = _win_attn_call(sink, q, k, v, wo_a, h_sources[:-1], row(norm_ffn[0]),
                                     batch=batch, seq=seq, n_real=n_real)
    o_tail = _win_meta_call(sink, q, k, v, batch=batch, seq=seq, n_real=n_real)
    h_tail, xn_tail = _wo_call(o_tail, wo_a, tail, row(norm_ffn[0]), tm=tp)
    fdim = f_wg.shape[2]
    tf = _pick_tile(fdim, (512, 256, 128))
    h, xn = _ffn_dense_call([xn_real, xn_tail], f_wg[0].astype(BF16), f_wu[0].astype(BF16),
                            f_wd[0].astype(BF16), [h_real, h_tail], row(norm_mix[1]), tm=tp, tf=tf)

    wd, wuq, wuk, wuv = _prep_mla(b_wdkv[0], b_wuq[0], b_wukv[0])
    q, k, vt = _mla_proj_call(xn, wd, row(b_qnorm[0]), row(b_kvnorm[0]), wuq, wuk, wuv, tabs_b,
                              tm=tp, n_real=n_real, seq=seq)
    tq = _pick_tile(seq, (512, 256, 128))
    o = _mla_attn_call(q, k, vt, batch=batch, seq=seq, n_real=n_real, tq=tq)
    o_tail = _mla_meta_call(q, k, vt, batch=batch, seq=seq, n_real=n_real)
    tr = tp
    h, xpk, route, counts = _wo_router_call(o, o_tail, b_wo[0].astype(BF16), h, row(norm_ffn[1]),
                                            _prep_router(m_router[0]), tm=tr)

    tmx = big
    counts = counts[0, :N_EXPERTS].astype(I32)
    padded = ((counts + tmx - 1) // tmx) * tmx
    pend = jnp.cumsum(padded)
    pstart = pend - padded
    e_idx = route[:, 0:2].astype(I32)
    rank = route[:, 2:4].astype(I32)
    dest = pstart[e_idx] + rank
    n_assign = 2 * ntot
    n_blocks = -(-n_assign // tmx) + N_EXPERTS
    rows = n_blocks * tmx
    block_e = jnp.clip(jnp.searchsorted(pend, jnp.arange(n_blocks, dtype=I32) * tmx, side="right"),
                       0, N_EXPERTS - 1).astype(I32)
    n_used = (pend[-1] // tmx).astype(I32).reshape(1)
    block_e = jnp.where(jnp.arange(n_blocks) < n_used[0], block_e, block_e[jnp.maximum(n_used[0] - 1, 0)])

    td = tp
    dest_t = dest.reshape(ntot // td, td, 2).transpose(0, 2, 1)
    xs = _dispatch_call(dest_t, xpk, jnp.zeros((rows * XPK_ROWS, LANES), U32), td=td)
    edim = m_wg.shape[3]
    tfx = _pick_tile(edim, (512, 256, 128))
    ys = _ffn_expert_call(block_e, n_used, xs, m_wg[0].astype(BF16), m_wu[0].astype(BF16),
                          m_wd[0].astype(BF16), tm=tmx, tf=tfx)

    outs = []
    tile0 = 0
    for b, x in zip(batches, x_groups):
        ntiles = b * seq // td
        y = _combine_call(dest_t, ys, h, route, row(norm_final), tc=td, tile0=tile0, ntiles=ntiles)
        outs.append(y.reshape(b, seq, D_MODEL))
        tile0 += ntiles
    return tuple(outs)


def kernel(x_prompt, x_sample, meta, norm_mix, norm_ffn, norm_final, a_wqkv, a_wo, a_sink, b_wdkv, b_qnorm, b_kvnorm, b_wuq, b_wukv, b_wo, f_wg, f_wu, f_wd, m_router, m_wg, m_wu, m_wd):
    assert x_prompt.shape[1] == x_sample.shape[1] and x_prompt.shape[1] % BLK == 0
    assert norm_mix.shape[0] == 2, "two layers: windowed GQA + dense FFN, then MLA + MoE"
    return _trunk_flat((x_prompt, x_sample), meta, norm_mix, norm_ffn, norm_final,
                       a_wqkv, a_wo, a_sink,
                       b_wdkv, b_qnorm, b_kvnorm, b_wuq, b_wukv, b_wo,
                       f_wg, f_wu, f_wd,
                       m_router, m_wg, m_wu, m_wd)
```

```python
import functools

import jax
import jax.numpy as jnp
from jax import lax
from jax.experimental import pallas as pl
from jax.experimental.pallas import tpu as pltpu

F32 = jnp.float32
BF16 = jnp.bfloat16
U32 = jnp.uint32
I32 = jnp.int32

D_MODEL = 1024
N_META = 16
RMS_EPS = 1e-6
BLK = 128
GQA_HEADS = 16
GQA_KV = 4
GQA_GROUP = 4
GQA_HD = 64
GQA_ROT = 16
GQA_THETA = 500000.0
MLA_HEADS = 16
MLA_NOPE = 64
MLA_ROPE = 32
MLA_V = 64
MLA_Q_RANK = 384
MLA_KV_RANK = 256
MLA_THETA = 10000.0
MLA_QK = MLA_NOPE + MLA_ROPE
N_EXPERTS = 8
LOG2E = 1.4426950408889634

LANES = 128
ATTN_W = GQA_HEADS * GQA_HD
MLA_QK_W = MLA_HEADS * LANES
assert ATTN_W == MLA_HEADS * MLA_V
VMEM_LIMIT = 56 * 1024 * 1024


def _params(sem, vmem=VMEM_LIMIT):
    return pltpu.CompilerParams(dimension_semantics=sem, vmem_limit_bytes=vmem)


def _rms(x, g):
    ms = jnp.mean(x * x, axis=-1, keepdims=True)
    return x * lax.rsqrt(ms + RMS_EPS) * g


def _dot(a, b):
    return jnp.dot(a, b, preferred_element_type=F32)


def _dot_nt(a, b):
    return lax.dot_general(a, b, (((1,), (1,)), ((), ())), preferred_element_type=F32)


def _log2(n):
    assert n & (n - 1) == 0, n
    return n.bit_length() - 1


def _pick_tile(n, candidates):
    for c in candidates:
        if n % c == 0:
            return c
    raise ValueError(f"no tile in {candidates} divides {n}")


def _rope_lane_tables(pos, theta, rot, period, offset):
    half = rot // 2
    inv = jnp.power(jnp.float32(theta), -(jnp.arange(0, rot, 2, dtype=F32) / rot))
    ang = pos[:, None] * inv[None, :]
    cos, sin = jnp.cos(ang), jnp.sin(ang)
    lane = jnp.arange(LANES)
    r = (lane % period) - offset
    is_x1 = (r >= 0) & (r < half)
    is_x2 = (r >= half) & (r < rot)
    f = jnp.clip(jnp.where(is_x2, r - half, r), 0, half - 1)
    cos_l = cos[:, f]
    sin_l = sin[:, f]
    c = jnp.where((is_x1 | is_x2)[None, :], cos_l, 1.0)
    s1 = jnp.where(is_x2[None, :], sin_l, 0.0)
    s2 = jnp.where(is_x1[None, :], -sin_l, 0.0)
    return c.astype(F32), s1.astype(F32), s2.astype(F32)


def _rope(x, c, s1, s2, half):
    return x * c + pltpu.roll(x, half, 1) * s1 + pltpu.roll(x, LANES - half, 1) * s2


def _source_specs(sources, tm, tile_of=lambda *grid: grid[0]):
    specs, start = [], 0
    for src in sources:
        n = src.shape[0] // tm
        specs.append(pl.BlockSpec(
            (tm, src.shape[1]),
            lambda *grid, start=start, n=n: (jnp.clip(tile_of(*grid) - start, 0, n - 1), 0)))
        start += n
    return specs


def _select_source(refs, src_rows, tm, tile=None):
    i = pl.program_id(0) if tile is None else tile
    ends, acc = [], 0
    for n in src_rows:
        acc += n // tm
        ends.append(acc)
    x = refs[-1][...]
    for r, end in reversed(list(zip(refs[:-1], ends[:-1]))):
        x = jnp.where(i < end, r[...], x)
    return x


def _qkv_kernel(*refs, n_src, src_rows, tm):
    h_refs = refs[:n_src]
    g_ref, w_ref, c_ref, s1_ref, s2_ref, q_ref, k_ref, v_ref = refs[n_src:]
    xn = _rms(_select_source(h_refs, src_rows, tm), g_ref[...]).astype(BF16)
    qkv = _dot(xn, w_ref[...])
    c, s1, s2 = c_ref[...], s1_ref[...], s2_ref[...]
    nq = GQA_HEADS * GQA_HD // LANES
    nk = GQA_KV
    for i in range(nq):
        q_ref[:, i * LANES:(i + 1) * LANES] = (_rope(
            qkv[:, i * LANES:(i + 1) * LANES], c, s1, s2, GQA_ROT // 2) * LOG2E).astype(BF16)
    for i in range(nk):
        lo = (nq + i) * LANES
        k_ref[:, i * LANES:(i + 1) * LANES] = _rope(
            qkv[:, lo:lo + LANES], c, s1, s2, GQA_ROT // 2).astype(BF16)
    v_ref[...] = qkv[:, (nq + nk) * LANES:].astype(BF16)


def _table_index(i, n_real_tiles, tiles_per_seq):
    return jnp.where(i < n_real_tiles, i % tiles_per_seq, tiles_per_seq + i - n_real_tiles)


def _qkv_call(h_sources, g, w, tabs, *, tm, n_real, seq):
    src_rows = tuple(s.shape[0] for s in h_sources)
    ntot = sum(src_rows)
    nrt, tps = n_real // tm, seq // tm
    tab_spec = pl.BlockSpec((tm, LANES), lambda i: (_table_index(i, nrt, tps), 0))
    return pl.pallas_call(
        functools.partial(_qkv_kernel, n_src=len(h_sources), src_rows=src_rows, tm=tm),
        grid=(ntot // tm,),
        in_specs=_source_specs(h_sources, tm) + [
            pl.BlockSpec((1, D_MODEL), lambda i: (0, 0)),
            pl.BlockSpec(w.shape, lambda i: (0, 0)),
            tab_spec, tab_spec, tab_spec,
        ],
        out_specs=[
            pl.BlockSpec((tm, ATTN_W), lambda i: (i, 0)),
            pl.BlockSpec((tm, GQA_KV * LANES), lambda i: (i, 0)),
            pl.BlockSpec((tm, GQA_KV * LANES), lambda i: (i, 0)),
        ],
        out_shape=[
            jax.ShapeDtypeStruct((ntot, ATTN_W), BF16),
            jax.ShapeDtypeStruct((ntot, GQA_KV * LANES), BF16),
            jax.ShapeDtypeStruct((ntot, GQA_KV * LANES), BF16),
        ],
        compiler_params=_params(("parallel",)),
        name="qkv_rope",
    )(*h_sources, g, w, *tabs)


def _gqa_attend(sink_ref, q_ref, o_ref, groups):
    def pad_of(bias, pieces):
        have = sum(p.shape[0] for p in pieces)
        return [jnp.zeros((bias.shape[1] - have, LANES), BF16)] if bias.shape[1] > have else []

    def scores(g, kv):
        rows, k_pieces, _, bias = groups[g]
        sl = slice(kv * LANES, (kv + 1) * LANES)
        kcat = jnp.concatenate([p[:, sl] for p in k_pieces] + pad_of(bias, k_pieces), axis=0)
        zero = jnp.zeros_like(kcat)
        lane_k = lax.broadcasted_iota(I32, kcat.shape, 1)
        lo = slice(2 * kv * LANES, (2 * kv + 1) * LANES)
        hi = slice((2 * kv + 1) * LANES, (2 * kv + 2) * LANES)
        qp = jnp.concatenate([q_ref[rows, lo], q_ref[rows, hi]], axis=0)
        return (_dot_nt(qp, jnp.where(lane_k < GQA_HD, kcat, zero)) + bias,
                _dot_nt(qp, jnp.where(lane_k >= GQA_HD, kcat, zero)) + bias)

    def softmax(g, kv, ss):
        nq = groups[g][3].shape[0] // 2
        row = lax.broadcasted_iota(I32, (2 * nq, 1), 0)
        out = []
        for par, s in enumerate(ss):
            head = kv * GQA_GROUP + par
            sink = jnp.where(row < nq, sink_ref[head], sink_ref[head + 2])
            m = jnp.maximum(jnp.max(s, axis=-1, keepdims=True), sink)
            p = jnp.exp2(s - m)
            den = jnp.sum(p, axis=-1, keepdims=True) + jnp.exp2(sink - m)
            out.append((p.astype(BF16), den))
        return out

    def weighted_values(g, kv, pd):
        rows, _, v_pieces, bias = groups[g]
        nq = bias.shape[0] // 2
        sl = slice(kv * LANES, (kv + 1) * LANES)
        vcat = jnp.concatenate([p[:, sl] for p in v_pieces] + pad_of(bias, v_pieces), axis=0)
        outs = [_dot(p, vcat) / den for p, den in pd]
        lane_o = lax.broadcasted_iota(I32, outs[0].shape, 1)
        o = jnp.where(lane_o < GQA_HD, outs[0], outs[1]).astype(BF16)
        o_ref[rows, 2 * kv * LANES:(2 * kv + 1) * LANES] = o[:nq]
        o_ref[rows, (2 * kv + 1) * LANES:(2 * kv + 2) * LANES] = o[nq:]

    units = [(g, kv) for g in range(len(groups)) for kv in range(GQA_KV)]
    nxt = scores(*units[0])
    pending = None
    for idx, unit in enumerate(units):
        ss = nxt
        if idx + 1 < len(units):
            nxt = scores(*units[idx + 1])
        pd = softmax(*unit, ss)
        if pending is not None:
            weighted_values(*pending)
        pending = (*unit, pd)
    weighted_values(*pending)


def _window_bias(prev_ok, next_ok):
    nkeys = 4 * BLK
    qi = lax.broadcasted_iota(I32, (2 * BLK, nkeys), 0) & (BLK - 1)
    c = lax.broadcasted_iota(I32, (2 * BLK, nkeys), 1)
    neg = jnp.float32(-jnp.inf)
    b_p = jnp.where((c >= qi) & prev_ok, 0.0, neg)
    b_n = jnp.where(((c - 2 * BLK) <= qi) & next_ok, 0.0, neg)
    b_m = jnp.where(c < 3 * BLK + N_META, 0.0, neg)
    return jnp.where(c < BLK, b_p, jnp.where(c < 2 * BLK, 0.0, jnp.where(c < 3 * BLK, b_n, b_m)))


def _win_attn_kernel(sink_ref, q_ref, *refs, nb, qb, n_src, src_rows):
    k_blk, km_ref = refs[:qb + 2], refs[qb + 2]
    v_blk, vm_ref = refs[qb + 3:2 * qb + 5], refs[2 * qb + 5]
    rest = refs[2 * qb + 6:]
    wo_ref, h_refs, g_ref = rest[0], rest[1:1 + n_src], rest[1 + n_src]
    hout_ref, xn_ref, o_scr = rest[2 + n_src:]
    j0 = pl.program_id(1) * qb
    groups = []
    for t in range(qb):
        bias = _window_bias(j0 + t > 0, j0 + t < nb - 1)
        groups.append((slice(t * BLK, (t + 1) * BLK),
                       (k_blk[t], k_blk[t + 1], k_blk[t + 2], km_ref),
                       (v_blk[t], v_blk[t + 1], v_blk[t + 2], vm_ref), bias))
    _gqa_attend(sink_ref, q_ref, o_scr, groups)
    tile = pl.program_id(0) * (nb // qb) + pl.program_id(1)
    hn = _dot(o_scr[...], wo_ref[...]) + _select_source(h_refs, src_rows, qb * BLK, tile)
    hout_ref[...] = hn
    xn_ref[...] = _rms(hn, g_ref[...]).astype(BF16)


def _win_attn_call(sink, q, k, v, wo, h_sources, g, *, batch, seq, n_real):
    nb = seq // BLK
    qb = _pick_tile(nb, (4, 2, 1))
    mrow = n_real // N_META
    kvw = GQA_KV * LANES
    tile_of = lambda b, j, s: b * (nb // qb) + j

    def qmap(b, j, s):
        return (tile_of(b, j, s), 0)

    def blk_spec(t):
        return pl.BlockSpec((BLK, kvw),
                            lambda b, j, s: (b * nb + jnp.clip(j * qb + t - 1, 0, nb - 1), 0))

    meta_spec = pl.BlockSpec((N_META, kvw), lambda b, j, s: (mrow + b, 0))
    kv_specs = [blk_spec(t) for t in range(qb + 2)] + [meta_spec]
    fix = lambda b, j, s: (0, 0)
    grid_spec = pltpu.PrefetchScalarGridSpec(
        num_scalar_prefetch=1,
        grid=(batch, nb // qb),
        in_specs=[pl.BlockSpec((qb * BLK, ATTN_W), qmap)] + kv_specs + kv_specs
        + [pl.BlockSpec((ATTN_W, D_MODEL), fix)] + _source_specs(h_sources, qb * BLK, tile_of)
        + [pl.BlockSpec((1, D_MODEL), fix)],
        out_specs=[pl.BlockSpec((qb * BLK, D_MODEL), qmap), pl.BlockSpec((qb * BLK, D_MODEL), qmap)],
        scratch_shapes=[pltpu.VMEM((qb * BLK, ATTN_W), BF16)],
    )
    n_kv = qb + 3
    return pl.pallas_call(
        functools.partial(_win_attn_kernel, nb=nb, qb=qb, n_src=len(h_sources),
                          src_rows=tuple(s.shape[0] for s in h_sources)),
        grid_spec=grid_spec,
        out_shape=[jax.ShapeDtypeStruct((n_real, D_MODEL), F32),
                   jax.ShapeDtypeStruct((n_real, D_MODEL), BF16)],
        compiler_params=_params(("parallel", "parallel")),
        name="win_attn_wo",
    )(sink, q, *([k] * n_kv), *([v] * n_kv), wo, *h_sources, g)


def _win_meta_kernel(sink_ref, q_ref, km_ref, kc_ref, vm_ref, vc_ref, o_ref, *, batch):
    i = pl.program_id(0)

    @pl.when(i < batch)
    def _():
        nkeys = 2 * BLK
        qp = lax.broadcasted_iota(I32, (2 * N_META, nkeys), 0) & (N_META - 1)
        c = lax.broadcasted_iota(I32, (2 * N_META, nkeys), 1)
        neg = jnp.float32(-jnp.inf)
        b_c = jnp.where((N_META + c - qp) <= BLK, 0.0, neg)
        bias = jnp.where(c < BLK, b_c, jnp.where(c < BLK + N_META, 0.0, neg))
        _gqa_attend(sink_ref, q_ref, o_ref,
                    [(slice(0, N_META), (kc_ref, km_ref), (vc_ref, vm_ref), bias)])

    @pl.when(i >= batch)
    def _():
        o_ref[...] = jnp.zeros_like(o_ref)


def _win_meta_call(sink, q, k, v, *, batch, seq, n_real):
    ntot = q.shape[0]
    nb = seq // BLK
    mrow = n_real // N_META
    ntail_blocks = (ntot - n_real) // N_META

    def tmap(i, s):
        return (mrow + i, 0)

    def cmap(i, s):
        return (jnp.minimum(i, batch - 1) * nb, 0)

    grid_spec = pltpu.PrefetchScalarGridSpec(
        num_scalar_prefetch=1,
        grid=(ntail_blocks,),
        in_specs=[
            pl.BlockSpec((N_META, ATTN_W), tmap),
            pl.BlockSpec((N_META, GQA_KV * LANES), tmap),
            pl.BlockSpec((BLK, GQA_KV * LANES), cmap),
            pl.BlockSpec((N_META, GQA_KV * LANES), tmap),
            pl.BlockSpec((BLK, GQA_KV * LANES), cmap),
        ],
        out_specs=pl.BlockSpec((N_META, ATTN_W), lambda i, s: (i, 0)),
    )
    return pl.pallas_call(
        functools.partial(_win_meta_kernel, batch=batch),
        grid_spec=grid_spec,
        out_shape=jax.ShapeDtypeStruct((ntot - n_real, ATTN_W), BF16),
        compiler_params=_params(("parallel",)),
        name="win_attn_meta",
    )(sink, q, k, k, v, v)


def _attn_out_tile(o_real_ref, o_tail_ref, n_real_tiles):
    is_real = pl.program_id(0) < n_real_tiles
    return jnp.where(is_real, o_real_ref[...], o_tail_ref[...])


def _attn_out_specs(tm, n_real_tiles):
    return [
        pl.BlockSpec((tm, ATTN_W), lambda i: (jnp.minimum(i, n_real_tiles - 1), 0)),
        pl.BlockSpec((tm, ATTN_W), lambda i: (jnp.maximum(i - n_real_tiles, 0), 0)),
    ]


def _wo_kernel(o_ref, w_ref, h_ref, g_ref, hout_ref, xn_ref):
    hn = _dot(o_ref[...], w_ref[...]) + h_ref[...]
    hout_ref[...] = hn
    xn_ref[...] = _rms(hn, g_ref[...]).astype(BF16)


def _wo_call(o, w, h, g, *, tm):
    n = o.shape[0]
    row = lambda i: (i, 0)
    fix = lambda i: (0, 0)
    return pl.pallas_call(
        _wo_kernel,
        grid=(n // tm,),
        in_specs=[
            pl.BlockSpec((tm, ATTN_W), row),
            pl.BlockSpec((ATTN_W, D_MODEL), fix),
            pl.BlockSpec((tm, D_MODEL), row),
            pl.BlockSpec((1, D_MODEL), fix),
        ],
        out_specs=[pl.BlockSpec((tm, D_MODEL), row), pl.BlockSpec((tm, D_MODEL), row)],
        out_shape=[jax.ShapeDtypeStruct((n, D_MODEL), F32),
                   jax.ShapeDtypeStruct((n, D_MODEL), BF16)],
        compiler_params=_params(("parallel",)),
        name="wo_res_norm_tail",
    )(o, w, h, g)


XPK_ROWS = D_MODEL // 2 // LANES
YS_ROWS = D_MODEL // LANES


ROW_SPLIT = 2
MLA_PROJ_SPLIT = 4


def _store_token_major(ref, x, row0=0):
    t, n = x.shape
    k = n // LANES
    for s in range(k):
        ref[pl.ds(row0 * k + s, t, stride=k), :] = x[:, s * LANES:(s + 1) * LANES]


def _load_token_major(ref, k):
    t = ref.shape[0] // k
    return [ref[pl.ds(s, t, stride=k), :] for s in range(k)]


def _pack_bf16_pair(a, b):
    ra = pltpu.bitcast(a.astype(BF16).astype(F32), U32)
    rb = pltpu.bitcast(b.astype(BF16).astype(F32), U32)
    return ra | (rb >> 16)


def _unpack_bf16_pair(p):
    a = pltpu.bitcast(p & jnp.uint32(0xFFFF0000), F32).astype(BF16)
    b = pltpu.bitcast(p << 16, F32).astype(BF16)
    return a, b


def _wo_router_kernel(o_real_ref, o_tail_ref, w_ref, h_ref, g_ref, r_ref,
                      hout_ref, xpk_ref, route_ref, cnt_ref, carry_ref, *, n_real_tiles):
    i = pl.program_id(0)
    tm = h_ref.shape[0]

    @pl.when(i == 0)
    def _():
        carry_ref[...] = jnp.zeros_like(carry_ref)

    o = _attn_out_tile(o_real_ref, o_tail_ref, n_real_tiles)
    th = tm // ROW_SPLIT
    parts = [slice(p * th, (p + 1) * th) for p in range(ROW_SPLIT)]
    half = D_MODEL // 2
    r2 = r_ref[...]
    lane = lax.broadcasted_iota(I32, (th, LANES), 1)
    rr = lax.broadcasted_iota(I32, (th, th), 0)
    cc = lax.broadcasted_iota(I32, (th, th), 1)
    lower = jnp.where(cc < rr, 1.0, 0.0).astype(BF16)

    ys = []
    for rs in parts:
        hn = _dot(o[rs], w_ref[...]) + h_ref[rs, :]
        hout_ref[rs, :] = hn
        ys.append(_rms(hn, g_ref[...]))
    logits = []
    for p, y in enumerate(ys):
        _store_token_major(xpk_ref, _pack_bf16_pair(y[:, :half], y[:, half:]), row0=p * th)
        yhi = y.astype(BF16)
        ylo = (y - yhi.astype(F32)).astype(BF16)
        t = _dot(yhi, r2)
        logits.append(t[:, :LANES] + t[:, LANES:] + _dot(ylo, r2[:, :LANES]))
    carry = carry_ref[0:1, :]
    for rs, lg in zip(parts, logits):
        lm = jnp.where(lane < N_EXPERTS, lg, -jnp.inf)
        m1 = jnp.max(lm, axis=-1, keepdims=True)
        i1 = jnp.min(jnp.where(lm == m1, lane, LANES), axis=-1, keepdims=True)
        lm2 = jnp.where(lane == i1, -jnp.inf, lm)
        m2 = jnp.max(lm2, axis=-1, keepdims=True)
        i2 = jnp.min(jnp.where(lm2 == m2, lane, LANES), axis=-1, keepdims=True)
        e21 = jnp.exp(m2 - m1)
        g0 = 1.0 / (1.0 + e21)
        g1 = e21 / (1.0 + e21)
        onehot = ((lane == i1) | (lane == i2)).astype(F32)
        before = _dot(lower, onehot.astype(BF16)) + carry
        rank1 = jnp.sum(jnp.where(lane == i1, before, 0.0), axis=-1, keepdims=True)
        rank2 = jnp.sum(jnp.where(lane == i2, before, 0.0), axis=-1, keepdims=True)
        carry = carry + jnp.sum(onehot, axis=0, keepdims=True)
        out = jnp.where(lane == 0, i1.astype(F32), 0.0)
        out = jnp.where(lane == 1, i2.astype(F32), out)
        out = jnp.where(lane == 2, rank1, out)
        out = jnp.where(lane == 3, rank2, out)
        out = jnp.where(lane == 4, g0, out)
        out = jnp.where(lane == 5, g1, out)
        route_ref[rs, :] = out
    carry_ref[...] = jnp.broadcast_to(carry, carry_ref.shape)
    cnt_ref[...] = jnp.broadcast_to(carry, cnt_ref.shape)


def _wo_router_call(o_real, o_tail, w, h, g, r2, *, tm):
    ntot = h.shape[0]
    nrt = o_real.shape[0] // tm
    row = lambda i: (i, 0)
    fix = lambda i: (0, 0)
    return pl.pallas_call(
        functools.partial(_wo_router_kernel, n_real_tiles=nrt),
        grid=(ntot // tm,),
        in_specs=_attn_out_specs(tm, nrt) + [
            pl.BlockSpec((ATTN_W, D_MODEL), fix),
            pl.BlockSpec((tm, D_MODEL), row),
            pl.BlockSpec((1, D_MODEL), fix),
            pl.BlockSpec((D_MODEL, 2 * LANES), fix),
        ],
        out_specs=[
            pl.BlockSpec((tm, D_MODEL), row),
            pl.BlockSpec((tm * XPK_ROWS, LANES), row),
            pl.BlockSpec((tm, LANES), row),
            pl.BlockSpec((8, LANES), fix),
        ],
        out_shape=[
            jax.ShapeDtypeStruct((ntot, D_MODEL), F32),
            jax.ShapeDtypeStruct((ntot * XPK_ROWS, LANES), U32),
            jax.ShapeDtypeStruct((ntot, LANES), F32),
            jax.ShapeDtypeStruct((8, LANES), F32),
        ],
        scratch_shapes=[pltpu.VMEM((8, LANES), F32)],
        compiler_params=_params(("arbitrary",)),
        name="wo_res_norm_router",
    )(o_real, o_tail, w, h, g, r2)


def _swiglu_chunk(x, wg, wu, wd):
    g = _dot(x, wg)
    u = _dot(x, wu)
    a = (g * (1.0 / (1.0 + jnp.exp(-g)))) * u
    return _dot(a.astype(BF16), wd)


def _ffn_dense_kernel(*refs, n_src, src_rows, tm, tf):
    x_refs, h_refs = refs[:n_src], refs[n_src:2 * n_src]
    wg_ref, wu_ref, wd_ref, g_ref, hout_ref, xn_ref = refs[2 * n_src:]
    x = _select_source(x_refs, src_rows, tm)
    hn = _select_source(h_refs, src_rows, tm)
    for c in range(wg_ref.shape[1] // tf):
        cs = slice(c * tf, (c + 1) * tf)
        hn = hn + _swiglu_chunk(x, wg_ref[:, cs], wu_ref[:, cs], wd_ref[cs, :])
    hout_ref[...] = hn
    xn_ref[...] = _rms(hn, g_ref[...]).astype(BF16)


def _ffn_dense_call(x_sources, wg, wu, wd, h_sources, g, *, tm, tf):
    src_rows = tuple(s.shape[0] for s in h_sources)
    ntot = sum(src_rows)
    fdim = wg.shape[1]
    row = lambda i: (i, 0)
    fix = lambda i: (0, 0)
    once = pl.Buffered(1)
    return pl.pallas_call(
        functools.partial(_ffn_dense_kernel, n_src=len(h_sources), src_rows=src_rows, tm=tm, tf=tf),
        grid=(ntot // tm,),
        in_specs=_source_specs(x_sources, tm) + _source_specs(h_sources, tm) + [
            pl.BlockSpec((D_MODEL, fdim), fix, pipeline_mode=once),
            pl.BlockSpec((D_MODEL, fdim), fix, pipeline_mode=once),
            pl.BlockSpec((fdim, D_MODEL), fix, pipeline_mode=once),
            pl.BlockSpec((1, D_MODEL), fix),
        ],
        out_specs=[pl.BlockSpec((tm, D_MODEL), row), pl.BlockSpec((tm, D_MODEL), row)],
        out_shape=[jax.ShapeDtypeStruct((ntot, D_MODEL), F32),
                   jax.ShapeDtypeStruct((ntot, D_MODEL), BF16)],
        compiler_params=_params(("parallel",)),
        name="ffn_dense",
    )(*x_sources, *h_sources, wg, wu, wd, g)


def _ffn_expert_kernel(be_ref, nu_ref, xs_ref, wg_ref, wu_ref, wd_ref, ys_ref, *, tf):
    del be_ref
    used = pl.program_id(0) < nu_ref[0]

    @pl.when(used)
    def _():
        halves = [_unpack_bf16_pair(blk) for blk in _load_token_major(xs_ref, XPK_ROWS)]
        x = jnp.concatenate([a for a, _ in halves] + [b for _, b in halves], axis=1)
        acc = None
        for c in range(wg_ref.shape[1] // tf):
            cs = slice(c * tf, (c + 1) * tf)
            y = _swiglu_chunk(x, wg_ref[:, cs], wu_ref[:, cs], wd_ref[cs, :])
            acc = y if acc is None else acc + y
        _store_token_major(ys_ref, acc)

    @pl.when(jnp.logical_not(used))
    def _():
        ys_ref[...] = jnp.zeros_like(ys_ref)


def _ffn_expert_call(block_e, n_used, xs, wg, wu, wd, *, tm, tf):
    rows = xs.shape[0] // XPK_ROWS
    once = pl.Buffered(1)
    wmap = lambda i, be, nu: (be[i], 0, 0)
    grid_spec = pltpu.PrefetchScalarGridSpec(
        num_scalar_prefetch=2,
        grid=(rows // tm,),
        in_specs=[
            pl.BlockSpec((tm * XPK_ROWS, LANES), lambda i, be, nu: (i, 0)),
            pl.BlockSpec((None,) + wg.shape[1:], wmap, pipeline_mode=once),
            pl.BlockSpec((None,) + wu.shape[1:], wmap, pipeline_mode=once),
            pl.BlockSpec((None,) + wd.shape[1:], wmap, pipeline_mode=once),
        ],
        out_specs=pl.BlockSpec((tm * YS_ROWS, LANES), lambda i, be, nu: (i, 0)),
    )
    return pl.pallas_call(
        functools.partial(_ffn_expert_kernel, tf=tf),
        grid_spec=grid_spec,
        out_shape=jax.ShapeDtypeStruct((rows * YS_ROWS, LANES), F32),
        compiler_params=_params(("arbitrary",)),
        name="ffn_expert",
    )(block_e, n_used, xs, wg, wu, wd)


DMA_UNROLL = 8


def _token_copy(src, src_tok, dst, dst_tok, rows, sem):
    return pltpu.make_async_copy(src.at[pl.ds(pl.multiple_of(src_tok * rows, rows), rows)],
                                 dst.at[pl.ds(pl.multiple_of(dst_tok * rows, rows), rows)], sem)


def _dispatch_kernel(dest_ref, x_ref, xs_in_ref, xs_ref, sem):
    del xs_in_ref
    td = x_ref.shape[0] // XPK_ROWS

    def issue(t, carry):
        for s in range(2):
            _token_copy(x_ref, t, xs_ref, dest_ref[0, s, t], XPK_ROWS, sem).start(priority=s)
        return carry

    lax.fori_loop(0, td, issue, 0, unroll=DMA_UNROLL)

    def drain(t, carry):
        for s in range(2):
            _token_copy(x_ref, 0, xs_ref, 0, XPK_ROWS, sem).wait()
        return carry

    lax.fori_loop(0, td, drain, 0, unroll=DMA_UNROLL)


def _dispatch_call(dest, xpk, xs_init, *, td):
    ntot = xpk.shape[0] // XPK_ROWS
    return pl.pallas_call(
        _dispatch_kernel,
        grid=(ntot // td,),
        in_specs=[
            pl.BlockSpec((1, 2, td), lambda i: (i, 0, 0), memory_space=pltpu.SMEM),
            pl.BlockSpec((td * XPK_ROWS, LANES), lambda i: (i, 0)),
            pl.BlockSpec(memory_space=pl.ANY),
        ],
        out_specs=pl.BlockSpec(memory_space=pl.ANY),
        out_shape=jax.ShapeDtypeStruct(xs_init.shape, U32),
        input_output_aliases={2: 0},
        scratch_shapes=[pltpu.SemaphoreType.DMA(())],
        compiler_params=_params(("arbitrary",)),
        name="moe_dispatch",
    )(dest, xpk, xs_init)


def _combine_kernel(dest_ref, dest_next_ref, ys_ref, h_ref, route_ref, g_ref, out_ref,
                    y0_ref, y1_ref, sems):
    i = pl.program_id(0)
    n = pl.num_programs(0)
    tc = h_ref.shape[0]
    bufs = (y0_ref, y1_ref)
    slot = i % 2

    def issue(d_ref, sl):
        def body(t, carry):
            for s in range(2):
                _token_copy(ys_ref, d_ref[0, s, t], bufs[s].at[sl], t, YS_ROWS,
                            sems.at[sl]).start(priority=s)
            return carry

        lax.fori_loop(0, tc, body, 0, unroll=DMA_UNROLL)

    @pl.when(i == 0)
    def _():
        issue(dest_ref, 0)

    @pl.when(i + 1 < n)
    def _():
        issue(dest_next_ref, 1 - slot)

    def drain(t, carry):
        for s in range(2):
            _token_copy(ys_ref, 0, bufs[s].at[slot], 0, YS_ROWS, sems.at[slot]).wait()
        return carry

    lax.fori_loop(0, tc, drain, 0, unroll=DMA_UNROLL)

    r = route_ref[...]
    lane = lax.broadcasted_iota(I32, r.shape, 1)
    g0 = jnp.sum(jnp.where(lane == 4, r, 0.0), axis=-1, keepdims=True)
    g1 = jnp.sum(jnp.where(lane == 5, r, 0.0), axis=-1, keepdims=True)
    y0 = _load_token_major(y0_ref.at[slot], YS_ROWS)
    y1 = _load_token_major(y1_ref.at[slot], YS_ROWS)
    moe = jnp.concatenate([a * g0 + b * g1 for a, b in zip(y0, y1)], axis=1)
    out_ref[...] = _rms(h_ref[...] + moe, g_ref[...])


def _combine_call(dest, ys, h, route, g, *, tc, tile0, ntiles):
    return pl.pallas_call(
        _combine_kernel,
        grid=(ntiles,),
        in_specs=[
            pl.BlockSpec((1, 2, tc), lambda i: (tile0 + i, 0, 0), memory_space=pltpu.SMEM),
            pl.BlockSpec((1, 2, tc), lambda i: (tile0 + jnp.minimum(i + 1, ntiles - 1), 0, 0),
                         memory_space=pltpu.SMEM),
            pl.BlockSpec(memory_space=pl.ANY),
            pl.BlockSpec((tc, D_MODEL), lambda i: (tile0 + i, 0)),
            pl.BlockSpec((tc, LANES), lambda i: (tile0 + i, 0)),
            pl.BlockSpec((1, D_MODEL), lambda i: (0, 0)),
        ],
        out_specs=pl.BlockSpec((tc, D_MODEL), lambda i: (i, 0)),
        out_shape=jax.ShapeDtypeStruct((ntiles * tc, D_MODEL), F32),
        scratch_shapes=[pltpu.VMEM((2, tc * YS_ROWS, LANES), F32), pltpu.VMEM((2, tc * YS_ROWS, LANES), F32),
                        pltpu.SemaphoreType.DMA((2,))],
        compiler_params=_params(("arbitrary",)),
        name="moe_combine_norm",
    )(dest, dest, ys, h, route, g)


def _mla_proj_kernel(x_ref, wd_ref, qn_ref, kvn_ref, wuq_ref, wuk_ref, wuvt_ref,
                     c_ref, s1_ref, s2_ref, q_ref, k_ref, vt_ref):
    tm = x_ref.shape[0]
    th = tm // MLA_PROJ_SPLIT
    parts = [slice(p * th, (p + 1) * th) for p in range(MLA_PROJ_SPLIT)]
    half = MLA_ROPE // 2
    scale = (MLA_QK ** -0.5) * LOG2E
    lats = [_dot(x_ref[rs, :], wd_ref[...]) for rs in parts]
    ups = []
    for lat in lats:
        cq = _rms(lat[:, :MLA_Q_RANK], qn_ref[...]).astype(BF16)
        ckv = _rms(lat[:, MLA_Q_RANK:MLA_Q_RANK + MLA_KV_RANK], kvn_ref[...]).astype(BF16)
        ups.append((_dot(cq, wuq_ref[...]),
                    _dot(ckv, wuk_ref[...]),
                    _dot_nt(wuvt_ref[...], ckv)))
    for rs, lat, (q, k, vt) in zip(parts, lats, ups):
        c, s1, s2 = c_ref[rs, :], s1_ref[rs, :], s2_ref[rs, :]
        kr = _rope(lat[:, MLA_Q_RANK + MLA_KV_RANK:], c, s1, s2, half)
        for h in range(MLA_HEADS):
            sl = slice(h * LANES, (h + 1) * LANES)
            q_ref[rs, sl] = (_rope(q[:, sl], c, s1, s2, half) * scale).astype(BF16)
            k_ref[rs, sl] = (k[:, sl] + kr).astype(BF16)
        vt_ref[:, rs] = vt.astype(BF16)


def _mla_proj_call(x, wd, qn, kvn, wuq, wuk, wuv, tabs, *, tm, n_real, seq):
    ntot = x.shape[0]
    nrt, tps = n_real // tm, seq // tm
    row = lambda i: (i, 0)
    fix = lambda i: (0, 0)
    tab_spec = pl.BlockSpec((tm, LANES), lambda i: (_table_index(i, nrt, tps), 0))
    return pl.pallas_call(
        _mla_proj_kernel,
        grid=(ntot // tm,),
        in_specs=[
            pl.BlockSpec((tm, D_MODEL), row),
            pl.BlockSpec(wd.shape, fix),
            pl.BlockSpec((1, MLA_Q_RANK), fix),
            pl.BlockSpec((1, MLA_KV_RANK), fix),
            pl.BlockSpec(wuq.shape, fix),
            pl.BlockSpec(wuk.shape, fix),
            pl.BlockSpec(wuv.shape[::-1], fix),
            tab_spec, tab_spec, tab_spec,
        ],
        out_specs=[
            pl.BlockSpec((tm, MLA_QK_W), row),
            pl.BlockSpec((tm, MLA_QK_W), row),
            pl.BlockSpec((ATTN_W, tm), lambda i: (0, i)),
        ],
        out_shape=[
            jax.ShapeDtypeStruct((ntot, MLA_QK_W), BF16),
            jax.ShapeDtypeStruct((ntot, MLA_QK_W), BF16),
            jax.ShapeDtypeStruct((ATTN_W, ntot), BF16),
        ],
        compiler_params=_params(("parallel",)),
        name="mla_proj",
    )(x, wd, qn, kvn, wuq, wuk, wuv.T, *tabs)


META_PER_BLOCK = LANES // N_META
REDUCE_ROWS = 64
SCORES_AHEAD = 1
VALUES_BEHIND = 1
MLA_QUERY_TILE = 256


def _col_reduce(x, op):
    rows, n = x.shape
    if rows > REDUCE_ROWS and rows % REDUCE_ROWS == 0:
        x = op(x.reshape(rows // REDUCE_ROWS, REDUCE_ROWS, n), axis=0)
    return op(x, axis=0, keepdims=True)


def _mla_attn_kernel(q_ref, kr_ref, kt_ref, vtr_ref, vtt_ref, o_ref, ot_ref):
    b = pl.program_id(0)
    r = lax.broadcasted_iota(I32, (LANES, 1), 0)
    mine = jnp.right_shift(r, _log2(N_META)) == (b % META_PER_BLOCK)
    mbias = jnp.where(mine, 0.0, -jnp.inf).astype(F32)
    tq = min(MLA_QUERY_TILE, q_ref.shape[0])
    units = [(g, h) for g in range(q_ref.shape[0] // tq) for h in range(MLA_HEADS)]

    def scores(g, h):
        hs = slice(h * LANES, (h + 1) * LANES)
        qh = q_ref[g * tq:(g + 1) * tq, hs]
        return _dot_nt(kr_ref[:, hs], qh), _dot_nt(kt_ref[:, hs], qh) + mbias

    def weighted_values(g, h, p_r, p_m, den):
        vs = slice(h * MLA_V, (h + 1) * MLA_V)
        acc = _dot(vtr_ref[vs, :], p_r) + _dot(vtt_ref[vs, :], p_m)
        ot_ref[vs, g * tq:(g + 1) * tq] = acc / den

    ahead = [scores(*u) for u in units[:SCORES_AHEAD]]
    pending = []
    for idx, unit in enumerate(units):
        s_r, s_m = ahead.pop(0)
        if idx + SCORES_AHEAD < len(units):
            ahead.append(scores(*units[idx + SCORES_AHEAD]))
        m = jnp.maximum(_col_reduce(s_r, jnp.max), _col_reduce(s_m, jnp.max))
        p_r = jnp.exp2(s_r - m)
        p_m = jnp.exp2(s_m - m)
        den = _col_reduce(p_r, jnp.sum) + _col_reduce(p_m, jnp.sum)
        pending.append((*unit, p_r.astype(BF16), p_m.astype(BF16), den))
        if len(pending) > VALUES_BEHIND:
            weighted_values(*pending.pop(0))
    for item in pending:
        weighted_values(*item)
    o_ref[...] = ot_ref[...].T.astype(BF16)


def _mla_attn_call(q, k, vt, *, batch, seq, n_real, tq):
    nq = seq // tq
    tail0 = n_real // LANES
    qmap = lambda b, j: (b * nq + j, 0)
    return pl.pallas_call(
        _mla_attn_kernel,
        grid=(batch, nq),
        in_specs=[
            pl.BlockSpec((tq, MLA_QK_W), qmap),
            pl.BlockSpec((seq, MLA_QK_W), lambda b, j: (b, 0)),
            pl.BlockSpec((LANES, MLA_QK_W), lambda b, j: (tail0 + b // META_PER_BLOCK, 0)),
            pl.BlockSpec((ATTN_W, seq), lambda b, j: (0, b)),
            pl.BlockSpec((ATTN_W, LANES), lambda b, j: (0, tail0 + b // META_PER_BLOCK)),
        ],
        out_specs=pl.BlockSpec((tq, ATTN_W), qmap),
        out_shape=jax.ShapeDtypeStruct((n_real, ATTN_W), BF16),
        scratch_shapes=[pltpu.VMEM((ATTN_W, tq), F32)],
        compiler_params=_params(("parallel", "arbitrary")),
        name="mla_attn",
    )(q, k, k, vt, vt)


def _mla_meta_kernel(q_ref, kr_ref, kt_ref, vtr_ref, vtt_ref, o_ref, *, batch):
    i = pl.program_id(0)
    ncol = MLA_HEADS * N_META

    @pl.when(i < batch)
    def _():
        qi = lax.broadcasted_iota(I32, (N_META, ncol), 0)
        ci = lax.broadcasted_iota(I32, (N_META, ncol), 1)
        spread = jnp.where((ci & (N_META - 1)) == qi, 1.0, 0.0).astype(BF16)
        w = lax.dot_general(q_ref[...], spread, (((0,), (0,)), ((), ())),
                            preferred_element_type=F32)
        wr = lax.broadcasted_iota(I32, w.shape, 0)
        wc = lax.broadcasted_iota(I32, w.shape, 1)
        on_diag = jnp.right_shift(wr, _log2(LANES)) == jnp.right_shift(wc, _log2(N_META))
        w = jnp.where(on_diag, w, 0.0).astype(BF16)
        r = lax.broadcasted_iota(I32, (LANES, 1), 0)
        mine = jnp.right_shift(r, _log2(N_META)) == (i % META_PER_BLOCK)
        mbias = jnp.where(mine, 0.0, -jnp.inf).astype(F32)
        s_r = _dot(kr_ref[...], w)
        s_m = _dot(kt_ref[...], w) + mbias
        m = jnp.maximum(_col_reduce(s_r, jnp.max), _col_reduce(s_m, jnp.max))
        p_r = jnp.exp2(s_r - m)
        p_m = jnp.exp2(s_m - m)
        den = _col_reduce(p_r, jnp.sum) + _col_reduce(p_m, jnp.sum)
        ot = (_dot(vtr_ref[...], p_r.astype(BF16)) + _dot(vtt_ref[...], p_m.astype(BF16))) / den
        o_all = ot.T
        lane = lax.broadcasted_iota(I32, (N_META, ATTN_W), 1)
        o = jnp.zeros((N_META, ATTN_W), F32)
        for h in range(MLA_HEADS):
            rows = o_all[h * N_META:(h + 1) * N_META, :]
            o = jnp.where(jnp.right_shift(lane, _log2(MLA_V)) == h, rows, o)
        o_ref[...] = o.astype(BF16)

    @pl.when(i >= batch)
    def _():
        o_ref[...] = jnp.zeros_like(o_ref)


def _mla_meta_call(q, k, vt, *, batch, seq, n_real):
    ntot = q.shape[0]
    mrow = n_real // N_META
    tail0 = n_real // LANES
    ntail_blocks = (ntot - n_real) // N_META
    last = (batch - 1) // META_PER_BLOCK
    bmap = lambda i: jnp.minimum(i, batch - 1)
    tmap = lambda i: jnp.minimum(i // META_PER_BLOCK, last)
    return pl.pallas_call(
        functools.partial(_mla_meta_kernel, batch=batch),
        grid=(ntail_blocks,),
        in_specs=[
            pl.BlockSpec((N_META, MLA_QK_W), lambda i: (mrow + i, 0)),
            pl.BlockSpec((seq, MLA_QK_W), lambda i: (bmap(i), 0)),
            pl.BlockSpec((LANES, MLA_QK_W), lambda i: (tail0 + tmap(i), 0)),
            pl.BlockSpec((ATTN_W, seq), lambda i: (0, bmap(i))),
            pl.BlockSpec((ATTN_W, LANES), lambda i: (0, tail0 + tmap(i))),
        ],
        out_specs=pl.BlockSpec((N_META, ATTN_W), lambda i: (i, 0)),
        out_shape=jax.ShapeDtypeStruct((ntot - n_real, ATTN_W), BF16),
        compiler_params=_params(("arbitrary",)),
        name="mla_attn_meta",
    )(q, k, k, vt, vt)


def _prep_gqa(a_wqkv, a_wo):
    d = a_wqkv.shape[0]
    nq = GQA_HEADS * GQA_HD
    nkv = GQA_KV * GQA_HD
    wq = a_wqkv[:, :nq] * (GQA_HD ** -0.5)

    def dup(w):
        w = w.reshape(d, GQA_KV, 1, GQA_HD)
        return jnp.broadcast_to(w, (d, GQA_KV, 2, GQA_HD)).reshape(d, 2 * nkv)

    w = jnp.concatenate([wq, dup(a_wqkv[:, nq:nq + nkv]), dup(a_wqkv[:, nq + nkv:])], axis=1)
    return w.astype(BF16), a_wo.astype(BF16)


def _prep_mla(b_wdkv, b_wuq, b_wukv):
    d = b_wdkv.shape[0]
    lat = MLA_Q_RANK + MLA_KV_RANK
    z = lambda n: jnp.zeros((d, n), F32)
    wd = jnp.concatenate([b_wdkv[:, :lat], z(MLA_NOPE), b_wdkv[:, lat:], z(LANES - MLA_QK)], axis=1)
    wuq = b_wuq.reshape(MLA_Q_RANK, MLA_HEADS, MLA_QK)
    wuq = jnp.pad(wuq, ((0, 0), (0, 0), (0, LANES - MLA_QK))).reshape(MLA_Q_RANK, MLA_HEADS * LANES)
    wukv = b_wukv.reshape(MLA_KV_RANK, MLA_HEADS, MLA_NOPE + MLA_V)
    wuk = jnp.pad(wukv[:, :, :MLA_NOPE], ((0, 0), (0, 0), (0, LANES - MLA_NOPE)))
    wuk = wuk.reshape(MLA_KV_RANK, MLA_HEADS * LANES)
    wuv = wukv[:, :, MLA_NOPE:].reshape(MLA_KV_RANK, MLA_HEADS * MLA_V)
    return wd.astype(BF16), wuq.astype(BF16), wuk.astype(BF16), wuv.astype(BF16)


def _prep_router(m_router):
    hi = m_router.astype(BF16)
    lo = (m_router - hi.astype(F32)).astype(BF16)
    pad = lambda r: jnp.pad(r, ((0, 0), (0, LANES - N_EXPERTS)))
    return jnp.concatenate([pad(hi), pad(lo)], axis=1)


def _positions(seq, n_tail, n_meta_rows):
    real = N_META + jnp.arange(seq, dtype=F32)
    t = jnp.arange(n_tail)
    tail = jnp.where(t < n_meta_rows, t % N_META, 0).astype(F32)
    return jnp.concatenate([real, tail])


def _trunk_flat(x_groups, meta, norm_mix, norm_ffn, norm_final,
                a_wqkv, a_wo, a_sink,
                b_wdkv, b_qnorm, b_kvnorm, b_wuq, b_wukv, b_wo,
                f_wg, f_wu, f_wd,
                m_router, m_wg, m_wu, m_wd):
    seq = x_groups[0].shape[1]
    batches = [x.shape[0] for x in x_groups]
    batch = sum(batches)
    n_real = batch * seq
    n_meta_rows = batch * N_META
    big = _pick_tile(seq, (1024, 512, 256, 128))
    tp = min(big, 512)
    ntot = -(-(n_real + n_meta_rows) // big) * big
    n_tail = ntot - n_real

    tail = jnp.concatenate([
        jnp.broadcast_to(meta.astype(F32)[None], (batch, N_META, D_MODEL)).reshape(n_meta_rows, D_MODEL),
        jnp.zeros((n_tail - n_meta_rows, D_MODEL), F32)], axis=0)
    h_sources = [x.reshape(-1, D_MODEL) for x in x_groups] + [tail]

    pos = _positions(seq, n_tail, n_meta_rows)
    tabs_a = _rope_lane_tables(pos, GQA_THETA, GQA_ROT, GQA_HD, 0)
    tabs_b = _rope_lane_tables(pos, MLA_THETA, MLA_ROPE, LANES, MLA_NOPE)
    row = lambda v: v.reshape(1, -1).astype(F32)

    wqkv, wo_a = _prep_gqa(a_wqkv[0], a_wo[0])
    q, k, v = _qkv_call(h_sources, row(norm_mix[0]), wqkv, tabs_a, tm=tp, n_real=n_real, seq=seq)
    sink = a_sink[0].astype(F32) * LOG2E
    h_real, xn_real = _win_attn_call(sink, q, k, v, wo_a, h_sources[:-1], row(norm_ffn[0]),
                                     batch=batch, seq=seq, n_real=n_real)
    o_tail = _win_meta_call(sink, q, k, v, batch=batch, seq=seq, n_real=n_real)
    h_tail, xn_tail = _wo_call(o_tail, wo_a, tail, row(norm_ffn[0]), tm=tp)
    fdim = f_wg.shape[2]
    tf = _pick_tile(fdim, (512, 256, 128))
    h, xn = _ffn_dense_call([xn_real, xn_tail], f_wg[0].astype(BF16), f_wu[0].astype(BF16),
                            f_wd[0].astype(BF16), [h_real, h_tail], row(norm_mix[1]), tm=tp, tf=tf)

    wd, wuq, wuk, wuv = _prep_mla(b_wdkv[0], b_wuq[0], b_wukv[0])
    q, k, vt = _mla_proj_call(xn, wd, row(b_qnorm[0]), row(b_kvnorm[0]), wuq, wuk, wuv, tabs_b,
                              tm=tp, n_real=n_real, seq=seq)
    tq = _pick_tile(seq, (512, 256, 128))
    o = _mla_attn_call(q, k, vt, batch=batch, seq=seq, n_real=n_real, tq=tq)
    o_tail = _mla_meta_call(q, k, vt, batch=batch, seq=seq, n_real=n_real)
    tr = tp
    h, xpk, route, counts = _wo_router_call(o, o_tail, b_wo[0].astype(BF16), h, row(norm_ffn[1]),
                                            _prep_router(m_router[0]), tm=tr)

    tmx = big
    counts = counts[0, :N_EXPERTS].astype(I32)
    padded = ((counts + tmx - 1) // tmx) * tmx
    pend = jnp.cumsum(padded)
    pstart = pend - padded
    e_idx = route[:, 0:2].astype(I32)
    rank = route[:, 2:4].astype(I32)
    dest = pstart[e_idx] + rank
    n_assign = 2 * ntot
    n_blocks = -(-n_assign // tmx) + N_EXPERTS
    rows = n_blocks * tmx
    block_e = jnp.clip(jnp.searchsorted(pend, jnp.arange(n_blocks, dtype=I32) * tmx, side="right"),
                       0, N_EXPERTS - 1).astype(I32)
    n_used = (pend[-1] // tmx).astype(I32).reshape(1)
    block_e = jnp.where(jnp.arange(n_blocks) < n_used[0], block_e, block_e[jnp.maximum(n_used[0] - 1, 0)])

    td = tp
    dest_t = dest.reshape(ntot // td, td, 2).transpose(0, 2, 1)
    xs = _dispatch_call(dest_t, xpk, jnp.zeros((rows * XPK_ROWS, LANES), U32), td=td)
    edim = m_wg.shape[3]
    tfx = _pick_tile(edim, (512, 256, 128))
    ys = _ffn_expert_call(block_e, n_used, xs, m_wg[0].astype(BF16), m_wu[0].astype(BF16),
                          m_wd[0].astype(BF16), tm=tmx, tf=tfx)

    outs = []
    tile0 = 0
    for b, x in zip(batches, x_groups):
        ntiles = b * seq // td
        y = _combine_call(dest_t, ys, h, route, row(norm_final), tc=td, tile0=tile0, ntiles=ntiles)
        outs.append(y.reshape(b, seq, D_MODEL))
        tile0 += ntiles
    return tuple(outs)


def kernel(x_prompt, x_sample, meta, norm_mix, norm_ffn, norm_final, a_wqkv, a_wo, a_sink, b_wdkv, b_qnorm, b_kvnorm, b_wuq, b_wukv, b_wo, f_wg, f_wu, f_wd, m_router, m_wg, m_wu, m_wd):
    assert x_prompt.shape[1] == x_sample.shape[1] and x_prompt.shape[1] % BLK == 0
    assert norm_mix.shape[0] == 2, "two layers: windowed GQA + dense FFN, then MLA + MoE"
    return _trunk_flat((x_prompt, x_sample), meta, norm_mix, norm_ffn, norm_final,
                       a_wqkv, a_wo, a_sink,
                       b_wdkv, b_qnorm, b_kvnorm, b_wuq, b_wukv, b_wo,
                       f_wg, f_wu, f_wd,
                       m_router, m_wg, m_wu, m_wd)
```

```python
import functools

import jax
import jax.numpy as jnp
from jax import lax
from jax.experimental import pallas as pl
from jax.experimental.pallas import tpu as pltpu

F32 = jnp.float32
BF16 = jnp.bfloat16
U32 = jnp.uint32
I32 = jnp.int32

D_MODEL = 1024
N_META = 16
RMS_EPS = 1e-6
BLK = 128
GQA_HEADS = 16
GQA_KV = 4
GQA_GROUP = 4
GQA_HD = 64
GQA_ROT = 16
GQA_THETA = 500000.0
MLA_HEADS = 16
MLA_NOPE = 64
MLA_ROPE = 32
MLA_V = 64
MLA_Q_RANK = 384
MLA_KV_RANK = 256
MLA_THETA = 10000.0
MLA_QK = MLA_NOPE + MLA_ROPE
N_EXPERTS = 8
LOG2E = 1.4426950408889634

LANES = 128
ATTN_W = GQA_HEADS * GQA_HD
MLA_QK_W = MLA_HEADS * LANES
assert ATTN_W == MLA_HEADS * MLA_V
VMEM_LIMIT = 56 * 1024 * 1024


def _params(sem, vmem=VMEM_LIMIT):
    return pltpu.CompilerParams(dimension_semantics=sem, vmem_limit_bytes=vmem)


def _rms(x, g):
    ms = jnp.mean(x * x, axis=-1, keepdims=True)
    return x * lax.rsqrt(ms + RMS_EPS) * g


def _dot(a, b):
    return jnp.dot(a, b, preferred_element_type=F32)


def _dot_nt(a, b):
    return lax.dot_general(a, b, (((1,), (1,)), ((), ())), preferred_element_type=F32)


def _log2(n):
    assert n & (n - 1) == 0, n
    return n.bit_length() - 1


def _pick_tile(n, candidates):
    for c in candidates:
        if n % c == 0:
            return c
    raise ValueError(f"no tile in {candidates} divides {n}")


def _rope_lane_tables(pos, theta, rot, period, offset):
    half = rot // 2
    inv = jnp.power(jnp.float32(theta), -(jnp.arange(0, rot, 2, dtype=F32) / rot))
    ang = pos[:, None] * inv[None, :]
    cos, sin = jnp.cos(ang), jnp.sin(ang)
    lane = jnp.arange(LANES)
    r = (lane % period) - offset
    is_x1 = (r >= 0) & (r < half)
    is_x2 = (r >= half) & (r < rot)
    f = jnp.clip(jnp.where(is_x2, r - half, r), 0, half - 1)
    cos_l = cos[:, f]
    sin_l = sin[:, f]
    c = jnp.where((is_x1 | is_x2)[None, :], cos_l, 1.0)
    s1 = jnp.where(is_x2[None, :], sin_l, 0.0)
    s2 = jnp.where(is_x1[None, :], -sin_l, 0.0)
    return c.astype(F32), s1.astype(F32), s2.astype(F32)


def _rope(x, c, s1, s2, half):
    return x * c + pltpu.roll(x, half, 1) * s1 + pltpu.roll(x, LANES - half, 1) * s2


def _source_specs(sources, tm, tile_of=lambda *grid: grid[0]):
    specs, start = [], 0
    for src in sources:
        n = src.shape[0] // tm
        specs.append(pl.BlockSpec(
            (tm, src.shape[1]),
            lambda *grid, start=start, n=n: (jnp.clip(tile_of(*grid) - start, 0, n - 1), 0)))
        start += n
    return specs


def _select_source(refs, src_rows, tm, tile=None):
    i = pl.program_id(0) if tile is None else tile
    ends, acc = [], 0
    for n in src_rows:
        acc += n // tm
        ends.append(acc)
    x = refs[-1][...]
    for r, end in reversed(list(zip(refs[:-1], ends[:-1]))):
        x = jnp.where(i < end, r[...], x)
    return x


def _qkv_kernel(*refs, n_src, src_rows, tm):
    h_refs = refs[:n_src]
    g_ref, w_ref, c_ref, s1_ref, s2_ref, q_ref, k_ref, v_ref = refs[n_src:]
    xn = _rms(_select_source(h_refs, src_rows, tm), g_ref[...]).astype(BF16)
    qkv = _dot(xn, w_ref[...])
    c, s1, s2 = c_ref[...], s1_ref[...], s2_ref[...]
    nq = GQA_HEADS * GQA_HD // LANES
    nk = GQA_KV
    for i in range(nq):
        q_ref[:, i * LANES:(i + 1) * LANES] = (_rope(
            qkv[:, i * LANES:(i + 1) * LANES], c, s1, s2, GQA_ROT // 2) * LOG2E).astype(BF16)
    for i in range(nk):
        lo = (nq + i) * LANES
        k_ref[:, i * LANES:(i + 1) * LANES] = _rope(
            qkv[:, lo:lo + LANES], c, s1, s2, GQA_ROT // 2).astype(BF16)
    v_ref[...] = qkv[:, (nq + nk) * LANES:].astype(BF16)


def _table_index(i, n_real_tiles, tiles_per_seq):
    return jnp.where(i < n_real_tiles, i % tiles_per_seq, tiles_per_seq + i - n_real_tiles)


def _qkv_call(h_sources, g, w, tabs, *, tm, n_real, seq):
    src_rows = tuple(s.shape[0] for s in h_sources)
    ntot = sum(src_rows)
    nrt, tps = n_real // tm, seq // tm
    tab_spec = pl.BlockSpec((tm, LANES), lambda i: (_table_index(i, nrt, tps), 0))
    return pl.pallas_call(
        functools.partial(_qkv_kernel, n_src=len(h_sources), src_rows=src_rows, tm=tm),
        grid=(ntot // tm,),
        in_specs=_source_specs(h_sources, tm) + [
            pl.BlockSpec((1, D_MODEL), lambda i: (0, 0)),
            pl.BlockSpec(w.shape, lambda i: (0, 0)),
            tab_spec, tab_spec, tab_spec,
        ],
        out_specs=[
            pl.BlockSpec((tm, ATTN_W), lambda i: (i, 0)),
            pl.BlockSpec((tm, GQA_KV * LANES), lambda i: (i, 0)),
            pl.BlockSpec((tm, GQA_KV * LANES), lambda i: (i, 0)),
        ],
        out_shape=[
            jax.ShapeDtypeStruct((ntot, ATTN_W), BF16),
            jax.ShapeDtypeStruct((ntot, GQA_KV * LANES), BF16),
            jax.ShapeDtypeStruct((ntot, GQA_KV * LANES), BF16),
        ],
        compiler_params=_params(("parallel",)),
        name="qkv_rope",
    )(*h_sources, g, w, *tabs)


def _gqa_attend(sink_ref, q_ref, o_ref, groups):
    def pad_of(bias, pieces):
        have = sum(p.shape[0] for p in pieces)
        return [jnp.zeros((bias.shape[1] - have, LANES), BF16)] if bias.shape[1] > have else []

    def scores(g, kv):
        rows, k_pieces, _, bias = groups[g]
        sl = slice(kv * LANES, (kv + 1) * LANES)
        kcat = jnp.concatenate([p[:, sl] for p in k_pieces] + pad_of(bias, k_pieces), axis=0)
        zero = jnp.zeros_like(kcat)
        lane_k = lax.broadcasted_iota(I32, kcat.shape, 1)
        lo = slice(2 * kv * LANES, (2 * kv + 1) * LANES)
        hi = slice((2 * kv + 1) * LANES, (2 * kv + 2) * LANES)
        qp = jnp.concatenate([q_ref[rows, lo], q_ref[rows, hi]], axis=0)
        return (_dot_nt(qp, jnp.where(lane_k < GQA_HD, kcat, zero)) + bias,
                _dot_nt(qp, jnp.where(lane_k >= GQA_HD, kcat, zero)) + bias)

    def softmax(g, kv, ss):
        nq = groups[g][3].shape[0] // 2
        row = lax.broadcasted_iota(I32, (2 * nq, 1), 0)
        out = []
        for par, s in enumerate(ss):
            head = kv * GQA_GROUP + par
            sink = jnp.where(row < nq, sink_ref[head], sink_ref[head + 2])
            m = jnp.maximum(jnp.max(s, axis=-1, keepdims=True), sink)
            p = jnp.exp2(s - m)
            den = jnp.sum(p, axis=-1, keepdims=True) + jnp.exp2(sink - m)
            out.append((p.astype(BF16), den))
        return out

    def weighted_values(g, kv, pd):
        rows, _, v_pieces, bias = groups[g]
        nq = bias.shape[0] // 2
        sl = slice(kv * LANES, (kv + 1) * LANES)
        vcat = jnp.concatenate([p[:, sl] for p in v_pieces] + pad_of(bias, v_pieces), axis=0)
        outs = [_dot(p, vcat) / den for p, den in pd]
        lane_o = lax.broadcasted_iota(I32, outs[0].shape, 1)
        o = jnp.where(lane_o < GQA_HD, outs[0], outs[1]).astype(BF16)
        o_ref[rows, 2 * kv * LANES:(2 * kv + 1) * LANES] = o[:nq]
        o_ref[rows, (2 * kv + 1) * LANES:(2 * kv + 2) * LANES] = o[nq:]

    units = [(g, kv) for g in range(len(groups)) for kv in range(GQA_KV)]
    nxt = scores(*units[0])
    pending = None
    for idx, unit in enumerate(units):
        ss = nxt
        if idx + 1 < len(units):
            nxt = scores(*units[idx + 1])
        pd = softmax(*unit, ss)
        if pending is not None:
            weighted_values(*pending)
        pending = (*unit, pd)
    weighted_values(*pending)


def _window_bias(prev_ok, next_ok):
    nkeys = 4 * BLK
    qi = lax.broadcasted_iota(I32, (2 * BLK, nkeys), 0) & (BLK - 1)
    c = lax.broadcasted_iota(I32, (2 * BLK, nkeys), 1)
    neg = jnp.float32(-jnp.inf)
    b_p = jnp.where((c >= qi) & prev_ok, 0.0, neg)
    b_n = jnp.where(((c - 2 * BLK) <= qi) & next_ok, 0.0, neg)
    b_m = jnp.where(c < 3 * BLK + N_META, 0.0, neg)
    return jnp.where(c < BLK, b_p, jnp.where(c < 2 * BLK, 0.0, jnp.where(c < 3 * BLK, b_n, b_m)))


def _win_attn_kernel(sink_ref, q_ref, *refs, nb, qb, n_src, src_rows):
    k_blk, km_ref = refs[:qb + 2], refs[qb + 2]
    v_blk, vm_ref = refs[qb + 3:2 * qb + 5], refs[2 * qb + 5]
    rest = refs[2 * qb + 6:]
    wo_ref, h_refs, g_ref = rest[0], rest[1:1 + n_src], rest[1 + n_src]
    hout_ref, xn_ref, o_scr = rest[2 + n_src:]
    j0 = pl.program_id(1) * qb
    groups = []
    for t in range(qb):
        bias = _window_bias(j0 + t > 0, j0 + t < nb - 1)
        groups.append((slice(t * BLK, (t + 1) * BLK),
                       (k_blk[t], k_blk[t + 1], k_blk[t + 2], km_ref),
                       (v_blk[t], v_blk[t + 1], v_blk[t + 2], vm_ref), bias))
    _gqa_attend(sink_ref, q_ref, o_scr, groups)
    tile = pl.program_id(0) * (nb // qb) + pl.program_id(1)
    hn = _dot(o_scr[...], wo_ref[...]) + _select_source(h_refs, src_rows, qb * BLK, tile)
    hout_ref[...] = hn
    xn_ref[...] = _rms(hn, g_ref[...]).astype(BF16)


def _win_attn_call(sink, q, k, v, wo, h_sources, g, *, batch, seq, n_real):
    nb = seq // BLK
    qb = _pick_tile(nb, (4, 2, 1))
    mrow = n_real // N_META
    kvw = GQA_KV * LANES
    tile_of = lambda b, j, s: b * (nb // qb) + j

    def qmap(b, j, s):
        return (tile_of(b, j, s), 0)

    def blk_spec(t):
        return pl.BlockSpec((BLK, kvw),
                            lambda b, j, s: (b * nb + jnp.clip(j * qb + t - 1, 0, nb - 1), 0))

    meta_spec = pl.BlockSpec((N_META, kvw), lambda b, j, s: (mrow + b, 0))
    kv_specs = [blk_spec(t) for t in range(qb + 2)] + [meta_spec]
    fix = lambda b, j, s: (0, 0)
    grid_spec = pltpu.PrefetchScalarGridSpec(
        num_scalar_prefetch=1,
        grid=(batch, nb // qb),
        in_specs=[pl.BlockSpec((qb * BLK, ATTN_W), qmap)] + kv_specs + kv_specs
        + [pl.BlockSpec((ATTN_W, D_MODEL), fix)] + _source_specs(h_sources, qb * BLK, tile_of)
        + [pl.BlockSpec((1, D_MODEL), fix)],
        out_specs=[pl.BlockSpec((qb * BLK, D_MODEL), qmap), pl.BlockSpec((qb * BLK, D_MODEL), qmap)],
        scratch_shapes=[pltpu.VMEM((qb * BLK, ATTN_W), BF16)],
    )
    n_kv = qb + 3
    return pl.pallas_call(
        functools.partial(_win_attn_kernel, nb=nb, qb=qb, n_src=len(h_sources),
                          src_rows=tuple(s.shape[0] for s in h_sources)),
        grid_spec=grid_spec,
        out_shape=[jax.ShapeDtypeStruct((n_real, D_MODEL), F32),
                   jax.ShapeDtypeStruct((n_real, D_MODEL), BF16)],
        compiler_params=_params(("parallel", "parallel")),
        name="win_attn_wo",
    )(sink, q, *([k] * n_kv), *([v] * n_kv), wo, *h_sources, g)


def _win_meta_kernel(sink_ref, q_ref, km_ref, kc_ref, vm_ref, vc_ref, o_ref, *, batch):
    i = pl.program_id(0)

    @pl.when(i < batch)
    def _():
        nkeys = 2 * BLK
        qp = lax.broadcasted_iota(I32, (2 * N_META, nkeys), 0) & (N_META - 1)
        c = lax.broadcasted_iota(I32, (2 * N_META, nkeys), 1)
        neg = jnp.float32(-jnp.inf)
        b_c = jnp.where((N_META + c - qp) <= BLK, 0.0, neg)
        bias = jnp.where(c < BLK, b_c, jnp.where(c < BLK + N_META, 0.0, neg))
        _gqa_attend(sink_ref, q_ref, o_ref,
                    [(slice(0, N_META), (kc_ref, km_ref), (vc_ref, vm_ref), bias)])

    @pl.when(i >= batch)
    def _():
        o_ref[...] = jnp.zeros_like(o_ref)


def _win_meta_call(sink, q, k, v, *, batch, seq, n_real):
    ntot = q.shape[0]
    nb = seq // BLK
    mrow = n_real // N_META
    ntail_blocks = (ntot - n_real) // N_META

    def tmap(i, s):
        return (mrow + i, 0)

    def cmap(i, s):
        return (jnp.minimum(i, batch - 1) * nb, 0)

    grid_spec = pltpu.PrefetchScalarGridSpec(
        num_scalar_prefetch=1,
        grid=(ntail_blocks,),
        in_specs=[
            pl.BlockSpec((N_META, ATTN_W), tmap),
            pl.BlockSpec((N_META, GQA_KV * LANES), tmap),
            pl.BlockSpec((BLK, GQA_KV * LANES), cmap),
            pl.BlockSpec((N_META, GQA_KV * LANES), tmap),
            pl.BlockSpec((BLK, GQA_KV * LANES), cmap),
        ],
        out_specs=pl.BlockSpec((N_META, ATTN_W), lambda i, s: (i, 0)),
    )
    return pl.pallas_call(
        functools.partial(_win_meta_kernel, batch=batch),
        grid_spec=grid_spec,
        out_shape=jax.ShapeDtypeStruct((ntot - n_real, ATTN_W), BF16),
        compiler_params=_params(("parallel",)),
        name="win_attn_meta",
    )(sink, q, k, k, v, v)


def _attn_out_tile(o_real_ref, o_tail_ref, n_real_tiles):
    is_real = pl.program_id(0) < n_real_tiles
    return jnp.where(is_real, o_real_ref[...], o_tail_ref[...])


def _attn_out_specs(tm, n_real_tiles):
    return [
        pl.BlockSpec((tm, ATTN_W), lambda i: (jnp.minimum(i, n_real_tiles - 1), 0)),
        pl.BlockSpec((tm, ATTN_W), lambda i: (jnp.maximum(i - n_real_tiles, 0), 0)),
    ]


def _wo_kernel(o_ref, w_ref, h_ref, g_ref, hout_ref, xn_ref):
    hn = _dot(o_ref[...], w_ref[...]) + h_ref[...]
    hout_ref[...] = hn
    xn_ref[...] = _rms(hn, g_ref[...]).astype(BF16)


def _wo_call(o, w, h, g, *, tm):
    n = o.shape[0]
    row = lambda i: (i, 0)
    fix = lambda i: (0, 0)
    return pl.pallas_call(
        _wo_kernel,
        grid=(n // tm,),
        in_specs=[
            pl.BlockSpec((tm, ATTN_W), row),
            pl.BlockSpec((ATTN_W, D_MODEL), fix),
            pl.BlockSpec((tm, D_MODEL), row),
            pl.BlockSpec((1, D_MODEL), fix),
        ],
        out_specs=[pl.BlockSpec((tm, D_MODEL), row), pl.BlockSpec((tm, D_MODEL), row)],
        out_shape=[jax.ShapeDtypeStruct((n, D_MODEL), F32),
                   jax.ShapeDtypeStruct((n, D_MODEL), BF16)],
        compiler_params=_params(("parallel",)),
        name="wo_res_norm_tail",
    )(o, w, h, g)


XPK_ROWS = D_MODEL // 2 // LANES
YS_ROWS = D_MODEL // LANES


ROW_SPLIT = 2
MLA_PROJ_SPLIT = 4


def _store_token_major(ref, x, row0=0):
    t, n = x.shape
    k = n // LANES
    for s in range(k):
        ref[pl.ds(row0 * k + s, t, stride=k), :] = x[:, s * LANES:(s + 1) * LANES]


def _load_token_major(ref, k):
    t = ref.shape[0] // k
    return [ref[pl.ds(s, t, stride=k), :] for s in range(k)]


def _pack_bf16_pair(a, b):
    ra = pltpu.bitcast(a.astype(BF16).astype(F32), U32)
    rb = pltpu.bitcast(b.astype(BF16).astype(F32), U32)
    return ra | (rb >> 16)


def _unpack_bf16_pair(p):
    a = pltpu.bitcast(p & jnp.uint32(0xFFFF0000), F32).astype(BF16)
    b = pltpu.bitcast(p << 16, F32).astype(BF16)
    return a, b


def _wo_router_kernel(o_real_ref, o_tail_ref, w_ref, h_ref, g_ref, r_ref,
                      hout_ref, xpk_ref, route_ref, cnt_ref, carry_ref, *, n_real_tiles):
    i = pl.program_id(0)
    tm = h_ref.shape[0]

    @pl.when(i == 0)
    def _():
        carry_ref[...] = jnp.zeros_like(carry_ref)

    o = _attn_out_tile(o_real_ref, o_tail_ref, n_real_tiles)
    th = tm // ROW_SPLIT
    parts = [slice(p * th, (p + 1) * th) for p in range(ROW_SPLIT)]
    half = D_MODEL // 2
    r2 = r_ref[...]
    lane = lax.broadcasted_iota(I32, (th, LANES), 1)
    rr = lax.broadcasted_iota(I32, (th, th), 0)
    cc = lax.broadcasted_iota(I32, (th, th), 1)
    lower = jnp.where(cc < rr, 1.0, 0.0).astype(BF16)

    ys = []
    for rs in parts:
        hn = _dot(o[rs], w_ref[...]) + h_ref[rs, :]
        hout_ref[rs, :] = hn
        ys.append(_rms(hn, g_ref[...]))
    logits = []
    for p, y in enumerate(ys):
        _store_token_major(xpk_ref, _pack_bf16_pair(y[:, :half], y[:, half:]), row0=p * th)
        yhi = y.astype(BF16)
        ylo = (y - yhi.astype(F32)).astype(BF16)
        t = _dot(yhi, r2)
        logits.append(t[:, :LANES] + t[:, LANES:] + _dot(ylo, r2[:, :LANES]))
    carry = carry_ref[0:1, :]
    for rs, lg in zip(parts, logits):
        lm = jnp.where(lane < N_EXPERTS, lg, -jnp.inf)
        m1 = jnp.max(lm, axis=-1, keepdims=True)
        i1 = jnp.min(jnp.where(lm == m1, lane, LANES), axis=-1, keepdims=True)
        lm2 = jnp.where(lane == i1, -jnp.inf, lm)
        m2 = jnp.max(lm2, axis=-1, keepdims=True)
        i2 = jnp.min(jnp.where(lm2 == m2, lane, LANES), axis=-1, keepdims=True)
        e21 = jnp.exp(m2 - m1)
        g0 = 1.0 / (1.0 + e21)
        g1 = e21 / (1.0 + e21)
        onehot = ((lane == i1) | (lane == i2)).astype(F32)
        before = _dot(lower, onehot.astype(BF16)) + carry
        rank1 = jnp.sum(jnp.where(lane == i1, before, 0.0), axis=-1, keepdims=True)
        rank2 = jnp.sum(jnp.where(lane == i2, before, 0.0), axis=-1, keepdims=True)
        carry = carry + jnp.sum(onehot, axis=0, keepdims=True)
        out = jnp.where(lane == 0, i1.astype(F32), 0.0)
        out = jnp.where(lane == 1, i2.astype(F32), out)
        out = jnp.where(lane == 2, rank1, out)
        out = jnp.where(lane == 3, rank2, out)
        out = jnp.where(lane == 4, g0, out)
        out = jnp.where(lane == 5, g1, out)
        route_ref[rs, :] = out
    carry_ref[...] = jnp.broadcast_to(carry, carry_ref.shape)
    cnt_ref[...] = jnp.broadcast_to(carry, cnt_ref.shape)


def _wo_router_call(o_real, o_tail, w, h, g, r2, *, tm):
    ntot = h.shape[0]
    nrt = o_real.shape[0] // tm
    row = lambda i: (i, 0)
    fix = lambda i: (0, 0)
    return pl.pallas_call(
        functools.partial(_wo_router_kernel, n_real_tiles=nrt),
        grid=(ntot // tm,),
        in_specs=_attn_out_specs(tm, nrt) + [
            pl.BlockSpec((ATTN_W, D_MODEL), fix),
            pl.BlockSpec((tm, D_MODEL), row),
            pl.BlockSpec((1, D_MODEL), fix),
            pl.BlockSpec((D_MODEL, 2 * LANES), fix),
        ],
        out_specs=[
            pl.BlockSpec((tm, D_MODEL), row),
            pl.BlockSpec((tm * XPK_ROWS, LANES), row),
            pl.BlockSpec((tm, LANES), row),
            pl.BlockSpec((8, LANES), fix),
        ],
        out_shape=[
            jax.ShapeDtypeStruct((ntot, D_MODEL), F32),
            jax.ShapeDtypeStruct((ntot * XPK_ROWS, LANES), U32),
            jax.ShapeDtypeStruct((ntot, LANES), F32),
            jax.ShapeDtypeStruct((8, LANES), F32),
        ],
        scratch_shapes=[pltpu.VMEM((8, LANES), F32)],
        compiler_params=_params(("arbitrary",)),
        name="wo_res_norm_router",
    )(o_real, o_tail, w, h, g, r2)


def _swiglu_chunk(x, wg, wu, wd):
    g = _dot(x, wg)
    u = _dot(x, wu)
    hg = 0.5 * g
    a = (hg + hg * jnp.tanh(hg)) * u
    return _dot(a.astype(BF16), wd)


def _ffn_dense_kernel(*refs, n_src, src_rows, tm, tf):
    x_refs, h_refs = refs[:n_src], refs[n_src:2 * n_src]
    wg_ref, wu_ref, wd_ref, g_ref, hout_ref, xn_ref = refs[2 * n_src:]
    x = _select_source(x_refs, src_rows, tm)
    hn = _select_source(h_refs, src_rows, tm)
    for c in range(wg_ref.shape[1] // tf):
        cs = slice(c * tf, (c + 1) * tf)
        hn = hn + _swiglu_chunk(x, wg_ref[:, cs], wu_ref[:, cs], wd_ref[cs, :])
    hout_ref[...] = hn
    xn_ref[...] = _rms(hn, g_ref[...]).astype(BF16)


def _ffn_dense_call(x_sources, wg, wu, wd, h_sources, g, *, tm, tf):
    src_rows = tuple(s.shape[0] for s in h_sources)
    ntot = sum(src_rows)
    fdim = wg.shape[1]
    row = lambda i: (i, 0)
    fix = lambda i: (0, 0)
    once = pl.Buffered(1)
    return pl.pallas_call(
        functools.partial(_ffn_dense_kernel, n_src=len(h_sources), src_rows=src_rows, tm=tm, tf=tf),
        grid=(ntot // tm,),
        in_specs=_source_specs(x_sources, tm) + _source_specs(h_sources, tm) + [
            pl.BlockSpec((D_MODEL, fdim), fix, pipeline_mode=once),
            pl.BlockSpec((D_MODEL, fdim), fix, pipeline_mode=once),
            pl.BlockSpec((fdim, D_MODEL), fix, pipeline_mode=once),
            pl.BlockSpec((1, D_MODEL), fix),
        ],
        out_specs=[pl.BlockSpec((tm, D_MODEL), row), pl.BlockSpec((tm, D_MODEL), row)],
        out_shape=[jax.ShapeDtypeStruct((ntot, D_MODEL), F32),
                   jax.ShapeDtypeStruct((ntot, D_MODEL), BF16)],
        compiler_params=_params(("parallel",)),
        name="ffn_dense",
    )(*x_sources, *h_sources, wg, wu, wd, g)


def _ffn_expert_kernel(be_ref, nu_ref, xs_ref, wg_ref, wu_ref, wd_ref, ys_ref, *, tf):
    del be_ref
    used = pl.program_id(0) < nu_ref[0]

    @pl.when(used)
    def _():
        halves = [_unpack_bf16_pair(blk) for blk in _load_token_major(xs_ref, XPK_ROWS)]
        x = jnp.concatenate([a for a, _ in halves] + [b for _, b in halves], axis=1)
        acc = None
        for c in range(wg_ref.shape[1] // tf):
            cs = slice(c * tf, (c + 1) * tf)
            y = _swiglu_chunk(x, wg_ref[:, cs], wu_ref[:, cs], wd_ref[cs, :])
            acc = y if acc is None else acc + y
        _store_token_major(ys_ref, acc)

    @pl.when(jnp.logical_not(used))
    def _():
        ys_ref[...] = jnp.zeros_like(ys_ref)


def _ffn_expert_call(block_e, n_used, xs, wg, wu, wd, *, tm, tf):
    rows = xs.shape[0] // XPK_ROWS
    once = pl.Buffered(1)
    wmap = lambda i, be, nu: (be[i], 0, 0)
    grid_spec = pltpu.PrefetchScalarGridSpec(
        num_scalar_prefetch=2,
        grid=(rows // tm,),
        in_specs=[
            pl.BlockSpec((tm * XPK_ROWS, LANES), lambda i, be, nu: (i, 0)),
            pl.BlockSpec((None,) + wg.shape[1:], wmap, pipeline_mode=once),
            pl.BlockSpec((None,) + wu.shape[1:], wmap, pipeline_mode=once),
            pl.BlockSpec((None,) + wd.shape[1:], wmap, pipeline_mode=once),
        ],
        out_specs=pl.BlockSpec((tm * YS_ROWS, LANES), lambda i, be, nu: (i, 0)),
    )
    return pl.pallas_call(
        functools.partial(_ffn_expert_kernel, tf=tf),
        grid_spec=grid_spec,
        out_shape=jax.ShapeDtypeStruct((rows * YS_ROWS, LANES), F32),
        compiler_params=_params(("arbitrary",)),
        name="ffn_expert",
    )(block_e, n_used, xs, wg, wu, wd)


DMA_UNROLL = 8


def _token_copy(src, src_tok, dst, dst_tok, rows, sem):
    return pltpu.make_async_copy(src.at[pl.ds(pl.multiple_of(src_tok * rows, rows), rows)],
                                 dst.at[pl.ds(pl.multiple_of(dst_tok * rows, rows), rows)], sem)


def _dispatch_kernel(dest_ref, x_ref, xs_in_ref, xs_ref, sem):
    del xs_in_ref
    td = x_ref.shape[0] // XPK_ROWS

    def issue(t, carry):
        for s in range(2):
            _token_copy(x_ref, t, xs_ref, dest_ref[0, s, t], XPK_ROWS, sem).start(priority=s)
        return carry

    lax.fori_loop(0, td, issue, 0, unroll=DMA_UNROLL)

    def drain(t, carry):
        for s in range(2):
            _token_copy(x_ref, 0, xs_ref, 0, XPK_ROWS, sem).wait()
        return carry

    lax.fori_loop(0, td, drain, 0, unroll=DMA_UNROLL)


def _dispatch_call(dest, xpk, xs_init, *, td):
    ntot = xpk.shape[0] // XPK_ROWS
    return pl.pallas_call(
        _dispatch_kernel,
        grid=(ntot // td,),
        in_specs=[
            pl.BlockSpec((1, 2, td), lambda i: (i, 0, 0), memory_space=pltpu.SMEM),
            pl.BlockSpec((td * XPK_ROWS, LANES), lambda i: (i, 0)),
            pl.BlockSpec(memory_space=pl.ANY),
        ],
        out_specs=pl.BlockSpec(memory_space=pl.ANY),
        out_shape=jax.ShapeDtypeStruct(xs_init.shape, U32),
        input_output_aliases={2: 0},
        scratch_shapes=[pltpu.SemaphoreType.DMA(())],
        compiler_params=_params(("arbitrary",)),
        name="moe_dispatch",
    )(dest, xpk, xs_init)


def _combine_kernel(dest_ref, dest_next_ref, ys_ref, h_ref, route_ref, g_ref, out_ref,
                    y0_ref, y1_ref, sems):
    i = pl.program_id(0)
    n = pl.num_programs(0)
    tc = h_ref.shape[0]
    bufs = (y0_ref, y1_ref)
    slot = i % 2

    def issue(d_ref, sl):
        def body(t, carry):
            for s in range(2):
                _token_copy(ys_ref, d_ref[0, s, t], bufs[s].at[sl], t, YS_ROWS,
                            sems.at[sl]).start(priority=s)
            return carry

        lax.fori_loop(0, tc, body, 0, unroll=DMA_UNROLL)

    @pl.when(i == 0)
    def _():
        issue(dest_ref, 0)

    @pl.when(i + 1 < n)
    def _():
        issue(dest_next_ref, 1 - slot)

    def drain(t, carry):
        for s in range(2):
            _token_copy(ys_ref, 0, bufs[s].at[slot], 0, YS_ROWS, sems.at[slot]).wait()
        return carry

    lax.fori_loop(0, tc, drain, 0, unroll=DMA_UNROLL)

    r = route_ref[...]
    lane = lax.broadcasted_iota(I32, r.shape, 1)
    g0 = jnp.sum(jnp.where(lane == 4, r, 0.0), axis=-1, keepdims=True)
    g1 = jnp.sum(jnp.where(lane == 5, r, 0.0), axis=-1, keepdims=True)
    y0 = _load_token_major(y0_ref.at[slot], YS_ROWS)
    y1 = _load_token_major(y1_ref.at[slot], YS_ROWS)
    moe = jnp.concatenate([a * g0 + b * g1 for a, b in zip(y0, y1)], axis=1)
    out_ref[...] = _rms(h_ref[...] + moe, g_ref[...])


def _combine_call(dest, ys, h, route, g, *, tc, tile0, ntiles):
    return pl.pallas_call(
        _combine_kernel,
        grid=(ntiles,),
        in_specs=[
            pl.BlockSpec((1, 2, tc), lambda i: (tile0 + i, 0, 0), memory_space=pltpu.SMEM),
            pl.BlockSpec((1, 2, tc), lambda i: (tile0 + jnp.minimum(i + 1, ntiles - 1), 0, 0),
                         memory_space=pltpu.SMEM),
            pl.BlockSpec(memory_space=pl.ANY),
            pl.BlockSpec((tc, D_MODEL), lambda i: (tile0 + i, 0)),
            pl.BlockSpec((tc, LANES), lambda i: (tile0 + i, 0)),
            pl.BlockSpec((1, D_MODEL), lambda i: (0, 0)),
        ],
        out_specs=pl.BlockSpec((tc, D_MODEL), lambda i: (i, 0)),
        out_shape=jax.ShapeDtypeStruct((ntiles * tc, D_MODEL), F32),
        scratch_shapes=[pltpu.VMEM((2, tc * YS_ROWS, LANES), F32), pltpu.VMEM((2, tc * YS_ROWS, LANES), F32),
                        pltpu.SemaphoreType.DMA((2,))],
        compiler_params=_params(("arbitrary",)),
        name="moe_combine_norm",
    )(dest, dest, ys, h, route, g)


def _mla_proj_kernel(x_ref, wd_ref, qn_ref, kvn_ref, wuq_ref, wuk_ref, wuvt_ref,
                     c_ref, s1_ref, s2_ref, q_ref, k_ref, vt_ref):
    tm = x_ref.shape[0]
    th = tm // MLA_PROJ_SPLIT
    parts = [slice(p * th, (p + 1) * th) for p in range(MLA_PROJ_SPLIT)]
    half = MLA_ROPE // 2
    scale = (MLA_QK ** -0.5) * LOG2E
    lats = [_dot(x_ref[rs, :], wd_ref[...]) for rs in parts]
    ups = []
    for lat in lats:
        cq = _rms(lat[:, :MLA_Q_RANK], qn_ref[...]).astype(BF16)
        ckv = _rms(lat[:, MLA_Q_RANK:MLA_Q_RANK + MLA_KV_RANK], kvn_ref[...]).astype(BF16)
        ups.append((_dot(cq, wuq_ref[...]),
                    _dot(ckv, wuk_ref[...]),
                    _dot_nt(wuvt_ref[...], ckv)))
    for rs, lat, (q, k, vt) in zip(parts, lats, ups):
        c, s1, s2 = c_ref[rs, :], s1_ref[rs, :], s2_ref[rs, :]
        kr = _rope(lat[:, MLA_Q_RANK + MLA_KV_RANK:], c, s1, s2, half)
        for h in range(MLA_HEADS):
            sl = slice(h * LANES, (h + 1) * LANES)
            q_ref[rs, sl] = (_rope(q[:, sl], c, s1, s2, half) * scale).astype(BF16)
            k_ref[rs, sl] = (k[:, sl] + kr).astype(BF16)
        vt_ref[:, rs] = vt.astype(BF16)


def _mla_proj_call(x, wd, qn, kvn, wuq, wuk, wuv, tabs, *, tm, n_real, seq):
    ntot = x.shape[0]
    nrt, tps = n_real // tm, seq // tm
    row = lambda i: (i, 0)
    fix = lambda i: (0, 0)
    tab_spec = pl.BlockSpec((tm, LANES), lambda i: (_table_index(i, nrt, tps), 0))
    return pl.pallas_call(
        _mla_proj_kernel,
        grid=(ntot // tm,),
        in_specs=[
            pl.BlockSpec((tm, D_MODEL), row),
            pl.BlockSpec(wd.shape, fix),
            pl.BlockSpec((1, MLA_Q_RANK), fix),
            pl.BlockSpec((1, MLA_KV_RANK), fix),
            pl.BlockSpec(wuq.shape, fix),
            pl.BlockSpec(wuk.shape, fix),
            pl.BlockSpec(wuv.shape[::-1], fix),
            tab_spec, tab_spec, tab_spec,
        ],
        out_specs=[
            pl.BlockSpec((tm, MLA_QK_W), row),
            pl.BlockSpec((tm, MLA_QK_W), row),
            pl.BlockSpec((ATTN_W, tm), lambda i: (0, i)),
        ],
        out_shape=[
            jax.ShapeDtypeStruct((ntot, MLA_QK_W), BF16),
            jax.ShapeDtypeStruct((ntot, MLA_QK_W), BF16),
            jax.ShapeDtypeStruct((ATTN_W, ntot), BF16),
        ],
        compiler_params=_params(("parallel",)),
        name="mla_proj",
    )(x, wd, qn, kvn, wuq, wuk, wuv.T, *tabs)


META_PER_BLOCK = LANES // N_META
REDUCE_ROWS = 64
SCORES_AHEAD = 1
VALUES_BEHIND = 1
MLA_QUERY_TILE = 256


def _col_reduce(x, op):
    rows, n = x.shape
    if rows > REDUCE_ROWS and rows % REDUCE_ROWS == 0:
        x = op(x.reshape(rows // REDUCE_ROWS, REDUCE_ROWS, n), axis=0)
    return op(x, axis=0, keepdims=True)


def _mla_attn_kernel(q_ref, kr_ref, kt_ref, vtr_ref, vtt_ref, o_ref, ot_ref):
    b = pl.program_id(0)
    r = lax.broadcasted_iota(I32, (LANES, 1), 0)
    mine = jnp.right_shift(r, _log2(N_META)) == (b % META_PER_BLOCK)
    mbias = jnp.where(mine, 0.0, -jnp.inf).astype(F32)
    tq = min(MLA_QUERY_TILE, q_ref.shape[0])
    units = [(g, h) for g in range(q_ref.shape[0] // tq) for h in range(MLA_HEADS)]

    def scores(g, h):
        hs = slice(h * LANES, (h + 1) * LANES)
        qh = q_ref[g * tq:(g + 1) * tq, hs]
        return _dot_nt(kr_ref[:, hs], qh), _dot_nt(kt_ref[:, hs], qh) + mbias

    def weighted_values(g, h, p_r, p_m, den):
        vs = slice(h * MLA_V, (h + 1) * MLA_V)
        acc = _dot(vtr_ref[vs, :], p_r) + _dot(vtt_ref[vs, :], p_m)
        ot_ref[vs, g * tq:(g + 1) * tq] = acc / den

    ahead = [scores(*u) for u in units[:SCORES_AHEAD]]
    pending = []
    for idx, unit in enumerate(units):
        s_r, s_m = ahead.pop(0)
        if idx + SCORES_AHEAD < len(units):
            ahead.append(scores(*units[idx + SCORES_AHEAD]))
        m = jnp.maximum(_col_reduce(s_r, jnp.max), _col_reduce(s_m, jnp.max))
        p_r = jnp.exp2(s_r - m)
        p_m = jnp.exp2(s_m - m)
        den = _col_reduce(p_r, jnp.sum) + _col_reduce(p_m, jnp.sum)
        pending.append((*unit, p_r.astype(BF16), p_m.astype(BF16), den))
        if len(pending) > VALUES_BEHIND:
            weighted_values(*pending.pop(0))
    for item in pending:
        weighted_values(*item)
    o_ref[...] = ot_ref[...].T.astype(BF16)


def _mla_attn_call(q, k, vt, *, batch, seq, n_real, tq):
    nq = seq // tq
    tail0 = n_real // LANES
    qmap = lambda b, j: (b * nq + j, 0)
    return pl.pallas_call(
        _mla_attn_kernel,
        grid=(batch, nq),
        in_specs=[
            pl.BlockSpec((tq, MLA_QK_W), qmap),
            pl.BlockSpec((seq, MLA_QK_W), lambda b, j: (b, 0)),
            pl.BlockSpec((LANES, MLA_QK_W), lambda b, j: (tail0 + b // META_PER_BLOCK, 0)),
            pl.BlockSpec((ATTN_W, seq), lambda b, j: (0, b)),
            pl.BlockSpec((ATTN_W, LANES), lambda b, j: (0, tail0 + b // META_PER_BLOCK)),
        ],
        out_specs=pl.BlockSpec((tq, ATTN_W), qmap),
        out_shape=jax.ShapeDtypeStruct((n_real, ATTN_W), BF16),
        scratch_shapes=[pltpu.VMEM((ATTN_W, tq), F32)],
        compiler_params=_params(("parallel", "arbitrary")),
        name="mla_attn",
    )(q, k, k, vt, vt)


def _mla_meta_kernel(q_ref, kr_ref, kt_ref, vtr_ref, vtt_ref, o_ref, *, batch):
    i = pl.program_id(0)
    ncol = MLA_HEADS * N_META

    @pl.when(i < batch)
    def _():
        qi = lax.broadcasted_iota(I32, (N_META, ncol), 0)
        ci = lax.broadcasted_iota(I32, (N_META, ncol), 1)
        spread = jnp.where((ci & (N_META - 1)) == qi, 1.0, 0.0).astype(BF16)
        w = lax.dot_general(q_ref[...], spread, (((0,), (0,)), ((), ())),
                            preferred_element_type=F32)
        wr = lax.broadcasted_iota(I32, w.shape, 0)
        wc = lax.broadcasted_iota(I32, w.shape, 1)
        on_diag = jnp.right_shift(wr, _log2(LANES)) == jnp.right_shift(wc, _log2(N_META))
        w = jnp.where(on_diag, w, 0.0).astype(BF16)
        r = lax.broadcasted_iota(I32, (LANES, 1), 0)
        mine = jnp.right_shift(r, _log2(N_META)) == (i % META_PER_BLOCK)
        mbias = jnp.where(mine, 0.0, -jnp.inf).astype(F32)
        s_r = _dot(kr_ref[...], w)
        s_m = _dot(kt_ref[...], w) + mbias
        m = jnp.maximum(_col_reduce(s_r, jnp.max), _col_reduce(s_m, jnp.max))
        p_r = jnp.exp2(s_r - m)
        p_m = jnp.exp2(s_m - m)
        den = _col_reduce(p_r, jnp.sum) + _col_reduce(p_m, jnp.sum)
        ot = (_dot(vtr_ref[...], p_r.astype(BF16)) + _dot(vtt_ref[...], p_m.astype(BF16))) / den
        o_all = ot.T
        lane = lax.broadcasted_iota(I32, (N_META, ATTN_W), 1)
        o = jnp.zeros((N_META, ATTN_W), F32)
        for h in range(MLA_HEADS):
            rows = o_all[h * N_META:(h + 1) * N_META, :]
            o = jnp.where(jnp.right_shift(lane, _log2(MLA_V)) == h, rows, o)
        o_ref[...] = o.astype(BF16)

    @pl.when(i >= batch)
    def _():
        o_ref[...] = jnp.zeros_like(o_ref)


def _mla_meta_call(q, k, vt, *, batch, seq, n_real):
    ntot = q.shape[0]
    mrow = n_real // N_META
    tail0 = n_real // LANES
    ntail_blocks = (ntot - n_real) // N_META
    last = (batch - 1) // META_PER_BLOCK
    bmap = lambda i: jnp.minimum(i, batch - 1)
    tmap = lambda i: jnp.minimum(i // META_PER_BLOCK, last)
    return pl.pallas_call(
        functools.partial(_mla_meta_kernel, batch=batch),
        grid=(ntail_blocks,),
        in_specs=[
            pl.BlockSpec((N_META, MLA_QK_W), lambda i: (mrow + i, 0)),
            pl.BlockSpec((seq, MLA_QK_W), lambda i: (bmap(i), 0)),
            pl.BlockSpec((LANES, MLA_QK_W), lambda i: (tail0 + tmap(i), 0)),
            pl.BlockSpec((ATTN_W, seq), lambda i: (0, bmap(i))),
            pl.BlockSpec((ATTN_W, LANES), lambda i: (0, tail0 + tmap(i))),
        ],
        out_specs=pl.BlockSpec((N_META, ATTN_W), lambda i: (i, 0)),
        out_shape=jax.ShapeDtypeStruct((ntot - n_real, ATTN_W), BF16),
        compiler_params=_params(("arbitrary",)),
        name="mla_attn_meta",
    )(q, k, k, vt, vt)


def _prep_gqa(a_wqkv, a_wo):
    d = a_wqkv.shape[0]
    nq = GQA_HEADS * GQA_HD
    nkv = GQA_KV * GQA_HD
    wq = a_wqkv[:, :nq] * (GQA_HD ** -0.5)

    def dup(w):
        w = w.reshape(d, GQA_KV, 1, GQA_HD)
        return jnp.broadcast_to(w, (d, GQA_KV, 2, GQA_HD)).reshape(d, 2 * nkv)

    w = jnp.concatenate([wq, dup(a_wqkv[:, nq:nq + nkv]), dup(a_wqkv[:, nq + nkv:])], axis=1)
    return w.astype(BF16), a_wo.astype(BF16)


def _prep_mla(b_wdkv, b_wuq, b_wukv):
    d = b_wdkv.shape[0]
    lat = MLA_Q_RANK + MLA_KV_RANK
    z = lambda n: jnp.zeros((d, n), F32)
    wd = jnp.concatenate([b_wdkv[:, :lat], z(MLA_NOPE), b_wdkv[:, lat:], z(LANES - MLA_QK)], axis=1)
    wuq = b_wuq.reshape(MLA_Q_RANK, MLA_HEADS, MLA_QK)
    wuq = jnp.pad(wuq, ((0, 0), (0, 0), (0, LANES - MLA_QK))).reshape(MLA_Q_RANK, MLA_HEADS * LANES)
    wukv = b_wukv.reshape(MLA_KV_RANK, MLA_HEADS, MLA_NOPE + MLA_V)
    wuk = jnp.pad(wukv[:, :, :MLA_NOPE], ((0, 0), (0, 0), (0, LANES - MLA_NOPE)))
    wuk = wuk.reshape(MLA_KV_RANK, MLA_HEADS * LANES)
    wuv = wukv[:, :, MLA_NOPE:].reshape(MLA_KV_RANK, MLA_HEADS * MLA_V)
    return wd.astype(BF16), wuq.astype(BF16), wuk.astype(BF16), wuv.astype(BF16)


def _prep_router(m_router):
    hi = m_router.astype(BF16)
    lo = (m_router - hi.astype(F32)).astype(BF16)
    pad = lambda r: jnp.pad(r, ((0, 0), (0, LANES - N_EXPERTS)))
    return jnp.concatenate([pad(hi), pad(lo)], axis=1)


def _positions(seq, n_tail, n_meta_rows):
    real = N_META + jnp.arange(seq, dtype=F32)
    t = jnp.arange(n_tail)
    tail = jnp.where(t < n_meta_rows, t % N_META, 0).astype(F32)
    return jnp.concatenate([real, tail])


def _trunk_flat(x_groups, meta, norm_mix, norm_ffn, norm_final,
                a_wqkv, a_wo, a_sink,
                b_wdkv, b_qnorm, b_kvnorm, b_wuq, b_wukv, b_wo,
                f_wg, f_wu, f_wd,
                m_router, m_wg, m_wu, m_wd):
    seq = x_groups[0].shape[1]
    batches = [x.shape[0] for x in x_groups]
    batch = sum(batches)
    n_real = batch * seq
    n_meta_rows = batch * N_META
    big = _pick_tile(seq, (1024, 512, 256, 128))
    tp = min(big, 512)
    ntot = -(-(n_real + n_meta_rows) // big) * big
    n_tail = ntot - n_real

    tail = jnp.concatenate([
        jnp.broadcast_to(meta.astype(F32)[None], (batch, N_META, D_MODEL)).reshape(n_meta_rows, D_MODEL),
        jnp.zeros((n_tail - n_meta_rows, D_MODEL), F32)], axis=0)
    h_sources = [x.reshape(-1, D_MODEL) for x in x_groups] + [tail]

    pos = _positions(seq, n_tail, n_meta_rows)
    tabs_a = _rope_lane_tables(pos, GQA_THETA, GQA_ROT, GQA_HD, 0)
    tabs_b = _rope_lane_tables(pos, MLA_THETA, MLA_ROPE, LANES, MLA_NOPE)
    row = lambda v: v.reshape(1, -1).astype(F32)

    wqkv, wo_a = _prep_gqa(a_wqkv[0], a_wo[0])
    q, k, v = _qkv_call(h_sources, row(norm_mix[0]), wqkv, tabs_a, tm=tp, n_real=n_real, seq=seq)
    sink = a_sink[0].astype(F32) * LOG2E
    h_real, xn_real = _win_attn_call(sink, q, k, v, wo_a, h_sources[:-1], row(norm_ffn[0]),
                                     batch=batch, seq=seq, n_real=n_real)
    o_tail = _win_meta_call(sink, q, k, v, batch=batch, seq=seq, n_real=n_real)
    h_tail, xn_tail = _wo_call(o_tail, wo_a, tail, row(norm_ffn[0]), tm=tp)
    fdim = f_wg.shape[2]
    tf = _pick_tile(fdim, (512, 256, 128))
    h, xn = _ffn_dense_call([xn_real, xn_tail], f_wg[0].astype(BF16), f_wu[0].astype(BF16),
                            f_wd[0].astype(BF16), [h_real, h_tail], row(norm_mix[1]), tm=tp, tf=tf)

    wd, wuq, wuk, wuv = _prep_mla(b_wdkv[0], b_wuq[0], b_wukv[0])
    q, k, vt = _mla_proj_call(xn, wd, row(b_qnorm[0]), row(b_kvnorm[0]), wuq, wuk, wuv, tabs_b,
                              tm=tp, n_real=n_real, seq=seq)
    tq = _pick_tile(seq, (512, 256, 128))
    o = _mla_attn_call(q, k, vt, batch=batch, seq=seq, n_real=n_real, tq=tq)
    o_tail = _mla_meta_call(q, k, vt, batch=batch, seq=seq, n_real=n_real)
    tr = tp
    h, xpk, route, counts = _wo_router_call(o, o_tail, b_wo[0].astype(BF16), h, row(norm_ffn[1]),
                                            _prep_router(m_router[0]), tm=tr)

    tmx = big
    counts = counts[0, :N_EXPERTS].astype(I32)
    padded = ((counts + tmx - 1) // tmx) * tmx
    pend = jnp.cumsum(padded)
    pstart = pend - padded
    e_idx = route[:, 0:2].astype(I32)
    rank = route[:, 2:4].astype(I32)
    dest = pstart[e_idx] + rank
    n_assign = 2 * ntot
    n_blocks = -(-n_assign // tmx) + N_EXPERTS
    rows = n_blocks * tmx
    block_e = jnp.clip(jnp.searchsorted(pend, jnp.arange(n_blocks, dtype=I32) * tmx, side="right"),
                       0, N_EXPERTS - 1).astype(I32)
    n_used = (pend[-1] // tmx).astype(I32).reshape(1)
    block_e = jnp.where(jnp.arange(n_blocks) < n_used[0], block_e, block_e[jnp.maximum(n_used[0] - 1, 0)])

    td = tp
    dest_t = dest.reshape(ntot // td, td, 2).transpose(0, 2, 1)
    xs = _dispatch_call(dest_t, xpk, jnp.zeros((rows * XPK_ROWS, LANES), U32), td=td)
    edim = m_wg.shape[3]
    tfx = _pick_tile(edim, (512, 256, 128))
    ys = _ffn_expert_call(block_e, n_used, xs, m_wg[0].astype(BF16), m_wu[0].astype(BF16),
                          m_wd[0].astype(BF16), tm=tmx, tf=tfx)

    outs = []
    tile0 = 0
    for b, x in zip(batches, x_groups):
        ntiles = b * seq // td
        y = _combine_call(dest_t, ys, h, route, row(norm_final), tc=td, tile0=tile0, ntiles=ntiles)
        outs.append(y.reshape(b, seq, D_MODEL))
        tile0 += ntiles
    return tuple(outs)


def kernel(x_prompt, x_sample, meta, norm_mix, norm_ffn, norm_final, a_wqkv, a_wo, a_sink, b_wdkv, b_qnorm, b_kvnorm, b_wuq, b_wukv, b_wo, f_wg, f_wu, f_wd, m_router, m_wg, m_wu, m_wd):
    assert x_prompt.shape[1] == x_sample.shape[1] and x_prompt.shape[1] % BLK == 0
    assert norm_mix.shape[0] == 2, "two layers: windowed GQA + dense FFN, then MLA + MoE"
    return _trunk_flat((x_prompt, x_sample), meta, norm_mix, norm_ffn, norm_final,
                       a_wqkv, a_wo, a_sink,
                       b_wdkv, b_qnorm, b_kvnorm, b_wuq, b_wukv, b_wo,
                       f_wg, f_wu, f_wd,
                       m_router, m_wg, m_wu, m_wd)
```

```python
import functools

import jax
import jax.numpy as jnp
from jax import lax
from jax.experimental import pallas as pl
from jax.experimental.pallas import tpu as pltpu

F32 = jnp.float32
BF16 = jnp.bfloat16
U32 = jnp.uint32
I32 = jnp.int32

D_MODEL = 1024
N_META = 16
RMS_EPS = 1e-6
BLK = 128
GQA_HEADS = 16
GQA_KV = 4
GQA_GROUP = 4
GQA_HD = 64
GQA_ROT = 16
GQA_THETA = 500000.0
MLA_HEADS = 16
MLA_NOPE = 64
MLA_ROPE = 32
MLA_V = 64
MLA_Q_RANK = 384
MLA_KV_RANK = 256
MLA_THETA = 10000.0
MLA_QK = MLA_NOPE + MLA_ROPE
N_EXPERTS = 8
LOG2E = 1.4426950408889634

LANES = 128
ATTN_W = GQA_HEADS * GQA_HD
MLA_QK_W = MLA_HEADS * LANES
assert ATTN_W == MLA_HEADS * MLA_V
VMEM_LIMIT = 56 * 1024 * 1024


def _params(sem, vmem=VMEM_LIMIT):
    return pltpu.CompilerParams(dimension_semantics=sem, vmem_limit_bytes=vmem)


def _rms(x, g):
    ms = jnp.mean(x * x, axis=-1, keepdims=True)
    return x * lax.rsqrt(ms + RMS_EPS) * g


def _dot(a, b):
    return jnp.dot(a, b, preferred_element_type=F32)


def _dot_nt(a, b):
    return lax.dot_general(a, b, (((1,), (1,)), ((), ())), preferred_element_type=F32)


def _log2(n):
    assert n & (n - 1) == 0, n
    return n.bit_length() - 1


def _pick_tile(n, candidates):
    for c in candidates:
        if n % c == 0:
            return c
    raise ValueError(f"no tile in {candidates} divides {n}")


def _rope_lane_tables(pos, theta, rot, period, offset):
    half = rot // 2
    inv = jnp.power(jnp.float32(theta), -(jnp.arange(0, rot, 2, dtype=F32) / rot))
    ang = pos[:, None] * inv[None, :]
    cos, sin = jnp.cos(ang), jnp.sin(ang)
    lane = jnp.arange(LANES)
    r = (lane % period) - offset
    is_x1 = (r >= 0) & (r < half)
    is_x2 = (r >= half) & (r < rot)
    f = jnp.clip(jnp.where(is_x2, r - half, r), 0, half - 1)
    cos_l = cos[:, f]
    sin_l = sin[:, f]
    c = jnp.where((is_x1 | is_x2)[None, :], cos_l, 1.0)
    s1 = jnp.where(is_x2[None, :], sin_l, 0.0)
    s2 = jnp.where(is_x1[None, :], -sin_l, 0.0)
    return c.astype(F32), s1.astype(F32), s2.astype(F32)


def _rope(x, c, s1, s2, half):
    return x * c + pltpu.roll(x, half, 1) * s1 + pltpu.roll(x, LANES - half, 1) * s2


def _source_specs(sources, tm, tile_of=lambda *grid: grid[0]):
    specs, start = [], 0
    for src in sources:
        n = src.shape[0] // tm
        specs.append(pl.BlockSpec(
            (tm, src.shape[1]),
            lambda *grid, start=start, n=n: (jnp.clip(tile_of(*grid) - start, 0, n - 1), 0)))
        start += n
    return specs


def _select_source(refs, src_rows, tm, tile=None):
    i = pl.program_id(0) if tile is None else tile
    ends, acc = [], 0
    for n in src_rows:
        acc += n // tm
        ends.append(acc)
    x = refs[-1][...]
    for r, end in reversed(list(zip(refs[:-1], ends[:-1]))):
        x = jnp.where(i < end, r[...], x)
    return x


def _qkv_kernel(*refs, n_src, src_rows, tm):
    h_refs = refs[:n_src]
    g_ref, w_ref, c_ref, s1_ref, s2_ref, q_ref, k_ref, v_ref = refs[n_src:]
    xn = _rms(_select_source(h_refs, src_rows, tm), g_ref[...]).astype(BF16)
    qkv = _dot(xn, w_ref[...])
    c, s1, s2 = c_ref[...], s1_ref[...], s2_ref[...]
    nq = GQA_HEADS * GQA_HD // LANES
    nk = GQA_KV
    for i in range(nq):
        q_ref[:, i * LANES:(i + 1) * LANES] = (_rope(
            qkv[:, i * LANES:(i + 1) * LANES], c, s1, s2, GQA_ROT // 2) * LOG2E).astype(BF16)
    for i in range(nk):
        lo = (nq + i) * LANES
        k_ref[:, i * LANES:(i + 1) * LANES] = _rope(
            qkv[:, lo:lo + LANES], c, s1, s2, GQA_ROT // 2).astype(BF16)
    v_ref[...] = qkv[:, (nq + nk) * LANES:].astype(BF16)


def _table_index(i, n_real_tiles, tiles_per_seq):
    return jnp.where(i < n_real_tiles, i % tiles_per_seq, tiles_per_seq + i - n_real_tiles)


def _qkv_call(h_sources, g, w, tabs, *, tm, n_real, seq):
    src_rows = tuple(s.shape[0] for s in h_sources)
    ntot = sum(src_rows)
    nrt, tps = n_real // tm, seq // tm
    tab_spec = pl.BlockSpec((tm, LANES), lambda i: (_table_index(i, nrt, tps), 0))
    return pl.pallas_call(
        functools.partial(_qkv_kernel, n_src=len(h_sources), src_rows=src_rows, tm=tm),
        grid=(ntot // tm,),
        in_specs=_source_specs(h_sources, tm) + [
            pl.BlockSpec((1, D_MODEL), lambda i: (0, 0)),
            pl.BlockSpec(w.shape, lambda i: (0, 0)),
            tab_spec, tab_spec, tab_spec,
        ],
        out_specs=[
            pl.BlockSpec((tm, ATTN_W), lambda i: (i, 0)),
            pl.BlockSpec((tm, GQA_KV * LANES), lambda i: (i, 0)),
            pl.BlockSpec((tm, GQA_KV * LANES), lambda i: (i, 0)),
        ],
        out_shape=[
            jax.ShapeDtypeStruct((ntot, ATTN_W), BF16),
            jax.ShapeDtypeStruct((ntot, GQA_KV * LANES), BF16),
            jax.ShapeDtypeStruct((ntot, GQA_KV * LANES), BF16),
        ],
        compiler_params=_params(("parallel",)),
        name="qkv_rope",
    )(*h_sources, g, w, *tabs)


def _gqa_attend(sink_ref, q_ref, o_ref, groups):
    def pad_of(bias, pieces):
        have = sum(p.shape[0] for p in pieces)
        return [jnp.zeros((bias.shape[1] - have, LANES), BF16)] if bias.shape[1] > have else []

    def scores(g, kv):
        rows, k_pieces, _, bias = groups[g]
        sl = slice(kv * LANES, (kv + 1) * LANES)
        kcat = jnp.concatenate([p[:, sl] for p in k_pieces] + pad_of(bias, k_pieces), axis=0)
        zero = jnp.zeros_like(kcat)
        lane_k = lax.broadcasted_iota(I32, kcat.shape, 1)
        lo = slice(2 * kv * LANES, (2 * kv + 1) * LANES)
        hi = slice((2 * kv + 1) * LANES, (2 * kv + 2) * LANES)
        qp = jnp.concatenate([q_ref[rows, lo], q_ref[rows, hi]], axis=0)
        return (_dot_nt(qp, jnp.where(lane_k < GQA_HD, kcat, zero)) + bias,
                _dot_nt(qp, jnp.where(lane_k >= GQA_HD, kcat, zero)) + bias)

    def softmax(g, kv, ss):
        nq = groups[g][3].shape[0] // 2
        row = lax.broadcasted_iota(I32, (2 * nq, 1), 0)
        out = []
        for par, s in enumerate(ss):
            head = kv * GQA_GROUP + par
            sink = jnp.where(row < nq, sink_ref[head], sink_ref[head + 2])
            m = jnp.maximum(jnp.max(s, axis=-1, keepdims=True), sink)
            p = jnp.exp2(s - m)
            den = jnp.sum(p, axis=-1, keepdims=True) + jnp.exp2(sink - m)
            out.append((p.astype(BF16), den))
        return out

    def weighted_values(g, kv, pd):
        rows, _, v_pieces, bias = groups[g]
        nq = bias.shape[0] // 2
        sl = slice(kv * LANES, (kv + 1) * LANES)
        vcat = jnp.concatenate([p[:, sl] for p in v_pieces] + pad_of(bias, v_pieces), axis=0)
        outs = [_dot(p, vcat) / den for p, den in pd]
        lane_o = lax.broadcasted_iota(I32, outs[0].shape, 1)
        o = jnp.where(lane_o < GQA_HD, outs[0], outs[1]).astype(BF16)
        o_ref[rows, 2 * kv * LANES:(2 * kv + 1) * LANES] = o[:nq]
        o_ref[rows, (2 * kv + 1) * LANES:(2 * kv + 2) * LANES] = o[nq:]

    units = [(g, kv) for g in range(len(groups)) for kv in range(GQA_KV)]
    nxt = scores(*units[0])
    pending = None
    for idx, unit in enumerate(units):
        ss = nxt
        if idx + 1 < len(units):
            nxt = scores(*units[idx + 1])
        pd = softmax(*unit, ss)
        if pending is not None:
            weighted_values(*pending)
        pending = (*unit, pd)
    weighted_values(*pending)


def _window_bias(prev_ok, next_ok):
    nkeys = 4 * BLK
    qi = lax.broadcasted_iota(I32, (2 * BLK, nkeys), 0) & (BLK - 1)
    c = lax.broadcasted_iota(I32, (2 * BLK, nkeys), 1)
    neg = jnp.float32(-jnp.inf)
    b_p = jnp.where((c >= qi) & prev_ok, 0.0, neg)
    b_n = jnp.where(((c - 2 * BLK) <= qi) & next_ok, 0.0, neg)
    b_m = jnp.where(c < 3 * BLK + N_META, 0.0, neg)
    return jnp.where(c < BLK, b_p, jnp.where(c < 2 * BLK, 0.0, jnp.where(c < 3 * BLK, b_n, b_m)))


def _win_attn_kernel(sink_ref, q_ref, *refs, nb, qb, n_src, src_rows):
    k_blk, km_ref = refs[:qb + 2], refs[qb + 2]
    v_blk, vm_ref = refs[qb + 3:2 * qb + 5], refs[2 * qb + 5]
    rest = refs[2 * qb + 6:]
    wo_ref, h_refs, g_ref = rest[0], rest[1:1 + n_src], rest[1 + n_src]
    hout_ref, xn_ref, o_scr = rest[2 + n_src:]
    j0 = pl.program_id(1) * qb
    groups = []
    for t in range(qb):
        bias = _window_bias(j0 + t > 0, j0 + t < nb - 1)
        groups.append((slice(t * BLK, (t + 1) * BLK),
                       (k_blk[t], k_blk[t + 1], k_blk[t + 2], km_ref),
                       (v_blk[t], v_blk[t + 1], v_blk[t + 2], vm_ref), bias))
    _gqa_attend(sink_ref, q_ref, o_scr, groups)
    tile = pl.program_id(0) * (nb // qb) + pl.program_id(1)
    hn = _dot(o_scr[...], wo_ref[...]) + _select_source(h_refs, src_rows, qb * BLK, tile)
    hout_ref[...] = hn
    xn_ref[...] = _rms(hn, g_ref[...]).astype(BF16)


def _win_attn_call(sink, q, k, v, wo, h_sources, g, *, batch, seq, n_real):
    nb = seq // BLK
    qb = _pick_tile(nb, (4, 2, 1))
    mrow = n_real // N_META
    kvw = GQA_KV * LANES
    tile_of = lambda b, j, s: b * (nb // qb) + j

    def qmap(b, j, s):
        return (tile_of(b, j, s), 0)

    def blk_spec(t):
        return pl.BlockSpec((BLK, kvw),
                            lambda b, j, s: (b * nb + jnp.clip(j * qb + t - 1, 0, nb - 1), 0))

    meta_spec = pl.BlockSpec((N_META, kvw), lambda b, j, s: (mrow + b, 0))
    kv_specs = [blk_spec(t) for t in range(qb + 2)] + [meta_spec]
    fix = lambda b, j, s: (0, 0)
    grid_spec = pltpu.PrefetchScalarGridSpec(
        num_scalar_prefetch=1,
        grid=(batch, nb // qb),
        in_specs=[pl.BlockSpec((qb * BLK, ATTN_W), qmap)] + kv_specs + kv_specs
        + [pl.BlockSpec((ATTN_W, D_MODEL), fix)] + _source_specs(h_sources, qb * BLK, tile_of)
        + [pl.BlockSpec((1, D_MODEL), fix)],
        out_specs=[pl.BlockSpec((qb * BLK, D_MODEL), qmap), pl.BlockSpec((qb * BLK, D_MODEL), qmap)],
        scratch_shapes=[pltpu.VMEM((qb * BLK, ATTN_W), BF16)],
    )
    n_kv = qb + 3
    return pl.pallas_call(
        functools.partial(_win_attn_kernel, nb=nb, qb=qb, n_src=len(h_sources),
                          src_rows=tuple(s.shape[0] for s in h_sources)),
        grid_spec=grid_spec,
        out_shape=[jax.ShapeDtypeStruct((n_real, D_MODEL), F32),
                   jax.ShapeDtypeStruct((n_real, D_MODEL), BF16)],
        compiler_params=_params(("parallel", "parallel")),
        name="win_attn_wo",
    )(sink, q, *([k] * n_kv), *([v] * n_kv), wo, *h_sources, g)


def _win_meta_kernel(sink_ref, q_ref, km_ref, kc_ref, vm_ref, vc_ref, o_ref, *, batch):
    i = pl.program_id(0)

    @pl.when(i < batch)
    def _():
        nkeys = 2 * BLK
        qp = lax.broadcasted_iota(I32, (2 * N_META, nkeys), 0) & (N_META - 1)
        c = lax.broadcasted_iota(I32, (2 * N_META, nkeys), 1)
        neg = jnp.float32(-jnp.inf)
        b_c = jnp.where((N_META + c - qp) <= BLK, 0.0, neg)
        bias = jnp.where(c < BLK, b_c, jnp.where(c < BLK + N_META, 0.0, neg))
        _gqa_attend(sink_ref, q_ref, o_ref,
                    [(slice(0, N_META), (kc_ref, km_ref), (vc_ref, vm_ref), bias)])

    @pl.when(i >= batch)
    def _():
        o_ref[...] = jnp.zeros_like(o_ref)


def _win_meta_call(sink, q, k, v, *, batch, seq, n_real):
    ntot = q.shape[0]
    nb = seq // BLK
    mrow = n_real // N_META
    ntail_blocks = (ntot - n_real) // N_META

    def tmap(i, s):
        return (mrow + i, 0)

    def cmap(i, s):
        return (jnp.minimum(i, batch - 1) * nb, 0)

    grid_spec = pltpu.PrefetchScalarGridSpec(
        num_scalar_prefetch=1,
        grid=(ntail_blocks,),
        in_specs=[
            pl.BlockSpec((N_META, ATTN_W), tmap),
            pl.BlockSpec((N_META, GQA_KV * LANES), tmap),
            pl.BlockSpec((BLK, GQA_KV * LANES), cmap),
            pl.BlockSpec((N_META, GQA_KV * LANES), tmap),
            pl.BlockSpec((BLK, GQA_KV * LANES), cmap),
        ],
        out_specs=pl.BlockSpec((N_META, ATTN_W), lambda i, s: (i, 0)),
    )
    return pl.pallas_call(
        functools.partial(_win_meta_kernel, batch=batch),
        grid_spec=grid_spec,
        out_shape=jax.ShapeDtypeStruct((ntot - n_real, ATTN_W), BF16),
        compiler_params=_params(("parallel",)),
        name="win_attn_meta",
    )(sink, q, k, k, v, v)


def _attn_out_tile(o_real_ref, o_tail_ref, n_real_tiles):
    is_real = pl.program_id(0) < n_real_tiles
    return jnp.where(is_real, o_real_ref[...], o_tail_ref[...])


def _attn_out_specs(tm, n_real_tiles):
    return [
        pl.BlockSpec((tm, ATTN_W), lambda i: (jnp.minimum(i, n_real_tiles - 1), 0)),
        pl.BlockSpec((tm, ATTN_W), lambda i: (jnp.maximum(i - n_real_tiles, 0), 0)),
    ]


def _wo_kernel(o_ref, w_ref, h_ref, g_ref, hout_ref, xn_ref):
    hn = _dot(o_ref[...], w_ref[...]) + h_ref[...]
    hout_ref[...] = hn
    xn_ref[...] = _rms(hn, g_ref[...]).astype(BF16)


def _wo_call(o, w, h, g, *, tm):
    n = o.shape[0]
    row = lambda i: (i, 0)
    fix = lambda i: (0, 0)
    return pl.pallas_call(
        _wo_kernel,
        grid=(n // tm,),
        in_specs=[
            pl.BlockSpec((tm, ATTN_W), row),
            pl.BlockSpec((ATTN_W, D_MODEL), fix),
            pl.BlockSpec((tm, D_MODEL), row),
            pl.BlockSpec((1, D_MODEL), fix),
        ],
        out_specs=[pl.BlockSpec((tm, D_MODEL), row), pl.BlockSpec((tm, D_MODEL), row)],
        out_shape=[jax.ShapeDtypeStruct((n, D_MODEL), F32),
                   jax.ShapeDtypeStruct((n, D_MODEL), BF16)],
        compiler_params=_params(("parallel",)),
        name="wo_res_norm_tail",
    )(o, w, h, g)


XPK_ROWS = D_MODEL // 2 // LANES
YS_ROWS = D_MODEL // LANES


ROW_SPLIT = 2
MLA_PROJ_SPLIT = 4


def _store_token_major(ref, x, row0=0):
    t, n = x.shape
    k = n // LANES
    for s in range(k):
        ref[pl.ds(row0 * k + s, t, stride=k), :] = x[:, s * LANES:(s + 1) * LANES]


def _load_token_major(ref, k):
    t = ref.shape[0] // k
    return [ref[pl.ds(s, t, stride=k), :] for s in range(k)]


def _pack_bf16_pair(a, b):
    ra = pltpu.bitcast(a.astype(BF16).astype(F32), U32)
    rb = pltpu.bitcast(b.astype(BF16).astype(F32), U32)
    return ra | (rb >> 16)


def _unpack_bf16_pair(p):
    a = pltpu.bitcast(p & jnp.uint32(0xFFFF0000), F32).astype(BF16)
    b = pltpu.bitcast(p << 16, F32).astype(BF16)
    return a, b


def _wo_router_kernel(o_real_ref, o_tail_ref, w_ref, h_ref, g_ref, r_ref,
                      hout_ref, xpk_ref, route_ref, cnt_ref, carry_ref, *, n_real_tiles):
    i = pl.program_id(0)
    tm = h_ref.shape[0]

    @pl.when(i == 0)
    def _():
        carry_ref[...] = jnp.zeros_like(carry_ref)

    o = _attn_out_tile(o_real_ref, o_tail_ref, n_real_tiles)
    th = tm // ROW_SPLIT
    parts = [slice(p * th, (p + 1) * th) for p in range(ROW_SPLIT)]
    half = D_MODEL // 2
    r2 = r_ref[...]
    lane = lax.broadcasted_iota(I32, (th, LANES), 1)
    rr = lax.broadcasted_iota(I32, (th, th), 0)
    cc = lax.broadcasted_iota(I32, (th, th), 1)
    lower = jnp.where(cc < rr, 1.0, 0.0).astype(BF16)

    ys = []
    for rs in parts:
        hn = _dot(o[rs], w_ref[...]) + h_ref[rs, :]
        hout_ref[rs, :] = hn
        ys.append(_rms(hn, g_ref[...]))
    logits = []
    for p, y in enumerate(ys):
        _store_token_major(xpk_ref, _pack_bf16_pair(y[:, :half], y[:, half:]), row0=p * th)
        yhi = y.astype(BF16)
        ylo = (y - yhi.astype(F32)).astype(BF16)
        t = _dot(yhi, r2)
        logits.append(t[:, :LANES] + t[:, LANES:] + _dot(ylo, r2[:, :LANES]))
    carry = carry_ref[0:1, :]
    for rs, lg in zip(parts, logits):
        lm = jnp.where(lane < N_EXPERTS, lg, -jnp.inf)
        m1 = jnp.max(lm, axis=-1, keepdims=True)
        i1 = jnp.min(jnp.where(lm == m1, lane, LANES), axis=-1, keepdims=True)
        lm2 = jnp.where(lane == i1, -jnp.inf, lm)
        m2 = jnp.max(lm2, axis=-1, keepdims=True)
        i2 = jnp.min(jnp.where(lm2 == m2, lane, LANES), axis=-1, keepdims=True)
        e21 = jnp.exp(m2 - m1)
        g0 = 1.0 / (1.0 + e21)
        g1 = e21 / (1.0 + e21)
        onehot = ((lane == i1) | (lane == i2)).astype(F32)
        before = _dot(lower, onehot.astype(BF16)) + carry
        rank1 = jnp.sum(jnp.where(lane == i1, before, 0.0), axis=-1, keepdims=True)
        rank2 = jnp.sum(jnp.where(lane == i2, before, 0.0), axis=-1, keepdims=True)
        carry = carry + jnp.sum(onehot, axis=0, keepdims=True)
        out = jnp.where(lane == 0, i1.astype(F32), 0.0)
        out = jnp.where(lane == 1, i2.astype(F32), out)
        out = jnp.where(lane == 2, rank1, out)
        out = jnp.where(lane == 3, rank2, out)
        out = jnp.where(lane == 4, g0, out)
        out = jnp.where(lane == 5, g1, out)
        route_ref[rs, :] = out
    carry_ref[...] = jnp.broadcast_to(carry, carry_ref.shape)
    cnt_ref[...] = jnp.broadcast_to(carry, cnt_ref.shape)


def _wo_router_call(o_real, o_tail, w, h, g, r2, *, tm):
    ntot = h.shape[0]
    nrt = o_real.shape[0] // tm
    row = lambda i: (i, 0)
    fix = lambda i: (0, 0)
    return pl.pallas_call(
        functools.partial(_wo_router_kernel, n_real_tiles=nrt),
        grid=(ntot // tm,),
        in_specs=_attn_out_specs(tm, nrt) + [
            pl.BlockSpec((ATTN_W, D_MODEL), fix),
            pl.BlockSpec((tm, D_MODEL), row),
            pl.BlockSpec((1, D_MODEL), fix),
            pl.BlockSpec((D_MODEL, 2 * LANES), fix),
        ],
        out_specs=[
            pl.BlockSpec((tm, D_MODEL), row),
            pl.BlockSpec((tm * XPK_ROWS, LANES), row),
            pl.BlockSpec((tm, LANES), row),
            pl.BlockSpec((8, LANES), fix),
        ],
        out_shape=[
            jax.ShapeDtypeStruct((ntot, D_MODEL), F32),
            jax.ShapeDtypeStruct((ntot * XPK_ROWS, LANES), U32),
            jax.ShapeDtypeStruct((ntot, LANES), F32),
            jax.ShapeDtypeStruct((8, LANES), F32),
        ],
        scratch_shapes=[pltpu.VMEM((8, LANES), F32)],
        compiler_params=_params(("arbitrary",)),
        name="wo_res_norm_router",
    )(o_real, o_tail, w, h, g, r2)


def _swiglu_chunk(x, wg, wu, wd):
    g = _dot(x, wg)
    u = _dot(x, wu)
    a = (g * (1.0 / (1.0 + jnp.exp(-g)))) * u
    return _dot(a.astype(BF16), wd)


def _ffn_dense_kernel(*refs, n_src, src_rows, tm, tf):
    x_refs, h_refs = refs[:n_src], refs[n_src:2 * n_src]
    wg_ref, wu_ref, wd_ref, g_ref, hout_ref, xn_ref = refs[2 * n_src:]
    x = _select_source(x_refs, src_rows, tm)
    hn = _select_source(h_refs, src_rows, tm)
    for c in range(wg_ref.shape[1] // tf):
        cs = slice(c * tf, (c + 1) * tf)
        hn = hn + _swiglu_chunk(x, wg_ref[:, cs], wu_ref[:, cs], wd_ref[cs, :])
    hout_ref[...] = hn
    xn_ref[...] = _rms(hn, g_ref[...]).astype(BF16)


def _ffn_dense_call(x_sources, wg, wu, wd, h_sources, g, *, tm, tf):
    src_rows = tuple(s.shape[0] for s in h_sources)
    ntot = sum(src_rows)
    fdim = wg.shape[1]
    row = lambda i: (i, 0)
    fix = lambda i: (0, 0)
    once = pl.Buffered(1)
    return pl.pallas_call(
        functools.partial(_ffn_dense_kernel, n_src=len(h_sources), src_rows=src_rows, tm=tm, tf=tf),
        grid=(ntot // tm,),
        in_specs=_source_specs(x_sources, tm) + _source_specs(h_sources, tm) + [
            pl.BlockSpec((D_MODEL, fdim), fix, pipeline_mode=once),
            pl.BlockSpec((D_MODEL, fdim), fix, pipeline_mode=once),
            pl.BlockSpec((fdim, D_MODEL), fix, pipeline_mode=once),
            pl.BlockSpec((1, D_MODEL), fix),
        ],
        out_specs=[pl.BlockSpec((tm, D_MODEL), row), pl.BlockSpec((tm, D_MODEL), row)],
        out_shape=[jax.ShapeDtypeStruct((ntot, D_MODEL), F32),
                   jax.ShapeDtypeStruct((ntot, D_MODEL), BF16)],
        compiler_params=_params(("parallel",)),
        name="ffn_dense",
    )(*x_sources, *h_sources, wg, wu, wd, g)


def _ffn_expert_kernel(be_ref, nu_ref, xs_ref, wg_ref, wu_ref, wd_ref, ys_ref, *, tf):
    del be_ref
    used = pl.program_id(0) < nu_ref[0]

    @pl.when(used)
    def _():
        halves = [_unpack_bf16_pair(blk) for blk in _load_token_major(xs_ref, XPK_ROWS)]
        x = jnp.concatenate([a for a, _ in halves] + [b for _, b in halves], axis=1)
        acc = None
        for c in range(wg_ref.shape[1] // tf):
            cs = slice(c * tf, (c + 1) * tf)
            y = _swiglu_chunk(x, wg_ref[:, cs], wu_ref[:, cs], wd_ref[cs, :])
            acc = y if acc is None else acc + y
        _store_token_major(ys_ref, acc)

    @pl.when(jnp.logical_not(used))
    def _():
        ys_ref[...] = jnp.zeros_like(ys_ref)


def _ffn_expert_call(block_e, n_used, xs, wg, wu, wd, *, tm, tf):
    rows = xs.shape[0] // XPK_ROWS
    once = pl.Buffered(1)
    wmap = lambda i, be, nu: (be[i], 0, 0)
    grid_spec = pltpu.PrefetchScalarGridSpec(
        num_scalar_prefetch=2,
        grid=(rows // tm,),
        in_specs=[
            pl.BlockSpec((tm * XPK_ROWS, LANES), lambda i, be, nu: (i, 0)),
            pl.BlockSpec((None,) + wg.shape[1:], wmap, pipeline_mode=once),
            pl.BlockSpec((None,) + wu.shape[1:], wmap, pipeline_mode=once),
            pl.BlockSpec((None,) + wd.shape[1:], wmap, pipeline_mode=once),
        ],
        out_specs=pl.BlockSpec((tm * YS_ROWS, LANES), lambda i, be, nu: (i, 0)),
    )
    return pl.pallas_call(
        functools.partial(_ffn_expert_kernel, tf=tf),
        grid_spec=grid_spec,
        out_shape=jax.ShapeDtypeStruct((rows * YS_ROWS, LANES), F32),
        compiler_params=_params(("arbitrary",)),
        name="ffn_expert",
    )(block_e, n_used, xs, wg, wu, wd)


DMA_UNROLL = 8


def _token_copy(src, src_tok, dst, dst_tok, rows, sem):
    return pltpu.make_async_copy(src.at[pl.ds(pl.multiple_of(src_tok * rows, rows), rows)],
                                 dst.at[pl.ds(pl.multiple_of(dst_tok * rows, rows), rows)], sem)


def _dispatch_kernel(dest_ref, x_ref, xs_in_ref, xs_ref, sem):
    del xs_in_ref
    td = x_ref.shape[0] // XPK_ROWS

    def issue(t, carry):
        for s in range(2):
            _token_copy(x_ref, t, xs_ref, dest_ref[0, s, t], XPK_ROWS, sem).start(priority=s)
        return carry

    lax.fori_loop(0, td, issue, 0, unroll=DMA_UNROLL)

    def drain(t, carry):
        for s in range(2):
            _token_copy(x_ref, 0, xs_ref, 0, XPK_ROWS, sem).wait()
        return carry

    lax.fori_loop(0, td, drain, 0, unroll=DMA_UNROLL)


def _dispatch_call(dest, xpk, xs_init, *, td):
    ntot = xpk.shape[0] // XPK_ROWS
    return pl.pallas_call(
        _dispatch_kernel,
        grid=(ntot // td,),
        in_specs=[
            pl.BlockSpec((1, 2, td), lambda i: (i, 0, 0), memory_space=pltpu.SMEM),
            pl.BlockSpec((td * XPK_ROWS, LANES), lambda i: (i, 0)),
            pl.BlockSpec(memory_space=pl.ANY),
        ],
        out_specs=pl.BlockSpec(memory_space=pl.ANY),
        out_shape=jax.ShapeDtypeStruct(xs_init.shape, U32),
        input_output_aliases={2: 0},
        scratch_shapes=[pltpu.SemaphoreType.DMA(())],
        compiler_params=_params(("arbitrary",)),
        name="moe_dispatch",
    )(dest, xpk, xs_init)


def _combine_kernel(dest_ref, dest_next_ref, ys_ref, h_ref, route_ref, g_ref, out_ref,
                    y0_ref, y1_ref, sems):
    i = pl.program_id(0)
    n = pl.num_programs(0)
    tc = h_ref.shape[0]
    bufs = (y0_ref, y1_ref)
    slot = i % 2

    def issue(d_ref, sl):
        def body(t, carry):
            for s in range(2):
                _token_copy(ys_ref, d_ref[0, s, t], bufs[s].at[sl], t, YS_ROWS,
                            sems.at[sl]).start(priority=s)
            return carry

        lax.fori_loop(0, tc, body, 0, unroll=DMA_UNROLL)

    @pl.when(i == 0)
    def _():
        issue(dest_ref, 0)

    @pl.when(i + 1 < n)
    def _():
        issue(dest_next_ref, 1 - slot)

    def drain(t, carry):
        for s in range(2):
            _token_copy(ys_ref, 0, bufs[s].at[slot], 0, YS_ROWS, sems.at[slot]).wait()
        return carry

    lax.fori_loop(0, tc, drain, 0, unroll=DMA_UNROLL)

    r = route_ref[...]
    lane = lax.broadcasted_iota(I32, r.shape, 1)
    g0 = jnp.sum(jnp.where(lane == 4, r, 0.0), axis=-1, keepdims=True)
    g1 = jnp.sum(jnp.where(lane == 5, r, 0.0), axis=-1, keepdims=True)
    y0 = _load_token_major(y0_ref.at[slot], YS_ROWS)
    y1 = _load_token_major(y1_ref.at[slot], YS_ROWS)
    moe = jnp.concatenate([a * g0 + b * g1 for a, b in zip(y0, y1)], axis=1)
    out_ref[...] = _rms(h_ref[...] + moe, g_ref[...])


def _combine_call(dest, ys, h, route, g, *, tc, tile0, ntiles):
    return pl.pallas_call(
        _combine_kernel,
        grid=(ntiles,),
        in_specs=[
            pl.BlockSpec((1, 2, tc), lambda i: (tile0 + i, 0, 0), memory_space=pltpu.SMEM),
            pl.BlockSpec((1, 2, tc), lambda i: (tile0 + jnp.minimum(i + 1, ntiles - 1), 0, 0),
                         memory_space=pltpu.SMEM),
            pl.BlockSpec(memory_space=pl.ANY),
            pl.BlockSpec((tc, D_MODEL), lambda i: (tile0 + i, 0)),
            pl.BlockSpec((tc, LANES), lambda i: (tile0 + i, 0)),
            pl.BlockSpec((1, D_MODEL), lambda i: (0, 0)),
        ],
        out_specs=pl.BlockSpec((tc, D_MODEL), lambda i: (i, 0)),
        out_shape=jax.ShapeDtypeStruct((ntiles * tc, D_MODEL), F32),
        scratch_shapes=[pltpu.VMEM((2, tc * YS_ROWS, LANES), F32), pltpu.VMEM((2, tc * YS_ROWS, LANES), F32),
                        pltpu.SemaphoreType.DMA((2,))],
        compiler_params=_params(("arbitrary",)),
        name="moe_combine_norm",
    )(dest, dest, ys, h, route, g)


def _mla_proj_kernel(x_ref, wd_ref, qn_ref, kvn_ref, wuq_ref, wuk_ref, wuvt_ref,
                     c_ref, s1_ref, s2_ref, q_ref, k_ref, vt_ref):
    tm = x_ref.shape[0]
    th = tm // MLA_PROJ_SPLIT
    parts = [slice(p * th, (p + 1) * th) for p in range(MLA_PROJ_SPLIT)]
    half = MLA_ROPE // 2
    scale = (MLA_QK ** -0.5) * LOG2E
    lats = [_dot(x_ref[rs, :], wd_ref[...]) for rs in parts]
    ups = []
    for lat in lats:
        cq = _rms(lat[:, :MLA_Q_RANK], qn_ref[...]).astype(BF16)
        ckv = _rms(lat[:, MLA_Q_RANK:MLA_Q_RANK + MLA_KV_RANK], kvn_ref[...]).astype(BF16)
        ups.append((_dot(cq, wuq_ref[...]),
                    _dot(ckv, wuk_ref[...]),
                    _dot_nt(wuvt_ref[...], ckv)))
    for rs, lat, (q, k, vt) in zip(parts, lats, ups):
        c, s1, s2 = c_ref[rs, :], s1_ref[rs, :], s2_ref[rs, :]
        kr = _rope(lat[:, MLA_Q_RANK + MLA_KV_RANK:], c, s1, s2, half)
        for h in range(MLA_HEADS):
            sl = slice(h * LANES, (h + 1) * LANES)
            q_ref[rs, sl] = (_rope(q[:, sl], c, s1, s2, half) * scale).astype(BF16)
            k_ref[rs, sl] = (k[:, sl] + kr).astype(BF16)
        vt_ref[:, rs] = vt.astype(BF16)


def _mla_proj_call(x, wd, qn, kvn, wuq, wuk, wuv, tabs, *, tm, n_real, seq):
    ntot = x.shape[0]
    nrt, tps = n_real // tm, seq // tm
    row = lambda i: (i, 0)
    fix = lambda i: (0, 0)
    tab_spec = pl.BlockSpec((tm, LANES), lambda i: (_table_index(i, nrt, tps), 0))
    return pl.pallas_call(
        _mla_proj_kernel,
        grid=(ntot // tm,),
        in_specs=[
            pl.BlockSpec((tm, D_MODEL), row),
            pl.BlockSpec(wd.shape, fix),
            pl.BlockSpec((1, MLA_Q_RANK), fix),
            pl.BlockSpec((1, MLA_KV_RANK), fix),
            pl.BlockSpec(wuq.shape, fix),
            pl.BlockSpec(wuk.shape, fix),
            pl.BlockSpec(wuv.shape[::-1], fix),
            tab_spec, tab_spec, tab_spec,
        ],
        out_specs=[
            pl.BlockSpec((tm, MLA_QK_W), row),
            pl.BlockSpec((tm, MLA_QK_W), row),
            pl.BlockSpec((ATTN_W, tm), lambda i: (0, i)),
        ],
        out_shape=[
            jax.ShapeDtypeStruct((ntot, MLA_QK_W), BF16),
            jax.ShapeDtypeStruct((ntot, MLA_QK_W), BF16),
            jax.ShapeDtypeStruct((ATTN_W, ntot), BF16),
        ],
        compiler_params=_params(("parallel",)),
        name="mla_proj",
    )(x, wd, qn, kvn, wuq, wuk, wuv.T, *tabs)


META_PER_BLOCK = LANES // N_META
REDUCE_ROWS = 64
SCORES_AHEAD = 1
VALUES_BEHIND = 1
MLA_QUERY_TILE = 256


def _col_reduce(x, op):
    rows, n = x.shape
    if rows > REDUCE_ROWS and rows % REDUCE_ROWS == 0:
        x = op(x.reshape(rows // REDUCE_ROWS, REDUCE_ROWS, n), axis=0)
    return op(x, axis=0, keepdims=True)


def _mla_attn_kernel(q_ref, kr_ref, kt_ref, vtr_ref, vtt_ref, o_ref, ot_ref):
    b = pl.program_id(0)
    r = lax.broadcasted_iota(I32, (LANES, 1), 0)
    mine = jnp.right_shift(r, _log2(N_META)) == (b % META_PER_BLOCK)
    mbias = jnp.where(mine, 0.0, -jnp.inf).astype(F32)
    tq = min(MLA_QUERY_TILE, q_ref.shape[0])
    units = [(g, h) for g in range(q_ref.shape[0] // tq) for h in range(MLA_HEADS)]

    def scores(g, h):
        hs = slice(h * LANES, (h + 1) * LANES)
        qh = q_ref[g * tq:(g + 1) * tq, hs]
        return _dot_nt(kr_ref[:, hs], qh), _dot_nt(kt_ref[:, hs], qh) + mbias

    def weighted_values(g, h, p_r, p_m, den):
        vs = slice(h * MLA_V, (h + 1) * MLA_V)
        acc = _dot(vtr_ref[vs, :], p_r) + _dot(vtt_ref[vs, :], p_m)
        ot_ref[vs, g * tq:(g + 1) * tq] = acc / den

    ahead = [scores(*u) for u in units[:SCORES_AHEAD]]
    pending = []
    for idx, unit in enumerate(units):
        s_r, s_m = ahead.pop(0)
        if idx + SCORES_AHEAD < len(units):
            ahead.append(scores(*units[idx + SCORES_AHEAD]))
        m = jnp.maximum(_col_reduce(s_r, jnp.max), _col_reduce(s_m, jnp.max))
        p_r = jnp.exp2(s_r - m)
        p_m = jnp.exp2(s_m - m)
        den = _col_reduce(p_r, jnp.sum) + _col_reduce(p_m, jnp.sum)
        pending.append((*unit, p_r.astype(BF16), p_m.astype(BF16), den))
        if len(pending) > VALUES_BEHIND:
            weighted_values(*pending.pop(0))
    for item in pending:
        weighted_values(*item)
    o_ref[...] = ot_ref[...].T.astype(BF16)


def _mla_attn_call(q, k, vt, *, batch, seq, n_real, tq):
    nq = seq // tq
    tail0 = n_real // LANES
    qmap = lambda b, j: (b * nq + j, 0)
    return pl.pallas_call(
        _mla_attn_kernel,
        grid=(batch, nq),
        in_specs=[
            pl.BlockSpec((tq, MLA_QK_W), qmap),
            pl.BlockSpec((seq, MLA_QK_W), lambda b, j: (b, 0)),
            pl.BlockSpec((LANES, MLA_QK_W), lambda b, j: (tail0 + b // META_PER_BLOCK, 0)),
            pl.BlockSpec((ATTN_W, seq), lambda b, j: (0, b)),
            pl.BlockSpec((ATTN_W, LANES), lambda b, j: (0, tail0 + b // META_PER_BLOCK)),
        ],
        out_specs=pl.BlockSpec((tq, ATTN_W), qmap),
        out_shape=jax.ShapeDtypeStruct((n_real, ATTN_W), BF16),
        scratch_shapes=[pltpu.VMEM((ATTN_W, tq), F32)],
        compiler_params=_params(("parallel", "arbitrary")),
        name="mla_attn",
    )(q, k, k, vt, vt)


def _mla_meta_kernel(q_ref, kr_ref, kt_ref, vtr_ref, vtt_ref, o_ref, *, batch):
    i = pl.program_id(0)
    ncol = MLA_HEADS * N_META

    @pl.when(i < batch)
    def _():
        qi = lax.broadcasted_iota(I32, (N_META, ncol), 0)
        ci = lax.broadcasted_iota(I32, (N_META, ncol), 1)
        spread = jnp.where((ci & (N_META - 1)) == qi, 1.0, 0.0).astype(BF16)
        w = lax.dot_general(q_ref[...], spread, (((0,), (0,)), ((), ())),
                            preferred_element_type=F32)
        wr = lax.broadcasted_iota(I32, w.shape, 0)
        wc = lax.broadcasted_iota(I32, w.shape, 1)
        on_diag = jnp.right_shift(wr, _log2(LANES)) == jnp.right_shift(wc, _log2(N_META))
        w = jnp.where(on_diag, w, 0.0).astype(BF16)
        r = lax.broadcasted_iota(I32, (LANES, 1), 0)
        mine = jnp.right_shift(r, _log2(N_META)) == (i % META_PER_BLOCK)
        mbias = jnp.where(mine, 0.0, -jnp.inf).astype(F32)
        s_r = _dot(kr_ref[...], w)
        s_m = _dot(kt_ref[...], w) + mbias
        m = jnp.maximum(_col_reduce(s_r, jnp.max), _col_reduce(s_m, jnp.max))
        p_r = jnp.exp2(s_r - m)
        p_m = jnp.exp2(s_m - m)
        den = _col_reduce(p_r, jnp.sum) + _col_reduce(p_m, jnp.sum)
        ot = (_dot(vtr_ref[...], p_r.astype(BF16)) + _dot(vtt_ref[...], p_m.astype(BF16))) / den
        o_all = ot.T
        lane = lax.broadcasted_iota(I32, (N_META, ATTN_W), 1)
        o = jnp.zeros((N_META, ATTN_W), F32)
        for h in range(MLA_HEADS):
            rows = o_all[h * N_META:(h + 1) * N_META, :]
            o = jnp.where(jnp.right_shift(lane, _log2(MLA_V)) == h, rows, o)
        o_ref[...] = o.astype(BF16)

    @pl.when(i >= batch)
    def _():
        o_ref[...] = jnp.zeros_like(o_ref)


def _mla_meta_call(q, k, vt, *, batch, seq, n_real):
    ntot = q.shape[0]
    mrow = n_real // N_META
    tail0 = n_real // LANES
    ntail_blocks = (ntot - n_real) // N_META
    last = (batch - 1) // META_PER_BLOCK
    bmap = lambda i: jnp.minimum(i, batch - 1)
    tmap = lambda i: jnp.minimum(i // META_PER_BLOCK, last)
    return pl.pallas_call(
        functools.partial(_mla_meta_kernel, batch=batch),
        grid=(ntail_blocks,),
        in_specs=[
            pl.BlockSpec((N_META, MLA_QK_W), lambda i: (mrow + i, 0)),
            pl.BlockSpec((seq, MLA_QK_W), lambda i: (bmap(i), 0)),
            pl.BlockSpec((LANES, MLA_QK_W), lambda i: (tail0 + tmap(i), 0)),
            pl.BlockSpec((ATTN_W, seq), lambda i: (0, bmap(i))),
            pl.BlockSpec((ATTN_W, LANES), lambda i: (0, tail0 + tmap(i))),
        ],
        out_specs=pl.BlockSpec((N_META, ATTN_W), lambda i: (i, 0)),
        out_shape=jax.ShapeDtypeStruct((ntot - n_real, ATTN_W), BF16),
        compiler_params=_params(("arbitrary",)),
        name="mla_attn_meta",
    )(q, k, k, vt, vt)


def _prep_gqa(a_wqkv, a_wo):
    d = a_wqkv.shape[0]
    nq = GQA_HEADS * GQA_HD
    nkv = GQA_KV * GQA_HD
    wq = a_wqkv[:, :nq] * (GQA_HD ** -0.5)

    def dup(w):
        w = w.reshape(d, GQA_KV, 1, GQA_HD)
        return jnp.broadcast_to(w, (d, GQA_KV, 2, GQA_HD)).reshape(d, 2 * nkv)

    w = jnp.concatenate([wq, dup(a_wqkv[:, nq:nq + nkv]), dup(a_wqkv[:, nq + nkv:])], axis=1)
    return w.astype(BF16), a_wo.astype(BF16)


def _prep_mla(b_wdkv, b_wuq, b_wukv):
    d = b_wdkv.shape[0]
    lat = MLA_Q_RANK + MLA_KV_RANK
    z = lambda n: jnp.zeros((d, n), F32)
    wd = jnp.concatenate([b_wdkv[:, :lat], z(MLA_NOPE), b_wdkv[:, lat:], z(LANES - MLA_QK)], axis=1)
    wuq = b_wuq.reshape(MLA_Q_RANK, MLA_HEADS, MLA_QK)
    wuq = jnp.pad(wuq, ((0, 0), (0, 0), (0, LANES - MLA_QK))).reshape(MLA_Q_RANK, MLA_HEADS * LANES)
    wukv = b_wukv.reshape(MLA_KV_RANK, MLA_HEADS, MLA_NOPE + MLA_V)
    wuk = jnp.pad(wukv[:, :, :MLA_NOPE], ((0, 0), (0, 0), (0, LANES - MLA_NOPE)))
    wuk = wuk.reshape(MLA_KV_RANK, MLA_HEADS * LANES)
    wuv = wukv[:, :, MLA_NOPE:].reshape(MLA_KV_RANK, MLA_HEADS * MLA_V)
    return wd.astype(BF16), wuq.astype(BF16), wuk.astype(BF16), wuv.astype(BF16)


def _prep_router(m_router):
    hi = m_router.astype(BF16)
    lo = (m_router - hi.astype(F32)).astype(BF16)
    pad = lambda r: jnp.pad(r, ((0, 0), (0, LANES - N_EXPERTS)))
    return jnp.concatenate([pad(hi), pad(lo)], axis=1)


def _positions(seq, n_tail, n_meta_rows):
    real = N_META + jnp.arange(seq, dtype=F32)
    t = jnp.arange(n_tail)
    tail = jnp.where(t < n_meta_rows, t % N_META, 0).astype(F32)
    return jnp.concatenate([real, tail])


def _trunk_flat(x_groups, meta, norm_mix, norm_ffn, norm_final,
                a_wqkv, a_wo, a_sink,
                b_wdkv, b_qnorm, b_kvnorm, b_wuq, b_wukv, b_wo,
                f_wg, f_wu, f_wd,
                m_router, m_wg, m_wu, m_wd):
    seq = x_groups[0].shape[1]
    batches = [x.shape[0] for x in x_groups]
    batch = sum(batches)
    n_real = batch * seq
    n_meta_rows = batch * N_META
    big = _pick_tile(seq, (1024, 512, 256, 128))
    tp = min(big, 512)
    ntot = -(-(n_real + n_meta_rows) // big) * big
    n_tail = ntot - n_real

    tail = jnp.concatenate([
        jnp.broadcast_to(meta.astype(F32)[None], (batch, N_META, D_MODEL)).reshape(n_meta_rows, D_MODEL),
        jnp.zeros((n_tail - n_meta_rows, D_MODEL), F32)], axis=0)
    h_sources = [x.reshape(-1, D_MODEL) for x in x_groups] + [tail]

    pos = _positions(seq, n_tail, n_meta_rows)
    tabs_a = _rope_lane_tables(pos, GQA_THETA, GQA_ROT, GQA_HD, 0)
    tabs_b = _rope_lane_tables(pos, MLA_THETA, MLA_ROPE, LANES, MLA_NOPE)
    row = lambda v: v.reshape(1, -1).astype(F32)

    wqkv, wo_a = _prep_gqa(a_wqkv[0], a_wo[0])
    q, k, v = _qkv_call(h_sources, row(norm_mix[0]), wqkv, tabs_a, tm=tp, n_real=n_real, seq=seq)
    sink = a_sink[0].astype(F32) * LOG2E
    h_real, xn_real = _win_attn_call(sink, q, k, v, wo_a, h_sources[:-1], row(norm_ffn[0]),
                                     batch=batch, seq=seq, n_real=n_real)
    o_tail = _win_meta_call(sink, q, k, v, batch=batch, seq=seq, n_real=n_real)
    h_tail, xn_tail = _wo_call(o_tail, wo_a, tail, row(norm_ffn[0]), tm=tp)
    fdim = f_wg.shape[2]
    tf = _pick_tile(fdim, (512, 256, 128))
    h, xn = _ffn_dense_call([xn_real, xn_tail], f_wg[0].astype(BF16), f_wu[0].astype(BF16),
                            f_wd[0].astype(BF16), [h_real, h_tail], row(norm_mix[1]), tm=tp, tf=tf)

    wd, wuq, wuk, wuv = _prep_mla(b_wdkv[0], b_wuq[0], b_wukv[0])
    q, k, vt = _mla_proj_call(xn, wd, row(b_qnorm[0]), row(b_kvnorm[0]), wuq, wuk, wuv, tabs_b,
                              tm=tp, n_real=n_real, seq=seq)
    tq = _pick_tile(seq, (512, 256, 128))
    o = _mla_attn_call(q, k, vt, batch=batch, seq=seq, n_real=n_real, tq=tq)
    o_tail = _mla_meta_call(q, k, vt, batch=batch, seq=seq, n_real=n_real)
    tr = tp
    h, xpk, route, counts = _wo_router_call(o, o_tail, b_wo[0].astype(BF16), h, row(norm_ffn[1]),
                                            _prep_router(m_router[0]), tm=tr)

    tmx = tp
    counts = counts[0, :N_EXPERTS].astype(I32)
    padded = ((counts + tmx - 1) // tmx) * tmx
    pend = jnp.cumsum(padded)
    pstart = pend - padded
    e_idx = route[:, 0:2].astype(I32)
    rank = route[:, 2:4].astype(I32)
    dest = pstart[e_idx] + rank
    n_assign = 2 * ntot
    n_blocks = -(-n_assign // tmx) + N_EXPERTS
    rows = n_blocks * tmx
    block_e = jnp.clip(jnp.searchsorted(pend, jnp.arange(n_blocks, dtype=I32) * tmx, side="right"),
                       0, N_EXPERTS - 1).astype(I32)
    n_used = (pend[-1] // tmx).astype(I32).reshape(1)
    block_e = jnp.where(jnp.arange(n_blocks) < n_used[0], block_e, block_e[jnp.maximum(n_used[0] - 1, 0)])

    td = tp
    dest_t = dest.reshape(ntot // td, td, 2).transpose(0, 2, 1)
    xs = _dispatch_call(dest_t, xpk, jnp.zeros((rows * XPK_ROWS, LANES), U32), td=td)
    edim = m_wg.shape[3]
    tfx = _pick_tile(edim, (512, 256, 128))
    ys = _ffn_expert_call(block_e, n_used, xs, m_wg[0].astype(BF16), m_wu[0].astype(BF16),
                          m_wd[0].astype(BF16), tm=tmx, tf=tfx)

    outs = []
    tile0 = 0
    for b, x in zip(batches, x_groups):
        ntiles = b * seq // td
        y = _combine_call(dest_t, ys, h, route, row(norm_final), tc=td, tile0=tile0, ntiles=ntiles)
        outs.append(y.reshape(b, seq, D_MODEL))
        tile0 += ntiles
    return tuple(outs)


def kernel(x_prompt, x_sample, meta, norm_mix, norm_ffn, norm_final, a_wqkv, a_wo, a_sink, b_wdkv, b_qnorm, b_kvnorm, b_wuq, b_wukv, b_wo, f_wg, f_wu, f_wd, m_router, m_wg, m_wu, m_wd):
    assert x_prompt.shape[1] == x_sample.shape[1] and x_prompt.shape[1] % BLK == 0
    assert norm_mix.shape[0] == 2, "two layers: windowed GQA + dense FFN, then MLA + MoE"
    return _trunk_flat((x_prompt, x_sample), meta, norm_mix, norm_ffn, norm_final,
                       a_wqkv, a_wo, a_sink,
                       b_wdkv, b_qnorm, b_kvnorm, b_wuq, b_wukv, b_wo,
                       f_wg, f_wu, f_wd,
                       m_router, m_wg, m_wu, m_wd)
```
